```python
import jax, jax.numpy as jnp
from jax import lax
import numpy as np

D_MODEL = 1024
BATCH = 2
SEQ = 8192
DEPTH = 2
DEC_BATCH = 32
DEC_SEQ = 16
PAST_LEN = 4096

CHUNK = 64
EPS = 1e-6
NEG_INF = -1e30

A_HEADS = 4
A_HEAD_DIM = 64
A_WIDTH = A_HEADS * A_HEAD_DIM
A_BACK = 8
A_BAND = (A_BACK + 1) * CHUNK
A_WINDOW = A_BACK * CHUNK
A_REL_CLIP = 128
A_SCALE = A_HEAD_DIM ** -0.5

B_HEADS = 4
B_Q_LORA = 192
B_KV_LORA = 128
B_NOPE = 64
B_ROPE = 32
B_V = 64
B_WIDTH = B_HEADS * B_V
B_QBLOCK = 128
B_SCALE = (B_NOPE + B_ROPE) ** -0.5
ROPE_THETA = 10000.0

C_D_INNER = 512
C_HEAD_DIM = 64
C_HEADS = C_D_INNER // C_HEAD_DIM
C_GROUPS = 2
C_D_STATE = 64
C_CONV_W = 4
C_CONV_DIM = C_D_INNER + 2 * C_GROUPS * C_D_STATE
SSD_CHUNK = 64

D_MIX = A_WIDTH + B_WIDTH + C_D_INNER
SPLIT_POINTS = (
    A_WIDTH, 2 * A_WIDTH, 3 * A_WIDTH,
    3 * A_WIDTH + B_Q_LORA,
    3 * A_WIDTH + B_Q_LORA + B_KV_LORA,
    3 * A_WIDTH + B_Q_LORA + B_KV_LORA + B_ROPE,
    3 * A_WIDTH + B_Q_LORA + B_KV_LORA + B_ROPE + C_D_INNER,
    3 * A_WIDTH + B_Q_LORA + B_KV_LORA + B_ROPE + C_D_INNER + C_CONV_DIM,
)
D_IN = SPLIT_POINTS[-1] + C_HEADS

E_GROUPS = 4
E_PER_GROUP = 4
N_EXPERTS = E_GROUPS * E_PER_GROUP
E_HIDDEN = 256
E_TOPK = 2

kernel_name = 'hybrid_streaming_encoder_step'

F32 = jnp.float32


def rmsnorm(x, g):
    xf = x.astype(F32)
    y = xf * lax.rsqrt(jnp.mean(xf * xf, axis=-1, keepdims=True) + EPS)
    return (y * g.astype(F32)).astype(x.dtype)


def modulate(h, shift, scale):
    return h * (1.0 + scale[:, None, :]) + shift[:, None, :]


def rope(x, pos):
    half = x.shape[-1] // 2
    inv = 1.0 / (ROPE_THETA ** (jnp.arange(half, dtype=F32) / half))
    ang = pos.astype(F32)[:, None] * inv[None, :]
    cos = jnp.cos(ang)[None, :, None, :]
    sin = jnp.sin(ang)[None, :, None, :]
    x1 = x[..., :half].astype(F32)
    x2 = x[..., half:].astype(F32)
    return jnp.concatenate([x1 * cos - x2 * sin, x1 * sin + x2 * cos], axis=-1).astype(x.dtype)


def band_attn_prompt(q, k, v, rel_bias):
    b, s, H, dh = q.shape
    nc = s // CHUNK
    qc = q.reshape(b, nc, CHUNK, H, dh)
    pad = ((0, 0), (A_BACK, 0), (0, 0), (0, 0), (0, 0))
    kp = jnp.pad(k.reshape(b, nc, CHUNK, H, dh), pad)
    vp = jnp.pad(v.reshape(b, nc, CHUNK, H, dh), pad)
    kb = jnp.concatenate([kp[:, j:j + nc] for j in range(A_BACK + 1)], axis=2)
    vb = jnp.concatenate([vp[:, j:j + nc] for j in range(A_BACK + 1)], axis=2)
    i = jnp.arange(CHUNK)
    u = jnp.arange(A_BAND)
    rel = A_BACK * CHUNK + i[:, None] - u[None, :]
    bias = rel_bias[:, jnp.clip(rel, -A_REL_CLIP, A_REL_CLIP) + A_REL_CLIP]
    valid = (jnp.arange(nc)[:, None] - A_BACK + u[None, :] // CHUNK) >= 0
    sc = jnp.einsum('bnihd,bnuhd->bnhiu', qc, kb).astype(F32) * A_SCALE + bias[None, None].astype(F32)
    sc = jnp.where(valid[None, :, None, None, :], sc, NEG_INF)
    pr = jax.nn.softmax(sc, axis=-1)
    o = jnp.einsum('bnhiu,bnuhd->bnihd', pr.astype(vb.dtype), vb)
    return o.reshape(b, s, H * dh)


def band_attn_sample(q, k_new, v_new, k_cache, v_cache, rel_bias):
    b, ds, H, dh = q.shape
    nr = k_cache.shape[1]
    kk = jnp.concatenate([k_cache, k_new], axis=1)
    vv = jnp.concatenate([v_cache, v_new], axis=1)
    kpos = jnp.concatenate([jnp.arange(nr) - nr, jnp.arange(ds)])
    rel = jnp.arange(ds)[:, None] - kpos[None, :]
    bias = rel_bias[:, jnp.clip(rel, -A_REL_CLIP, A_REL_CLIP) + A_REL_CLIP]
    sc = jnp.einsum('bihd,bjhd->bhij', q, kk).astype(F32) * A_SCALE + bias[None].astype(F32)
    pr = jax.nn.softmax(sc, axis=-1)
    o = jnp.einsum('bhij,bjhd->bihd', pr.astype(vv.dtype), vv)
    return o.reshape(b, ds, H * dh)


def mla_prompt(q_lat, q_pe, ckv, kpe):
    b, s = q_lat.shape[:2]
    nqb = s // B_QBLOCK
    key_chunk = jnp.arange(s) // CHUNK

    def block(i):
        ql = lax.dynamic_slice_in_dim(q_lat, i * B_QBLOCK, B_QBLOCK, axis=1)
        qp = lax.dynamic_slice_in_dim(q_pe, i * B_QBLOCK, B_QBLOCK, axis=1)
        sc = jnp.einsum('bqhr,bkr->bhqk', ql, ckv) + jnp.einsum('bqhe,bke->bhqk', qp, kpe)
        q_chunk = (i * B_QBLOCK + jnp.arange(B_QBLOCK)) // CHUNK
        allowed = key_chunk[None, :] <= q_chunk[:, None]
        sc = jnp.where(allowed[None, None], sc.astype(F32) * B_SCALE, NEG_INF)
        pr = jax.nn.softmax(sc, axis=-1)
        return jnp.einsum('bhqk,bkr->bqhr', pr.astype(ckv.dtype), ckv)

    o = lax.map(block, jnp.arange(nqb))
    return jnp.moveaxis(o, 0, 1).reshape(b, s, B_HEADS, B_KV_LORA)


def mla_sample(q_lat, q_pe, ckv_new, kpe_new, ckv_cache, kpe_cache):
    ckv = jnp.concatenate([ckv_cache, ckv_new], axis=1)
    kpe = jnp.concatenate([kpe_cache, kpe_new], axis=1)
    sc = jnp.einsum('bqhr,bkr->bhqk', q_lat, ckv) + jnp.einsum('bqhe,bke->bhqk', q_pe, kpe)
    pr = jax.nn.softmax(sc.astype(F32) * B_SCALE, axis=-1)
    return jnp.einsum('bhqk,bkr->bqhr', pr.astype(ckv.dtype), ckv)


def causal_conv(u, prefix, w, bias):
    L = u.shape[1]
    up = jnp.concatenate([prefix.astype(u.dtype), u], axis=1)
    out = up[:, 0:L] * w[0]
    for k in range(1, C_CONV_W):
        out = out + up[:, k:k + L] * w[k]
    return out + bias, up[:, -(C_CONV_W - 1):]


def ssd(x, dt, a, bm, cm, h0):
    b, L, H, P = x.shape
    rep = H // bm.shape[2]
    ln = min(SSD_CHUNK, L)
    nc = L // ln
    xf = x.astype(F32).reshape(b, nc, ln, H, P)
    dtc = dt.astype(F32).reshape(b, nc, ln, H)
    bh = jnp.repeat(bm.astype(F32), rep, axis=2).reshape(b, nc, ln, H, -1)
    ch = jnp.repeat(cm.astype(F32), rep, axis=2).reshape(b, nc, ln, H, -1)
    cum = jnp.cumsum(dtc * a, axis=2)
    seg = cum[:, :, :, None, :] - cum[:, :, None, :, :]
    causal = jnp.tril(jnp.ones((ln, ln), dtype=bool))
    lmat = jnp.exp(jnp.where(causal[None, None, :, :, None], seg, NEG_INF))
    xdt = xf * dtc[..., None]
    cb = jnp.einsum('bclhn,bcshn->bclsh', ch, bh) * lmat
    y_diag = jnp.einsum('bclsh,bcshp->bclhp', cb, xdt)
    decay_end = jnp.exp(cum[:, :, -1:, :] - cum)
    chunk_states = jnp.einsum('bclhn,bclhp->bchpn', bh * decay_end[..., None], xdt)
    chunk_decay = jnp.exp(cum[:, :, -1, :])

    def step(h, inp):
        st, dec = inp
        return dec[:, :, None, None] * h + st, h

    h_final, h_start = lax.scan(step, h0.astype(F32),
                                (jnp.moveaxis(chunk_states, 1, 0), jnp.moveaxis(chunk_decay, 1, 0)))
    h_start = jnp.moveaxis(h_start, 0, 1)
    y_off = jnp.einsum('bclhn,bchpn->bclhp', ch * jnp.exp(cum)[..., None], h_start)
    return (y_diag + y_off).reshape(b, L, H, P), h_final


def ssm_mixer(z, xbc, dtr, conv_prefix, h0, p):
    b, s = z.shape[:2]
    xbc_c, conv_state = causal_conv(xbc, conv_prefix, p['c_conv_w'], p['c_conv_b'])
    xbc_c = jax.nn.silu(xbc_c)
    xs = xbc_c[..., :C_D_INNER].reshape(b, s, C_HEADS, C_HEAD_DIM)
    bm = xbc_c[..., C_D_INNER:C_D_INNER + C_GROUPS * C_D_STATE].reshape(b, s, C_GROUPS, C_D_STATE)
    cm = xbc_c[..., C_D_INNER + C_GROUPS * C_D_STATE:].reshape(b, s, C_GROUPS, C_D_STATE)
    dt = jax.nn.softplus(dtr.astype(F32) + p['c_dt_bias'].astype(F32))
    a = -jnp.exp(p['c_a_log'].astype(F32))
    y, h_final = ssd(xs, dt, a, bm, cm, h0)
    y = y + p['c_d'].astype(F32)[:, None] * xs.astype(F32)
    y = y.reshape(b, s, C_D_INNER) * jax.nn.silu(z.astype(F32))
    return rmsnorm(y, p['c_g_norm']), conv_state, h_final


def hier_moe(h, p):
    b, s, d = h.shape
    t = h.reshape(b * s, d)
    p_grp = jax.nn.softmax((t @ p['moe_w_rg'] + p['moe_b_rg']).astype(F32), axis=-1)
    p_top, g_idx = lax.top_k(p_grp, 1)
    le = (t @ p['moe_w_re'] + p['moe_b_re']).astype(F32).reshape(-1, E_GROUPS, E_PER_GROUP)
    le_sel = jnp.take_along_axis(le, g_idx[:, :, None], axis=1)[:, 0]
    w_top, e_idx = lax.top_k(jax.nn.softmax(le_sel, axis=-1), E_TOPK)
    w_top = w_top / jnp.sum(w_top, axis=-1, keepdims=True)
    eid = g_idx * E_PER_GROUP + e_idx
    gate = jnp.einsum('tk,tke->te', p_top * w_top, jax.nn.one_hot(eid, N_EXPERTS, dtype=F32))
    gu = jnp.einsum('td,edf->tef', t, p['moe_w_gu'])
    act = jax.nn.silu(gu[..., :E_HIDDEN]) * gu[..., E_HIDDEN:] * gate[..., None]
    out = jnp.einsum('tef,efd->td', act, p['moe_w_down'])
    return out.reshape(b, s, d).astype(h.dtype)


def layer(x, c, pos, cache, p):
    b, s, _ = x.shape
    mod = jax.nn.silu(c) @ p['w_ada'] + p['b_ada']
    sh1, sc1, g1, sh2, sc2, g2 = jnp.split(mod, 6, axis=-1)
    h = modulate(rmsnorm(x, p['g_mix']), sh1, sc1)
    u = h @ p['w_in']
    qa, ka, va, cq, ckv_raw, kpe_raw, z, xbc, dtr = jnp.split(u, SPLIT_POINTS, axis=-1)
    qa = qa.reshape(b, s, A_HEADS, A_HEAD_DIM)
    ka = ka.reshape(b, s, A_HEADS, A_HEAD_DIM)
    va = va.reshape(b, s, A_HEADS, A_HEAD_DIM)
    qb = jnp.einsum('bsr,rhe->bshe', rmsnorm(cq, p['b_g_q']), p['b_w_uq'])
    q_pe = rope(qb[..., B_NOPE:], pos)
    q_lat = jnp.einsum('bshd,rhd->bshr', qb[..., :B_NOPE], p['b_w_uk'])
    ckv = rmsnorm(ckv_raw, p['b_g_kv'])
    kpe = rope(kpe_raw[:, :, None, :], pos)[:, :, 0]
    if cache is None:
        ya = band_attn_prompt(qa, ka, va, p['a_rel_bias'])
        ob = mla_prompt(q_lat, q_pe, ckv, kpe)
        conv_prefix = jnp.zeros((b, C_CONV_W - 1, C_CONV_DIM), xbc.dtype)
        h0 = jnp.zeros((b, C_HEADS, C_HEAD_DIM, C_D_STATE), F32)
        keep = max(s - A_WINDOW, 0)
        new_k, new_v = ka[:, keep:], va[:, keep:]
    else:
        ya = band_attn_sample(qa, ka, va, cache['a_k'], cache['a_v'], p['a_rel_bias'])
        ob = mla_sample(q_lat, q_pe, ckv, kpe, cache['b_ckv'], cache['b_kpe'])
        conv_prefix = cache['c_conv']
        h0 = cache['c_ssm']
        new_k, new_v = ka, va
    yb = jnp.einsum('bshr,rhd->bshd', ob, p['b_w_uv']).reshape(b, s, B_WIDTH)
    yc, conv_state, ssm_state = ssm_mixer(z, xbc, dtr, conv_prefix, h0, p)
    o = jnp.concatenate([ya, yb.astype(ya.dtype), yc.astype(ya.dtype)], axis=-1) @ p['w_out']
    x = x + g1[:, None, :] * o
    h2 = modulate(rmsnorm(x, p['g_ffn']), sh2, sc2)
    x = x + g2[:, None, :] * hier_moe(h2, p)
    return x, (new_k, new_v, ckv, kpe, conv_state, ssm_state)


def final_norm(x, c, g, w, bvec):
    sh, sc = jnp.split(jax.nn.silu(c) @ w + bvec, 2, axis=-1)
    return modulate(rmsnorm(x, g), sh, sc)


def stack_states(st):
    return (jnp.stack([t[0] for t in st]), jnp.stack([t[1] for t in st]),
            jnp.stack([t[2] for t in st]), jnp.stack([t[3] for t in st]),
            jnp.stack([t[4] for t in st]), jnp.stack([t[5] for t in st]))


def setup_inputs(seed: int = 0) -> dict:
    key = jax.random.key(seed)
    ks = iter(jax.random.split(key, 48))

    def nrm(shape, scale):
        return jax.random.normal(next(ks), shape, F32) * scale

    def gain(shape):
        return 1.0 + 0.1 * jax.random.normal(next(ks), shape, F32)

    D = D_MODEL
    a_rows = min(A_WINDOW, PAST_LEN)
    dt0 = jnp.exp(jax.random.uniform(next(ks), (DEPTH, C_HEADS), F32, np.log(1e-3), np.log(1e-1)))
    inp = {}
    inp['x_prompt'] = nrm((BATCH, SEQ, D), 1.0)
    inp['x_sample'] = nrm((DEC_BATCH, DEC_SEQ, D), 1.0)
    inp['c_prompt'] = nrm((BATCH, D), 1.0)
    inp['c_sample'] = nrm((DEC_BATCH, D), 1.0)
    inp['cache_a_k'] = nrm((DEPTH, DEC_BATCH, a_rows, A_HEADS, A_HEAD_DIM), 1.0)
    inp['cache_a_v'] = nrm((DEPTH, DEC_BATCH, a_rows, A_HEADS, A_HEAD_DIM), 1.0)
    inp['cache_b_ckv'] = nrm((DEPTH, DEC_BATCH, PAST_LEN, B_KV_LORA), 1.0)
    inp['cache_b_kpe'] = nrm((DEPTH, DEC_BATCH, PAST_LEN, B_ROPE), 1.0)
    inp['state_c_conv'] = nrm((DEPTH, DEC_BATCH, C_CONV_W - 1, C_CONV_DIM), 1.0)
    inp['state_c_ssm'] = nrm((DEPTH, DEC_BATCH, C_HEADS, C_HEAD_DIM, C_D_STATE), 0.1)
    inp['w_ada'] = nrm((DEPTH, D, 6 * D), 0.5 * D ** -0.5)
    inp['b_ada'] = nrm((DEPTH, 6 * D), 0.02)
    inp['g_mix'] = gain((DEPTH, D))
    inp['w_in'] = nrm((DEPTH, D, D_IN), D ** -0.5)
    inp['a_rel_bias'] = nrm((DEPTH, A_HEADS, 2 * A_REL_CLIP + 1), 0.5)
    inp['b_g_q'] = gain((DEPTH, B_Q_LORA))
    inp['b_w_uq'] = nrm((DEPTH, B_Q_LORA, B_HEADS, B_NOPE + B_ROPE), B_Q_LORA ** -0.5)
    inp['b_g_kv'] = gain((DEPTH, B_KV_LORA))
    inp['b_w_uk'] = nrm((DEPTH, B_KV_LORA, B_HEADS, B_NOPE), B_KV_LORA ** -0.5)
    inp['b_w_uv'] = nrm((DEPTH, B_KV_LORA, B_HEADS, B_V), B_KV_LORA ** -0.5)
    inp['c_conv_w'] = nrm((DEPTH, C_CONV_W, C_CONV_DIM), C_CONV_W ** -0.5)
    inp['c_conv_b'] = nrm((DEPTH, C_CONV_DIM), 0.02)
    inp['c_dt_bias'] = dt0 + jnp.log(-jnp.expm1(-dt0))
    inp['c_a_log'] = jnp.log(jax.random.uniform(next(ks), (DEPTH, C_HEADS), F32, 1.0, 16.0))
    inp['c_d'] = gain((DEPTH, C_HEADS))
    inp['c_g_norm'] = gain((DEPTH, C_D_INNER))
    inp['w_out'] = nrm((DEPTH, D_MIX, D), D_MIX ** -0.5)
    inp['g_ffn'] = gain((DEPTH, D))
    inp['moe_w_rg'] = nrm((DEPTH, D, E_GROUPS), D ** -0.5)
    inp['moe_b_rg'] = nrm((DEPTH, E_GROUPS), 0.01)
    inp['moe_w_re'] = nrm((DEPTH, D, N_EXPERTS), D ** -0.5)
    inp['moe_b_re'] = nrm((DEPTH, N_EXPERTS), 0.01)
    inp['moe_w_gu'] = nrm((DEPTH, N_EXPERTS, D, 2 * E_HIDDEN), D ** -0.5)
    inp['moe_w_down'] = nrm((DEPTH, N_EXPERTS, E_HIDDEN, D), E_HIDDEN ** -0.5)
    inp['g_final'] = gain((D,))
    inp['w_ada_f'] = nrm((D, 2 * D), 0.5 * D ** -0.5)
    inp['b_ada_f'] = nrm((2 * D,), 0.02)
    return inp


def reference(x_prompt, x_sample, c_prompt, c_sample, cache_a_k, cache_a_v, cache_b_ckv, cache_b_kpe,
              state_c_conv, state_c_ssm, w_ada, b_ada, g_mix, w_in, a_rel_bias, b_g_q, b_w_uq, b_g_kv,
              b_w_uk, b_w_uv, c_conv_w, c_conv_b, c_dt_bias, c_a_log, c_d, c_g_norm, w_out, g_ffn,
              moe_w_rg, moe_b_rg, moe_w_re, moe_b_re, moe_w_gu, moe_w_down, g_final, w_ada_f, b_ada_f):
    past = cache_b_ckv.shape[2]
    pos_p = jnp.arange(x_prompt.shape[1])
    pos_s = past + jnp.arange(x_sample.shape[1])
    xp, xs = x_prompt, x_sample
    st_p, st_s = [], []
    for l in range(DEPTH):
        p = {'w_ada': w_ada[l], 'b_ada': b_ada[l], 'g_mix': g_mix[l], 'w_in': w_in[l],
             'a_rel_bias': a_rel_bias[l], 'b_g_q': b_g_q[l], 'b_w_uq': b_w_uq[l], 'b_g_kv': b_g_kv[l],
             'b_w_uk': b_w_uk[l], 'b_w_uv': b_w_uv[l], 'c_conv_w': c_conv_w[l], 'c_conv_b': c_conv_b[l],
             'c_dt_bias': c_dt_bias[l], 'c_a_log': c_a_log[l], 'c_d': c_d[l], 'c_g_norm': c_g_norm[l],
             'w_out': w_out[l], 'g_ffn': g_ffn[l], 'moe_w_rg': moe_w_rg[l], 'moe_b_rg': moe_b_rg[l],
             'moe_w_re': moe_w_re[l], 'moe_b_re': moe_b_re[l], 'moe_w_gu': moe_w_gu[l],
             'moe_w_down': moe_w_down[l]}
        cache = {'a_k': cache_a_k[l], 'a_v': cache_a_v[l], 'b_ckv': cache_b_ckv[l], 'b_kpe': cache_b_kpe[l],
                 'c_conv': state_c_conv[l], 'c_ssm': state_c_ssm[l]}
        xp, sp = layer(xp, c_prompt, pos_p, None, p)
        xs, ss = layer(xs, c_sample, pos_s, cache, p)
        st_p.append(sp)
        st_s.append(ss)
    y_prompt = final_norm(xp, c_prompt, g_final, w_ada_f, b_ada_f)
    y_sample = final_norm(xs, c_sample, g_final, w_ada_f, b_ada_f)
    ak_p, av_p, ckv_p, kpe_p, conv_p, ssm_p = stack_states(st_p)
    ak_s, av_s, ckv_s, kpe_s, conv_s, ssm_s = stack_states(st_s)
    return (y_prompt, y_sample, ak_p, av_p, ckv_p, kpe_p, conv_p, ssm_p,
            ak_s, av_s, ckv_s, kpe_s, conv_s, ssm_s)
```

```python
import functools
import math

import numpy as np
import jax
import jax.numpy as jnp
from jax import lax
from jax.experimental import pallas as pl
from jax.experimental.pallas import tpu as pltpu

F32 = jnp.float32
BF16 = jnp.bfloat16

D_MODEL = 1024
CHUNK = 64
EPS = 1e-6
NEG_INF = -1e30

A_HEADS = 4
A_HEAD_DIM = 64
A_WIDTH = A_HEADS * A_HEAD_DIM
A_BACK = 8
A_BAND = (A_BACK + 1) * CHUNK
A_WINDOW = A_BACK * CHUNK
A_REL_CLIP = 128
A_SCALE = A_HEAD_DIM ** -0.5

B_HEADS = 4
B_Q_LORA = 192
B_KV_LORA = 128
B_NOPE = 64
B_ROPE = 32
B_V = 64
B_WIDTH = B_HEADS * B_V
B_SCALE = (B_NOPE + B_ROPE) ** -0.5
B_QK = B_KV_LORA + B_ROPE
ROPE_THETA = 10000.0

C_D_INNER = 512
C_HEAD_DIM = 64
C_HEADS = C_D_INNER // C_HEAD_DIM
C_GROUPS = 2
C_D_STATE = 64
C_CONV_W = 4
C_CONV_DIM = C_D_INNER + 2 * C_GROUPS * C_D_STATE
C_GN = C_GROUPS * C_D_STATE
C_HPG = C_HEADS // C_GROUPS
SSD_CHUNK = 64

E_GROUPS = 4
E_PER_GROUP = 4
N_EXPERTS = E_GROUPS * E_PER_GROUP
E_HIDDEN = 256
ROUTER_LANES = 128
ROUTER_E_OFF = 16

LANES = 128
VMEM_LIMIT = 56 * 1024 * 1024

B_QSCALE = B_SCALE * math.log2(math.e)

_W1_QKV = 0
_W1_CQ = 3 * A_WIDTH
_W1_CKV = _W1_CQ + 256
_W1_KPE = _W1_CKV + B_KV_LORA
_W1_Z = _W1_KPE + LANES
_W1_XBC = _W1_Z + C_D_INNER
_W1_N = _W1_XBC + C_CONV_DIM


def _cparams(*sem):
    return pltpu.CompilerParams(dimension_semantics=sem, vmem_limit_bytes=VMEM_LIMIT)


def _silu(v):
    return v * jax.nn.sigmoid(v)


def _split3(v):
    hi = v.astype(BF16).astype(F32)
    r1 = v - hi
    mid = r1.astype(BF16).astype(F32)
    lo = (r1 - mid).astype(BF16).astype(F32)
    return hi, mid, lo


def _ada_kernel(c_ref, w_ref, b_ref, o_ref):
    s = _silu(c_ref[...]).astype(BF16)
    o_ref[0] = jnp.dot(s, w_ref[0].astype(BF16), preferred_element_type=F32) + b_ref[0]


def _ada(c_all, w, b):
    nl, d, n = w.shape
    bp = c_all.shape[0]
    tn = 1024
    return pl.pallas_call(
        _ada_kernel,
        grid=(nl, n // tn),
        in_specs=[pl.BlockSpec((bp, d), lambda l, j: (0, 0)),
                  pl.BlockSpec((1, d, tn), lambda l, j: (l, 0, j)),
                  pl.BlockSpec((1, 1, tn), lambda l, j: (l, 0, j))],
        out_specs=pl.BlockSpec((1, bp, tn), lambda l, j: (l, 0, j)),
        out_shape=jax.ShapeDtypeStruct((nl, bp, n), F32),
        compiler_params=_cparams("arbitrary", "arbitrary"),
        name="ada",
    )(c_all, w, b)


def _rows(ref_or_val, nb, r):
    v = ref_or_val[...]
    return jnp.broadcast_to(v, (nb, r, v.shape[-1])).reshape(nb * r, v.shape[-1])


def _inproj_kernel(x_ref, sh_ref, sc_ref, g_ref, w1_ref, gq_ref, wuq_ref, bd_ref, gkv_ref, cos_ref, sin_ref,
                   qkv_ref, knew_ref, vnew_ref, qmla_ref, kmla_ref, ckv_ref, kpe_ref, z_ref, xbc_ref, dtr_ref,
                   *, nb, r, keep_tile):
    R = nb * r
    D = x_ref.shape[-1]

    def out3(v):
        return v.reshape(nb, r, v.shape[-1])

    x = x_ref[...].reshape(R, D)
    h = x * lax.rsqrt(jnp.mean(x * x, axis=-1, keepdims=True) + EPS) * g_ref[...]
    h = h * (1.0 + _rows(sc_ref, nb, r)) + _rows(sh_ref, nb, r)
    u = jnp.dot(h.astype(BF16), w1_ref[...], preferred_element_type=F32)

    qkv_ref[:, :, 0:A_WIDTH] = out3((u[:, 0:A_WIDTH] * A_SCALE).astype(BF16))
    qkv_ref[:, :, A_WIDTH:3 * A_WIDTH] = out3(u[:, A_WIDTH:3 * A_WIDTH].astype(BF16))

    @pl.when(pl.program_id(1) == keep_tile)
    def _():
        knew_ref[...] = out3(u[:, A_WIDTH:2 * A_WIDTH])
        vnew_ref[...] = out3(u[:, 2 * A_WIDTH:3 * A_WIDTH])

    blk = u[:, _W1_CQ:_W1_CQ + 256]
    lane = lax.broadcasted_iota(jnp.int32, (1, 256), 1)
    cq = jnp.where(lane < B_Q_LORA, blk, 0.0)
    cqn = cq * lax.rsqrt(jnp.sum(cq * cq, axis=-1, keepdims=True) * (1.0 / B_Q_LORA) + EPS) * gq_ref[...]
    qb = jnp.dot(cqn.astype(BF16), wuq_ref[...], preferred_element_type=F32)
    nope_w = B_HEADS * B_NOPE
    rope_w = B_HEADS * B_ROPE
    qlat = jnp.dot(qb[:, 0:nope_w].astype(BF16), bd_ref[...], preferred_element_type=F32) * B_QSCALE
    cosr = _rows(cos_ref, nb, r)
    sinr = _rows(sin_ref, nb, r)
    qpe = (qb[:, nope_w:nope_w + rope_w] * cosr + qb[:, nope_w + rope_w:nope_w + 2 * rope_w] * sinr) * B_QSCALE
    for hd in range(B_HEADS):
        qmla_ref[:, hd, :, 0:B_KV_LORA] = out3(qlat[:, hd * B_KV_LORA:(hd + 1) * B_KV_LORA].astype(BF16))
        qmla_ref[:, hd, :, B_KV_LORA:B_QK] = out3(qpe[:, hd * B_ROPE:(hd + 1) * B_ROPE].astype(BF16))
    dtr_ref[...] = out3(blk[:, B_Q_LORA:B_Q_LORA + C_HEADS])

    cr = u[:, _W1_CKV:_W1_CKV + B_KV_LORA]
    ckv = cr * lax.rsqrt(jnp.mean(cr * cr, axis=-1, keepdims=True) + EPS) * gkv_ref[...]
    ckv_ref[...] = out3(ckv)
    kmla_ref[:, :, 0:B_KV_LORA] = out3(ckv.astype(BF16))
    kb = u[:, _W1_KPE:_W1_KPE + LANES]
    rot = kb * cosr + pltpu.roll(kb, LANES - B_ROPE, axis=1) * sinr
    kpe_ref[...] = out3(rot[:, 0:B_ROPE])
    kmla_ref[:, :, B_KV_LORA:B_QK] = out3(rot[:, 0:B_ROPE].astype(BF16))

    z_ref[...] = out3(u[:, _W1_Z:_W1_Z + C_D_INNER].astype(BF16))
    xbc_ref[...] = out3(u[:, _W1_XBC:_W1_XBC + C_CONV_DIM])


def _inproj(x, mod, g_mix, w1, gq, wuq, bd, gkv, cos_t, sin_t, *, nb, r):
    B, S, D = x.shape
    nbt, nst = B // nb, S // r
    keep = min(S, A_WINDOW)
    assert keep == r and B % nb == 0 and S % r == 0
    grid = (nbt, nst)
    row = lambda c: pl.BlockSpec((nb, r, c), lambda b, i: (b, i, 0))
    full = lambda a: pl.BlockSpec(a.shape, lambda b, i: (0,) * a.ndim)
    in_specs = [row(D),
                pl.BlockSpec((nb, 1, D), lambda b, i: (b, 0, 0)),
                pl.BlockSpec((nb, 1, D), lambda b, i: (b, 0, 1)),
                full(g_mix), full(w1), full(gq), full(wuq), full(bd), full(gkv),
                pl.BlockSpec((r, LANES), lambda b, i: (i, 0)),
                pl.BlockSpec((r, LANES), lambda b, i: (i, 0))]
    out_shape = [jax.ShapeDtypeStruct((B, S, 3 * A_WIDTH), BF16),
                 jax.ShapeDtypeStruct((B, keep, A_WIDTH), F32),
                 jax.ShapeDtypeStruct((B, keep, A_WIDTH), F32),
                 jax.ShapeDtypeStruct((B, B_HEADS, S, B_QK), BF16),
                 jax.ShapeDtypeStruct((B, S, B_QK), BF16),
                 jax.ShapeDtypeStruct((B, S, B_KV_LORA), F32),
                 jax.ShapeDtypeStruct((B, S, B_ROPE), F32),
                 jax.ShapeDtypeStruct((B, S, C_D_INNER), BF16),
                 jax.ShapeDtypeStruct((B, S, C_CONV_DIM), F32),
                 jax.ShapeDtypeStruct((B, S, C_HEADS), F32)]
    out_specs = [row(3 * A_WIDTH),
                 pl.BlockSpec((nb, keep, A_WIDTH), lambda b, i: (b, 0, 0)),
                 pl.BlockSpec((nb, keep, A_WIDTH), lambda b, i: (b, 0, 0)),
                 pl.BlockSpec((nb, B_HEADS, r, B_QK), lambda b, i: (b, 0, i, 0)),
                 row(B_QK), row(B_KV_LORA), row(B_ROPE), row(C_D_INNER), row(C_CONV_DIM), row(C_HEADS)]
    return pl.pallas_call(
        functools.partial(_inproj_kernel, nb=nb, r=r, keep_tile=nst - 1),
        grid=grid, in_specs=in_specs, out_specs=out_specs, out_shape=out_shape,
        compiler_params=_cparams("arbitrary", "arbitrary"),
        name="inproj",
    )(x, mod, mod, g_mix, w1, gq, wuq, bd, gkv, cos_t, sin_t)


def _band_core(q, k, v, bias, valid, hmask):
    nq = q.shape[0]
    qs = jnp.where(hmask, jnp.concatenate([q] * A_HEADS, axis=0), jnp.zeros((), BF16))
    s = lax.dot_general(qs, k, (((1,), (1,)), ((), ())), preferred_element_type=F32) + bias
    if valid is not None:
        s = jnp.where(valid, s, NEG_INF)
    m = jnp.max(s, axis=-1, keepdims=True)
    e = jnp.exp(s - m)
    den = jnp.sum(e, axis=-1, keepdims=True)
    o = jnp.dot(e.astype(BF16), v, preferred_element_type=F32)
    o = jnp.where(hmask, o / den, 0.0)
    y = o[0:nq]
    for hd in range(1, A_HEADS):
        y = y + o[hd * nq:(hd + 1) * nq]
    return y


def _head_mask(nq):
    row = lax.broadcasted_iota(jnp.int32, (A_HEADS * nq, A_WIDTH), 0)
    col = lax.broadcasted_iota(jnp.int32, (A_HEADS * nq, A_WIDTH), 1)
    return (row // nq) == (col // A_HEAD_DIM)


def _band_prompt_kernel(cur_ref, prev_ref, bias_ref, o_ref, kwin, vwin, *, tq):
    i = pl.program_id(1)
    nch = tq // CHUNK
    kwin[0:A_WINDOW] = prev_ref[0, tq - A_WINDOW:tq, A_WIDTH:2 * A_WIDTH]
    kwin[A_WINDOW:A_WINDOW + tq] = cur_ref[0, :, A_WIDTH:2 * A_WIDTH]
    vwin[0:A_WINDOW] = prev_ref[0, tq - A_WINDOW:tq, 2 * A_WIDTH:3 * A_WIDTH]
    vwin[A_WINDOW:A_WINDOW + tq] = cur_ref[0, :, 2 * A_WIDTH:3 * A_WIDTH]
    hmask = _head_mask(CHUNK)
    ucol = lax.broadcasted_iota(jnp.int32, (1, A_BAND), 1)
    for c in range(nch):
        q = cur_ref[0, c * CHUNK:(c + 1) * CHUNK, 0:A_WIDTH]
        valid = (ucol // CHUNK + (i * nch + c - A_BACK)) >= 0
        y = _band_core(q, kwin[c * CHUNK:c * CHUNK + A_BAND], vwin[c * CHUNK:c * CHUNK + A_BAND],
                       bias_ref[...], valid, hmask)
        o_ref[0, c * CHUNK:(c + 1) * CHUNK, :] = y.astype(BF16)


def _band_prompt(qkv, bias_stack):
    B, S, _ = qkv.shape
    tq = A_WINDOW
    assert S % tq == 0
    return pl.pallas_call(
        functools.partial(_band_prompt_kernel, tq=tq),
        grid=(B, S // tq),
        in_specs=[pl.BlockSpec((1, tq, 3 * A_WIDTH), lambda b, i: (b, i, 0)),
                  pl.BlockSpec((1, tq, 3 * A_WIDTH), lambda b, i: (b, jnp.maximum(i - 1, 0), 0)),
                  pl.BlockSpec(bias_stack.shape, lambda b, i: (0, 0))],
        out_specs=pl.BlockSpec((1, tq, A_WIDTH), lambda b, i: (b, i, 0)),
        out_shape=jax.ShapeDtypeStruct((B, S, A_WIDTH), BF16),
        scratch_shapes=[pltpu.VMEM((A_WINDOW + tq, A_WIDTH), BF16), pltpu.VMEM((A_WINDOW + tq, A_WIDTH), BF16)],
        compiler_params=_cparams("arbitrary", "arbitrary"),
        name="band_prompt",
    )(qkv, qkv, bias_stack)


def _band_sample_kernel(qkv_ref, kc_ref, vc_ref, bias_ref, o_ref, kbuf, vbuf, *, nbb, nr, ds):
    hmask = _head_mask(ds)
    for b in range(nbb):
        kbuf[0:nr] = kc_ref[0, b].astype(BF16)
        kbuf[nr:nr + ds] = qkv_ref[b, :, A_WIDTH:2 * A_WIDTH]
        vbuf[0:nr] = vc_ref[0, b].astype(BF16)
        vbuf[nr:nr + ds] = qkv_ref[b, :, 2 * A_WIDTH:3 * A_WIDTH]
        y = _band_core(qkv_ref[b, :, 0:A_WIDTH], kbuf[...], vbuf[...], bias_ref[...], None, hmask)
        o_ref[b] = y.astype(BF16)


def _band_sample(qkv, cache_k, cache_v, layer, bias_stack):
    B, ds, _ = qkv.shape
    nr = cache_k.shape[2]
    nbb = 4
    assert B % nbb == 0
    return pl.pallas_call(
        functools.partial(_band_sample_kernel, nbb=nbb, nr=nr, ds=ds),
        grid=(B // nbb,),
        in_specs=[pl.BlockSpec((nbb, ds, 3 * A_WIDTH), lambda b: (b, 0, 0)),
                  pl.BlockSpec((1, nbb, nr, A_WIDTH), lambda b: (layer, b, 0, 0)),
                  pl.BlockSpec((1, nbb, nr, A_WIDTH), lambda b: (layer, b, 0, 0)),
                  pl.BlockSpec(bias_stack.shape, lambda b: (0, 0))],
        out_specs=pl.BlockSpec((nbb, ds, A_WIDTH), lambda b: (b, 0, 0)),
        out_shape=jax.ShapeDtypeStruct((B, ds, A_WIDTH), BF16),
        scratch_shapes=[pltpu.VMEM((nr + ds, A_WIDTH), BF16), pltpu.VMEM((nr + ds, A_WIDTH), BF16)],
        compiler_params=_cparams("arbitrary"),
        name="band_sample",
    )(qkv, cache_k, cache_v, bias_stack)


def _mla_finish(acc, den, wuv_ref, bq):
    o = acc / den
    o_all = jnp.concatenate([o[hd * bq:(hd + 1) * bq] for hd in range(B_HEADS)], axis=1)
    return jnp.dot(o_all.astype(BF16), wuv_ref[...], preferred_element_type=F32)


def _mla_prompt_kernel(qi_ref, kj_ref, q_ref, k_ref, wuv_ref, o_ref, m_sc, l_sc, acc_sc, *, bq, bk):
    p = pl.program_id(1)
    i = qi_ref[p]
    j = kj_ref[p]
    M = B_HEADS * bq

    @pl.when(j == 0)
    def _():
        m_sc[...] = jnp.full(m_sc.shape, NEG_INF, F32)
        l_sc[...] = jnp.zeros(l_sc.shape, F32)
        acc_sc[...] = jnp.zeros(acc_sc.shape, F32)

    def step(masked):
        q = q_ref[0].reshape(M, B_QK)
        k = k_ref[0]
        s = lax.dot_general(q, k, (((1,), (1,)), ((), ())), preferred_element_type=F32)
        if masked:
            row = lax.broadcasted_iota(jnp.int32, (M, bk), 0)
            col = lax.broadcasted_iota(jnp.int32, (M, bk), 1)
            s = jnp.where((col // CHUNK) <= ((row % bq) // CHUNK), s, NEG_INF)
        m_prev = m_sc[...]
        m_new = jnp.maximum(m_prev, jnp.max(s, axis=1, keepdims=True))
        alpha = jnp.exp2(m_prev - m_new)
        pr = jnp.exp2(s - jnp.tile(m_new, (1, bk // LANES)))
        l_sc[...] = alpha * l_sc[...] + jnp.sum(pr, axis=1, keepdims=True)
        acc_sc[...] = acc_sc[...] * alpha + jnp.dot(pr.astype(BF16), k[:, 0:B_KV_LORA],
                                                    preferred_element_type=F32)
        m_sc[...] = m_new

    @pl.when(j < i)
    def _():
        step(False)

    @pl.when(j == i)
    def _():
        step(True)
        o_ref[0] = _mla_finish(acc_sc[...], l_sc[...], wuv_ref, bq).astype(BF16)


def _mla_prompt(qmla, kmla, wuv_bd):
    B, _, S, _ = qmla.shape
    bq = bk = 512
    assert S % bq == 0 and bq % CHUNK == 0
    nq = S // bq
    qi = np.concatenate([np.full(i + 1, i, np.int32) for i in range(nq)])
    kj = np.concatenate([np.arange(i + 1, dtype=np.int32) for i in range(nq)])
    grid_spec = pltpu.PrefetchScalarGridSpec(
        num_scalar_prefetch=2,
        grid=(B, len(qi)),
        in_specs=[pl.BlockSpec((1, B_HEADS, bq, B_QK), lambda b, p, qi, kj: (b, 0, qi[p], 0)),
                  pl.BlockSpec((1, bk, B_QK), lambda b, p, qi, kj: (b, kj[p], 0)),
                  pl.BlockSpec(wuv_bd.shape, lambda b, p, qi, kj: (0, 0))],
        out_specs=pl.BlockSpec((1, bq, B_WIDTH), lambda b, p, qi, kj: (b, qi[p], 0)),
        scratch_shapes=[pltpu.VMEM((B_HEADS * bq, LANES), F32), pltpu.VMEM((B_HEADS * bq, LANES), F32),
                        pltpu.VMEM((B_HEADS * bq, B_KV_LORA), F32)])
    return pl.pallas_call(
        functools.partial(_mla_prompt_kernel, bq=bq, bk=bk),
        grid_spec=grid_spec,
        out_shape=jax.ShapeDtypeStruct((B, S, B_WIDTH), BF16),
        compiler_params=_cparams("arbitrary", "arbitrary"),
        name="mla_prompt",
    )(jnp.asarray(qi), jnp.asarray(kj), qmla, kmla, wuv_bd)


def _mla_sample_kernel(q_ref, kn_ref, ckv_ref, kpe_ref, wuv_ref, o_ref, kbuf, *, past, ds):
    M = B_HEADS * ds
    kbuf[0:past, 0:B_KV_LORA] = ckv_ref[0, 0].astype(BF16)
    kbuf[0:past, B_KV_LORA:B_QK] = kpe_ref[0, 0].astype(BF16)
    kbuf[past:past + ds, :] = kn_ref[0]
    q = q_ref[0].reshape(M, B_QK)
    k = kbuf[...]
    s = lax.dot_general(q, k, (((1,), (1,)), ((), ())), preferred_element_type=F32)
    m = jnp.max(s, axis=1, keepdims=True)
    pr = jnp.exp2(s - m)
    den = jnp.sum(pr, axis=1, keepdims=True)
    acc = jnp.dot(pr.astype(BF16), k[:, 0:B_KV_LORA], preferred_element_type=F32)
    o_ref[0] = _mla_finish(acc, den, wuv_ref, ds).astype(BF16)


def _mla_sample(qmla, kmla, cache_ckv, cache_kpe, layer, wuv_bd):
    B, _, ds, _ = qmla.shape
    past = cache_ckv.shape[2]
    return pl.pallas_call(
        functools.partial(_mla_sample_kernel, past=past, ds=ds),
        grid=(B,),
        in_specs=[pl.BlockSpec((1, B_HEADS, ds, B_QK), lambda b: (b, 0, 0, 0)),
                  pl.BlockSpec((1, ds, B_QK), lambda b: (b, 0, 0)),
                  pl.BlockSpec((1, 1, past, B_KV_LORA), lambda b: (layer, b, 0, 0)),
                  pl.BlockSpec((1, 1, past, B_ROPE), lambda b: (layer, b, 0, 0)),
                  pl.BlockSpec(wuv_bd.shape, lambda b: (0, 0))],
        out_specs=pl.BlockSpec((1, ds, B_WIDTH), lambda b: (b, 0, 0)),
        out_shape=jax.ShapeDtypeStruct((B, ds, B_WIDTH), BF16),
        scratch_shapes=[pltpu.VMEM((past + ds, B_QK), BF16)],
        compiler_params=_cparams("arbitrary"),
        name="mla_sample",
    )(qmla, kmla, cache_ckv, cache_kpe, wuv_bd)


def _ssd_kernel(xbc_ref, z_ref, dtr_ref, pre_ref, h0_ref, cw_ref, cb_ref, dtb_ref, alog_ref, dsk_ref, gn_ref,
                es_ref, ep_ref, y_ref, hout_ref,
                cbuf, xs_sc, b_sc, c_sc, ces_sc, cep_sc, dep_sc, st_sc, *, lt, lc, nt):
    t = pl.program_id(1)
    nch = lt // lc
    HS = C_HEADS * lc

    @pl.when(t == 0)
    def _():
        cbuf[8 - (C_CONV_W - 1):8] = pre_ref[0]
        st_sc[...] = h0_ref[0]

    cbuf[8:8 + lt] = xbc_ref[0]
    acc = jnp.broadcast_to(cb_ref[...], (lt, C_CONV_DIM))
    for kk in range(C_CONV_W):
        off = 8 - (C_CONV_W - 1) + kk
        acc = acc + cbuf[off:off + lt] * cw_ref[kk:kk + 1, :]
    tail = cbuf[8 + lt - (C_CONV_W - 1):8 + lt]
    cbuf[8 - (C_CONV_W - 1):8] = tail
    xc = _silu(acc)
    xs_sc[...] = xc[:, 0:C_D_INNER]
    b_sc[...] = xc[:, C_D_INNER:C_D_INNER + C_GN]
    c_sc[...] = xc[:, C_D_INNER + C_GN:C_D_INNER + 2 * C_GN]

    dt = jax.nn.softplus(dtr_ref[0] + dtb_ref[...])
    da = dt * (-jnp.exp(alog_ref[...]))
    ri = lax.broadcasted_iota(jnp.int32, (lt, lt), 0)
    ci = lax.broadcasted_iota(jnp.int32, (lt, lt), 1)
    tri = jnp.where((ci <= ri) & ((ci // lc) == (ri // lc)), 1.0, 0.0).astype(F32)
    cum = jnp.zeros((lt, C_HEADS), F32)
    for part in _split3(da):
        cum = cum + jnp.dot(tri, part, preferred_element_type=F32)
    ces = jnp.zeros((lt, HS), F32)
    cep = jnp.zeros((lt, C_D_INNER), F32)
    dep = jnp.zeros((lt, C_D_INNER), F32)
    for part in _split3(cum):
        ces = ces + jnp.dot(part, es_ref[...], preferred_element_type=F32)
        cep = cep + jnp.dot(part, ep_ref[...], preferred_element_type=F32)
    for part in _split3(dt):
        dep = dep + jnp.dot(part, ep_ref[...], preferred_element_type=F32)
    ces_sc[...] = ces
    cep_sc[...] = cep
    dep_sc[...] = dep

    r_s = lax.broadcasted_iota(jnp.int32, (lc, HS), 0)
    l_s = lax.broadcasted_iota(jnp.int32, (lc, HS), 1)
    eye_t = (l_s % lc) == r_s
    causal_t = (l_s % lc) <= r_s
    r_w = lax.broadcasted_iota(jnp.int32, (HS, C_GN), 0)
    l_w = lax.broadcasted_iota(jnp.int32, (HS, C_GN), 1)
    gmask = (r_w // (C_HPG * lc)) == (l_w // C_D_STATE)
    gw = C_HPG * C_HEAD_DIM
    r_b = lax.broadcasted_iota(jnp.int32, (C_HPG * lc, gw), 0)
    l_b = lax.broadcasted_iota(jnp.int32, (C_HPG * lc, gw), 1)
    bmask = (r_b // lc) == (l_b // C_HEAD_DIM)
    r_g = lax.broadcasted_iota(jnp.int32, (C_GN, C_D_INNER), 0)
    l_g = lax.broadcasted_iota(jnp.int32, (C_GN, C_D_INNER), 1)
    smask = (r_g // C_D_STATE) == (l_g // gw)

    def chunk(c, carry):
        r0 = pl.multiple_of(c * lc, lc)
        rows = pl.ds(r0, lc)
        ce_s = ces_sc[rows, :]
        ce_p = cep_sc[rows, :]
        de_p = dep_sc[rows, :]
        xs = xs_sc[rows, :]
        bm = b_sc[rows, :]
        cm = c_sc[rows, :]
        rflat = jnp.sum(jnp.where(eye_t, ce_s, 0.0), axis=0, keepdims=True)
        lcat = jnp.exp(jnp.where(causal_t, ce_s - rflat, NEG_INF))
        w_nt = jnp.where(gmask, jnp.concatenate([bm] * C_HEADS, axis=0), 0.0).astype(BF16)
        cbcat = lax.dot_general(cm.astype(BF16), w_nt, (((1,), (1,)), ((), ())), preferred_element_type=F32)
        mcat = (cbcat * lcat).astype(BF16)
        xdt = xs * de_p
        xdt_b = xdt.astype(BF16)
        ydiag = []
        for g in range(C_GROUPS):
            bdx = jnp.where(bmask, jnp.concatenate([xdt_b[:, g * gw:(g + 1) * gw]] * C_HPG, axis=0),
                            jnp.zeros((), BF16))
            ydiag.append(jnp.dot(mcat[:, g * C_HPG * lc:(g + 1) * C_HPG * lc], bdx, preferred_element_type=F32))
        y = jnp.concatenate(ydiag, axis=1)
        st = st_sc[...]
        y = y + jnp.dot(cm.astype(BF16), st.astype(BF16), preferred_element_type=F32) * jnp.exp(ce_p)
        y = y + dsk_ref[...] * xs
        last = ce_p[lc - 1:lc, :]
        xw = (xdt * jnp.exp(last - ce_p)).astype(BF16)
        upd = lax.dot_general(bm.astype(BF16), xw, (((0,), (0,)), ((), ())), preferred_element_type=F32)
        st_sc[...] = jnp.exp(last) * st + jnp.where(smask, upd, 0.0)
        y = y * _silu(z_ref[0, rows, :].astype(F32))
        y = y * lax.rsqrt(jnp.mean(y * y, axis=-1, keepdims=True) + EPS) * gn_ref[...]
        y_ref[0, rows, :] = y.astype(BF16)
        return carry

    lax.fori_loop(0, nch, chunk, 0)

    @pl.when(t == nt - 1)
    def _():
        hout_ref[0] = st_sc[...]


def _ssd(xbc, z, dtr, prefix, h0t, cw, cb, dtb, alog, dsk, gn, *, lt, lc):
    B, S, _ = xbc.shape
    nt = S // lt
    assert S % lt == 0 and lt % lc == 0
    hs = C_HEADS * lc
    es = (np.arange(hs)[None, :] // lc == np.arange(C_HEADS)[:, None]).astype(np.float32)
    ep = (np.arange(C_D_INNER)[None, :] // C_HEAD_DIM == np.arange(C_HEADS)[:, None]).astype(np.float32)
    row = lambda c: pl.BlockSpec((1, lt, c), lambda b, t: (b, t, 0))
    per_b = lambda a: pl.BlockSpec((1,) + a.shape[1:], lambda b, t: (b,) + (0,) * (a.ndim - 1))
    full = lambda a: pl.BlockSpec(a.shape, lambda b, t: (0,) * a.ndim)
    es, ep = jnp.asarray(es), jnp.asarray(ep)
    return pl.pallas_call(
        functools.partial(_ssd_kernel, lt=lt, lc=lc, nt=nt),
        grid=(B, nt),
        in_specs=[row(C_CONV_DIM), row(C_D_INNER), row(C_HEADS), per_b(prefix), per_b(h0t),
                  full(cw), full(cb), full(dtb), full(alog), full(dsk), full(gn), full(es), full(ep)],
        out_specs=[row(C_D_INNER), pl.BlockSpec((1, C_GN, C_D_INNER), lambda b, t: (b, 0, 0))],
        out_shape=[jax.ShapeDtypeStruct((B, S, C_D_INNER), BF16),
                   jax.ShapeDtypeStruct((B, C_GN, C_D_INNER), F32)],
        scratch_shapes=[pltpu.VMEM((lt + 8, C_CONV_DIM), F32),
                        pltpu.VMEM((lt, C_D_INNER), F32), pltpu.VMEM((lt, C_GN), F32), pltpu.VMEM((lt, C_GN), F32),
                        pltpu.VMEM((lt, hs), F32), pltpu.VMEM((lt, C_D_INNER), F32),
                        pltpu.VMEM((lt, C_D_INNER), F32), pltpu.VMEM((C_GN, C_D_INNER), F32)],
        compiler_params=_cparams("arbitrary", "arbitrary"),
        name="ssd",
    )(xbc, z, dtr, prefix, h0t, cw, cb, dtb, alog, dsk, gn, es, ep)


def _state_to_kernel(h):
    B = h.shape[0]
    hg = h.reshape(B, C_GROUPS, C_HPG, C_HEAD_DIM, C_D_STATE)
    blk = jnp.transpose(hg, (0, 1, 4, 2, 3)).reshape(B, C_GROUPS, C_D_STATE, C_HPG * C_HEAD_DIM)
    eye = jnp.eye(C_GROUPS, dtype=h.dtype)
    full = blk[:, :, :, None, :] * eye[None, :, None, :, None]
    return full.reshape(B, C_GN, C_D_INNER)


def _state_from_kernel(st):
    B = st.shape[0]
    s5 = st.reshape(B, C_GROUPS, C_D_STATE, C_GROUPS, C_HPG, C_HEAD_DIM)
    diag = jnp.stack([s5[:, g, :, g] for g in range(C_GROUPS)], axis=1)
    return jnp.transpose(diag, (0, 1, 3, 4, 2)).reshape(B, C_HEADS, C_HEAD_DIM, C_D_STATE)


def _outproj_kernel(ya_ref, yb_ref, yc_ref, x_ref, g1_ref, sh_ref, sc_ref, gf_ref, wo_ref, wr_ref, br_ref,
                    x1_ref, h2_ref, gate_ref, *, nb, r):
    R = nb * r
    D = x_ref.shape[-1]

    def in2(ref):
        return ref[...].reshape(R, ref.shape[-1])

    o = jnp.dot(in2(ya_ref), wo_ref[0:A_WIDTH, :], preferred_element_type=F32)
    o = o + jnp.dot(in2(yb_ref), wo_ref[A_WIDTH:A_WIDTH + B_WIDTH, :], preferred_element_type=F32)
    o = o + jnp.dot(in2(yc_ref), wo_ref[A_WIDTH + B_WIDTH:, :], preferred_element_type=F32)
    x1 = in2(x_ref) + _rows(g1_ref, nb, r) * o
    x1_ref[...] = x1.reshape(nb, r, D)
    h2 = x1 * lax.rsqrt(jnp.mean(x1 * x1, axis=-1, keepdims=True) + EPS) * gf_ref[...]
    h2 = h2 * (1.0 + _rows(sc_ref, nb, r)) + _rows(sh_ref, nb, r)
    h2_ref[...] = h2.astype(BF16).reshape(nb, r, D)

    h_hi = h2.astype(BF16)
    h_lo = (h2 - h_hi.astype(F32)).astype(BF16)
    wr = wr_ref[...]
    w_hi = wr.astype(BF16)
    w_lo = (wr - w_hi.astype(F32)).astype(BF16)
    lg = (jnp.dot(h_hi, w_hi, preferred_element_type=F32) + jnp.dot(h_hi, w_lo, preferred_element_type=F32)
          + jnp.dot(h_lo, w_hi, preferred_element_type=F32) + br_ref[...])
    lane = lax.broadcasted_iota(jnp.int32, (R, ROUTER_LANES), 1)
    big = jnp.int32(ROUTER_LANES)
    is_g = lane < E_GROUPS
    gl = jnp.where(is_g, lg, NEG_INF)
    gmax = jnp.max(gl, axis=-1, keepdims=True)
    p_top = 1.0 / jnp.sum(jnp.where(is_g, jnp.exp(gl - gmax), 0.0), axis=-1, keepdims=True)
    g_idx = jnp.min(jnp.where(is_g & (gl == gmax), lane, big), axis=-1, keepdims=True)
    e_lane = lane - ROUTER_E_OFF
    sel = (e_lane >= 0) & (e_lane < N_EXPERTS) & ((e_lane // E_PER_GROUP) == g_idx)
    l1 = jnp.where(sel, lg, NEG_INF)
    m1 = jnp.max(l1, axis=-1, keepdims=True)
    i1 = jnp.min(jnp.where(sel & (l1 == m1), lane, big), axis=-1, keepdims=True)
    sel2 = sel & (lane != i1)
    l2 = jnp.where(sel2, lg, NEG_INF)
    m2 = jnp.max(l2, axis=-1, keepdims=True)
    i2 = jnp.min(jnp.where(sel2 & (l2 == m2), lane, big), axis=-1, keepdims=True)
    e2 = jnp.exp(m2 - m1)
    w1 = p_top / (1.0 + e2)
    w2 = p_top * e2 / (1.0 + e2)
    gate = jnp.where(lane == i1, w1, 0.0) + jnp.where(lane == i2, w2, 0.0)
    gate_ref[...] = gate.reshape(nb, r, ROUTER_LANES)


def _outproj(ya, yb, yc, x, mod, g_ffn, wo, wr, br, *, nb, r):
    B, S, D = x.shape
    row = lambda c: pl.BlockSpec((nb, r, c), lambda b, i: (b, i, 0))
    modc = lambda j: pl.BlockSpec((nb, 1, D), lambda b, i: (b, 0, j))
    full = lambda a: pl.BlockSpec(a.shape, lambda b, i: (0,) * a.ndim)
    return pl.pallas_call(
        functools.partial(_outproj_kernel, nb=nb, r=r),
        grid=(B // nb, S // r),
        in_specs=[row(A_WIDTH), row(B_WIDTH), row(C_D_INNER), row(D), modc(2), modc(3), modc(4),
                  full(g_ffn), full(wo), full(wr), full(br)],
        out_specs=[row(D), row(D), row(ROUTER_LANES)],
        out_shape=[jax.ShapeDtypeStruct((B, S, D), F32), jax.ShapeDtypeStruct((B, S, D), BF16),
                   jax.ShapeDtypeStruct((B, S, ROUTER_LANES), F32)],
        compiler_params=_cparams("arbitrary", "arbitrary"),
        name="outproj",
    )(ya, yb, yc, x, mod, mod, mod, g_ffn, wo, wr, br)


def _moe_kernel(h_ref, gate_ref, x_ref, g2_ref, wgu_ref, wd_ref, o_ref, acc_sc, *, nb, r, ne):
    e = pl.program_id(2)
    R = nb * r
    D = x_ref.shape[-1]

    @pl.when(e == 0)
    def _():
        acc_sc[...] = jnp.zeros(acc_sc.shape, F32)

    h = h_ref[...].reshape(R, D)
    gu = jnp.dot(h, wgu_ref[0, 0], preferred_element_type=F32)
    lane = lax.broadcasted_iota(jnp.int32, (R, ROUTER_LANES), 1)
    gcol = jnp.sum(jnp.where(lane == e + ROUTER_E_OFF, gate_ref[...].reshape(R, ROUTER_LANES), 0.0),
                   axis=-1, keepdims=True)
    act = _silu(gu[:, 0:E_HIDDEN]) * gu[:, E_HIDDEN:] * gcol
    acc_sc[...] += jnp.dot(act.astype(BF16), wd_ref[0, 0], preferred_element_type=F32)

    @pl.when(e == ne - 1)
    def _():
        o_ref[...] = (x_ref[...].reshape(R, D) + _rows(g2_ref, nb, r) * acc_sc[...]).reshape(nb, r, D)


def _moe(h2, gate, x1, mod, wgu, wd, layer, *, nb, r):
    B, S, D = x1.shape
    ne = wgu.shape[1]
    row = lambda c: pl.BlockSpec((nb, r, c), lambda b, i, e: (b, i, 0))
    return pl.pallas_call(
        functools.partial(_moe_kernel, nb=nb, r=r, ne=ne),
        grid=(B // nb, S // r, ne),
        in_specs=[row(D), row(ROUTER_LANES), row(D),
                  pl.BlockSpec((nb, 1, D), lambda b, i, e: (b, 0, 5)),
                  pl.BlockSpec((1, 1, D, 2 * E_HIDDEN), lambda b, i, e: (layer, e, 0, 0)),
                  pl.BlockSpec((1, 1, E_HIDDEN, D), lambda b, i, e: (layer, e, 0, 0))],
        out_specs=row(D),
        out_shape=jax.ShapeDtypeStruct((B, S, D), F32),
        scratch_shapes=[pltpu.VMEM((nb * r, D), F32)],
        compiler_params=_cparams("arbitrary", "arbitrary", "arbitrary"),
        name="moe",
    )(h2, gate, x1, mod, wgu, wd)


def _final_kernel(x_ref, sh_ref, sc_ref, g_ref, o_ref, *, nb, r):
    R = nb * r
    D = x_ref.shape[-1]
    x = x_ref[...].reshape(R, D)
    y = x * lax.rsqrt(jnp.mean(x * x, axis=-1, keepdims=True) + EPS) * g_ref[...]
    o_ref[...] = (y * (1.0 + _rows(sc_ref, nb, r)) + _rows(sh_ref, nb, r)).reshape(nb, r, D)


def _final(x, modf, g, *, nb, r):
    B, S, D = x.shape
    row = pl.BlockSpec((nb, r, D), lambda b, i: (b, i, 0))
    return pl.pallas_call(
        functools.partial(_final_kernel, nb=nb, r=r),
        grid=(B // nb, S // r),
        in_specs=[row, pl.BlockSpec((nb, 1, D), lambda b, i: (b, 0, 0)),
                  pl.BlockSpec((nb, 1, D), lambda b, i: (b, 0, 1)),
                  pl.BlockSpec(g.shape, lambda b, i: (0, 0))],
        out_specs=row,
        out_shape=jax.ShapeDtypeStruct((B, S, D), F32),
        compiler_params=_cparams("arbitrary", "arbitrary"),
        name="final_norm",
    )(x, modf, modf, g)


def _rope_tables(pos):
    half = B_ROPE // 2
    inv = 1.0 / (ROPE_THETA ** (jnp.arange(half, dtype=F32) / half))
    ang = pos.astype(F32)[:, None] * inv[None, :]
    cos, sin = jnp.cos(ang), jnp.sin(ang)
    return (jnp.tile(jnp.concatenate([cos, cos], axis=1), (1, B_HEADS)),
            jnp.tile(jnp.concatenate([-sin, sin], axis=1), (1, B_HEADS)))


def _swap_halves(w):
    half = w.shape[-1] // 2
    return jnp.concatenate([w[..., half:], w[..., :half]], axis=-1)


def _layer_weights(l, w_in, b_g_q, b_w_uq, b_g_kv, b_w_uk, b_w_uv, w_out, moe_w_rg, moe_b_rg, moe_w_re, moe_b_re):
    D = w_in.shape[1]
    w = w_in[l]
    o_cq = 3 * A_WIDTH
    o_ckv = o_cq + B_Q_LORA
    o_kpe = o_ckv + B_KV_LORA
    o_z = o_kpe + B_ROPE
    o_xbc = o_z + C_D_INNER
    o_dt = o_xbc + C_CONV_DIM
    wkpe = w[:, o_kpe:o_z]
    zc = lambda n: jnp.zeros((D, n), F32)
    w1 = jnp.concatenate([w[:, 0:o_cq], w[:, o_cq:o_ckv], w[:, o_dt:o_dt + C_HEADS],
                          zc(256 - B_Q_LORA - C_HEADS), w[:, o_ckv:o_kpe], wkpe, _swap_halves(wkpe),
                          zc(LANES - 2 * B_ROPE), w[:, o_z:o_xbc], w[:, o_xbc:o_dt]], axis=1).astype(BF16)
    assert w1.shape[1] == _W1_N
    gq = jnp.pad(b_g_q[l], (0, 256 - B_Q_LORA))[None, :]
    uq = b_w_uq[l]
    pe = uq[:, :, B_NOPE:]
    wuq = jnp.concatenate([uq[:, :, :B_NOPE].reshape(B_Q_LORA, -1), pe.reshape(B_Q_LORA, -1),
                           _swap_halves(pe).reshape(B_Q_LORA, -1)], axis=1)
    wuq = jnp.pad(wuq, ((0, 256 - B_Q_LORA), (0, 0))).astype(BF16)
    eye = jnp.eye(B_HEADS, dtype=F32)
    bd = (jnp.transpose(b_w_uk[l], (1, 2, 0))[:, :, None, :] * eye[:, None, :, None]).reshape(
        B_HEADS * B_NOPE, B_HEADS * B_KV_LORA).astype(BF16)
    wuv = (jnp.transpose(b_w_uv[l], (1, 0, 2))[:, :, None, :] * eye[:, None, :, None]).reshape(
        B_HEADS * B_KV_LORA, B_WIDTH).astype(BF16)
    wr = jnp.zeros((D, ROUTER_LANES), F32)
    wr = wr.at[:, 0:E_GROUPS].set(moe_w_rg[l]).at[:, ROUTER_E_OFF:ROUTER_E_OFF + N_EXPERTS].set(moe_w_re[l])
    br = jnp.zeros((1, ROUTER_LANES), F32)
    br = br.at[0, 0:E_GROUPS].set(moe_b_rg[l]).at[0, ROUTER_E_OFF:ROUTER_E_OFF + N_EXPERTS].set(moe_b_re[l])
    return dict(w1=w1, gq=gq, wuq=wuq, bd=bd, gkv=b_g_kv[l][None, :], wuv=wuv, wo=w_out[l].astype(BF16), wr=wr, br=br)


def _band_bias_prompt(rel_bias):
    i = jnp.arange(CHUNK)
    u = jnp.arange(A_BAND)
    rel = A_BACK * CHUNK + i[:, None] - u[None, :]
    bias = rel_bias[:, jnp.clip(rel, -A_REL_CLIP, A_REL_CLIP) + A_REL_CLIP]
    return bias.reshape(A_HEADS * CHUNK, A_BAND)


def _band_bias_sample(rel_bias, nr, ds):
    kpos = jnp.concatenate([jnp.arange(nr) - nr, jnp.arange(ds)])
    rel = jnp.arange(ds)[:, None] - kpos[None, :]
    bias = rel_bias[:, jnp.clip(rel, -A_REL_CLIP, A_REL_CLIP) + A_REL_CLIP]
    return bias.reshape(A_HEADS * ds, nr + ds)


def _layer(x, mod, pos, lw, l, cache, prm, *, nb, r):
    B, S, D = x.shape
    cos_t, sin_t = _rope_tables(pos)
    (qkv, knew, vnew, qmla, kmla, ckv, kpe, z, xbc, dtr) = _inproj(
        x, mod, prm['g_mix'], lw['w1'], lw['gq'], lw['wuq'], lw['bd'], lw['gkv'], cos_t, sin_t, nb=nb, r=r)
    if cache is None:
        ya = _band_prompt(qkv, _band_bias_prompt(prm['a_rel_bias']))
        yb = _mla_prompt(qmla, kmla, lw['wuv'])
        prefix = jnp.zeros((B, C_CONV_W - 1, C_CONV_DIM), F32)
        h0t = jnp.zeros((B, C_GN, C_D_INNER), F32)
        lt, lc = min(S, 512), SSD_CHUNK
    else:
        nr = cache['a_k'].shape[2]
        ya = _band_sample(qkv, cache['a_k'], cache['a_v'], l, _band_bias_sample(prm['a_rel_bias'], nr, S))
        yb = _mla_sample(qmla, kmla, cache['b_ckv'], cache['b_kpe'], l, lw['wuv'])
        prefix = cache['c_conv'][l]
        h0t = _state_to_kernel(cache['c_ssm'][l])
        lt = lc = min(SSD_CHUNK, S)
    yc, hft = _ssd(xbc, z, dtr, prefix, h0t, prm['c_conv_w'], prm['c_conv_b'], prm['c_dt_bias'], prm['c_a_log'],
                   prm['c_d_exp'], prm['c_g_norm'], lt=lt, lc=lc)
    conv_state = jnp.concatenate([prefix, xbc], axis=1)[:, -(C_CONV_W - 1):]
    x1, h2, gate = _outproj(ya, yb, yc, x, mod, prm['g_ffn'], lw['wo'], lw['wr'], lw['br'], nb=nb, r=r)
    r_moe = r if nb > 1 else min(S, 1024)
    xo = _moe(h2, gate, x1, mod, prm['moe_w_gu'], prm['moe_w_down'], l, nb=nb, r=r_moe)
    states = (knew.reshape(B, -1, A_HEADS, A_HEAD_DIM), vnew.reshape(B, -1, A_HEADS, A_HEAD_DIM),
              ckv, kpe, conv_state, _state_from_kernel(hft))
    return xo, states


def kernel(x_prompt, x_sample, c_prompt, c_sample, cache_a_k, cache_a_v, cache_b_ckv, cache_b_kpe,
           state_c_conv, state_c_ssm, w_ada, b_ada, g_mix, w_in, a_rel_bias, b_g_q, b_w_uq, b_g_kv,
           b_w_uk, b_w_uv, c_conv_w, c_conv_b, c_dt_bias, c_a_log, c_d, c_g_norm, w_out, g_ffn,
           moe_w_rg, moe_b_rg, moe_w_re, moe_b_re, moe_w_gu, moe_w_down, g_final, w_ada_f, b_ada_f):
    depth = w_in.shape[0]
    D = x_prompt.shape[-1]
    Bp, Sp, _ = x_prompt.shape
    Bs, Ss, _ = x_sample.shape
    past = cache_b_ckv.shape[2]
    pos_p = jnp.arange(Sp)
    pos_s = past + jnp.arange(Ss)

    nc = Bp + Bs
    ncp = -(-nc // 8) * 8
    c_all = jnp.pad(jnp.concatenate([c_prompt, c_sample], axis=0), ((0, ncp - nc), (0, 0)))
    mod = _ada(c_all, w_ada, b_ada[:, None, :])
    modf = _ada(c_all, w_ada_f[None], b_ada_f[None, None, :])[0]

    nr = cache_a_k.shape[2]
    cache = {'a_k': cache_a_k.reshape(depth, Bs, nr, A_WIDTH), 'a_v': cache_a_v.reshape(depth, Bs, nr, A_WIDTH),
             'b_ckv': cache_b_ckv, 'b_kpe': cache_b_kpe, 'c_conv': state_c_conv, 'c_ssm': state_c_ssm}

    rp = min(Sp, A_WINDOW)
    moe_w_gu = moe_w_gu.astype(BF16)
    moe_w_down = moe_w_down.astype(BF16)
    xp, xs = x_prompt, x_sample
    st_p, st_s = [], []
    for l in range(depth):
        lw = _layer_weights(l, w_in, b_g_q, b_w_uq, b_g_kv, b_w_uk, b_w_uv, w_out,
                            moe_w_rg, moe_b_rg, moe_w_re, moe_b_re)
        prm = {'g_mix': g_mix[l][None, :], 'a_rel_bias': a_rel_bias[l], 'c_conv_w': c_conv_w[l],
               'c_conv_b': c_conv_b[l][None, :], 'c_dt_bias': c_dt_bias[l][None, :], 'c_a_log': c_a_log[l][None, :],
               'c_d_exp': jnp.repeat(c_d[l], C_HEAD_DIM)[None, :], 'c_g_norm': c_g_norm[l][None, :],
               'g_ffn': g_ffn[l][None, :], 'moe_w_gu': moe_w_gu, 'moe_w_down': moe_w_down}
        xp, sp = _layer(xp, mod[l, 0:Bp][:, None, :], pos_p, lw, l, None, prm, nb=1, r=rp)
        xs, ss = _layer(xs, mod[l, Bp:nc][:, None, :], pos_s, lw, l, cache, prm, nb=Bs, r=Ss)
        st_p.append(sp)
        st_s.append(ss)
    gf = g_final[None, :]
    y_prompt = _final(xp, modf[0:Bp][:, None, :], gf, nb=1, r=rp)
    y_sample = _final(xs, modf[Bp:nc][:, None, :], gf, nb=Bs, r=Ss)
    stack = lambda st, k: jnp.stack([t[k] for t in st])
    return ((y_prompt, y_sample) + tuple(stack(st_p, k) for k in range(6))
            + tuple(stack(st_s, k) for k in range(6)))
```

```python
import functools
import math

import numpy as np
import jax
import jax.numpy as jnp
from jax import lax
from jax.experimental import pallas as pl
from jax.experimental.pallas import tpu as pltpu

F32 = jnp.float32
BF16 = jnp.bfloat16

D_MODEL = 1024
CHUNK = 64
EPS = 1e-6
NEG_INF = -1e30

A_HEADS = 4
A_HEAD_DIM = 64
A_WIDTH = A_HEADS * A_HEAD_DIM
A_BACK = 8
A_BAND = (A_BACK + 1) * CHUNK
A_WINDOW = A_BACK * CHUNK
A_REL_CLIP = 128
A_SCALE = A_HEAD_DIM ** -0.5

B_HEADS = 4
B_Q_LORA = 192
B_KV_LORA = 128
B_NOPE = 64
B_ROPE = 32
B_V = 64
B_WIDTH = B_HEADS * B_V
B_SCALE = (B_NOPE + B_ROPE) ** -0.5
B_QK = B_KV_LORA + B_ROPE
ROPE_THETA = 10000.0

C_D_INNER = 512
C_HEAD_DIM = 64
C_HEADS = C_D_INNER // C_HEAD_DIM
C_GROUPS = 2
C_D_STATE = 64
C_CONV_W = 4
C_CONV_DIM = C_D_INNER + 2 * C_GROUPS * C_D_STATE
C_GN = C_GROUPS * C_D_STATE
C_HPG = C_HEADS // C_GROUPS
SSD_CHUNK = 64

E_GROUPS = 4
E_PER_GROUP = 4
N_EXPERTS = E_GROUPS * E_PER_GROUP
E_HIDDEN = 256
ROUTER_LANES = 128
ROUTER_E_OFF = 16

LANES = 128
VMEM_LIMIT = 56 * 1024 * 1024

B_QSCALE = B_SCALE * math.log2(math.e)

_W1_QKV = 0
_W1_CQ = 3 * A_WIDTH
_W1_CKV = _W1_CQ + 256
_W1_KPE = _W1_CKV + B_KV_LORA
_W1_Z = _W1_KPE + LANES
_W1_XBC = _W1_Z + C_D_INNER
_W1_N = _W1_XBC + C_CONV_DIM


def _cparams(*sem):
    return pltpu.CompilerParams(dimension_semantics=sem, vmem_limit_bytes=VMEM_LIMIT)


def _silu(v):
    return v * jax.nn.sigmoid(v)


def _split3_lanes(v, lane):
    x3 = v + pltpu.roll(v, C_HEADS, axis=1) + pltpu.roll(v, 2 * C_HEADS, axis=1)
    hi = x3.astype(BF16).astype(F32)
    r1 = x3 - hi
    mid = r1.astype(BF16).astype(F32)
    lo = r1 - mid
    return jnp.where(lane < C_HEADS, hi, jnp.where(lane < 2 * C_HEADS, mid, lo)).astype(BF16)


def _ada_kernel(c_ref, w_ref, b_ref, o_ref):
    s = _silu(c_ref[...]).astype(BF16)
    o_ref[0] = jnp.dot(s, w_ref[0].astype(BF16), preferred_element_type=F32) + b_ref[0]


def _ada(c_all, w, b):
    nl, d, n = w.shape
    bp = c_all.shape[0]
    tn = 1024
    return pl.pallas_call(
        _ada_kernel,
        grid=(nl, n // tn),
        in_specs=[pl.BlockSpec((bp, d), lambda l, j: (0, 0)),
                  pl.BlockSpec((1, d, tn), lambda l, j: (l, 0, j)),
                  pl.BlockSpec((1, 1, tn), lambda l, j: (l, 0, j))],
        out_specs=pl.BlockSpec((1, bp, tn), lambda l, j: (l, 0, j)),
        out_shape=jax.ShapeDtypeStruct((nl, bp, n), F32),
        compiler_params=_cparams("arbitrary", "arbitrary"),
        name="ada",
    )(c_all, w, b)


def _rows(ref_or_val, nb, r):
    v = ref_or_val[...]
    return jnp.broadcast_to(v, (nb, r, v.shape[-1])).reshape(nb * r, v.shape[-1])


def _inproj_kernel(x_ref, sh_ref, sc_ref, g_ref, w1_ref, gq_ref, wuq_ref, bd_ref, gkv_ref, cos_ref, sin_ref,
                   qkv_ref, knew_ref, vnew_ref, qmla_ref, kmla_ref, ckv_ref, kpe_ref, z_ref, xbc_ref, dtr_ref,
                   *, nb, r, keep_tile):
    R = nb * r
    D = x_ref.shape[-1]

    def out3(v):
        return v.reshape(nb, r, v.shape[-1])

    x = x_ref[...].reshape(R, D)
    h = x * lax.rsqrt(jnp.mean(x * x, axis=-1, keepdims=True) + EPS) * g_ref[...]
    h = h * (1.0 + _rows(sc_ref, nb, r)) + _rows(sh_ref, nb, r)
    u = jnp.dot(h.astype(BF16), w1_ref[...], preferred_element_type=F32)

    qkv_ref[:, :, 0:A_WIDTH] = out3((u[:, 0:A_WIDTH] * A_SCALE).astype(BF16))
    qkv_ref[:, :, A_WIDTH:3 * A_WIDTH] = out3(u[:, A_WIDTH:3 * A_WIDTH].astype(BF16))

    @pl.when(pl.program_id(1) == keep_tile)
    def _():
        knew_ref[...] = out3(u[:, A_WIDTH:2 * A_WIDTH])
        vnew_ref[...] = out3(u[:, 2 * A_WIDTH:3 * A_WIDTH])

    blk = u[:, _W1_CQ:_W1_CQ + 256]
    lane = lax.broadcasted_iota(jnp.int32, (1, 256), 1)
    cq = jnp.where(lane < B_Q_LORA, blk, 0.0)
    cqn = cq * lax.rsqrt(jnp.sum(cq * cq, axis=-1, keepdims=True) * (1.0 / B_Q_LORA) + EPS) * gq_ref[...]
    qb = jnp.dot(cqn.astype(BF16), wuq_ref[...], preferred_element_type=F32)
    nope_w = B_HEADS * B_NOPE
    rope_w = B_HEADS * B_ROPE
    qlat = jnp.dot(qb[:, 0:nope_w].astype(BF16), bd_ref[...], preferred_element_type=F32) * B_QSCALE
    cosr = _rows(cos_ref, nb, r)
    sinr = _rows(sin_ref, nb, r)
    qpe = (qb[:, nope_w:nope_w + rope_w] * cosr + qb[:, nope_w + rope_w:nope_w + 2 * rope_w] * sinr) * B_QSCALE
    for hd in range(B_HEADS):
        qmla_ref[:, hd, :, 0:B_KV_LORA] = out3(qlat[:, hd * B_KV_LORA:(hd + 1) * B_KV_LORA].astype(BF16))
        qmla_ref[:, hd, :, B_KV_LORA:B_QK] = out3(qpe[:, hd * B_ROPE:(hd + 1) * B_ROPE].astype(BF16))

    cr = u[:, _W1_CKV:_W1_CKV + B_KV_LORA]
    ckv = cr * lax.rsqrt(jnp.mean(cr * cr, axis=-1, keepdims=True) + EPS) * gkv_ref[...]
    ckv_ref[...] = out3(ckv)
    kmla_ref[:, :, 0:B_KV_LORA] = out3(ckv.astype(BF16))
    kb = u[:, _W1_KPE:_W1_KPE + LANES]
    rot = kb * cosr + pltpu.roll(kb, LANES - B_ROPE, axis=1) * sinr
    kpe_ref[...] = out3(rot[:, 0:B_ROPE])
    kmla_ref[:, :, B_KV_LORA:B_QK] = out3(rot[:, 0:B_ROPE].astype(BF16))
    lane1 = lax.broadcasted_iota(jnp.int32, (1, LANES), 1)
    dtr_ref[...] = out3(jnp.where(lane1 < C_HEADS, pltpu.roll(kb, LANES - 2 * B_ROPE, axis=1), 0.0))

    z_ref[...] = out3(u[:, _W1_Z:_W1_Z + C_D_INNER].astype(BF16))
    xbc_ref[...] = out3(u[:, _W1_XBC:_W1_XBC + C_CONV_DIM])


def _inproj(x, mod, g_mix, w1, gq, wuq, bd, gkv, cos_t, sin_t, *, nb, r):
    B, S, D = x.shape
    nbt, nst = B // nb, S // r
    keep = min(S, A_WINDOW)
    assert keep == r and B % nb == 0 and S % r == 0
    grid = (nbt, nst)
    row = lambda c: pl.BlockSpec((nb, r, c), lambda b, i: (b, i, 0))
    full = lambda a: pl.BlockSpec(a.shape, lambda b, i: (0,) * a.ndim)
    in_specs = [row(D),
                pl.BlockSpec((nb, 1, D), lambda b, i: (b, 0, 0)),
                pl.BlockSpec((nb, 1, D), lambda b, i: (b, 0, 1)),
                full(g_mix), full(w1), full(gq), full(wuq), full(bd), full(gkv),
                pl.BlockSpec((r, LANES), lambda b, i: (i, 0)),
                pl.BlockSpec((r, LANES), lambda b, i: (i, 0))]
    out_shape = [jax.ShapeDtypeStruct((B, S, 3 * A_WIDTH), BF16),
                 jax.ShapeDtypeStruct((B, keep, A_WIDTH), F32),
                 jax.ShapeDtypeStruct((B, keep, A_WIDTH), F32),
                 jax.ShapeDtypeStruct((B, B_HEADS, S, B_QK), BF16),
                 jax.ShapeDtypeStruct((B, S, B_QK), BF16),
                 jax.ShapeDtypeStruct((B, S, B_KV_LORA), F32),
                 jax.ShapeDtypeStruct((B, S, B_ROPE), F32),
                 jax.ShapeDtypeStruct((B, S, C_D_INNER), BF16),
                 jax.ShapeDtypeStruct((B, S, C_CONV_DIM), F32),
                 jax.ShapeDtypeStruct((B, S, LANES), F32)]
    out_specs = [row(3 * A_WIDTH),
                 pl.BlockSpec((nb, keep, A_WIDTH), lambda b, i: (b, 0, 0)),
                 pl.BlockSpec((nb, keep, A_WIDTH), lambda b, i: (b, 0, 0)),
                 pl.BlockSpec((nb, B_HEADS, r, B_QK), lambda b, i: (b, 0, i, 0)),
                 row(B_QK), row(B_KV_LORA), row(B_ROPE), row(C_D_INNER), row(C_CONV_DIM), row(LANES)]
    return pl.pallas_call(
        functools.partial(_inproj_kernel, nb=nb, r=r, keep_tile=nst - 1),
        grid=grid, in_specs=in_specs, out_specs=out_specs, out_shape=out_shape,
        compiler_params=_cparams("arbitrary", "arbitrary"),
        name="inproj",
    )(x, mod, mod, g_mix, w1, gq, wuq, bd, gkv, cos_t, sin_t)


def _band_core(q, k, v, bias, valid, hmask):
    nq = q.shape[0]
    qs = jnp.where(hmask, jnp.concatenate([q] * A_HEADS, axis=0), jnp.zeros((), BF16))
    s = lax.dot_general(qs, k, (((1,), (1,)), ((), ())), preferred_element_type=F32) + bias
    if valid is not None:
        s = jnp.where(valid, s, NEG_INF)
    m = jnp.max(s, axis=-1, keepdims=True)
    e = jnp.exp(s - m)
    den = jnp.sum(e, axis=-1, keepdims=True)
    o = jnp.dot(e.astype(BF16), v, preferred_element_type=F32)
    o = jnp.where(hmask, o / den, 0.0)
    y = o[0:nq]
    for hd in range(1, A_HEADS):
        y = y + o[hd * nq:(hd + 1) * nq]
    return y


def _head_mask(nq):
    row = lax.broadcasted_iota(jnp.int32, (A_HEADS * nq, A_WIDTH), 0)
    col = lax.broadcasted_iota(jnp.int32, (A_HEADS * nq, A_WIDTH), 1)
    return (row // nq) == (col // A_HEAD_DIM)


def _band_prompt_kernel(cur_ref, prev_ref, bias_ref, o_ref, kwin, vwin, *, tq):
    i = pl.program_id(1)
    nch = tq // CHUNK
    kwin[0:A_WINDOW] = prev_ref[0, tq - A_WINDOW:tq, A_WIDTH:2 * A_WIDTH]
    kwin[A_WINDOW:A_WINDOW + tq] = cur_ref[0, :, A_WIDTH:2 * A_WIDTH]
    vwin[0:A_WINDOW] = prev_ref[0, tq - A_WINDOW:tq, 2 * A_WIDTH:3 * A_WIDTH]
    vwin[A_WINDOW:A_WINDOW + tq] = cur_ref[0, :, 2 * A_WIDTH:3 * A_WIDTH]
    hmask = _head_mask(CHUNK)
    ucol = lax.broadcasted_iota(jnp.int32, (1, A_BAND), 1)
    for c in range(nch):
        q = cur_ref[0, c * CHUNK:(c + 1) * CHUNK, 0:A_WIDTH]
        valid = (ucol // CHUNK + (i * nch + c - A_BACK)) >= 0
        y = _band_core(q, kwin[c * CHUNK:c * CHUNK + A_BAND], vwin[c * CHUNK:c * CHUNK + A_BAND],
                       bias_ref[...], valid, hmask)
        o_ref[0, c * CHUNK:(c + 1) * CHUNK, :] = y.astype(BF16)


def _band_prompt(qkv, bias_stack):
    B, S, _ = qkv.shape
    tq = A_WINDOW
    assert S % tq == 0
    return pl.pallas_call(
        functools.partial(_band_prompt_kernel, tq=tq),
        grid=(B, S // tq),
        in_specs=[pl.BlockSpec((1, tq, 3 * A_WIDTH), lambda b, i: (b, i, 0)),
                  pl.BlockSpec((1, tq, 3 * A_WIDTH), lambda b, i: (b, jnp.maximum(i - 1, 0), 0)),
                  pl.BlockSpec(bias_stack.shape, lambda b, i: (0, 0))],
        out_specs=pl.BlockSpec((1, tq, A_WIDTH), lambda b, i: (b, i, 0)),
        out_shape=jax.ShapeDtypeStruct((B, S, A_WIDTH), BF16),
        scratch_shapes=[pltpu.VMEM((A_WINDOW + tq, A_WIDTH), BF16), pltpu.VMEM((A_WINDOW + tq, A_WIDTH), BF16)],
        compiler_params=_cparams("arbitrary", "arbitrary"),
        name="band_prompt",
    )(qkv, qkv, bias_stack)


def _band_sample_kernel(qkv_ref, kt_ref, vt_ref, biasc_ref, biasn_ref, o_ref, *, nbb, ds):
    hmask = _head_mask(ds)
    nt_dims = (((1,), (1,)), ((), ()))
    for b in range(nbb):
        q = qkv_ref[b, :, 0:A_WIDTH]
        kn = qkv_ref[b, :, A_WIDTH:2 * A_WIDTH]
        vn = qkv_ref[b, :, 2 * A_WIDTH:3 * A_WIDTH]
        qs = jnp.where(hmask, jnp.concatenate([q] * A_HEADS, axis=0), jnp.zeros((), BF16))
        sc = jnp.dot(qs, kt_ref[0, b].astype(BF16), preferred_element_type=F32) + biasc_ref[...]
        sn = lax.dot_general(qs, kn, nt_dims, preferred_element_type=F32) + biasn_ref[...]
        m = jnp.maximum(jnp.max(sc, axis=-1, keepdims=True), jnp.max(sn, axis=-1, keepdims=True))
        ec = jnp.exp(sc - m)
        en = jnp.exp(sn - m)
        den = jnp.sum(ec, axis=-1, keepdims=True) + jnp.sum(en, axis=-1, keepdims=True)
        o = lax.dot_general(ec.astype(BF16), vt_ref[0, b].astype(BF16), nt_dims, preferred_element_type=F32)
        o = o + jnp.dot(en.astype(BF16), vn, preferred_element_type=F32)
        o = jnp.where(hmask, o / den, 0.0)
        y = o[0:ds]
        for hd in range(1, A_HEADS):
            y = y + o[hd * ds:(hd + 1) * ds]
        o_ref[b] = y.astype(BF16)


def _band_sample(qkv, cache_kt, cache_vt, layer, bias_stack):
    B, ds, _ = qkv.shape
    nr = cache_kt.shape[3]
    nbb = 4
    assert B % nbb == 0
    bias_c, bias_n = bias_stack[:, 0:nr], bias_stack[:, nr:]
    return pl.pallas_call(
        functools.partial(_band_sample_kernel, nbb=nbb, ds=ds),
        grid=(B // nbb,),
        in_specs=[pl.BlockSpec((nbb, ds, 3 * A_WIDTH), lambda b: (b, 0, 0)),
                  pl.BlockSpec((1, nbb, A_WIDTH, nr), lambda b: (layer, b, 0, 0)),
                  pl.BlockSpec((1, nbb, A_WIDTH, nr), lambda b: (layer, b, 0, 0)),
                  pl.BlockSpec(bias_c.shape, lambda b: (0, 0)),
                  pl.BlockSpec(bias_n.shape, lambda b: (0, 0))],
        out_specs=pl.BlockSpec((nbb, ds, A_WIDTH), lambda b: (b, 0, 0)),
        out_shape=jax.ShapeDtypeStruct((B, ds, A_WIDTH), BF16),
        compiler_params=_cparams("arbitrary"),
        name="band_sample",
    )(qkv, cache_kt, cache_vt, bias_c, bias_n)


def _mla_finish(acc, den, wuv_ref, bq):
    o = acc / den
    o_all = jnp.concatenate([o[hd * bq:(hd + 1) * bq] for hd in range(B_HEADS)], axis=1)
    return jnp.dot(o_all.astype(BF16), wuv_ref[...], preferred_element_type=F32)


def _mla_prompt_kernel(qi_ref, kj_ref, q_ref, k_ref, wuv_ref, o_ref, m_sc, l_sc, acc_sc, *, bq, bk):
    p = pl.program_id(1)
    i = qi_ref[p]
    j = kj_ref[p]
    M = B_HEADS * bq

    @pl.when(j == 0)
    def _():
        m_sc[...] = jnp.full(m_sc.shape, NEG_INF, F32)
        l_sc[...] = jnp.zeros(l_sc.shape, F32)
        acc_sc[...] = jnp.zeros(acc_sc.shape, F32)

    def step(masked):
        q = q_ref[0].reshape(M, B_QK)
        k = k_ref[0]
        s = lax.dot_general(q, k, (((1,), (1,)), ((), ())), preferred_element_type=F32)
        if masked:
            row = lax.broadcasted_iota(jnp.int32, (M, bk), 0)
            col = lax.broadcasted_iota(jnp.int32, (M, bk), 1)
            s = jnp.where((col // CHUNK) <= ((row % bq) // CHUNK), s, NEG_INF)
        m_prev = m_sc[...]
        m_new = jnp.maximum(m_prev, jnp.max(s, axis=1, keepdims=True))
        alpha = jnp.exp2(m_prev - m_new)
        pr = jnp.exp2(s - jnp.tile(m_new, (1, bk // LANES)))
        l_sc[...] = alpha * l_sc[...] + jnp.sum(pr, axis=1, keepdims=True)
        acc_sc[...] = acc_sc[...] * alpha + jnp.dot(pr.astype(BF16), k[:, 0:B_KV_LORA],
                                                    preferred_element_type=F32)
        m_sc[...] = m_new

    @pl.when(j < i)
    def _():
        step(False)

    @pl.when(j == i)
    def _():
        step(True)
        o_ref[0] = _mla_finish(acc_sc[...], l_sc[...], wuv_ref, bq).astype(BF16)


def _mla_prompt(qmla, kmla, wuv_bd):
    B, S = kmla.shape[0:2]
    bq = bk = 512
    assert S % bq == 0 and bq % CHUNK == 0
    nq = S // bq
    qi = np.concatenate([np.full(i + 1, i, np.int32) for i in range(nq)])
    kj = np.concatenate([np.arange(i + 1, dtype=np.int32) for i in range(nq)])
    grid_spec = pltpu.PrefetchScalarGridSpec(
        num_scalar_prefetch=2,
        grid=(B, len(qi)),
        in_specs=[pl.BlockSpec((1, B_HEADS, bq, B_QK), lambda b, p, qi, kj: (b, 0, qi[p], 0)),
                  pl.BlockSpec((1, bk, B_QK), lambda b, p, qi, kj: (b, kj[p], 0)),
                  pl.BlockSpec(wuv_bd.shape, lambda b, p, qi, kj: (0, 0))],
        out_specs=pl.BlockSpec((1, bq, B_WIDTH), lambda b, p, qi, kj: (b, qi[p], 0)),
        scratch_shapes=[pltpu.VMEM((B_HEADS * bq, LANES), F32), pltpu.VMEM((B_HEADS * bq, LANES), F32),
                        pltpu.VMEM((B_HEADS * bq, B_KV_LORA), F32)])
    return pl.pallas_call(
        functools.partial(_mla_prompt_kernel, bq=bq, bk=bk),
        grid_spec=grid_spec,
        out_shape=jax.ShapeDtypeStruct((B, S, B_WIDTH), BF16),
        compiler_params=_cparams("arbitrary", "arbitrary"),
        name="mla_prompt",
    )(jnp.asarray(qi), jnp.asarray(kj), qmla, kmla, wuv_bd)


def _mla_sample_kernel(q_ref, kn_ref, ckv_ref, kpet_ref, wuv_ref, o_ref, *, ds):
    M = B_HEADS * ds
    nt_dims = (((1,), (1,)), ((), ()))
    q = q_ref[0].reshape(M, B_QK)
    kn = kn_ref[0]
    ckv = ckv_ref[0, 0].astype(BF16)
    sc = (lax.dot_general(q[:, 0:B_KV_LORA], ckv, nt_dims, preferred_element_type=F32)
          + jnp.dot(q[:, B_KV_LORA:B_QK], kpet_ref[0, 0].astype(BF16), preferred_element_type=F32))
    sn = lax.dot_general(q, kn, nt_dims, preferred_element_type=F32)
    m = jnp.maximum(jnp.max(sc, axis=1, keepdims=True), jnp.max(sn, axis=1, keepdims=True))
    pc = jnp.exp2(sc - m)
    pn = jnp.exp2(sn - m)
    den = jnp.sum(pc, axis=1, keepdims=True) + jnp.sum(pn, axis=1, keepdims=True)
    acc = (jnp.dot(pc.astype(BF16), ckv, preferred_element_type=F32)
           + jnp.dot(pn.astype(BF16), kn[:, 0:B_KV_LORA], preferred_element_type=F32))
    o_ref[0] = _mla_finish(acc, den, wuv_ref, ds).astype(BF16)


def _mla_sample(qmla, kmla, cache_ckv, cache_kpet, layer, wuv_bd):
    B, ds = kmla.shape[0:2]
    past = cache_ckv.shape[2]
    return pl.pallas_call(
        functools.partial(_mla_sample_kernel, ds=ds),
        grid=(B,),
        in_specs=[pl.BlockSpec((1, B_HEADS, ds, B_QK), lambda b: (b, 0, 0, 0)),
                  pl.BlockSpec((1, ds, B_QK), lambda b: (b, 0, 0)),
                  pl.BlockSpec((1, 1, past, B_KV_LORA), lambda b: (layer, b, 0, 0)),
                  pl.BlockSpec((1, 1, B_ROPE, past), lambda b: (layer, b, 0, 0)),
                  pl.BlockSpec(wuv_bd.shape, lambda b: (0, 0))],
        out_specs=pl.BlockSpec((1, ds, B_WIDTH), lambda b: (b, 0, 0)),
        out_shape=jax.ShapeDtypeStruct((B, ds, B_WIDTH), BF16),
        compiler_params=_cparams("arbitrary"),
        name="mla_sample",
    )(qmla, kmla, cache_ckv, cache_kpet, wuv_bd)


def _ssd_kernel(xbc_ref, z_ref, dtr_ref, pre_ref, h0_ref, cw_ref, cb_ref, dtb_ref, alog_ref, dsk_ref, gn_ref,
                es_ref, ep_ref, tri_ref, y_ref, hout_ref,
                cbuf, xs_sc, b_sc, c_sc, ces_sc, cep_sc, dep_sc, st_sc, *, lt, lc, nt):
    t = pl.program_id(1)
    nch = lt // lc
    HS = C_HEADS * lc
    gw = C_HPG * C_HEAD_DIM

    @pl.when(t == 0)
    def _():
        cbuf[8 - (C_CONV_W - 1):8] = pre_ref[0]
        st_sc[...] = jnp.zeros(st_sc.shape, F32)
        for g in range(C_GROUPS):
            blk = jnp.concatenate([h0_ref[0, g * C_HPG + hl] for hl in range(C_HPG)], axis=1)
            st_sc[g * C_D_STATE:(g + 1) * C_D_STATE, g * gw:(g + 1) * gw] = blk

    cbuf[8:8 + lt] = xbc_ref[0]
    acc = jnp.broadcast_to(cb_ref[...], (lt, C_CONV_DIM))
    for kk in range(C_CONV_W):
        off = 8 - (C_CONV_W - 1) + kk
        acc = acc + cbuf[off:off + lt] * cw_ref[kk:kk + 1, :]
    tail = cbuf[8 + lt - (C_CONV_W - 1):8 + lt]
    cbuf[8 - (C_CONV_W - 1):8] = tail
    xc = _silu(acc)
    xs_sc[...] = xc[:, 0:C_D_INNER]
    b_sc[...] = xc[:, C_D_INNER:C_D_INNER + C_GN]
    c_sc[...] = xc[:, C_D_INNER + C_GN:C_D_INNER + 2 * C_GN]

    hl = lax.broadcasted_iota(jnp.int32, (lt, LANES), 1)
    dt = jnp.where(hl < C_HEADS, jax.nn.softplus(dtr_ref[0] + dtb_ref[...]), 0.0)
    da = dt * (-jnp.exp(alog_ref[...]))
    c3 = jnp.dot(tri_ref[...], _split3_lanes(da, hl), preferred_element_type=F32)
    cum = jnp.where(hl < C_HEADS, c3 + pltpu.roll(c3, LANES - C_HEADS, axis=1)
                    + pltpu.roll(c3, LANES - 2 * C_HEADS, axis=1), 0.0)
    cum3 = _split3_lanes(cum, hl)
    cep = jnp.dot(cum3, ep_ref[...], preferred_element_type=F32)
    cep_sc[...] = cep
    ces_sc[...] = cep if HS == C_D_INNER and lc == C_HEAD_DIM else jnp.dot(cum3, es_ref[...],
                                                                         preferred_element_type=F32)
    dep_sc[...] = jnp.dot(_split3_lanes(dt, hl), ep_ref[...], preferred_element_type=F32)

    r_s = lax.broadcasted_iota(jnp.int32, (lc, HS), 0)
    l_s = lax.broadcasted_iota(jnp.int32, (lc, HS), 1)
    eye_t = (l_s % lc) == r_s
    causal_t = (l_s % lc) <= r_s
    r_w = lax.broadcasted_iota(jnp.int32, (HS, C_GN), 0)
    l_w = lax.broadcasted_iota(jnp.int32, (HS, C_GN), 1)
    gmask = (r_w // (C_HPG * lc)) == (l_w // C_D_STATE)
    r_b =lax.broadcasted_iota(jnp.int32, (C_HPG * lc, gw), 0)
    l_b = lax.broadcasted_iota(jnp.int32, (C_HPG * lc, gw), 1)
    bmask = (r_b // lc) == (l_b // C_HEAD_DIM)
    r_g = lax.broadcasted_iota(jnp.int32, (C_GN, C_D_INNER), 0)
    l_g = lax.broadcasted_iota(jnp.int32, (C_GN, C_D_INNER), 1)
    smask = (r_g // C_D_STATE) == (l_g // gw)

    def chunk(c, carry):
        r0 = pl.multiple_of(c * lc, lc)
        rows = pl.ds(r0, lc)
        ce_s = ces_sc[rows, :]
        ce_p = cep_sc[rows, :]
        de_p = dep_sc[rows, :]
        xs = xs_sc[rows, :]
        bm = b_sc[rows, :]
        cm = c_sc[rows, :]
        rflat = jnp.sum(jnp.where(eye_t, ce_s, 0.0), axis=0, keepdims=True)
        lcat = jnp.exp(jnp.where(causal_t, ce_s - rflat, NEG_INF))
        w_nt = jnp.where(gmask, jnp.concatenate([bm] * C_HEADS, axis=0), 0.0).astype(BF16)
        cbcat = lax.dot_general(cm.astype(BF16), w_nt, (((1,), (1,)), ((), ())), preferred_element_type=F32)
        mcat = (cbcat * lcat).astype(BF16)
        xdt = xs * de_p
        xdt_b = xdt.astype(BF16)
        ydiag = []
        for g in range(C_GROUPS):
            bdx = jnp.where(bmask, jnp.concatenate([xdt_b[:, g * gw:(g + 1) * gw]] * C_HPG, axis=0),
                            jnp.zeros((), BF16))
            ydiag.append(jnp.dot(mcat[:, g * C_HPG * lc:(g + 1) * C_HPG * lc], bdx, preferred_element_type=F32))
        y = jnp.concatenate(ydiag, axis=1)
        st = st_sc[...]
        y = y + jnp.dot(cm.astype(BF16), st.astype(BF16), preferred_element_type=F32) * jnp.exp(ce_p)
        y = y + dsk_ref[...] * xs
        last = ce_p[lc - 1:lc, :]
        xw = (xdt * jnp.exp(last - ce_p)).astype(BF16)
        upd = lax.dot_general(bm.astype(BF16), xw, (((0,), (0,)), ((), ())), preferred_element_type=F32)
        st_sc[...] = jnp.exp(last) * st + jnp.where(smask, upd, 0.0)
        y = y * _silu(z_ref[0, rows, :].astype(F32))
        y = y * lax.rsqrt(jnp.mean(y * y, axis=-1, keepdims=True) + EPS) * gn_ref[...]
        y_ref[0, rows, :] = y.astype(BF16)
        return carry

    lax.fori_loop(0, nch, chunk, 0)

    @pl.when(t == nt - 1)
    def _():
        for hd in range(C_HEADS):
            g = hd // C_HPG
            hout_ref[0, hd] = st_sc[g * C_D_STATE:(g + 1) * C_D_STATE, hd * C_HEAD_DIM:(hd + 1) * C_HEAD_DIM]


def _ssd(xbc, z, dtr, prefix, h0t, cw, cb, dtb, alog, dsk, gn, *, lt, lc):
    B, S, _ = xbc.shape
    nt = S // lt
    assert S % lt == 0 and lt % lc == 0
    hs = C_HEADS * lc

    def expand(width, per_head):
        e = np.zeros((LANES, width), np.float32)
        for part in range(3):
            e[part * C_HEADS:(part + 1) * C_HEADS] = (np.arange(width)[None, :] // per_head
                                                      == np.arange(C_HEADS)[:, None])
        return jnp.asarray(e, dtype=BF16)

    es, ep = expand(hs, lc), expand(C_D_INNER, C_HEAD_DIM)
    pos = np.arange(lt)
    tri = jnp.asarray((pos[None, :] <= pos[:, None]) & (pos[None, :] // lc == pos[:, None] // lc), dtype=BF16)
    row = lambda c: pl.BlockSpec((1, lt, c), lambda b, t: (b, t, 0))
    per_b = lambda a: pl.BlockSpec((1,) + a.shape[1:], lambda b, t: (b,) + (0,) * (a.ndim - 1))
    full = lambda a: pl.BlockSpec(a.shape, lambda b, t: (0,) * a.ndim)
    return pl.pallas_call(
        functools.partial(_ssd_kernel, lt=lt, lc=lc, nt=nt),
        grid=(B, nt),
        in_specs=[row(C_CONV_DIM), row(C_D_INNER), row(LANES), per_b(prefix), per_b(h0t),
                  full(cw), full(cb), full(dtb), full(alog), full(dsk), full(gn), full(es), full(ep), full(tri)],
        out_specs=[row(C_D_INNER), per_b(h0t)],
        out_shape=[jax.ShapeDtypeStruct((B, S, C_D_INNER), BF16),
                   jax.ShapeDtypeStruct(h0t.shape, F32)],
        scratch_shapes=[pltpu.VMEM((lt + 8, C_CONV_DIM), F32),
                        pltpu.VMEM((lt, C_D_INNER), F32), pltpu.VMEM((lt, C_GN), F32), pltpu.VMEM((lt, C_GN), F32),
                        pltpu.VMEM((lt, hs), F32), pltpu.VMEM((lt, C_D_INNER), F32),
                        pltpu.VMEM((lt, C_D_INNER), F32), pltpu.VMEM((C_GN, C_D_INNER), F32)],
        compiler_params=_cparams("arbitrary", "arbitrary"),
        name="ssd",
    )(xbc, z, dtr, prefix, h0t, cw, cb, dtb, alog, dsk, gn, es, ep, tri)


def _outproj_kernel(ya_ref, yb_ref, yc_ref, x_ref, g1_ref, sh_ref, sc_ref, gf_ref, wo_ref, wr_ref, br_ref,
                    x1_ref, h2_ref, gate_ref, *, nb, r):
    R = nb * r
    D = x_ref.shape[-1]

    def in2(ref):
        return ref[...].reshape(R, ref.shape[-1])

    o = jnp.dot(in2(ya_ref), wo_ref[0:A_WIDTH, :], preferred_element_type=F32)
    o = o + jnp.dot(in2(yb_ref), wo_ref[A_WIDTH:A_WIDTH + B_WIDTH, :], preferred_element_type=F32)
    o = o + jnp.dot(in2(yc_ref), wo_ref[A_WIDTH + B_WIDTH:, :], preferred_element_type=F32)
    x1 = in2(x_ref) + _rows(g1_ref, nb, r) * o
    x1_ref[...] = x1.reshape(nb, r, D)
    h2 = x1 * lax.rsqrt(jnp.mean(x1 * x1, axis=-1, keepdims=True) + EPS) * gf_ref[...]
    h2 = h2 * (1.0 + _rows(sc_ref, nb, r)) + _rows(sh_ref, nb, r)
    h2_ref[...] = h2.astype(BF16).reshape(nb, r, D)

    h_hi = h2.astype(BF16)
    h_lo = (h2 - h_hi.astype(F32)).astype(BF16)
    wr = wr_ref[...]
    w_hi = wr.astype(BF16)
    w_lo = (wr - w_hi.astype(F32)).astype(BF16)
    lg = (jnp.dot(h_hi, w_hi, preferred_element_type=F32) + jnp.dot(h_hi, w_lo, preferred_element_type=F32)
          + jnp.dot(h_lo, w_hi, preferred_element_type=F32) + br_ref[...])
    lane = lax.broadcasted_iota(jnp.int32, (R, ROUTER_LANES), 1)
    big = jnp.int32(ROUTER_LANES)
    is_g = lane < E_GROUPS
    gl = jnp.where(is_g, lg, NEG_INF)
    gmax = jnp.max(gl, axis=-1, keepdims=True)
    p_top = 1.0 / jnp.sum(jnp.where(is_g, jnp.exp(gl - gmax), 0.0), axis=-1, keepdims=True)
    g_idx = jnp.min(jnp.where(is_g & (gl == gmax), lane, big), axis=-1, keepdims=True)
    e_lane = lane - ROUTER_E_OFF
    sel = (e_lane >= 0) & (e_lane < N_EXPERTS) & ((e_lane // E_PER_GROUP) == g_idx)
    l1 = jnp.where(sel, lg, NEG_INF)
    m1 = jnp.max(l1, axis=-1, keepdims=True)
    i1 = jnp.min(jnp.where(sel & (l1 == m1), lane, big), axis=-1, keepdims=True)
    sel2 = sel & (lane != i1)
    l2 = jnp.where(sel2, lg, NEG_INF)
    m2 = jnp.max(l2, axis=-1, keepdims=True)
    i2 = jnp.min(jnp.where(sel2 & (l2 == m2), lane, big), axis=-1, keepdims=True)
    e2 = jnp.exp(m2 - m1)
    w1 = p_top / (1.0 + e2)
    w2 = p_top * e2 / (1.0 + e2)
    gate = jnp.where(lane == i1, w1, 0.0) + jnp.where(lane == i2, w2, 0.0)
    gate_ref[...] = gate.reshape(nb, r, ROUTER_LANES)


def _outproj(ya, yb, yc, x, mod, g_ffn, wo, wr, br, *, nb, r):
    B, S, D = x.shape
    row = lambda c: pl.BlockSpec((nb, r, c), lambda b, i: (b, i, 0))
    modc = lambda j: pl.BlockSpec((nb, 1, D), lambda b, i: (b, 0, j))
    full = lambda a: pl.BlockSpec(a.shape, lambda b, i: (0,) * a.ndim)
    return pl.pallas_call(
        functools.partial(_outproj_kernel, nb=nb, r=r),
        grid=(B // nb, S // r),
        in_specs=[row(A_WIDTH), row(B_WIDTH), row(C_D_INNER), row(D), modc(2), modc(3), modc(4),
                  full(g_ffn), full(wo), full(wr), full(br)],
        out_specs=[row(D), row(D), row(ROUTER_LANES)],
        out_shape=[jax.ShapeDtypeStruct((B, S, D), F32), jax.ShapeDtypeStruct((B, S, D), BF16),
                   jax.ShapeDtypeStruct((B, S, ROUTER_LANES), F32)],
        compiler_params=_cparams("arbitrary", "arbitrary"),
        name="outproj",
    )(ya, yb, yc, x, mod, mod, mod, g_ffn, wo, wr, br)


def _moe_kernel(h_ref, gate_ref, x_ref, g2_ref, wgu_ref, wd_ref, o_ref, acc_sc, *, nb, r, ne):
    e = pl.program_id(2)
    R = nb * r
    D = x_ref.shape[-1]

    @pl.when(e == 0)
    def _():
        acc_sc[...] = jnp.zeros(acc_sc.shape, F32)

    h = h_ref[...].reshape(R, D)
    gu = jnp.dot(h, wgu_ref[0, 0], preferred_element_type=F32)
    lane = lax.broadcasted_iota(jnp.int32, (R, ROUTER_LANES), 1)
    gcol = jnp.sum(jnp.where(lane == e + ROUTER_E_OFF, gate_ref[...].reshape(R, ROUTER_LANES), 0.0),
                   axis=-1, keepdims=True)
    act = _silu(gu[:, 0:E_HIDDEN]) * gu[:, E_HIDDEN:] * gcol
    acc_sc[...] += jnp.dot(act.astype(BF16), wd_ref[0, 0], preferred_element_type=F32)

    @pl.when(e == ne - 1)
    def _():
        o_ref[...] = (x_ref[...].reshape(R, D) + _rows(g2_ref, nb, r) * acc_sc[...]).reshape(nb, r, D)


def _moe(h2, gate, x1, mod, wgu, wd, layer, *, nb, r):
    B, S, D = x1.shape
    ne = wgu.shape[1]
    row = lambda c: pl.BlockSpec((nb, r, c), lambda b, i, e: (b, i, 0))
    return pl.pallas_call(
        functools.partial(_moe_kernel, nb=nb, r=r, ne=ne),
        grid=(B // nb, S // r, ne),
        in_specs=[row(D), row(ROUTER_LANES), row(D),
                  pl.BlockSpec((nb, 1, D), lambda b, i, e: (b, 0, 5)),
                  pl.BlockSpec((1, 1, D, 2 * E_HIDDEN), lambda b, i, e: (layer, e, 0, 0)),
                  pl.BlockSpec((1, 1, E_HIDDEN, D), lambda b, i, e: (layer, e, 0, 0))],
        out_specs=row(D),
        out_shape=jax.ShapeDtypeStruct((B, S, D), F32),
        scratch_shapes=[pltpu.VMEM((nb * r, D), F32)],
        compiler_params=_cparams("arbitrary", "arbitrary", "arbitrary"),
        name="moe",
    )(h2, gate, x1, mod, wgu, wd)


def _final_kernel(x_ref, sh_ref, sc_ref, g_ref, o_ref, *, nb, r):
    R = nb * r
    D = x_ref.shape[-1]
    x = x_ref[...].reshape(R, D)
    y = x * lax.rsqrt(jnp.mean(x * x, axis=-1, keepdims=True) + EPS) * g_ref[...]
    o_ref[...] = (y * (1.0 + _rows(sc_ref, nb, r)) + _rows(sh_ref, nb, r)).reshape(nb, r, D)


def _final(x, modf, g, *, nb, r):
    B, S, D = x.shape
    row = pl.BlockSpec((nb, r, D), lambda b, i: (b, i, 0))
    return pl.pallas_call(
        functools.partial(_final_kernel, nb=nb, r=r),
        grid=(B // nb, S // r),
        in_specs=[row, pl.BlockSpec((nb, 1, D), lambda b, i: (b, 0, 0)),
                  pl.BlockSpec((nb, 1, D), lambda b, i: (b, 0, 1)),
                  pl.BlockSpec(g.shape, lambda b, i: (0, 0))],
        out_specs=row,
        out_shape=jax.ShapeDtypeStruct((B, S, D), F32),
        compiler_params=_cparams("arbitrary", "arbitrary"),
        name="final_norm",
    )(x, modf, modf, g)


def _rope_tables(pos):
    half = B_ROPE // 2
    inv = 1.0 / (ROPE_THETA ** (jnp.arange(half, dtype=F32) / half))
    ang = pos.astype(F32)[:, None] * inv[None, :]
    cos, sin = jnp.cos(ang), jnp.sin(ang)
    return (jnp.tile(jnp.concatenate([cos, cos], axis=1), (1, B_HEADS)),
            jnp.tile(jnp.concatenate([-sin, sin], axis=1), (1, B_HEADS)))


def _swap_halves(w):
    half = w.shape[-1] // 2
    return jnp.concatenate([w[..., half:], w[..., :half]], axis=-1)


def _layer_weights(l, w_in, b_g_q, b_w_uq, b_g_kv, b_w_uk, b_w_uv, w_out, moe_w_rg, moe_b_rg, moe_w_re, moe_b_re):
    D = w_in.shape[1]
    w = w_in[l]
    o_cq = 3 * A_WIDTH
    o_ckv = o_cq + B_Q_LORA
    o_kpe = o_ckv + B_KV_LORA
    o_z = o_kpe + B_ROPE
    o_xbc = o_z + C_D_INNER
    o_dt = o_xbc + C_CONV_DIM
    wkpe = w[:, o_kpe:o_z]
    zc = lambda n: jnp.zeros((D, n), F32)
    w1 = jnp.concatenate([w[:, 0:o_cq], w[:, o_cq:o_ckv], zc(256 - B_Q_LORA), w[:, o_ckv:o_kpe],
                          wkpe, _swap_halves(wkpe), w[:, o_dt:o_dt + C_HEADS],
                          zc(LANES - 2 * B_ROPE - C_HEADS), w[:, o_z:o_xbc], w[:, o_xbc:o_dt]],
                         axis=1).astype(BF16)
    assert w1.shape[1] == _W1_N
    gq = jnp.pad(b_g_q[l], (0, 256 - B_Q_LORA))[None, :]
    uq = b_w_uq[l]
    pe = uq[:, :, B_NOPE:]
    wuq = jnp.concatenate([uq[:, :, :B_NOPE].reshape(B_Q_LORA, -1), pe.reshape(B_Q_LORA, -1),
                           _swap_halves(pe).reshape(B_Q_LORA, -1)], axis=1)
    wuq = jnp.pad(wuq, ((0, 256 - B_Q_LORA), (0, 0))).astype(BF16)
    eye = jnp.eye(B_HEADS, dtype=F32)
    bd = (jnp.transpose(b_w_uk[l], (1, 2, 0))[:, :, None, :] * eye[:, None, :, None]).reshape(
        B_HEADS * B_NOPE, B_HEADS * B_KV_LORA).astype(BF16)
    wuv = (jnp.transpose(b_w_uv[l], (1, 0, 2))[:, :, None, :] * eye[:, None, :, None]).reshape(
        B_HEADS * B_KV_LORA, B_WIDTH).astype(BF16)
    wr = jnp.zeros((D, ROUTER_LANES), F32)
    wr = wr.at[:, 0:E_GROUPS].set(moe_w_rg[l]).at[:, ROUTER_E_OFF:ROUTER_E_OFF + N_EXPERTS].set(moe_w_re[l])
    br = jnp.zeros((1, ROUTER_LANES), F32)
    br = br.at[0, 0:E_GROUPS].set(moe_b_rg[l]).at[0, ROUTER_E_OFF:ROUTER_E_OFF + N_EXPERTS].set(moe_b_re[l])
    return dict(w1=w1, gq=gq, wuq=wuq, bd=bd, gkv=b_g_kv[l][None, :], wuv=wuv, wo=w_out[l].astype(BF16), wr=wr, br=br)


def _band_bias(rel_bias, nq, nk, back):
    k = np.arange(nk + nq - 1)
    idx = np.clip(back + nq - 1 - k, -A_REL_CLIP, A_REL_CLIP) + A_REL_CLIP
    line = rel_bias[:, idx]
    rows = [line[:, nq - 1 - i:nq - 1 - i + nk] for i in range(nq)]
    return jnp.stack(rows, axis=1).reshape(A_HEADS * nq, nk)


def _layer(x, mod, pos, lw, l, cache, prm, *, nb, r):
    B, S, D = x.shape
    cos_t, sin_t = _rope_tables(pos)
    (qkv, knew, vnew, qmla, kmla, ckv, kpe, z, xbc, dtr) = _inproj(
        x, mod, prm['g_mix'], lw['w1'], lw['gq'], lw['wuq'], lw['bd'], lw['gkv'], cos_t, sin_t, nb=nb, r=r)
    if cache is None:
        ya = _band_prompt(qkv, _band_bias(prm['a_rel_bias'], CHUNK, A_BAND, A_WINDOW))
        yb = _mla_prompt(qmla, kmla, lw['wuv'])
        prefix = jnp.zeros((B, C_CONV_W - 1, C_CONV_DIM), F32)
        h0t = jnp.zeros((B, C_HEADS, C_D_STATE, C_HEAD_DIM), F32)
        lt, lc = min(S, 512), SSD_CHUNK
    else:
        nr = cache['a_kt'].shape[3]
        ya = _band_sample(qkv, cache['a_kt'], cache['a_vt'], l, _band_bias(prm['a_rel_bias'], S, nr + S, nr))
        yb = _mla_sample(qmla, kmla, cache['b_ckv'], cache['b_kpet'], l, lw['wuv'])
        prefix = cache['c_conv'][l]
        h0t = jnp.swapaxes(cache['c_ssm'][l], -1, -2)
        lt = lc = min(SSD_CHUNK, S)
    yc, hft = _ssd(xbc, z, dtr, prefix, h0t, prm['c_conv_w'], prm['c_conv_b'], prm['c_dt_bias'], prm['c_a_log'],
                   prm['c_d_exp'], prm['c_g_norm'], lt=lt, lc=lc)
    conv_state = jnp.concatenate([prefix, xbc], axis=1)[:, -(C_CONV_W - 1):]
    x1, h2, gate = _outproj(ya, yb, yc, x, mod, prm['g_ffn'], lw['wo'], lw['wr'], lw['br'], nb=nb, r=r)
    r_moe = r if nb > 1 else min(S, 1024)
    xo = _moe(h2, gate, x1, mod, prm['moe_w_gu'], prm['moe_w_down'], l, nb=nb, r=r_moe)
    states = (knew.reshape(B, -1, A_HEADS, A_HEAD_DIM), vnew.reshape(B, -1, A_HEADS, A_HEAD_DIM),
              ckv, kpe, conv_state, jnp.swapaxes(hft, -1, -2))
    return xo, states


def kernel(x_prompt, x_sample, c_prompt, c_sample, cache_a_k, cache_a_v, cache_b_ckv, cache_b_kpe,
           state_c_conv, state_c_ssm, w_ada, b_ada, g_mix, w_in, a_rel_bias, b_g_q, b_w_uq, b_g_kv,
           b_w_uk, b_w_uv, c_conv_w, c_conv_b, c_dt_bias, c_a_log, c_d, c_g_norm, w_out, g_ffn,
           moe_w_rg, moe_b_rg, moe_w_re, moe_b_re, moe_w_gu, moe_w_down, g_final, w_ada_f, b_ada_f):
    depth = w_in.shape[0]
    D = x_prompt.shape[-1]
    Bp, Sp, _ = x_prompt.shape
    Bs, Ss, _ = x_sample.shape
    past = cache_b_ckv.shape[2]
    pos_p = jnp.arange(Sp)
    pos_s = past + jnp.arange(Ss)

    nc = Bp + Bs
    ncp = -(-nc // 8) * 8
    c_all = jnp.pad(jnp.concatenate([c_prompt, c_sample], axis=0), ((0, ncp - nc), (0, 0)))
    mod = _ada(c_all, w_ada, b_ada[:, None, :])
    modf = _ada(c_all, w_ada_f[None], b_ada_f[None, None, :])[0]

    nr = cache_a_k.shape[2]
    to_t = lambda c: jnp.transpose(c, (0, 1, 3, 4, 2)).reshape(depth, Bs, A_WIDTH, nr)
    cache = {'a_kt': to_t(cache_a_k), 'a_vt': to_t(cache_a_v), 'b_ckv': cache_b_ckv,
             'b_kpet': jnp.swapaxes(cache_b_kpe, 2, 3), 'c_conv': state_c_conv, 'c_ssm': state_c_ssm}

    rp = min(Sp, A_WINDOW)
    moe_w_gu = moe_w_gu.astype(BF16)
    moe_w_down = moe_w_down.astype(BF16)
    xp, xs = x_prompt, x_sample
    st_p, st_s = [], []
    for l in range(depth):
        lw = _layer_weights(l, w_in, b_g_q, b_w_uq, b_g_kv, b_w_uk, b_w_uv, w_out,
                            moe_w_rg, moe_b_rg, moe_w_re, moe_b_re)
        prm = {'g_mix': g_mix[l][None, :], 'a_rel_bias': a_rel_bias[l], 'c_conv_w': c_conv_w[l],
               'c_conv_b': c_conv_b[l][None, :], 'c_dt_bias': jnp.pad(c_dt_bias[l], (0, LANES - C_HEADS))[None, :],
               'c_a_log': jnp.pad(c_a_log[l], (0, LANES - C_HEADS))[None, :],
               'c_d_exp': jnp.repeat(c_d[l], C_HEAD_DIM)[None, :], 'c_g_norm': c_g_norm[l][None, :],
               'g_ffn': g_ffn[l][None, :], 'moe_w_gu': moe_w_gu, 'moe_w_down': moe_w_down}
        xp, sp = _layer(xp, mod[l, 0:Bp][:, None, :], pos_p, lw, l, None, prm, nb=1, r=rp)
        xs, ss = _layer(xs, mod[l, Bp:nc][:, None, :], pos_s, lw, l, cache, prm, nb=Bs, r=Ss)
        st_p.append(sp)
        st_s.append(ss)
    gf = g_final[None, :]
    y_prompt = _final(xp, modf[0:Bp][:, None, :], gf, nb=1, r=rp)
    y_sample = _final(xs, modf[Bp:nc][:, None, :], gf, nb=Bs, r=Ss)
    stack = lambda st, k: jnp.stack([t[k] for t in st])
    return ((y_prompt, y_sample) + tuple(stack(st_p, k) for k in range(6))
            + tuple(stack(st_s, k) for k in range(6)))
```

```python
import functools
import math

import numpy as np
import jax
import jax.numpy as jnp
from jax import lax
from jax.experimental import pallas as pl
from jax.experimental.pallas import tpu as pltpu

F32 = jnp.float32
BF16 = jnp.bfloat16

D_MODEL = 1024
CHUNK = 64
EPS = 1e-6
NEG_INF = -1e30

A_HEADS = 4
A_HEAD_DIM = 64
A_WIDTH = A_HEADS * A_HEAD_DIM
A_BACK = 8
A_BAND = (A_BACK + 1) * CHUNK
A_WINDOW = A_BACK * CHUNK
A_REL_CLIP = 128
A_SCALE = A_HEAD_DIM ** -0.5

B_HEADS = 4
B_Q_LORA = 192
B_KV_LORA = 128
B_NOPE = 64
B_ROPE = 32
B_V = 64
B_WIDTH = B_HEADS * B_V
B_SCALE = (B_NOPE + B_ROPE) ** -0.5
B_QK = B_KV_LORA + B_ROPE
ROPE_THETA = 10000.0

C_D_INNER = 512
C_HEAD_DIM = 64
C_HEADS = C_D_INNER // C_HEAD_DIM
C_GROUPS = 2
C_D_STATE = 64
C_CONV_W = 4
C_CONV_DIM = C_D_INNER + 2 * C_GROUPS * C_D_STATE
C_GN = C_GROUPS * C_D_STATE
C_HPG = C_HEADS // C_GROUPS
SSD_CHUNK = 64

E_GROUPS = 4
E_PER_GROUP = 4
N_EXPERTS = E_GROUPS * E_PER_GROUP
E_HIDDEN = 256
ROUTER_LANES = 128
ROUTER_E_OFF = 16

LANES = 128
VMEM_LIMIT = 56 * 1024 * 1024

B_QSCALE = B_SCALE * math.log2(math.e)

_W1_QKV = 0
_W1_CQ = 3 * A_WIDTH
_W1_CKV = _W1_CQ + 256
_W1_KPE = _W1_CKV + B_KV_LORA
_W1_Z = _W1_KPE + LANES
_W1_XBC = _W1_Z + C_D_INNER
_W1_N = _W1_XBC + C_CONV_DIM


def _cparams(*sem):
    return pltpu.CompilerParams(dimension_semantics=sem, vmem_limit_bytes=VMEM_LIMIT)


def _silu(v):
    return v * jax.nn.sigmoid(v)


def _split3_lanes(v, lane):
    x3 = v + pltpu.roll(v, C_HEADS, axis=1) + pltpu.roll(v, 2 * C_HEADS, axis=1)
    hi = x3.astype(BF16).astype(F32)
    r1 = x3 - hi
    mid = r1.astype(BF16).astype(F32)
    lo = r1 - mid
    return jnp.where(lane < C_HEADS, hi, jnp.where(lane < 2 * C_HEADS, mid, lo)).astype(BF16)


def _ada_kernel(c_ref, w_ref, b_ref, o_ref):
    s = _silu(c_ref[...]).astype(BF16)
    o_ref[0] = jnp.dot(s, w_ref[0].astype(BF16), preferred_element_type=F32) + b_ref[0]


def _ada(c_all, w, b):
    nl, d, n = w.shape
    bp = c_all.shape[0]
    tn = 1024
    return pl.pallas_call(
        _ada_kernel,
        grid=(nl, n // tn),
        in_specs=[pl.BlockSpec((bp, d), lambda l, j: (0, 0)),
                  pl.BlockSpec((1, d, tn), lambda l, j: (l, 0, j)),
                  pl.BlockSpec((1, 1, tn), lambda l, j: (l, 0, j))],
        out_specs=pl.BlockSpec((1, bp, tn), lambda l, j: (l, 0, j)),
        out_shape=jax.ShapeDtypeStruct((nl, bp, n), F32),
        compiler_params=_cparams("arbitrary", "arbitrary"),
        name="ada",
    )(c_all, w, b)


def _rows(ref_or_val, nb, r):
    v = ref_or_val[...]
    return jnp.broadcast_to(v, (nb, r, v.shape[-1])).reshape(nb * r, v.shape[-1])


def _inproj_kernel(x_ref, sh_ref, sc_ref, g_ref, w1_ref, gq_ref, wuq_ref, bd_ref, gkv_ref, cos_ref, sin_ref,
                   qkv_ref, knew_ref, vnew_ref, qmla_ref, kmla_ref, ckv_ref, kpe_ref, z_ref, xbc_ref, dtr_ref,
                   *, nb, r, keep_tile):
    R = nb * r
    D = x_ref.shape[-1]

    def out3(v):
        return v.reshape(nb, r, v.shape[-1])

    x = x_ref[...].reshape(R, D)
    h = x * lax.rsqrt(jnp.mean(x * x, axis=-1, keepdims=True) + EPS) * g_ref[...]
    h = h * (1.0 + _rows(sc_ref, nb, r)) + _rows(sh_ref, nb, r)
    u = jnp.dot(h.astype(BF16), w1_ref[...], preferred_element_type=F32)

    qkv_ref[:, :, 0:A_WIDTH] = out3((u[:, 0:A_WIDTH] * A_SCALE).astype(BF16))
    qkv_ref[:, :, A_WIDTH:3 * A_WIDTH] = out3(u[:, A_WIDTH:3 * A_WIDTH].astype(BF16))

    @pl.when(pl.program_id(1) == keep_tile)
    def _():
        knew_ref[...] = out3(u[:, A_WIDTH:2 * A_WIDTH])
        vnew_ref[...] = out3(u[:, 2 * A_WIDTH:3 * A_WIDTH])

    blk = u[:, _W1_CQ:_W1_CQ + 256]
    lane = lax.broadcasted_iota(jnp.int32, (1, 256), 1)
    cq = jnp.where(lane < B_Q_LORA, blk, 0.0)
    cqn = cq * lax.rsqrt(jnp.sum(cq * cq, axis=-1, keepdims=True) * (1.0 / B_Q_LORA) + EPS) * gq_ref[...]
    qb = jnp.dot(cqn.astype(BF16), wuq_ref[...], preferred_element_type=F32)
    nope_w = B_HEADS * B_NOPE
    rope_w = B_HEADS * B_ROPE
    qlat = jnp.dot(qb[:, 0:nope_w].astype(BF16), bd_ref[...], preferred_element_type=F32) * B_QSCALE
    cosr = _rows(cos_ref, nb, r)
    sinr = _rows(sin_ref, nb, r)
    qpe = (qb[:, nope_w:nope_w + rope_w] * cosr + qb[:, nope_w + rope_w:nope_w + 2 * rope_w] * sinr) * B_QSCALE
    for hd in range(B_HEADS):
        qmla_ref[:, hd, :, 0:B_KV_LORA] = out3(qlat[:, hd * B_KV_LORA:(hd + 1) * B_KV_LORA].astype(BF16))
        qmla_ref[:, hd, :, B_KV_LORA:B_QK] = out3(qpe[:, hd * B_ROPE:(hd + 1) * B_ROPE].astype(BF16))

    cr = u[:, _W1_CKV:_W1_CKV + B_KV_LORA]
    ckv = cr * lax.rsqrt(jnp.mean(cr * cr, axis=-1, keepdims=True) + EPS) * gkv_ref[...]
    ckv_ref[...] = out3(ckv)
    kmla_ref[:, :, 0:B_KV_LORA] = out3(ckv.astype(BF16))
    kb = u[:, _W1_KPE:_W1_KPE + LANES]
    rot = kb * cosr + pltpu.roll(kb, LANES - B_ROPE, axis=1) * sinr
    kpe_ref[...] = out3(rot[:, 0:B_ROPE])
    kmla_ref[:, :, B_KV_LORA:B_QK] = out3(rot[:, 0:B_ROPE].astype(BF16))
    lane1 = lax.broadcasted_iota(jnp.int32, (1, LANES), 1)
    dtr_ref[...] = out3(jnp.where(lane1 < C_HEADS, pltpu.roll(kb, LANES - 2 * B_ROPE, axis=1), 0.0))

    z_ref[...] = out3(u[:, _W1_Z:_W1_Z + C_D_INNER].astype(BF16))
    xbc_ref[...] = out3(u[:, _W1_XBC:_W1_XBC + C_CONV_DIM])


def _inproj(x, mod, g_mix, w1, gq, wuq, bd, gkv, cos_t, sin_t, *, nb, r):
    B, S, D = x.shape
    nbt, nst = B // nb, S // r
    keep = min(S, A_WINDOW)
    assert keep == r and B % nb == 0 and S % r == 0
    grid = (nbt, nst)
    row = lambda c: pl.BlockSpec((nb, r, c), lambda b, i: (b, i, 0))
    full = lambda a: pl.BlockSpec(a.shape, lambda b, i: (0,) * a.ndim)
    in_specs = [row(D),
                pl.BlockSpec((nb, 1, D), lambda b, i: (b, 0, 0)),
                pl.BlockSpec((nb, 1, D), lambda b, i: (b, 0, 1)),
                full(g_mix), full(w1), full(gq), full(wuq), full(bd), full(gkv),
                pl.BlockSpec((r, LANES), lambda b, i: (i, 0)),
                pl.BlockSpec((r, LANES), lambda b, i: (i, 0))]
    out_shape = [jax.ShapeDtypeStruct((B, S, 3 * A_WIDTH), BF16),
                 jax.ShapeDtypeStruct((B, keep, A_WIDTH), F32),
                 jax.ShapeDtypeStruct((B, keep, A_WIDTH), F32),
                 jax.ShapeDtypeStruct((B, B_HEADS, S, B_QK), BF16),
                 jax.ShapeDtypeStruct((B, S, B_QK), BF16),
                 jax.ShapeDtypeStruct((B, S, B_KV_LORA), F32),
                 jax.ShapeDtypeStruct((B, S, B_ROPE), F32),
                 jax.ShapeDtypeStruct((B, S, C_D_INNER), BF16),
                 jax.ShapeDtypeStruct((B, S, C_CONV_DIM), F32),
                 jax.ShapeDtypeStruct((B, S, LANES), F32)]
    out_specs = [row(3 * A_WIDTH),
                 pl.BlockSpec((nb, keep, A_WIDTH), lambda b, i: (b, 0, 0)),
                 pl.BlockSpec((nb, keep, A_WIDTH), lambda b, i: (b, 0, 0)),
                 pl.BlockSpec((nb, B_HEADS, r, B_QK), lambda b, i: (b, 0, i, 0)),
                 row(B_QK), row(B_KV_LORA), row(B_ROPE), row(C_D_INNER), row(C_CONV_DIM), row(LANES)]
    return pl.pallas_call(
        functools.partial(_inproj_kernel, nb=nb, r=r, keep_tile=nst - 1),
        grid=grid, in_specs=in_specs, out_specs=out_specs, out_shape=out_shape,
        compiler_params=_cparams("arbitrary", "arbitrary"),
        name="inproj",
    )(x, mod, mod, g_mix, w1, gq, wuq, bd, gkv, cos_t, sin_t)


def _band_core(q, k, v, bias, valid, hmask):
    nq = q.shape[0]
    qs = jnp.where(hmask, jnp.concatenate([q] * A_HEADS, axis=0), jnp.zeros((), BF16))
    s = lax.dot_general(qs, k, (((1,), (1,)), ((), ())), preferred_element_type=F32) + bias
    if valid is not None:
        s = jnp.where(valid, s, NEG_INF)
    m = jnp.max(s, axis=-1, keepdims=True)
    e = jnp.exp(s - m)
    den = jnp.sum(e, axis=-1, keepdims=True)
    o = jnp.dot(e.astype(BF16), v, preferred_element_type=F32)
    o = jnp.where(hmask, o / den, 0.0)
    y = o[0:nq]
    for hd in range(1, A_HEADS):
        y = y + o[hd * nq:(hd + 1) * nq]
    return y


def _head_mask(nq):
    row = lax.broadcasted_iota(jnp.int32, (A_HEADS * nq, A_WIDTH), 0)
    col = lax.broadcasted_iota(jnp.int32, (A_HEADS * nq, A_WIDTH), 1)
    return (row // nq) == (col // A_HEAD_DIM)


def _band_prompt_kernel(cur_ref, prev_ref, bias_ref, o_ref, kwin, vwin, *, tq):
    i = pl.program_id(1)
    nch = tq // CHUNK
    kwin[0:A_WINDOW] = prev_ref[0, tq - A_WINDOW:tq, A_WIDTH:2 * A_WIDTH]
    kwin[A_WINDOW:A_WINDOW + tq] = cur_ref[0, :, A_WIDTH:2 * A_WIDTH]
    vwin[0:A_WINDOW] = prev_ref[0, tq - A_WINDOW:tq, 2 * A_WIDTH:3 * A_WIDTH]
    vwin[A_WINDOW:A_WINDOW + tq] = cur_ref[0, :, 2 * A_WIDTH:3 * A_WIDTH]
    hmask = _head_mask(CHUNK)
    ucol = lax.broadcasted_iota(jnp.int32, (1, A_BAND), 1)
    for c in range(nch):
        q = cur_ref[0, c * CHUNK:(c + 1) * CHUNK, 0:A_WIDTH]
        valid = (ucol // CHUNK + (i * nch + c - A_BACK)) >= 0
        y = _band_core(q, kwin[c * CHUNK:c * CHUNK + A_BAND], vwin[c * CHUNK:c * CHUNK + A_BAND],
                       bias_ref[...], valid, hmask)
        o_ref[0, c * CHUNK:(c + 1) * CHUNK, :] = y.astype(BF16)


def _band_prompt(qkv, bias_stack):
    B, S, _ = qkv.shape
    tq = A_WINDOW
    assert S % tq == 0
    return pl.pallas_call(
        functools.partial(_band_prompt_kernel, tq=tq),
        grid=(B, S // tq),
        in_specs=[pl.BlockSpec((1, tq, 3 * A_WIDTH), lambda b, i: (b, i, 0)),
                  pl.BlockSpec((1, tq, 3 * A_WIDTH), lambda b, i: (b, jnp.maximum(i - 1, 0), 0)),
                  pl.BlockSpec(bias_stack.shape, lambda b, i: (0, 0))],
        out_specs=pl.BlockSpec((1, tq, A_WIDTH), lambda b, i: (b, i, 0)),
        out_shape=jax.ShapeDtypeStruct((B, S, A_WIDTH), BF16),
        scratch_shapes=[pltpu.VMEM((A_WINDOW + tq, A_WIDTH), BF16), pltpu.VMEM((A_WINDOW + tq, A_WIDTH), BF16)],
        compiler_params=_cparams("arbitrary", "arbitrary"),
        name="band_prompt",
    )(qkv, qkv, bias_stack)


def _band_sample_kernel(qkv_ref, kt_ref, vt_ref, biasc_ref, biasn_ref, o_ref, *, nbb, ds):
    hmask = _head_mask(ds)
    nt_dims = (((1,), (1,)), ((), ()))
    for b in range(nbb):
        q = qkv_ref[b, :, 0:A_WIDTH]
        kn = qkv_ref[b, :, A_WIDTH:2 * A_WIDTH]
        vn = qkv_ref[b, :, 2 * A_WIDTH:3 * A_WIDTH]
        qs = jnp.where(hmask, jnp.concatenate([q] * A_HEADS, axis=0), jnp.zeros((), BF16))
        sc = jnp.dot(qs, kt_ref[0, b].astype(BF16), preferred_element_type=F32) + biasc_ref[...]
        sn = lax.dot_general(qs, kn, nt_dims, preferred_element_type=F32) + biasn_ref[...]
        m = jnp.maximum(jnp.max(sc, axis=-1, keepdims=True), jnp.max(sn, axis=-1, keepdims=True))
        ec = jnp.exp(sc - m)
        en = jnp.exp(sn - m)
        den = jnp.sum(ec, axis=-1, keepdims=True) + jnp.sum(en, axis=-1, keepdims=True)
        o = lax.dot_general(ec.astype(BF16), vt_ref[0, b].astype(BF16), nt_dims, preferred_element_type=F32)
        o = o + jnp.dot(en.astype(BF16), vn, preferred_element_type=F32)
        o = jnp.where(hmask, o / den, 0.0)
        y = o[0:ds]
        for hd in range(1, A_HEADS):
            y = y + o[hd * ds:(hd + 1) * ds]
        o_ref[b] = y.astype(BF16)


def _band_sample(qkv, cache_kt, cache_vt, layer, bias_stack):
    B, ds, _ = qkv.shape
    nr = cache_kt.shape[3]
    nbb = 4
    assert B % nbb == 0
    bias_c, bias_n = bias_stack[:, 0:nr], bias_stack[:, nr:]
    return pl.pallas_call(
        functools.partial(_band_sample_kernel, nbb=nbb, ds=ds),
        grid=(B // nbb,),
        in_specs=[pl.BlockSpec((nbb, ds, 3 * A_WIDTH), lambda b: (b, 0, 0)),
                  pl.BlockSpec((1, nbb, A_WIDTH, nr), lambda b: (layer, b, 0, 0)),
                  pl.BlockSpec((1, nbb, A_WIDTH, nr), lambda b: (layer, b, 0, 0)),
                  pl.BlockSpec(bias_c.shape, lambda b: (0, 0)),
                  pl.BlockSpec(bias_n.shape, lambda b: (0, 0))],
        out_specs=pl.BlockSpec((nbb, ds, A_WIDTH), lambda b: (b, 0, 0)),
        out_shape=jax.ShapeDtypeStruct((B, ds, A_WIDTH), BF16),
        compiler_params=_cparams("arbitrary"),
        name="band_sample",
    )(qkv, cache_kt, cache_vt, bias_c, bias_n)


def _mla_finish(acc, den, wuv_ref, bq):
    o = acc / den
    o_all = jnp.concatenate([o[hd * bq:(hd + 1) * bq] for hd in range(B_HEADS)], axis=1)
    return jnp.dot(o_all.astype(BF16), wuv_ref[...], preferred_element_type=F32)


def _mla_prompt_kernel(qi_ref, kj_ref, q_ref, k_ref, vt_ref, wuvt_ref, o_ref, m_sc, l_sc, acc_sc, *, bq, bk):
    p = pl.program_id(1)
    i = qi_ref[p]
    j = kj_ref[p]
    M = B_HEADS * bq

    @pl.when(j == 0)
    def _():
        m_sc[...] = jnp.full(m_sc.shape, NEG_INF, F32)
        l_sc[...] = jnp.zeros(l_sc.shape, F32)
        acc_sc[...] = jnp.zeros(acc_sc.shape, F32)

    def step(masked):
        q = q_ref[0].reshape(M, B_QK)
        st = lax.dot_general(k_ref[0], q, (((1,), (1,)), ((), ())), preferred_element_type=F32)
        if masked:
            row = lax.broadcasted_iota(jnp.int32, (bk, M), 0)
            col = lax.broadcasted_iota(jnp.int32, (bk, M), 1)
            st = jnp.where((row // CHUNK) <= ((col % bq) // CHUNK), st, NEG_INF)
        m_prev = m_sc[...]
        m_new = jnp.maximum(m_prev, jnp.max(st, axis=0, keepdims=True))
        alpha = jnp.exp2(m_prev - m_new)
        pt = jnp.exp2(st - m_new)
        l_sc[...] = alpha * l_sc[...] + jnp.sum(pt, axis=0, keepdims=True)
        acc_sc[...] = acc_sc[...] * alpha + jnp.dot(vt_ref[0], pt.astype(BF16), preferred_element_type=F32)
        m_sc[...] = m_new

    @pl.when(j < i)
    def _():
        step(False)

    @pl.when(j == i)
    def _():
        step(True)
        o = acc_sc[...] / l_sc[...]
        o_all = jnp.concatenate([o[:, hd * bq:(hd + 1) * bq] for hd in range(B_HEADS)], axis=0)
        o_ref[0] = jnp.dot(wuvt_ref[...], o_all.astype(BF16), preferred_element_type=F32).astype(BF16)


def _mla_prompt(qmla, kmla, wuv_bd):
    B, S = kmla.shape[0:2]
    bq = bk = 512
    assert S % bq == 0 and bq % CHUNK == 0
    nq = S // bq
    qi = np.concatenate([np.full(i + 1, i, np.int32) for i in range(nq)])
    kj = np.concatenate([np.arange(i + 1, dtype=np.int32) for i in range(nq)])
    vt = jnp.swapaxes(kmla[:, :, 0:B_KV_LORA], 1, 2)
    wuvt = wuv_bd.T
    grid_spec = pltpu.PrefetchScalarGridSpec(
        num_scalar_prefetch=2,
        grid=(B, len(qi)),
        in_specs=[pl.BlockSpec((1, B_HEADS, bq, B_QK), lambda b, p, qi, kj: (b, 0, qi[p], 0)),
                  pl.BlockSpec((1, bk, B_QK), lambda b, p, qi, kj: (b, kj[p], 0)),
                  pl.BlockSpec((1, B_KV_LORA, bk), lambda b, p, qi, kj: (b, 0, kj[p])),
                  pl.BlockSpec(wuvt.shape, lambda b, p, qi, kj: (0, 0))],
        out_specs=pl.BlockSpec((1, B_WIDTH, bq), lambda b, p, qi, kj: (b, 0, qi[p])),
        scratch_shapes=[pltpu.VMEM((1, B_HEADS * bq), F32), pltpu.VMEM((1, B_HEADS * bq), F32),
                        pltpu.VMEM((B_KV_LORA, B_HEADS * bq), F32)])
    ybt = pl.pallas_call(
        functools.partial(_mla_prompt_kernel, bq=bq, bk=bk),
        grid_spec=grid_spec,
        out_shape=jax.ShapeDtypeStruct((B, B_WIDTH, S), BF16),
        compiler_params=_cparams("arbitrary", "arbitrary"),
        name="mla_prompt",
    )(jnp.asarray(qi), jnp.asarray(kj), qmla, kmla, vt, wuvt)
    return jnp.swapaxes(ybt, 1, 2)


def _mla_sample_kernel(q_ref, kn_ref, ckv_ref, kpet_ref, wuv_ref, o_ref, *, ds):
    M = B_HEADS * ds
    nt_dims = (((1,), (1,)), ((), ()))
    q = q_ref[0].reshape(M, B_QK)
    kn = kn_ref[0]
    ckv = ckv_ref[0, 0].astype(BF16)
    sc = (lax.dot_general(q[:, 0:B_KV_LORA], ckv, nt_dims, preferred_element_type=F32)
          + jnp.dot(q[:, B_KV_LORA:B_QK], kpet_ref[0, 0].astype(BF16), preferred_element_type=F32))
    sn = lax.dot_general(q, kn, nt_dims, preferred_element_type=F32)
    m = jnp.maximum(jnp.max(sc, axis=1, keepdims=True), jnp.max(sn, axis=1, keepdims=True))
    pc = jnp.exp2(sc - m)
    pn = jnp.exp2(sn - m)
    den = jnp.sum(pc, axis=1, keepdims=True) + jnp.sum(pn, axis=1, keepdims=True)
    acc = (jnp.dot(pc.astype(BF16), ckv, preferred_element_type=F32)
           + jnp.dot(pn.astype(BF16), kn[:, 0:B_KV_LORA], preferred_element_type=F32))
    o_ref[0] = _mla_finish(acc, den, wuv_ref, ds).astype(BF16)


def _mla_sample(qmla, kmla, cache_ckv, cache_kpet, layer, wuv_bd):
    B, ds = kmla.shape[0:2]
    past = cache_ckv.shape[2]
    return pl.pallas_call(
        functools.partial(_mla_sample_kernel, ds=ds),
        grid=(B,),
        in_specs=[pl.BlockSpec((1, B_HEADS, ds, B_QK), lambda b: (b, 0, 0, 0)),
                  pl.BlockSpec((1, ds, B_QK), lambda b: (b, 0, 0)),
                  pl.BlockSpec((1, 1, past, B_KV_LORA), lambda b: (layer, b, 0, 0)),
                  pl.BlockSpec((1, 1, B_ROPE, past), lambda b: (layer, b, 0, 0)),
                  pl.BlockSpec(wuv_bd.shape, lambda b: (0, 0))],
        out_specs=pl.BlockSpec((1, ds, B_WIDTH), lambda b: (b, 0, 0)),
        out_shape=jax.ShapeDtypeStruct((B, ds, B_WIDTH), BF16),
        compiler_params=_cparams("arbitrary"),
        name="mla_sample",
    )(qmla, kmla, cache_ckv, cache_kpet, wuv_bd)


def _ssd_kernel(xbc_ref, z_ref, dtr_ref, pre_ref, h0_ref, cw_ref, cb_ref, dtb_ref, alog_ref, dsk_ref, gn_ref,
                es_ref, ep_ref, tri_ref, y_ref, hout_ref,
                cbuf, xs_sc, b_sc, c_sc, ces_sc, cep_sc, dep_sc, st_sc, *, nbat, lt, lc, nt):
    t = pl.program_id(1)
    nch = lt // lc
    HS = C_HEADS * lc
    gw = C_HPG * C_HEAD_DIM

    @pl.when(t == 0)
    def _():
        st_sc[...] = jnp.zeros(st_sc.shape, F32)
        for b in range(nbat):
            cbuf[b, 8 - (C_CONV_W - 1):8] = pre_ref[b]
            for g in range(C_GROUPS):
                blk = jnp.concatenate([h0_ref[b, g * C_HPG + hl] for hl in range(C_HPG)], axis=1)
                st_sc[b, g * C_D_STATE:(g + 1) * C_D_STATE, g * gw:(g + 1) * gw] = blk

    hl = lax.broadcasted_iota(jnp.int32, (lt, LANES), 1)
    for b in range(nbat):
        cbuf[b, 8:8 + lt] = xbc_ref[b]
        acc = jnp.broadcast_to(cb_ref[...], (lt, C_CONV_DIM))
        for kk in range(C_CONV_W):
            off = 8 - (C_CONV_W - 1) + kk
            acc = acc + cbuf[b, off:off + lt] * cw_ref[kk:kk + 1, :]
        tail = cbuf[b, 8 + lt - (C_CONV_W - 1):8 + lt]
        cbuf[b, 8 - (C_CONV_W - 1):8] = tail
        xc = _silu(acc)
        xs_sc[b] = xc[:, 0:C_D_INNER]
        b_sc[b] = xc[:, C_D_INNER:C_D_INNER + C_GN]
        c_sc[b] = xc[:, C_D_INNER + C_GN:C_D_INNER + 2 * C_GN]

        dt = jnp.where(hl < C_HEADS, jax.nn.softplus(dtr_ref[b] + dtb_ref[...]), 0.0)
        da = dt * (-jnp.exp(alog_ref[...]))
        c3 = jnp.dot(tri_ref[...], _split3_lanes(da, hl), preferred_element_type=F32)
        cum = jnp.where(hl < C_HEADS, c3 + pltpu.roll(c3, LANES - C_HEADS, axis=1)
                        + pltpu.roll(c3, LANES - 2 * C_HEADS, axis=1), 0.0)
        cum3 = _split3_lanes(cum, hl)
        cep = jnp.dot(cum3, ep_ref[...], preferred_element_type=F32)
        cep_sc[b] = cep
        ces_sc[b] = cep if HS == C_D_INNER and lc == C_HEAD_DIM else jnp.dot(cum3, es_ref[...],
                                                                           preferred_element_type=F32)
        dep_sc[b] = jnp.dot(_split3_lanes(dt, hl), ep_ref[...], preferred_element_type=F32)

    r_s = lax.broadcasted_iota(jnp.int32, (lc, HS), 0)
    l_s = lax.broadcasted_iota(jnp.int32, (lc, HS), 1)
    eye_t = (l_s % lc) == r_s
    causal_t = (l_s % lc) <= r_s
    r_w = lax.broadcasted_iota(jnp.int32, (HS, C_GN), 0)
    l_w = lax.broadcasted_iota(jnp.int32, (HS, C_GN), 1)
    gmask = (r_w // (C_HPG * lc)) == (l_w // C_D_STATE)
    r_b =lax.broadcasted_iota(jnp.int32, (C_HPG * lc, gw), 0)
    l_b = lax.broadcasted_iota(jnp.int32, (C_HPG * lc, gw), 1)
    bmask = (r_b // lc) == (l_b // C_HEAD_DIM)
    r_g = lax.broadcasted_iota(jnp.int32, (C_GN, C_D_INNER), 0)
    l_g = lax.broadcasted_iota(jnp.int32, (C_GN, C_D_INNER), 1)
    smask = (r_g // C_D_STATE) == (l_g // gw)

    def chunk_one(b, rows):
        ce_s = ces_sc[b, rows, :]
        ce_p = cep_sc[b, rows, :]
        de_p = dep_sc[b, rows, :]
        xs = xs_sc[b, rows, :]
        bm = b_sc[b, rows, :]
        cm = c_sc[b, rows, :]
        rflat = jnp.sum(jnp.where(eye_t, ce_s, 0.0), axis=0, keepdims=True)
        lcat = jnp.exp(jnp.where(causal_t, ce_s - rflat, NEG_INF))
        w_nt = jnp.where(gmask, jnp.concatenate([bm] * C_HEADS, axis=0), 0.0).astype(BF16)
        cbcat = lax.dot_general(cm.astype(BF16), w_nt, (((1,), (1,)), ((), ())), preferred_element_type=F32)
        mcat = (cbcat * lcat).astype(BF16)
        xdt = xs * de_p
        xdt_b = xdt.astype(BF16)
        ydiag = []
        for g in range(C_GROUPS):
            bdx = jnp.where(bmask, jnp.concatenate([xdt_b[:, g * gw:(g + 1) * gw]] * C_HPG, axis=0),
                            jnp.zeros((), BF16))
            ydiag.append(jnp.dot(mcat[:, g * C_HPG * lc:(g + 1) * C_HPG * lc], bdx, preferred_element_type=F32))
        y = jnp.concatenate(ydiag, axis=1)
        st = st_sc[b]
        y = y + jnp.dot(cm.astype(BF16), st.astype(BF16), preferred_element_type=F32) * jnp.exp(ce_p)
        y = y + dsk_ref[...] * xs
        last = ce_p[lc - 1:lc, :]
        xw = (xdt * jnp.exp(last - ce_p)).astype(BF16)
        upd = lax.dot_general(bm.astype(BF16), xw, (((0,), (0,)), ((), ())), preferred_element_type=F32)
        st_sc[b] = jnp.exp(last) * st + jnp.where(smask, upd, 0.0)
        y = y * _silu(z_ref[b, rows, :].astype(F32))
        y = y * lax.rsqrt(jnp.mean(y * y, axis=-1, keepdims=True) + EPS) * gn_ref[...]
        y_ref[b, rows, :] = y.astype(BF16)

    def chunk(c, carry):
        rows = pl.ds(pl.multiple_of(c * lc, lc), lc)
        for b in range(nbat):
            chunk_one(b, rows)
        return carry

    lax.fori_loop(0, nch, chunk, 0)

    @pl.when(t == nt - 1)
    def _():
        for b in range(nbat):
            for hd in range(C_HEADS):
                g = hd // C_HPG
                hout_ref[b, hd] = st_sc[b, g * C_D_STATE:(g + 1) * C_D_STATE,
                                        hd * C_HEAD_DIM:(hd + 1) * C_HEAD_DIM]


def _ssd(xbc, z, dtr, prefix, h0t, cw, cb, dtb, alog, dsk, gn, *, nbat, lt, lc):
    B, S, _ = xbc.shape
    nt = S // lt
    assert S % lt == 0 and lt % lc == 0 and B % nbat == 0
    hs = C_HEADS * lc

    def expand(width, per_head):
        e = np.zeros((LANES, width), np.float32)
        for part in range(3):
            e[part * C_HEADS:(part + 1) * C_HEADS] = (np.arange(width)[None, :] // per_head
                                                      == np.arange(C_HEADS)[:, None])
        return jnp.asarray(e, dtype=BF16)

    es, ep = expand(hs, lc), expand(C_D_INNER, C_HEAD_DIM)
    pos = np.arange(lt)
    tri = jnp.asarray((pos[None, :] <= pos[:, None]) & (pos[None, :] // lc == pos[:, None] // lc), dtype=BF16)
    row = lambda c: pl.BlockSpec((nbat, lt, c), lambda b, t: (b, t, 0))
    per_b = lambda a: pl.BlockSpec((nbat,) + a.shape[1:], lambda b, t: (b,) + (0,) * (a.ndim - 1))
    full = lambda a: pl.BlockSpec(a.shape, lambda b, t: (0,) * a.ndim)
    sc = lambda *shape: pltpu.VMEM((nbat,) + shape, F32)
    return pl.pallas_call(
        functools.partial(_ssd_kernel, nbat=nbat, lt=lt, lc=lc, nt=nt),
        grid=(B // nbat, nt),
        in_specs=[row(C_CONV_DIM), row(C_D_INNER), row(LANES), per_b(prefix), per_b(h0t),
                  full(cw), full(cb), full(dtb), full(alog), full(dsk), full(gn), full(es), full(ep), full(tri)],
        out_specs=[row(C_D_INNER), per_b(h0t)],
        out_shape=[jax.ShapeDtypeStruct((B, S, C_D_INNER), BF16),
                   jax.ShapeDtypeStruct(h0t.shape, F32)],
        scratch_shapes=[sc(lt + 8, C_CONV_DIM), sc(lt, C_D_INNER), sc(lt, C_GN), sc(lt, C_GN),
                        sc(lt, hs), sc(lt, C_D_INNER), sc(lt, C_D_INNER), sc(C_GN, C_D_INNER)],
        compiler_params=_cparams("arbitrary", "arbitrary"),
        name="ssd",
    )(xbc, z, dtr, prefix, h0t, cw, cb, dtb, alog, dsk, gn, es, ep, tri)


def _outproj_kernel(ya_ref, yb_ref, yc_ref, x_ref, g1_ref, sh_ref, sc_ref, gf_ref, wo_ref, wr_ref, br_ref,
                    x1_ref, h2_ref, gate_ref, *, nb, r):
    R = nb * r
    D = x_ref.shape[-1]

    def in2(ref):
        return ref[...].reshape(R, ref.shape[-1])

    o = jnp.dot(in2(ya_ref), wo_ref[0:A_WIDTH, :], preferred_element_type=F32)
    o = o + jnp.dot(in2(yb_ref), wo_ref[A_WIDTH:A_WIDTH + B_WIDTH, :], preferred_element_type=F32)
    o = o + jnp.dot(in2(yc_ref), wo_ref[A_WIDTH + B_WIDTH:, :], preferred_element_type=F32)
    x1 = in2(x_ref) + _rows(g1_ref, nb, r) * o
    x1_ref[...] = x1.reshape(nb, r, D)
    h2 = x1 * lax.rsqrt(jnp.mean(x1 * x1, axis=-1, keepdims=True) + EPS) * gf_ref[...]
    h2 = h2 * (1.0 + _rows(sc_ref, nb, r)) + _rows(sh_ref, nb, r)
    h2_ref[...] = h2.astype(BF16).reshape(nb, r, D)

    h_hi = h2.astype(BF16)
    h_lo = (h2 - h_hi.astype(F32)).astype(BF16)
    wr = wr_ref[...]
    w_hi = wr.astype(BF16)
    w_lo = (wr - w_hi.astype(F32)).astype(BF16)
    lg = (jnp.dot(h_hi, w_hi, preferred_element_type=F32) + jnp.dot(h_hi, w_lo, preferred_element_type=F32)
          + jnp.dot(h_lo, w_hi, preferred_element_type=F32) + br_ref[...])
    lane = lax.broadcasted_iota(jnp.int32, (R, ROUTER_LANES), 1)
    big = jnp.int32(ROUTER_LANES)
    is_g = lane < E_GROUPS
    gl = jnp.where(is_g, lg, NEG_INF)
    gmax = jnp.max(gl, axis=-1, keepdims=True)
    p_top = 1.0 / jnp.sum(jnp.where(is_g, jnp.exp(gl - gmax), 0.0), axis=-1, keepdims=True)
    g_idx = jnp.min(jnp.where(is_g & (gl == gmax), lane, big), axis=-1, keepdims=True)
    e_lane = lane - ROUTER_E_OFF
    sel = (e_lane >= 0) & (e_lane < N_EXPERTS) & ((e_lane // E_PER_GROUP) == g_idx)
    l1 = jnp.where(sel, lg, NEG_INF)
    m1 = jnp.max(l1, axis=-1, keepdims=True)
    i1 = jnp.min(jnp.where(sel & (l1 == m1), lane, big), axis=-1, keepdims=True)
    sel2 = sel & (lane != i1)
    l2 = jnp.where(sel2, lg, NEG_INF)
    m2 = jnp.max(l2, axis=-1, keepdims=True)
    i2 = jnp.min(jnp.where(sel2 & (l2 == m2), lane, big), axis=-1, keepdims=True)
    e2 = jnp.exp(m2 - m1)
    w1 = p_top / (1.0 + e2)
    w2 = p_top * e2 / (1.0 + e2)
    gate = jnp.where(lane == i1, w1, 0.0) + jnp.where(lane == i2, w2, 0.0)
    gate_ref[...] = gate.reshape(nb, r, ROUTER_LANES)


def _outproj(ya, yb, yc, x, mod, g_ffn, wo, wr, br, *, nb, r):
    B, S, D = x.shape
    row = lambda c: pl.BlockSpec((nb, r, c), lambda b, i: (b, i, 0))
    modc = lambda j: pl.BlockSpec((nb, 1, D), lambda b, i: (b, 0, j))
    full = lambda a: pl.BlockSpec(a.shape, lambda b, i: (0,) * a.ndim)
    return pl.pallas_call(
        functools.partial(_outproj_kernel, nb=nb, r=r),
        grid=(B // nb, S // r),
        in_specs=[row(A_WIDTH), row(B_WIDTH), row(C_D_INNER), row(D), modc(2), modc(3), modc(4),
                  full(g_ffn), full(wo), full(wr), full(br)],
        out_specs=[row(D), row(D), row(ROUTER_LANES)],
        out_shape=[jax.ShapeDtypeStruct((B, S, D), F32), jax.ShapeDtypeStruct((B, S, D), BF16),
                   jax.ShapeDtypeStruct((B, S, ROUTER_LANES), F32)],
        compiler_params=_cparams("arbitrary", "arbitrary"),
        name="outproj",
    )(ya, yb, yc, x, mod, mod, mod, g_ffn, wo, wr, br)


def _moe_kernel(h_ref, gate_ref, x_ref, g2_ref, wgu_ref, wd_ref, *rest, nb, r, ne, final):
    if final:
        shf_ref, scf_ref, gfin_ref, o_ref, acc_sc = rest
    else:
        o_ref, acc_sc = rest
    e = pl.program_id(2)
    R = nb * r
    D = x_ref.shape[-1]

    @pl.when(e == 0)
    def _():
        acc_sc[...] = jnp.zeros(acc_sc.shape, F32)

    h = h_ref[...].reshape(R, D)
    gu = jnp.dot(h, wgu_ref[0, 0], preferred_element_type=F32)
    lane = lax.broadcasted_iota(jnp.int32, (R, ROUTER_LANES), 1)
    gcol = jnp.sum(jnp.where(lane == e + ROUTER_E_OFF, gate_ref[...].reshape(R, ROUTER_LANES), 0.0),
                   axis=-1, keepdims=True)
    act = _silu(gu[:, 0:E_HIDDEN]) * gu[:, E_HIDDEN:] * gcol
    acc_sc[...] += jnp.dot(act.astype(BF16), wd_ref[0, 0], preferred_element_type=F32)

    @pl.when(e == ne - 1)
    def _():
        xo = x_ref[...].reshape(R, D) + _rows(g2_ref, nb, r) * acc_sc[...]
        if final:
            xo = xo * lax.rsqrt(jnp.mean(xo * xo, axis=-1, keepdims=True) + EPS) * gfin_ref[...]
            xo = xo * (1.0 + _rows(scf_ref, nb, r)) + _rows(shf_ref, nb, r)
        o_ref[...] = xo.reshape(nb, r, D)


def _moe(h2, gate, x1, mod, wgu, wd, layer, final, *, nb, r):
    B, S, D = x1.shape
    ne = wgu.shape[1]
    row = lambda c: pl.BlockSpec((nb, r, c), lambda b, i, e: (b, i, 0))
    modc = lambda j: pl.BlockSpec((nb, 1, D), lambda b, i, e: (b, 0, j))
    in_specs = [row(D), row(ROUTER_LANES), row(D), modc(5),
                pl.BlockSpec((1, 1, D, 2 * E_HIDDEN), lambda b, i, e: (layer, e, 0, 0)),
                pl.BlockSpec((1, 1, E_HIDDEN, D), lambda b, i, e: (layer, e, 0, 0))]
    args = [h2, gate, x1, mod, wgu, wd]
    if final is not None:
        modf, gfin = final
        in_specs += [modc(0), modc(1), pl.BlockSpec(gfin.shape, lambda b, i, e: (0, 0))]
        args += [modf, modf, gfin]
    return pl.pallas_call(
        functools.partial(_moe_kernel, nb=nb, r=r, ne=ne, final=final is not None),
        grid=(B // nb, S // r, ne),
        in_specs=in_specs,
        out_specs=row(D),
        out_shape=jax.ShapeDtypeStruct((B, S, D), F32),
        scratch_shapes=[pltpu.VMEM((nb * r, D), F32)],
        compiler_params=_cparams("arbitrary", "arbitrary", "arbitrary"),
        name="moe",
    )(*args)


def _rope_tables(pos):
    half = B_ROPE // 2
    inv = 1.0 / (ROPE_THETA ** (jnp.arange(half, dtype=F32) / half))
    ang = pos.astype(F32)[:, None] * inv[None, :]
    cos, sin = jnp.cos(ang), jnp.sin(ang)
    return (jnp.tile(jnp.concatenate([cos, cos], axis=1), (1, B_HEADS)),
            jnp.tile(jnp.concatenate([-sin, sin], axis=1), (1, B_HEADS)))


def _swap_halves(w):
    half = w.shape[-1] // 2
    return jnp.concatenate([w[..., half:], w[..., :half]], axis=-1)


def _layer_weights(l, w_in, b_g_q, b_w_uq, b_g_kv, b_w_uk, b_w_uv, w_out, moe_w_rg, moe_b_rg, moe_w_re, moe_b_re):
    D = w_in.shape[1]
    w = w_in[l]
    o_cq = 3 * A_WIDTH
    o_ckv = o_cq + B_Q_LORA
    o_kpe = o_ckv + B_KV_LORA
    o_z = o_kpe + B_ROPE
    o_xbc = o_z + C_D_INNER
    o_dt = o_xbc + C_CONV_DIM
    wkpe = w[:, o_kpe:o_z]
    zc = lambda n: jnp.zeros((D, n), F32)
    w1 = jnp.concatenate([w[:, 0:o_cq], w[:, o_cq:o_ckv], zc(256 - B_Q_LORA), w[:, o_ckv:o_kpe],
                          wkpe, _swap_halves(wkpe), w[:, o_dt:o_dt + C_HEADS],
                          zc(LANES - 2 * B_ROPE - C_HEADS), w[:, o_z:o_xbc], w[:, o_xbc:o_dt]],
                         axis=1).astype(BF16)
    assert w1.shape[1] == _W1_N
    gq = jnp.pad(b_g_q[l], (0, 256 - B_Q_LORA))[None, :]
    uq = b_w_uq[l]
    pe = uq[:, :, B_NOPE:]
    wuq = jnp.concatenate([uq[:, :, :B_NOPE].reshape(B_Q_LORA, -1), pe.reshape(B_Q_LORA, -1),
                           _swap_halves(pe).reshape(B_Q_LORA, -1)], axis=1)
    wuq = jnp.pad(wuq, ((0, 256 - B_Q_LORA), (0, 0))).astype(BF16)
    eye = jnp.eye(B_HEADS, dtype=F32)
    bd = (jnp.transpose(b_w_uk[l], (1, 2, 0))[:, :, None, :] * eye[:, None, :, None]).reshape(
        B_HEADS * B_NOPE, B_HEADS * B_KV_LORA).astype(BF16)
    wuv = (jnp.transpose(b_w_uv[l], (1, 0, 2))[:, :, None, :] * eye[:, None, :, None]).reshape(
        B_HEADS * B_KV_LORA, B_WIDTH).astype(BF16)
    wr = jnp.zeros((D, ROUTER_LANES), F32)
    wr = wr.at[:, 0:E_GROUPS].set(moe_w_rg[l]).at[:, ROUTER_E_OFF:ROUTER_E_OFF + N_EXPERTS].set(moe_w_re[l])
    br = jnp.zeros((1, ROUTER_LANES), F32)
    br = br.at[0, 0:E_GROUPS].set(moe_b_rg[l]).at[0, ROUTER_E_OFF:ROUTER_E_OFF + N_EXPERTS].set(moe_b_re[l])
    return dict(w1=w1, gq=gq, wuq=wuq, bd=bd, gkv=b_g_kv[l][None, :], wuv=wuv, wo=w_out[l].astype(BF16), wr=wr, br=br)


def _band_bias(rel_bias, nq, nk, back):
    n = nk + nq - 1
    idx = np.clip(back + nq - 1 - np.arange(n + 1), -A_REL_CLIP, A_REL_CLIP) + A_REL_CLIP
    line = rel_bias[:, idx]
    skew = jnp.tile(line, (1, nq))[:, 0:nq * n].reshape(A_HEADS, nq, n)
    return skew[:, :, nq - 1:nq - 1 + nk].reshape(A_HEADS * nq, nk)


def _layer(x, mod, pos, lw, l, cache, prm, final, *, nb, r):
    B, S, D = x.shape
    cos_t, sin_t = _rope_tables(pos)
    (qkv, knew, vnew, qmla, kmla, ckv, kpe, z, xbc, dtr) = _inproj(
        x, mod, prm['g_mix'], lw['w1'], lw['gq'], lw['wuq'], lw['bd'], lw['gkv'], cos_t, sin_t, nb=nb, r=r)
    if cache is None:
        ya = _band_prompt(qkv, _band_bias(prm['a_rel_bias'], CHUNK, A_BAND, A_WINDOW))
        yb = _mla_prompt(qmla, kmla, lw['wuv'])
        prefix = jnp.zeros((B, C_CONV_W - 1, C_CONV_DIM), F32)
        h0t = jnp.zeros((B, C_HEADS, C_D_STATE, C_HEAD_DIM), F32)
        lt, lc, nbat = min(S, 512), SSD_CHUNK, B
    else:
        nr = cache['a_kt'].shape[3]
        ya = _band_sample(qkv, cache['a_kt'], cache['a_vt'], l, _band_bias(prm['a_rel_bias'], S, nr + S, nr))
        yb = _mla_sample(qmla, kmla, cache['b_ckv'], cache['b_kpet'], l, lw['wuv'])
        prefix = cache['c_conv'][l]
        h0t = jnp.swapaxes(cache['c_ssm'][l], -1, -2)
        lt = lc = min(SSD_CHUNK, S)
        nbat = math.gcd(B, 4)
    yc, hft = _ssd(xbc, z, dtr, prefix, h0t, prm['c_conv_w'], prm['c_conv_b'], prm['c_dt_bias'], prm['c_a_log'],
                   prm['c_d_exp'], prm['c_g_norm'], nbat=nbat, lt=lt, lc=lc)
    conv_state = jnp.concatenate([prefix, xbc], axis=1)[:, -(C_CONV_W - 1):]
    x1, h2, gate = _outproj(ya, yb, yc, x, mod, prm['g_ffn'], lw['wo'], lw['wr'], lw['br'], nb=nb, r=r)
    r_moe = r if nb > 1 else min(S, 1024)
    xo = _moe(h2, gate, x1, mod, prm['moe_w_gu'], prm['moe_w_down'], l, final, nb=nb, r=r_moe)
    states = (knew.reshape(B, -1, A_HEADS, A_HEAD_DIM), vnew.reshape(B, -1, A_HEADS, A_HEAD_DIM),
              ckv, kpe, conv_state, jnp.swapaxes(hft, -1, -2))
    return xo, states


def kernel(x_prompt, x_sample, c_prompt, c_sample, cache_a_k, cache_a_v, cache_b_ckv, cache_b_kpe,
           state_c_conv, state_c_ssm, w_ada, b_ada, g_mix, w_in, a_rel_bias, b_g_q, b_w_uq, b_g_kv,
           b_w_uk, b_w_uv, c_conv_w, c_conv_b, c_dt_bias, c_a_log, c_d, c_g_norm, w_out, g_ffn,
           moe_w_rg, moe_b_rg, moe_w_re, moe_b_re, moe_w_gu, moe_w_down, g_final, w_ada_f, b_ada_f):
    depth = w_in.shape[0]
    D = x_prompt.shape[-1]
    Bp, Sp, _ = x_prompt.shape
    Bs, Ss, _ = x_sample.shape
    past = cache_b_ckv.shape[2]
    pos_p = jnp.arange(Sp)
    pos_s = past + jnp.arange(Ss)

    nc = Bp + Bs
    ncp = -(-nc // 8) * 8
    c_all = jnp.pad(jnp.concatenate([c_prompt, c_sample], axis=0), ((0, ncp - nc), (0, 0)))
    mod = _ada(c_all, w_ada, b_ada[:, None, :])
    modf = _ada(c_all, w_ada_f[None], b_ada_f[None, None, :])[0]

    nr = cache_a_k.shape[2]
    to_t = lambda c: jnp.transpose(c, (0, 1, 3, 4, 2)).reshape(depth, Bs, A_WIDTH, nr)
    cache = {'a_kt': to_t(cache_a_k), 'a_vt': to_t(cache_a_v), 'b_ckv': cache_b_ckv,
             'b_kpet': jnp.swapaxes(cache_b_kpe, 2, 3), 'c_conv': state_c_conv, 'c_ssm': state_c_ssm}

    rp = min(Sp, A_WINDOW)
    moe_w_gu = moe_w_gu.astype(BF16)
    moe_w_down = moe_w_down.astype(BF16)
    xp, xs = x_prompt, x_sample
    st_p, st_s = [], []
    for l in range(depth):
        lw = _layer_weights(l, w_in, b_g_q, b_w_uq, b_g_kv, b_w_uk, b_w_uv, w_out,
                            moe_w_rg, moe_b_rg, moe_w_re, moe_b_re)
        prm = {'g_mix': g_mix[l][None, :], 'a_rel_bias': a_rel_bias[l], 'c_conv_w': c_conv_w[l],
               'c_conv_b': c_conv_b[l][None, :], 'c_dt_bias': jnp.pad(c_dt_bias[l], (0, LANES - C_HEADS))[None, :],
               'c_a_log': jnp.pad(c_a_log[l], (0, LANES - C_HEADS))[None, :],
               'c_d_exp': jnp.repeat(c_d[l], C_HEAD_DIM)[None, :], 'c_g_norm': c_g_norm[l][None, :],
               'g_ffn': g_ffn[l][None, :], 'moe_w_gu': moe_w_gu, 'moe_w_down': moe_w_down}
        fin_p = (modf[0:Bp][:, None, :], g_final[None, :]) if l == depth - 1 else None
        fin_s = (modf[Bp:nc][:, None, :], g_final[None, :]) if l == depth - 1 else None
        xp, sp = _layer(xp, mod[l, 0:Bp][:, None, :], pos_p, lw, l, None, prm, fin_p, nb=1, r=rp)
        xs, ss = _layer(xs, mod[l, Bp:nc][:, None, :], pos_s, lw, l, cache, prm, fin_s, nb=Bs, r=Ss)
        st_p.append(sp)
        st_s.append(ss)
    stack = lambda st, k: jnp.stack([t[k] for t in st])
    return ((xp, xs) + tuple(stack(st_p, k) for k in range(6))
            + tuple(stack(st_s, k) for k in range(6)))
```

```python
import functools
import math

import numpy as np
import jax
import jax.numpy as jnp
from jax import lax
from jax.experimental import pallas as pl
from jax.experimental.pallas import tpu as pltpu

F32 = jnp.float32
BF16 = jnp.bfloat16

D_MODEL = 1024
CHUNK = 64
EPS = 1e-6
NEG_INF = -1e30

A_HEADS = 4
A_HEAD_DIM = 64
A_WIDTH = A_HEADS * A_HEAD_DIM
A_BACK = 8
A_BAND = (A_BACK + 1) * CHUNK
A_WINDOW = A_BACK * CHUNK
A_REL_CLIP = 128
A_SCALE = A_HEAD_DIM ** -0.5

B_HEADS = 4
B_Q_LORA = 192
B_KV_LORA = 128
B_NOPE = 64
B_ROPE = 32
B_V = 64
B_WIDTH = B_HEADS * B_V
B_SCALE = (B_NOPE + B_ROPE) ** -0.5
B_QK = B_KV_LORA + B_ROPE
ROPE_THETA = 10000.0

C_D_INNER = 512
C_HEAD_DIM = 64
C_HEADS = C_D_INNER // C_HEAD_DIM
C_GROUPS = 2
C_D_STATE = 64
C_CONV_W = 4
C_CONV_DIM = C_D_INNER + 2 * C_GROUPS * C_D_STATE
C_GN = C_GROUPS * C_D_STATE
C_HPG = C_HEADS // C_GROUPS
SSD_CHUNK = 64

E_GROUPS = 4
E_PER_GROUP = 4
N_EXPERTS = E_GROUPS * E_PER_GROUP
E_HIDDEN = 256
ROUTER_LANES = 128
ROUTER_E_OFF = 16

LANES = 128
VMEM_LIMIT = 56 * 1024 * 1024

B_QSCALE = B_SCALE * math.log2(math.e)

_W1_QKV = 0
_W1_CQ = 3 * A_WIDTH
_W1_CKV = _W1_CQ + 256
_W1_KPE = _W1_CKV + B_KV_LORA
_W1_Z = _W1_KPE + LANES
_W1_XBC = _W1_Z + C_D_INNER
_W1_N = _W1_XBC + C_CONV_DIM


def _cparams(*sem):
    return pltpu.CompilerParams(dimension_semantics=sem, vmem_limit_bytes=VMEM_LIMIT)


def _silu(v):
    return v * jax.nn.sigmoid(v)


def _split3_lanes(v, lane):
    x3 = v + pltpu.roll(v, C_HEADS, axis=1) + pltpu.roll(v, 2 * C_HEADS, axis=1)
    hi = x3.astype(BF16).astype(F32)
    r1 = x3 - hi
    mid = r1.astype(BF16).astype(F32)
    lo = r1 - mid
    return jnp.where(lane < C_HEADS, hi, jnp.where(lane < 2 * C_HEADS, mid, lo)).astype(BF16)


def _ada_kernel(c_ref, w_ref, b_ref, o_ref):
    s = _silu(c_ref[...]).astype(BF16)
    o_ref[0] = jnp.dot(s, w_ref[0].astype(BF16), preferred_element_type=F32) + b_ref[0]


def _ada(c_all, w, b):
    nl, d, n = w.shape
    bp = c_all.shape[0]
    tn = 1024
    return pl.pallas_call(
        _ada_kernel,
        grid=(nl, n // tn),
        in_specs=[pl.BlockSpec((bp, d), lambda l, j: (0, 0)),
                  pl.BlockSpec((1, d, tn), lambda l, j: (l, 0, j)),
                  pl.BlockSpec((1, 1, tn), lambda l, j: (l, 0, j))],
        out_specs=pl.BlockSpec((1, bp, tn), lambda l, j: (l, 0, j)),
        out_shape=jax.ShapeDtypeStruct((nl, bp, n), F32),
        compiler_params=_cparams("arbitrary", "arbitrary"),
        name="ada",
    )(c_all, w, b)


def _rows(ref_or_val, nb, r):
    v = ref_or_val[...]
    return jnp.broadcast_to(v, (nb, r, v.shape[-1])).reshape(nb * r, v.shape[-1])


def _inproj_kernel(x_ref, sh_ref, sc_ref, g_ref, w1_ref, gq_ref, wuq_ref, bd_ref, gkv_ref, cos_ref, sin_ref,
                   qkv_ref, knew_ref, vnew_ref, qmla_ref, kmla_ref, ckv_ref, kpe_ref, z_ref, xbc_ref, dtr_ref,
                   *, nb, r, keep_tile, nparts):
    D = x_ref.shape[-1]
    pnb, pr = (nb // nparts, r) if nb > 1 else (1, r // nparts)
    PR = pnb * pr

    def sel(s):
        return (slice(s * pnb, (s + 1) * pnb), slice(None)) if nb > 1 else (slice(None), slice(s * pr, (s + 1) * pr))

    def per_batch(ref, s):
        v = ref[sel(s)[0]] if nb > 1 else ref[...]
        return jnp.broadcast_to(v, (pnb, pr, v.shape[-1])).reshape(PR, v.shape[-1])

    def per_pos(ref, s):
        v = ref[...] if nb > 1 else ref[sel(s)[1]]
        return jnp.broadcast_to(v, (pnb, pr, v.shape[-1])).reshape(PR, v.shape[-1])

    def project(s):
        bs, rs = sel(s)
        x = x_ref[bs, rs].reshape(PR, D)
        h = x * lax.rsqrt(jnp.mean(x * x, axis=-1, keepdims=True) + EPS) * g_ref[...]
        h = h * (1.0 + per_batch(sc_ref, s)) + per_batch(sh_ref, s)
        return jnp.dot(h.astype(BF16), w1_ref[...], preferred_element_type=F32)

    def finish(s, u):
        bs, rs = sel(s)

        def out3(v):
            return v.reshape(pnb, pr, v.shape[-1])

        qkv_ref[bs, rs, 0:A_WIDTH] = out3((u[:, 0:A_WIDTH] * A_SCALE).astype(BF16))
        qkv_ref[bs, rs, A_WIDTH:3 * A_WIDTH] = out3(u[:, A_WIDTH:3 * A_WIDTH].astype(BF16))

        @pl.when(pl.program_id(1) == keep_tile)
        def _():
            knew_ref[bs, rs] = out3(u[:, A_WIDTH:2 * A_WIDTH])
            vnew_ref[bs, rs] = out3(u[:, 2 * A_WIDTH:3 * A_WIDTH])

        blk = u[:, _W1_CQ:_W1_CQ + 256]
        lane = lax.broadcasted_iota(jnp.int32, (1, 256), 1)
        cq = jnp.where(lane < B_Q_LORA, blk, 0.0)
        cqn = cq * lax.rsqrt(jnp.sum(cq * cq, axis=-1, keepdims=True) * (1.0 / B_Q_LORA) + EPS) * gq_ref[...]
        qb = jnp.dot(cqn.astype(BF16), wuq_ref[...], preferred_element_type=F32)
        nope_w = B_HEADS * B_NOPE
        rope_w = B_HEADS * B_ROPE
        qlat = jnp.dot(qb[:, 0:nope_w].astype(BF16), bd_ref[...], preferred_element_type=F32) * B_QSCALE
        cosr = per_pos(cos_ref, s)
        sinr = per_pos(sin_ref, s)
        qpe = (qb[:, nope_w:nope_w + rope_w] * cosr
               + qb[:, nope_w + rope_w:nope_w + 2 * rope_w] * sinr) * B_QSCALE
        for hd in range(B_HEADS):
            qmla_ref[bs, hd, rs, 0:B_KV_LORA] = out3(qlat[:, hd * B_KV_LORA:(hd + 1) * B_KV_LORA].astype(BF16))
            qmla_ref[bs, hd, rs, B_KV_LORA:B_QK] = out3(qpe[:, hd * B_ROPE:(hd + 1) * B_ROPE].astype(BF16))

        cr = u[:, _W1_CKV:_W1_CKV + B_KV_LORA]
        ckv = cr * lax.rsqrt(jnp.mean(cr * cr, axis=-1, keepdims=True) + EPS) * gkv_ref[...]
        ckv_ref[bs, rs] = out3(ckv)
        kmla_ref[bs, rs, 0:B_KV_LORA] = out3(ckv.astype(BF16))
        kb = u[:, _W1_KPE:_W1_KPE + LANES]
        rot = kb * cosr + pltpu.roll(kb, LANES - B_ROPE, axis=1) * sinr
        kpe_ref[bs, rs] = out3(rot[:, 0:B_ROPE])
        kmla_ref[bs, rs, B_KV_LORA:B_QK] = out3(rot[:, 0:B_ROPE].astype(BF16))
        lane1 = lax.broadcasted_iota(jnp.int32, (1, LANES), 1)
        dtr_ref[bs, rs] = out3(jnp.where(lane1 < C_HEADS, pltpu.roll(kb, LANES - 2 * B_ROPE, axis=1), 0.0))

        z_ref[bs, rs] = out3(u[:, _W1_Z:_W1_Z + C_D_INNER].astype(BF16))
        xbc_ref[bs, rs] = out3(u[:, _W1_XBC:_W1_XBC + C_CONV_DIM])

    us = [project(s) for s in range(nparts)]
    for s in range(nparts):
        finish(s, us[s])


def _inproj(x, mod, g_mix, w1, gq, wuq, bd, gkv, cos_t, sin_t, *, nb, r, nparts):
    B, S, D = x.shape
    nbt, nst = B // nb, S // r
    keep = min(S, A_WINDOW)
    assert keep == r and B % nb == 0 and S % r == 0
    grid = (nbt, nst)
    row = lambda c: pl.BlockSpec((nb, r, c), lambda b, i: (b, i, 0))
    full = lambda a: pl.BlockSpec(a.shape, lambda b, i: (0,) * a.ndim)
    in_specs = [row(D),
                pl.BlockSpec((nb, 1, D), lambda b, i: (b, 0, 0)),
                pl.BlockSpec((nb, 1, D), lambda b, i: (b, 0, 1)),
                full(g_mix), full(w1), full(gq), full(wuq), full(bd), full(gkv),
                pl.BlockSpec((r, LANES), lambda b, i: (i, 0)),
                pl.BlockSpec((r, LANES), lambda b, i: (i, 0))]
    out_shape = [jax.ShapeDtypeStruct((B, S, 3 * A_WIDTH), BF16),
                 jax.ShapeDtypeStruct((B, keep, A_WIDTH), F32),
                 jax.ShapeDtypeStruct((B, keep, A_WIDTH), F32),
                 jax.ShapeDtypeStruct((B, B_HEADS, S, B_QK), BF16),
                 jax.ShapeDtypeStruct((B, S, B_QK), BF16),
                 jax.ShapeDtypeStruct((B, S, B_KV_LORA), F32),
                 jax.ShapeDtypeStruct((B, S, B_ROPE), F32),
                 jax.ShapeDtypeStruct((B, S, C_D_INNER), BF16),
                 jax.ShapeDtypeStruct((B, S, C_CONV_DIM), F32),
                 jax.ShapeDtypeStruct((B, S, LANES), F32)]
    out_specs = [row(3 * A_WIDTH),
                 pl.BlockSpec((nb, keep, A_WIDTH), lambda b, i: (b, 0, 0)),
                 pl.BlockSpec((nb, keep, A_WIDTH), lambda b, i: (b, 0, 0)),
                 pl.BlockSpec((nb, B_HEADS, r, B_QK), lambda b, i: (b, 0, i, 0)),
                 row(B_QK), row(B_KV_LORA), row(B_ROPE), row(C_D_INNER), row(C_CONV_DIM), row(LANES)]
    return pl.pallas_call(
        functools.partial(_inproj_kernel, nb=nb, r=r, keep_tile=nst - 1, nparts=nparts),
        grid=grid, in_specs=in_specs, out_specs=out_specs, out_shape=out_shape,
        compiler_params=_cparams("arbitrary", "arbitrary"),
        name="inproj",
    )(x, mod, mod, g_mix, w1, gq, wuq, bd, gkv, cos_t, sin_t)


def _band_core(q, k, v, bias, valid, hmask):
    nq = q.shape[0]
    qs = jnp.where(hmask, jnp.concatenate([q] * A_HEADS, axis=0), jnp.zeros((), BF16))
    s = lax.dot_general(qs, k, (((1,), (1,)), ((), ())), preferred_element_type=F32) + bias
    if valid is not None:
        s = jnp.where(valid, s, NEG_INF)
    m = jnp.max(s, axis=-1, keepdims=True)
    e = jnp.exp(s - m)
    den = jnp.sum(e, axis=-1, keepdims=True)
    o = jnp.dot(e.astype(BF16), v, preferred_element_type=F32)
    o = jnp.where(hmask, o / den, 0.0)
    y = o[0:nq]
    for hd in range(1, A_HEADS):
        y = y + o[hd * nq:(hd + 1) * nq]
    return y


def _head_mask(nq):
    row = lax.broadcasted_iota(jnp.int32, (A_HEADS * nq, A_WIDTH), 0)
    col = lax.broadcasted_iota(jnp.int32, (A_HEADS * nq, A_WIDTH), 1)
    return (row // nq) == (col // A_HEAD_DIM)


def _band_prompt_kernel(cur_ref, prev_ref, bias_ref, o_ref, kwin, vwin, *, tq):
    i = pl.program_id(1)
    nch = tq // CHUNK
    kwin[0:A_WINDOW] = prev_ref[0, tq - A_WINDOW:tq, A_WIDTH:2 * A_WIDTH]
    kwin[A_WINDOW:A_WINDOW + tq] = cur_ref[0, :, A_WIDTH:2 * A_WIDTH]
    vwin[0:A_WINDOW] = prev_ref[0, tq - A_WINDOW:tq, 2 * A_WIDTH:3 * A_WIDTH]
    vwin[A_WINDOW:A_WINDOW + tq] = cur_ref[0, :, 2 * A_WIDTH:3 * A_WIDTH]
    hmask = _head_mask(CHUNK)
    ucol = lax.broadcasted_iota(jnp.int32, (1, A_BAND), 1)
    for c in range(nch):
        q = cur_ref[0, c * CHUNK:(c + 1) * CHUNK, 0:A_WIDTH]
        valid = (ucol // CHUNK + (i * nch + c - A_BACK)) >= 0
        y = _band_core(q, kwin[c * CHUNK:c * CHUNK + A_BAND], vwin[c * CHUNK:c * CHUNK + A_BAND],
                       bias_ref[...], valid, hmask)
        o_ref[0, c * CHUNK:(c + 1) * CHUNK, :] = y.astype(BF16)


def _band_prompt(qkv, bias_stack):
    B, S, _ = qkv.shape
    tq = A_WINDOW
    assert S % tq == 0
    return pl.pallas_call(
        functools.partial(_band_prompt_kernel, tq=tq),
        grid=(B, S // tq),
        in_specs=[pl.BlockSpec((1, tq, 3 * A_WIDTH), lambda b, i: (b, i, 0)),
                  pl.BlockSpec((1, tq, 3 * A_WIDTH), lambda b, i: (b, jnp.maximum(i - 1, 0), 0)),
                  pl.BlockSpec(bias_stack.shape, lambda b, i: (0, 0))],
        out_specs=pl.BlockSpec((1, tq, A_WIDTH), lambda b, i: (b, i, 0)),
        out_shape=jax.ShapeDtypeStruct((B, S, A_WIDTH), BF16),
        scratch_shapes=[pltpu.VMEM((A_WINDOW + tq, A_WIDTH), BF16), pltpu.VMEM((A_WINDOW + tq, A_WIDTH), BF16)],
        compiler_params=_cparams("arbitrary", "arbitrary"),
        name="band_prompt",
    )(qkv, qkv, bias_stack)


def _band_sample_kernel(qkv_ref, kt_ref, vt_ref, biasc_ref, biasn_ref, o_ref, *, nbb, ds):
    hmask = _head_mask(ds)
    nt_dims = (((1,), (1,)), ((), ()))
    for b in range(nbb):
        q = qkv_ref[b, :, 0:A_WIDTH]
        kn = qkv_ref[b, :, A_WIDTH:2 * A_WIDTH]
        vn = qkv_ref[b, :, 2 * A_WIDTH:3 * A_WIDTH]
        qs = jnp.where(hmask, jnp.concatenate([q] * A_HEADS, axis=0), jnp.zeros((), BF16))
        sc = jnp.dot(qs, kt_ref[0, b].astype(BF16), preferred_element_type=F32) + biasc_ref[...]
        sn = lax.dot_general(qs, kn, nt_dims, preferred_element_type=F32) + biasn_ref[...]
        m = jnp.maximum(jnp.max(sc, axis=-1, keepdims=True), jnp.max(sn, axis=-1, keepdims=True))
        ec = jnp.exp(sc - m)
        en = jnp.exp(sn - m)
        den = jnp.sum(ec, axis=-1, keepdims=True) + jnp.sum(en, axis=-1, keepdims=True)
        o = lax.dot_general(ec.astype(BF16), vt_ref[0, b].astype(BF16), nt_dims, preferred_element_type=F32)
        o = o + jnp.dot(en.astype(BF16), vn, preferred_element_type=F32)
        o = jnp.where(hmask, o / den, 0.0)
        y = o[0:ds]
        for hd in range(1, A_HEADS):
            y = y + o[hd * ds:(hd + 1) * ds]
        o_ref[b] = y.astype(BF16)


def _band_sample(qkv, cache_kt, cache_vt, layer, bias_stack):
    B, ds, _ = qkv.shape
    nr = cache_kt.shape[3]
    nbb = 4
    assert B % nbb == 0
    bias_c, bias_n = bias_stack[:, 0:nr], bias_stack[:, nr:]
    return pl.pallas_call(
        functools.partial(_band_sample_kernel, nbb=nbb, ds=ds),
        grid=(B // nbb,),
        in_specs=[pl.BlockSpec((nbb, ds, 3 * A_WIDTH), lambda b: (b, 0, 0)),
                  pl.BlockSpec((1, nbb, A_WIDTH, nr), lambda b: (layer, b, 0, 0)),
                  pl.BlockSpec((1, nbb, A_WIDTH, nr), lambda b: (layer, b, 0, 0)),
                  pl.BlockSpec(bias_c.shape, lambda b: (0, 0)),
                  pl.BlockSpec(bias_n.shape, lambda b: (0, 0))],
        out_specs=pl.BlockSpec((nbb, ds, A_WIDTH), lambda b: (b, 0, 0)),
        out_shape=jax.ShapeDtypeStruct((B, ds, A_WIDTH), BF16),
        compiler_params=_cparams("arbitrary"),
        name="band_sample",
    )(qkv, cache_kt, cache_vt, bias_c, bias_n)


def _mla_finish(acc, den, wuv_ref, bq):
    o = acc / den
    o_all = jnp.concatenate([o[hd * bq:(hd + 1) * bq] for hd in range(B_HEADS)], axis=1)
    return jnp.dot(o_all.astype(BF16), wuv_ref[...], preferred_element_type=F32)


def _mla_prompt_kernel(qi_ref, kj_ref, last_ref, q_ref, k_ref, wuv_ref, o_ref, m_sc, l_sc, acc_sc, *, bq, bk):
    p = pl.program_id(1)
    i = qi_ref[p]
    j = kj_ref[p]
    M = B_HEADS * bq

    @pl.when(j == 0)
    def _():
        m_sc[...] = jnp.full(m_sc.shape, NEG_INF, F32)
        l_sc[...] = jnp.zeros(l_sc.shape, F32)
        acc_sc[...] = jnp.zeros(acc_sc.shape, F32)

    def step(masked):
        q = q_ref[0].reshape(M, B_QK)
        k = k_ref[0]
        s = lax.dot_general(q, k, (((1,), (1,)), ((), ())), preferred_element_type=F32)
        if masked:
            row = lax.broadcasted_iota(jnp.int32, (M, bk), 0)
            col = lax.broadcasted_iota(jnp.int32, (M, bk), 1)
            s = jnp.where((col + j * bk) // CHUNK <= ((row % bq) + i * bq) // CHUNK, s, NEG_INF)
        m_prev = m_sc[...]
        m_new = jnp.maximum(m_prev, jnp.max(s, axis=1, keepdims=True))
        alpha = jnp.exp2(m_prev - m_new)
        pr = jnp.exp2(s - jnp.tile(m_new, (1, bk // LANES)))
        l_sc[...] = alpha * l_sc[...] + jnp.sum(pr, axis=1, keepdims=True)
        acc_sc[...] = acc_sc[...] * alpha + jnp.dot(pr.astype(BF16), k[:, 0:B_KV_LORA],
                                                    preferred_element_type=F32)
        m_sc[...] = m_new

    @pl.when(last_ref[p] == 0)
    def _():
        step(False)

    @pl.when(last_ref[p] == 1)
    def _():
        step(True)
        o_ref[0] = _mla_finish(acc_sc[...], l_sc[...], wuv_ref, bq).astype(BF16)


def _mla_prompt(qmla, kmla, wuv_bd, *, bq, bk):
    B, S = kmla.shape[0:2]
    bk = min(bk, S)
    assert S % bq == 0 and S % bk == 0 and bq % CHUNK == 0 and bk % bq == 0
    nq = S // bq
    n_kv = [(i * bq + bq - 1) // bk + 1 for i in range(nq)]
    qi = np.concatenate([np.full(n, i, np.int32) for i, n in enumerate(n_kv)])
    kj = np.concatenate([np.arange(n, dtype=np.int32) for n in n_kv])
    last = np.concatenate([(np.arange(n) == n - 1).astype(np.int32) for n in n_kv])
    grid_spec = pltpu.PrefetchScalarGridSpec(
        num_scalar_prefetch=3,
        grid=(B, len(qi)),
        in_specs=[pl.BlockSpec((1, B_HEADS, bq, B_QK), lambda b, p, qi, kj, last: (b, 0, qi[p], 0)),
                  pl.BlockSpec((1, bk, B_QK), lambda b, p, qi, kj, last: (b, kj[p], 0)),
                  pl.BlockSpec(wuv_bd.shape, lambda b, p, qi, kj, last: (0, 0))],
        out_specs=pl.BlockSpec((1, bq, B_WIDTH), lambda b, p, qi, kj, last: (b, qi[p], 0)),
        scratch_shapes=[pltpu.VMEM((B_HEADS * bq, LANES), F32), pltpu.VMEM((B_HEADS * bq, LANES), F32),
                        pltpu.VMEM((B_HEADS * bq, B_KV_LORA), F32)])
    return pl.pallas_call(
        functools.partial(_mla_prompt_kernel, bq=bq, bk=bk),
        grid_spec=grid_spec,
        out_shape=jax.ShapeDtypeStruct((B, S, B_WIDTH), BF16),
        compiler_params=_cparams("arbitrary", "arbitrary"),
        name="mla_prompt",
    )(jnp.asarray(qi), jnp.asarray(kj), jnp.asarray(last), qmla, kmla, wuv_bd)


def _mla_sample_kernel(q_ref, kn_ref, ckv_ref, kpet_ref, wuv_ref, o_ref, *, ds):
    M = B_HEADS * ds
    nt_dims = (((1,), (1,)), ((), ()))
    q = q_ref[0].reshape(M, B_QK)
    kn = kn_ref[0]
    ckv = ckv_ref[0, 0].astype(BF16)
    sc = (lax.dot_general(q[:, 0:B_KV_LORA], ckv, nt_dims, preferred_element_type=F32)
          + jnp.dot(q[:, B_KV_LORA:B_QK], kpet_ref[0, 0].astype(BF16), preferred_element_type=F32))
    sn = lax.dot_general(q, kn, nt_dims, preferred_element_type=F32)
    m = jnp.maximum(jnp.max(sc, axis=1, keepdims=True), jnp.max(sn, axis=1, keepdims=True))
    pc = jnp.exp2(sc - m)
    pn = jnp.exp2(sn - m)
    den = jnp.sum(pc, axis=1, keepdims=True) + jnp.sum(pn, axis=1, keepdims=True)
    acc = (jnp.dot(pc.astype(BF16), ckv, preferred_element_type=F32)
           + jnp.dot(pn.astype(BF16), kn[:, 0:B_KV_LORA], preferred_element_type=F32))
    o_ref[0] = _mla_finish(acc, den, wuv_ref, ds).astype(BF16)


def _mla_sample(qmla, kmla, cache_ckv, cache_kpet, layer, wuv_bd):
    B, ds = kmla.shape[0:2]
    past = cache_ckv.shape[2]
    return pl.pallas_call(
        functools.partial(_mla_sample_kernel, ds=ds),
        grid=(B,),
        in_specs=[pl.BlockSpec((1, B_HEADS, ds, B_QK), lambda b: (b, 0, 0, 0)),
                  pl.BlockSpec((1, ds, B_QK), lambda b: (b, 0, 0)),
                  pl.BlockSpec((1, 1, past, B_KV_LORA), lambda b: (layer, b, 0, 0)),
                  pl.BlockSpec((1, 1, B_ROPE, past), lambda b: (layer, b, 0, 0)),
                  pl.BlockSpec(wuv_bd.shape, lambda b: (0, 0))],
        out_specs=pl.BlockSpec((1, ds, B_WIDTH), lambda b: (b, 0, 0)),
        out_shape=jax.ShapeDtypeStruct((B, ds, B_WIDTH), BF16),
        compiler_params=_cparams("arbitrary"),
        name="mla_sample",
    )(qmla, kmla, cache_ckv, cache_kpet, wuv_bd)


def _ssd_kernel(xbc_ref, z_ref, dtr_ref, pre_ref, h0_ref, cw_ref, cb_ref, dtb_ref, alog_ref, dsk_ref, gn_ref,
                es_ref, ep_ref, tri_ref, y_ref, hout_ref,
                cbuf, xs_sc, b_sc, c_sc, ces_sc, cep_sc, dep_sc, st_sc, *, nbat, lt, lc, nt):
    t = pl.program_id(1)
    nch = lt // lc
    HS = C_HEADS * lc
    gw = C_HPG * C_HEAD_DIM

    @pl.when(t == 0)
    def _():
        st_sc[...] = jnp.zeros(st_sc.shape, F32)
        for b in range(nbat):
            cbuf[b, 8 - (C_CONV_W - 1):8] = pre_ref[b]
            for g in range(C_GROUPS):
                blk = jnp.concatenate([h0_ref[b, g * C_HPG + hl] for hl in range(C_HPG)], axis=1)
                st_sc[b, g * C_D_STATE:(g + 1) * C_D_STATE, g * gw:(g + 1) * gw] = blk

    hl = lax.broadcasted_iota(jnp.int32, (lt, LANES), 1)
    for b in range(nbat):
        cbuf[b, 8:8 + lt] = xbc_ref[b]
        acc = jnp.broadcast_to(cb_ref[...], (lt, C_CONV_DIM))
        for kk in range(C_CONV_W):
            off = 8 - (C_CONV_W - 1) + kk
            acc = acc + cbuf[b, off:off + lt] * cw_ref[kk:kk + 1, :]
        tail = cbuf[b, 8 + lt - (C_CONV_W - 1):8 + lt]
        cbuf[b, 8 - (C_CONV_W - 1):8] = tail
        xc = _silu(acc)
        xs_sc[b] = xc[:, 0:C_D_INNER]
        b_sc[b] = xc[:, C_D_INNER:C_D_INNER + C_GN]
        c_sc[b] = xc[:, C_D_INNER + C_GN:C_D_INNER + 2 * C_GN]

        dt = jnp.where(hl < C_HEADS, jax.nn.softplus(dtr_ref[b] + dtb_ref[...]), 0.0)
        da = dt * (-jnp.exp(alog_ref[...]))
        c3 = jnp.dot(tri_ref[...], _split3_lanes(da, hl), preferred_element_type=F32)
        cum = jnp.where(hl < C_HEADS, c3 + pltpu.roll(c3, LANES - C_HEADS, axis=1)
                        + pltpu.roll(c3, LANES - 2 * C_HEADS, axis=1), 0.0)
        cum3 = _split3_lanes(cum, hl)
        cep = jnp.dot(cum3, ep_ref[...], preferred_element_type=F32)
        cep_sc[b] = cep
        ces_sc[b] = cep if HS == C_D_INNER and lc == C_HEAD_DIM else jnp.dot(cum3, es_ref[...],
                                                                           preferred_element_type=F32)
        dep_sc[b] = jnp.dot(_split3_lanes(dt, hl), ep_ref[...], preferred_element_type=F32)

    r_s = lax.broadcasted_iota(jnp.int32, (lc, HS), 0)
    l_s = lax.broadcasted_iota(jnp.int32, (lc, HS), 1)
    eye_t = (l_s % lc) == r_s
    causal_t = (l_s % lc) <= r_s
    r_w = lax.broadcasted_iota(jnp.int32, (HS, C_GN), 0)
    l_w = lax.broadcasted_iota(jnp.int32, (HS, C_GN), 1)
    gmask = (r_w // (C_HPG * lc)) == (l_w // C_D_STATE)
    r_b =lax.broadcasted_iota(jnp.int32, (C_HPG * lc, gw), 0)
    l_b = lax.broadcasted_iota(jnp.int32, (C_HPG * lc, gw), 1)
    bmask = (r_b // lc) == (l_b // C_HEAD_DIM)
    r_g = lax.broadcasted_iota(jnp.int32, (C_GN, C_D_INNER), 0)
    l_g = lax.broadcasted_iota(jnp.int32, (C_GN, C_D_INNER), 1)
    smask = (r_g // C_D_STATE) == (l_g // gw)

    def chunk_one(b, rows):
        ce_s = ces_sc[b, rows, :]
        ce_p = cep_sc[b, rows, :]
        de_p = dep_sc[b, rows, :]
        xs = xs_sc[b, rows, :]
        bm = b_sc[b, rows, :]
        cm = c_sc[b, rows, :]
        rflat = jnp.sum(jnp.where(eye_t, ce_s, 0.0), axis=0, keepdims=True)
        lcat = jnp.exp(jnp.where(causal_t, ce_s - rflat, NEG_INF))
        w_nt = jnp.where(gmask, jnp.concatenate([bm] * C_HEADS, axis=0), 0.0).astype(BF16)
        cbcat = lax.dot_general(cm.astype(BF16), w_nt, (((1,), (1,)), ((), ())), preferred_element_type=F32)
        mcat = (cbcat * lcat).astype(BF16)
        xdt = xs * de_p
        xdt_b = xdt.astype(BF16)
        ydiag = []
        for g in range(C_GROUPS):
            bdx = jnp.where(bmask, jnp.concatenate([xdt_b[:, g * gw:(g + 1) * gw]] * C_HPG, axis=0),
                            jnp.zeros((), BF16))
            ydiag.append(jnp.dot(mcat[:, g * C_HPG * lc:(g + 1) * C_HPG * lc], bdx, preferred_element_type=F32))
        y = jnp.concatenate(ydiag, axis=1)
        st = st_sc[b]
        y = y + jnp.dot(cm.astype(BF16), st.astype(BF16), preferred_element_type=F32) * jnp.exp(ce_p)
        y = y + dsk_ref[...] * xs
        last = ce_p[lc - 1:lc, :]
        xw = (xdt * jnp.exp(last - ce_p)).astype(BF16)
        upd = lax.dot_general(bm.astype(BF16), xw, (((0,), (0,)), ((), ())), preferred_element_type=F32)
        st_sc[b] = jnp.exp(last) * st + jnp.where(smask, upd, 0.0)
        y = y * _silu(z_ref[b, rows, :].astype(F32))
        y = y * lax.rsqrt(jnp.mean(y * y, axis=-1, keepdims=True) + EPS) * gn_ref[...]
        y_ref[b, rows, :] = y.astype(BF16)

    def chunk(c, carry):
        rows = pl.ds(pl.multiple_of(c * lc, lc), lc)
        for b in range(nbat):
            chunk_one(b, rows)
        return carry

    lax.fori_loop(0, nch, chunk, 0)

    @pl.when(t == nt - 1)
    def _():
        for b in range(nbat):
            for hd in range(C_HEADS):
                g = hd // C_HPG
                hout_ref[b, hd] = st_sc[b, g * C_D_STATE:(g + 1) * C_D_STATE,
                                        hd * C_HEAD_DIM:(hd + 1) * C_HEAD_DIM]


def _ssd(xbc, z, dtr, prefix, h0t, cw, cb, dtb, alog, dsk, gn, *, nbat, lt, lc):
    B, S, _ = xbc.shape
    nt = S // lt
    assert S % lt == 0 and lt % lc == 0 and B % nbat == 0
    hs = C_HEADS * lc

    def expand(width, per_head):
        e = np.zeros((LANES, width), np.float32)
        for part in range(3):
            e[part * C_HEADS:(part + 1) * C_HEADS] = (np.arange(width)[None, :] // per_head
                                                      == np.arange(C_HEADS)[:, None])
        return jnp.asarray(e, dtype=BF16)

    es, ep = expand(hs, lc), expand(C_D_INNER, C_HEAD_DIM)
    pos = np.arange(lt)
    tri = jnp.asarray((pos[None, :] <= pos[:, None]) & (pos[None, :] // lc == pos[:, None] // lc), dtype=BF16)
    row = lambda c: pl.BlockSpec((nbat, lt, c), lambda b, t: (b, t, 0))
    per_b = lambda a: pl.BlockSpec((nbat,) + a.shape[1:], lambda b, t: (b,) + (0,) * (a.ndim - 1))
    full = lambda a: pl.BlockSpec(a.shape, lambda b, t: (0,) * a.ndim)
    sc = lambda *shape: pltpu.VMEM((nbat,) + shape, F32)
    return pl.pallas_call(
        functools.partial(_ssd_kernel, nbat=nbat, lt=lt, lc=lc, nt=nt),
        grid=(B // nbat, nt),
        in_specs=[row(C_CONV_DIM), row(C_D_INNER), row(LANES), per_b(prefix), per_b(h0t),
                  full(cw), full(cb), full(dtb), full(alog), full(dsk), full(gn), full(es), full(ep), full(tri)],
        out_specs=[row(C_D_INNER), per_b(h0t)],
        out_shape=[jax.ShapeDtypeStruct((B, S, C_D_INNER), BF16),
                   jax.ShapeDtypeStruct(h0t.shape, F32)],
        scratch_shapes=[sc(lt + 8, C_CONV_DIM), sc(lt, C_D_INNER), sc(lt, C_GN), sc(lt, C_GN),
                        sc(lt, hs), sc(lt, C_D_INNER), sc(lt, C_D_INNER), sc(C_GN, C_D_INNER)],
        compiler_params=_cparams("arbitrary", "arbitrary"),
        name="ssd",
    )(xbc, z, dtr, prefix, h0t, cw, cb, dtb, alog, dsk, gn, es, ep, tri)


def _outproj_kernel(ya_ref, yb_ref, yc_ref, x_ref, g1_ref, sh_ref, sc_ref, gf_ref, wo_ref, wr_ref, br_ref,
                    x1_ref, h2_ref, gate_ref, *, nb, r, nparts):
    D = x_ref.shape[-1]
    pnb, pr = (nb // nparts, r) if nb > 1 else (1, r // nparts)
    PR = pnb * pr

    def sel_part(s):
        return (slice(s * pnb, (s + 1) * pnb), slice(None)) if nb > 1 else (slice(None), slice(s * pr, (s + 1) * pr))

    def in2(ref, s):
        bs, rs = sel_part(s)
        return ref[bs, rs].reshape(PR, ref.shape[-1])

    def per_batch(ref, s):
        v = ref[sel_part(s)[0]] if nb > 1 else ref[...]
        return jnp.broadcast_to(v, (pnb, pr, v.shape[-1])).reshape(PR, v.shape[-1])

    def project(s):
        o = jnp.dot(in2(ya_ref, s), wo_ref[0:A_WIDTH, :], preferred_element_type=F32)
        o = o + jnp.dot(in2(yb_ref, s), wo_ref[A_WIDTH:A_WIDTH + B_WIDTH, :], preferred_element_type=F32)
        return o + jnp.dot(in2(yc_ref, s), wo_ref[A_WIDTH + B_WIDTH:, :], preferred_element_type=F32)

    wr = wr_ref[...]
    w_hi = wr.astype(BF16)
    w_lo = (wr - w_hi.astype(F32)).astype(BF16)

    def logits(s, o):
        bs, rs = sel_part(s)
        x1 = in2(x_ref, s) + per_batch(g1_ref, s) * o
        x1_ref[bs, rs] = x1.reshape(pnb, pr, D)
        h2 = x1 * lax.rsqrt(jnp.mean(x1 * x1, axis=-1, keepdims=True) + EPS) * gf_ref[...]
        h2 = h2 * (1.0 + per_batch(sc_ref, s)) + per_batch(sh_ref, s)
        h_hi = h2.astype(BF16)
        h2_ref[bs, rs] = h_hi.reshape(pnb, pr, D)
        h_lo = (h2 - h_hi.astype(F32)).astype(BF16)
        return (jnp.dot(h_hi, w_hi, preferred_element_type=F32) + jnp.dot(h_hi, w_lo, preferred_element_type=F32)
                + jnp.dot(h_lo, w_hi, preferred_element_type=F32) + br_ref[...])

    def route(s, lg):
        bs, rs = sel_part(s)
        lane = lax.broadcasted_iota(jnp.int32, (PR, ROUTER_LANES), 1)
        big = jnp.int32(ROUTER_LANES)
        is_g = lane < E_GROUPS
        gl = jnp.where(is_g, lg, NEG_INF)
        gmax = jnp.max(gl, axis=-1, keepdims=True)
        p_top = 1.0 / jnp.sum(jnp.where(is_g, jnp.exp(gl - gmax), 0.0), axis=-1, keepdims=True)
        g_idx = jnp.min(jnp.where(is_g & (gl == gmax), lane, big), axis=-1, keepdims=True)
        e_lane = lane - ROUTER_E_OFF
        sel = (e_lane >= 0) & (e_lane < N_EXPERTS) & ((e_lane // E_PER_GROUP) == g_idx)
        l1 = jnp.where(sel, lg, NEG_INF)
        m1 = jnp.max(l1, axis=-1, keepdims=True)
        i1 = jnp.min(jnp.where(sel & (l1 == m1), lane, big), axis=-1, keepdims=True)
        sel2 = sel & (lane != i1)
        l2 = jnp.where(sel2, lg, NEG_INF)
        m2 = jnp.max(l2, axis=-1, keepdims=True)
        i2 = jnp.min(jnp.where(sel2 & (l2 == m2), lane, big), axis=-1, keepdims=True)
        e2 = jnp.exp(m2 - m1)
        w1 = p_top / (1.0 + e2)
        w2 = p_top * e2 / (1.0 + e2)
        gate = jnp.where(lane == i1, w1, 0.0) + jnp.where(lane == i2, w2, 0.0)
        gate_ref[bs, rs] = gate.reshape(pnb, pr, ROUTER_LANES)

    outs = [project(s) for s in range(nparts)]
    lgs = [logits(s, outs[s]) for s in range(nparts)]
    for s in range(nparts):
        route(s, lgs[s])


def _outproj(ya, yb, yc, x, mod, g_ffn, wo, wr, br, *, nb, r, nparts):
    B, S, D = x.shape
    row = lambda c: pl.BlockSpec((nb, r, c), lambda b, i: (b, i, 0))
    modc = lambda j: pl.BlockSpec((nb, 1, D), lambda b, i: (b, 0, j))
    full = lambda a: pl.BlockSpec(a.shape, lambda b, i: (0,) * a.ndim)
    return pl.pallas_call(
        functools.partial(_outproj_kernel, nb=nb, r=r, nparts=nparts),
        grid=(B // nb, S // r),
        in_specs=[row(A_WIDTH), row(B_WIDTH), row(C_D_INNER), row(D), modc(2), modc(3), modc(4),
                  full(g_ffn), full(wo), full(wr), full(br)],
        out_specs=[row(D), row(D), row(ROUTER_LANES)],
        out_shape=[jax.ShapeDtypeStruct((B, S, D), F32), jax.ShapeDtypeStruct((B, S, D), BF16),
                   jax.ShapeDtypeStruct((B, S, ROUTER_LANES), F32)],
        compiler_params=_cparams("arbitrary", "arbitrary"),
        name="outproj",
    )(ya, yb, yc, x, mod, mod, mod, g_ffn, wo, wr, br)


def _moe_kernel(h_ref, gate_ref, x_ref, g2_ref, wgu_ref, wd_ref, *rest, nb, r, nsteps, epb, final):
    if final:
        shf_ref, scf_ref, gfin_ref, o_ref, acc_sc = rest
    else:
        o_ref, acc_sc = rest
    e = pl.program_id(2)
    R = nb * r
    D = x_ref.shape[-1]

    @pl.when(e == 0)
    def _():
        acc_sc[...] = jnp.zeros(acc_sc.shape, F32)

    def as_bf16(w):
        return w if w.dtype == BF16 else w.astype(BF16)

    h = h_ref[...].reshape(R, D)
    gate = gate_ref[...].reshape(R, ROUTER_LANES)
    lane = lax.broadcasted_iota(jnp.int32, (R, ROUTER_LANES), 1)
    acts = []
    for k in range(epb):
        gu = jnp.dot(h, as_bf16(wgu_ref[0, k]), preferred_element_type=F32)
        gcol = jnp.sum(jnp.where(lane == e * epb + k + ROUTER_E_OFF, gate, 0.0), axis=-1, keepdims=True)
        acts.append((_silu(gu[:, 0:E_HIDDEN]) * gu[:, E_HIDDEN:] * gcol).astype(BF16))
    wd = as_bf16(wd_ref[0]).reshape(epb * E_HIDDEN, D)
    acc_sc[...] += jnp.dot(jnp.concatenate(acts, axis=1), wd, preferred_element_type=F32)

    @pl.when(e == nsteps - 1)
    def _():
        xo = x_ref[...].reshape(R, D) + _rows(g2_ref, nb, r) * acc_sc[...]
        if final:
            xo = xo * lax.rsqrt(jnp.mean(xo * xo, axis=-1, keepdims=True) + EPS) * gfin_ref[...]
            xo = xo * (1.0 + _rows(scf_ref, nb, r)) + _rows(shf_ref, nb, r)
        o_ref[...] = xo.reshape(nb, r, D)


def _moe(h2, gate, x1, mod, wgu, wd, layer, final, *, nb, r, epb):
    B, S, D = x1.shape
    ne = wgu.shape[1]
    assert ne % epb == 0
    nsteps = ne // epb
    row = lambda c: pl.BlockSpec((nb, r, c), lambda b, i, e: (b, i, 0))
    modc = lambda j: pl.BlockSpec((nb, 1, D), lambda b, i, e: (b, 0, j))
    in_specs = [row(D), row(ROUTER_LANES), row(D), modc(5),
                pl.BlockSpec((1, epb, D, 2 * E_HIDDEN), lambda b, i, e: (layer, e, 0, 0)),
                pl.BlockSpec((1, epb, E_HIDDEN, D), lambda b, i, e: (layer, e, 0, 0))]
    args = [h2, gate, x1, mod, wgu, wd]
    if final is not None:
        modf, gfin = final
        in_specs += [modc(0), modc(1), pl.BlockSpec(gfin.shape, lambda b, i, e: (0, 0))]
        args += [modf, modf, gfin]
    return pl.pallas_call(
        functools.partial(_moe_kernel, nb=nb, r=r, nsteps=nsteps, epb=epb, final=final is not None),
        grid=(B // nb, S // r, nsteps),
        in_specs=in_specs,
        out_specs=row(D),
        out_shape=jax.ShapeDtypeStruct((B, S, D), F32),
        scratch_shapes=[pltpu.VMEM((nb * r, D), F32)],
        compiler_params=_cparams("arbitrary", "arbitrary", "arbitrary"),
        name="moe",
    )(*args)


def _rope_tables(pos):
    half = B_ROPE // 2
    inv = 1.0 / (ROPE_THETA ** (jnp.arange(half, dtype=F32) / half))
    ang = pos.astype(F32)[:, None] * inv[None, :]
    cos, sin = jnp.cos(ang), jnp.sin(ang)
    return (jnp.tile(jnp.concatenate([cos, cos], axis=1), (1, B_HEADS)),
            jnp.tile(jnp.concatenate([-sin, sin], axis=1), (1, B_HEADS)))


def _swap_halves(w):
    half = w.shape[-1] // 2
    return jnp.concatenate([w[..., half:], w[..., :half]], axis=-1)


def _layer_weights(l, w_in, b_g_q, b_w_uq, b_g_kv, b_w_uk, b_w_uv, w_out, moe_w_rg, moe_b_rg, moe_w_re, moe_b_re):
    D = w_in.shape[1]
    w = w_in[l]
    o_cq = 3 * A_WIDTH
    o_ckv = o_cq + B_Q_LORA
    o_kpe = o_ckv + B_KV_LORA
    o_z = o_kpe + B_ROPE
    o_xbc = o_z + C_D_INNER
    o_dt = o_xbc + C_CONV_DIM
    wkpe = w[:, o_kpe:o_z]
    zc = lambda n: jnp.zeros((D, n), F32)
    w1 = jnp.concatenate([w[:, 0:o_cq], w[:, o_cq:o_ckv], zc(256 - B_Q_LORA), w[:, o_ckv:o_kpe],
                          wkpe, _swap_halves(wkpe), w[:, o_dt:o_dt + C_HEADS],
                          zc(LANES - 2 * B_ROPE - C_HEADS), w[:, o_z:o_xbc], w[:, o_xbc:o_dt]],
                         axis=1).astype(BF16)
    assert w1.shape[1] == _W1_N
    gq = jnp.pad(b_g_q[l], (0, 256 - B_Q_LORA))[None, :]
    uq = b_w_uq[l]
    pe = uq[:, :, B_NOPE:]
    wuq = jnp.concatenate([uq[:, :, :B_NOPE].reshape(B_Q_LORA, -1), pe.reshape(B_Q_LORA, -1),
                           _swap_halves(pe).reshape(B_Q_LORA, -1)], axis=1)
    wuq = jnp.pad(wuq, ((0, 256 - B_Q_LORA), (0, 0))).astype(BF16)
    eye = jnp.eye(B_HEADS, dtype=F32)
    bd = (jnp.transpose(b_w_uk[l], (1, 2, 0))[:, :, None, :] * eye[:, None, :, None]).reshape(
        B_HEADS * B_NOPE, B_HEADS * B_KV_LORA).astype(BF16)
    wuv = (jnp.transpose(b_w_uv[l], (1, 0, 2))[:, :, None, :] * eye[:, None, :, None]).reshape(
        B_HEADS * B_KV_LORA, B_WIDTH).astype(BF16)
    wr = jnp.zeros((D, ROUTER_LANES), F32)
    wr = wr.at[:, 0:E_GROUPS].set(moe_w_rg[l]).at[:, ROUTER_E_OFF:ROUTER_E_OFF + N_EXPERTS].set(moe_w_re[l])
    br = jnp.zeros((1, ROUTER_LANES), F32)
    br = br.at[0, 0:E_GROUPS].set(moe_b_rg[l]).at[0, ROUTER_E_OFF:ROUTER_E_OFF + N_EXPERTS].set(moe_b_re[l])
    return dict(w1=w1, gq=gq, wuq=wuq, bd=bd, gkv=b_g_kv[l][None, :], wuv=wuv, wo=w_out[l].astype(BF16), wr=wr, br=br)


def _band_bias(rel_bias, nq, nk, back):
    n = nk + nq - 1
    idx = np.clip(back + nq - 1 - np.arange(n + 1), -A_REL_CLIP, A_REL_CLIP) + A_REL_CLIP
    line = rel_bias[:, idx]
    skew = jnp.tile(line, (1, nq))[:, 0:nq * n].reshape(A_HEADS, nq, n)
    return skew[:, :, nq - 1:nq - 1 + nk].reshape(A_HEADS * nq, nk)


def _layer(x, mod, pos, lw, l, cache, prm, final, *, nb, r):
    B, S, D = x.shape
    cos_t, sin_t = _rope_tables(pos)
    (qkv, knew, vnew, qmla, kmla, ckv, kpe, z, xbc, dtr) = _inproj(
        x, mod, prm['g_mix'], lw['w1'], lw['gq'], lw['wuq'], lw['bd'], lw['gkv'], cos_t, sin_t, nb=nb, r=r,
        nparts=prm['in_parts'])
    if cache is None:
        ya = _band_prompt(qkv, _band_bias(prm['a_rel_bias'], CHUNK, A_BAND, A_WINDOW))
        yb = _mla_prompt(qmla, kmla, lw['wuv'], bq=512, bk=prm['mla_bk'])
        prefix = jnp.zeros((B, C_CONV_W - 1, C_CONV_DIM), F32)
        h0t = jnp.zeros((B, C_HEADS, C_D_STATE, C_HEAD_DIM), F32)
        lt, lc, nbat = min(S, 512), SSD_CHUNK, B
    else:
        nr = cache['a_kt'].shape[3]
        ya = _band_sample(qkv, cache['a_kt'], cache['a_vt'], l, _band_bias(prm['a_rel_bias'], S, nr + S, nr))
        yb = _mla_sample(qmla, kmla, cache['b_ckv'], cache['b_kpet'], l, lw['wuv'])
        prefix = cache['c_conv'][l]
        h0t = jnp.swapaxes(cache['c_ssm'][l], -1, -2)
        lt = lc = min(SSD_CHUNK, S)
        nbat = math.gcd(B, 4)
    yc, hft = _ssd(xbc, z, dtr, prefix, h0t, prm['c_conv_w'], prm['c_conv_b'], prm['c_dt_bias'], prm['c_a_log'],
                   prm['c_d_exp'], prm['c_g_norm'], nbat=nbat, lt=lt, lc=lc)
    conv_state = jnp.concatenate([prefix, xbc], axis=1)[:, -(C_CONV_W - 1):]
    x1, h2, gate = _outproj(ya, yb, yc, x, mod, prm['g_ffn'], lw['wo'], lw['wr'], lw['br'], nb=nb, r=r,
                            nparts=prm['out_parts'])
    r_moe = r if nb > 1 else min(S, 1024)
    xo = _moe(h2, gate, x1, mod, prm['moe_w_gu'], prm['moe_w_down'], l if prm['moe_w_gu'].shape[0] > 1 else 0,
              final, nb=nb, r=r_moe, epb=2)
    states = (knew.reshape(B, -1, A_HEADS, A_HEAD_DIM), vnew.reshape(B, -1, A_HEADS, A_HEAD_DIM),
              ckv, kpe, conv_state, jnp.swapaxes(hft, -1, -2))
    return xo, states


def kernel(x_prompt, x_sample, c_prompt, c_sample, cache_a_k, cache_a_v, cache_b_ckv, cache_b_kpe,
           state_c_conv, state_c_ssm, w_ada, b_ada, g_mix, w_in, a_rel_bias, b_g_q, b_w_uq, b_g_kv,
           b_w_uk, b_w_uv, c_conv_w, c_conv_b, c_dt_bias, c_a_log, c_d, c_g_norm, w_out, g_ffn,
           moe_w_rg, moe_b_rg, moe_w_re, moe_b_re, moe_w_gu, moe_w_down, g_final, w_ada_f, b_ada_f):
    depth = w_in.shape[0]
    D = x_prompt.shape[-1]
    Bp, Sp, _ = x_prompt.shape
    Bs, Ss, _ = x_sample.shape
    past = cache_b_ckv.shape[2]
    pos_p = jnp.arange(Sp)
    pos_s = past + jnp.arange(Ss)

    nc = Bp + Bs
    ncp = -(-nc // 8) * 8
    c_all = jnp.pad(jnp.concatenate([c_prompt, c_sample], axis=0), ((0, ncp - nc), (0, 0)))
    mod = _ada(c_all, w_ada, b_ada[:, None, :])
    modf = _ada(c_all, w_ada_f[None], b_ada_f[None, None, :])[0]

    nr = cache_a_k.shape[2]
    to_t = lambda c: jnp.transpose(c, (0, 1, 3, 4, 2)).reshape(depth, Bs, A_WIDTH, nr)
    cache = {'a_kt': to_t(cache_a_k), 'a_vt': to_t(cache_a_v), 'b_ckv': cache_b_ckv,
             'b_kpet': jnp.swapaxes(cache_b_kpe, 2, 3), 'c_conv': state_c_conv, 'c_ssm': state_c_ssm}

    rp = min(Sp, A_WINDOW)
    moe_w_gu_b = moe_w_gu[0:1].astype(BF16)
    moe_w_down_b = moe_w_down[0:1].astype(BF16)
    xp, xs = x_prompt, x_sample
    st_p, st_s = [], []
    for l in range(depth):
        lw = _layer_weights(l, w_in, b_g_q, b_w_uq, b_g_kv, b_w_uk, b_w_uv, w_out,
                            moe_w_rg, moe_b_rg, moe_w_re, moe_b_re)
        prm = {'g_mix': g_mix[l][None, :], 'a_rel_bias': a_rel_bias[l], 'c_conv_w': c_conv_w[l],
               'c_conv_b': c_conv_b[l][None, :], 'c_dt_bias': jnp.pad(c_dt_bias[l], (0, LANES - C_HEADS))[None, :],
               'c_a_log': jnp.pad(c_a_log[l], (0, LANES - C_HEADS))[None, :],
               'c_d_exp': jnp.repeat(c_d[l], C_HEAD_DIM)[None, :], 'c_g_norm': c_g_norm[l][None, :],
               'g_ffn': g_ffn[l][None, :],
               'moe_w_gu': moe_w_gu_b if l == 0 else moe_w_gu,
               'moe_w_down': moe_w_down_b if l == 0 else moe_w_down,
               'mla_bk': 512 if l == 0 else 1024,
               'in_parts': 1 if l == 0 else 2, 'out_parts': 2 if l == 0 else 1}
        fin_p = (modf[0:Bp][:, None, :], g_final[None, :]) if l == depth - 1 else None
        fin_s = (modf[Bp:nc][:, None, :], g_final[None, :]) if l == depth - 1 else None
        xp, sp = _layer(xp, mod[l, 0:Bp][:, None, :], pos_p, lw, l, None, prm, fin_p, nb=1, r=rp)
        xs, ss = _layer(xs, mod[l, Bp:nc][:, None, :], pos_s, lw, l, cache, prm, fin_s, nb=Bs, r=Ss)
        st_p.append(sp)
        st_s.append(ss)
    stack = lambda st, k: jnp.stack([t[k] for t in st])
    return ((xp, xs) + tuple(stack(st_p, k) for k in range(6))
            + tuple(stack(st_s, k) for k in range(6)))
```

```python
import functools
import math

import numpy as np
import jax
import jax.numpy as jnp
from jax import lax
from jax.experimental import pallas as pl
from jax.experimental.pallas import tpu as pltpu

F32 = jnp.float32
BF16 = jnp.bfloat16

D_MODEL = 1024
CHUNK = 64
EPS = 1e-6
NEG_INF = -1e30

A_HEADS = 4
A_HEAD_DIM = 64
A_WIDTH = A_HEADS * A_HEAD_DIM
A_BACK = 8
A_BAND = (A_BACK + 1) * CHUNK
A_WINDOW = A_BACK * CHUNK
A_REL_CLIP = 128
A_SCALE = A_HEAD_DIM ** -0.5

B_HEADS = 4
B_Q_LORA = 192
B_KV_LORA = 128
B_NOPE = 64
B_ROPE = 32
B_V = 64
B_WIDTH = B_HEADS * B_V
B_SCALE = (B_NOPE + B_ROPE) ** -0.5
B_QK = B_KV_LORA + B_ROPE
ROPE_THETA = 10000.0

C_D_INNER = 512
C_HEAD_DIM = 64
C_HEADS = C_D_INNER // C_HEAD_DIM
C_GROUPS = 2
C_D_STATE = 64
C_CONV_W = 4
C_CONV_DIM = C_D_INNER + 2 * C_GROUPS * C_D_STATE
C_GN = C_GROUPS * C_D_STATE
C_HPG = C_HEADS // C_GROUPS
SSD_CHUNK = 64

E_GROUPS = 4
E_PER_GROUP = 4
N_EXPERTS = E_GROUPS * E_PER_GROUP
E_HIDDEN = 256
ROUTER_LANES = 128
ROUTER_E_OFF = 16

LANES = 128
VMEM_LIMIT = 56 * 1024 * 1024

B_QSCALE = B_SCALE * math.log2(math.e)

_W1_QKV = 0
_W1_CQ = 3 * A_WIDTH
_W1_CKV = _W1_CQ + 256
_W1_KPE = _W1_CKV + B_KV_LORA
_W1_Z = _W1_KPE + LANES
_W1_XBC = _W1_Z + C_D_INNER
_W1_N = _W1_XBC + C_CONV_DIM


def _cparams(*sem):
    return pltpu.CompilerParams(dimension_semantics=sem, vmem_limit_bytes=VMEM_LIMIT)


def _silu(v):
    return v * jax.nn.sigmoid(v)


def _split3_lanes(v, lane):
    x3 = v + pltpu.roll(v, C_HEADS, axis=1) + pltpu.roll(v, 2 * C_HEADS, axis=1)
    hi = x3.astype(BF16).astype(F32)
    r1 = x3 - hi
    mid = r1.astype(BF16).astype(F32)
    lo = r1 - mid
    return jnp.where(lane < C_HEADS, hi, jnp.where(lane < 2 * C_HEADS, mid, lo)).astype(BF16)


def _ada_kernel(c_ref, w_ref, b_ref, o_ref):
    s = _silu(c_ref[...]).astype(BF16)
    o_ref[0] = jnp.dot(s, w_ref[0].astype(BF16), preferred_element_type=F32) + b_ref[0]


def _ada(c_all, w, b):
    nl, d, n = w.shape
    bp = c_all.shape[0]
    tn = 1024
    return pl.pallas_call(
        _ada_kernel,
        grid=(nl, n // tn),
        in_specs=[pl.BlockSpec((bp, d), lambda l, j: (0, 0)),
                  pl.BlockSpec((1, d, tn), lambda l, j: (l, 0, j)),
                  pl.BlockSpec((1, 1, tn), lambda l, j: (l, 0, j))],
        out_specs=pl.BlockSpec((1, bp, tn), lambda l, j: (l, 0, j)),
        out_shape=jax.ShapeDtypeStruct((nl, bp, n), F32),
        compiler_params=_cparams("arbitrary", "arbitrary"),
        name="ada",
    )(c_all, w, b)


def _rows(ref_or_val, nb, r):
    v = ref_or_val[...]
    return jnp.broadcast_to(v, (nb, r, v.shape[-1])).reshape(nb * r, v.shape[-1])


def _inproj_kernel(x_ref, sh_ref, sc_ref, g_ref, w1_ref, gq_ref, wuq_ref, bd_ref, gkv_ref, cos_ref, sin_ref,
                   qkv_ref, knew_ref, vnew_ref, qmla_ref, kmla_ref, ckv_ref, kpe_ref, z_ref, xbc_ref, dtr_ref,
                   *, nb, r, keep_tile, nparts):
    D = x_ref.shape[-1]
    pnb, pr = (nb // nparts, r) if nb > 1 else (1, r // nparts)
    PR = pnb * pr

    def sel(s):
        return (slice(s * pnb, (s + 1) * pnb), slice(None)) if nb > 1 else (slice(None), slice(s * pr, (s + 1) * pr))

    def per_batch(ref, s):
        v = ref[sel(s)[0]] if nb > 1 else ref[...]
        return jnp.broadcast_to(v, (pnb, pr, v.shape[-1])).reshape(PR, v.shape[-1])

    def per_pos(ref, s):
        v = ref[...] if nb > 1 else ref[sel(s)[1]]
        return jnp.broadcast_to(v, (pnb, pr, v.shape[-1])).reshape(PR, v.shape[-1])

    def project(s):
        bs, rs = sel(s)
        x = x_ref[bs, rs].reshape(PR, D)
        h = x * lax.rsqrt(jnp.mean(x * x, axis=-1, keepdims=True) + EPS) * g_ref[...]
        h = h * (1.0 + per_batch(sc_ref, s)) + per_batch(sh_ref, s)
        return jnp.dot(h.astype(BF16), w1_ref[...], preferred_element_type=F32)

    def finish(s, u):
        bs, rs = sel(s)

        def out3(v):
            return v.reshape(pnb, pr, v.shape[-1])

        qkv_ref[bs, rs, 0:A_WIDTH] = out3((u[:, 0:A_WIDTH] * A_SCALE).astype(BF16))
        qkv_ref[bs, rs, A_WIDTH:3 * A_WIDTH] = out3(u[:, A_WIDTH:3 * A_WIDTH].astype(BF16))

        @pl.when(pl.program_id(1) == keep_tile)
        def _():
            knew_ref[bs, rs] = out3(u[:, A_WIDTH:2 * A_WIDTH])
            vnew_ref[bs, rs] = out3(u[:, 2 * A_WIDTH:3 * A_WIDTH])

        blk = u[:, _W1_CQ:_W1_CQ + 256]
        lane = lax.broadcasted_iota(jnp.int32, (1, 256), 1)
        cq = jnp.where(lane < B_Q_LORA, blk, 0.0)
        cqn = cq * lax.rsqrt(jnp.sum(cq * cq, axis=-1, keepdims=True) * (1.0 / B_Q_LORA) + EPS) * gq_ref[...]
        qb = jnp.dot(cqn.astype(BF16), wuq_ref[...], preferred_element_type=F32)
        nope_w = B_HEADS * B_NOPE
        rope_w = B_HEADS * B_ROPE
        qlat = jnp.dot(qb[:, 0:nope_w].astype(BF16), bd_ref[...], preferred_element_type=F32) * B_QSCALE
        cosr = per_pos(cos_ref, s)
        sinr = per_pos(sin_ref, s)
        qpe = (qb[:, nope_w:nope_w + rope_w] * cosr
               + qb[:, nope_w + rope_w:nope_w + 2 * rope_w] * sinr) * B_QSCALE
        for hd in range(B_HEADS):
            qmla_ref[bs, hd, rs, 0:B_KV_LORA] = out3(qlat[:, hd * B_KV_LORA:(hd + 1) * B_KV_LORA].astype(BF16))
            qmla_ref[bs, hd, rs, B_KV_LORA:B_QK] = out3(qpe[:, hd * B_ROPE:(hd + 1) * B_ROPE].astype(BF16))

        cr = u[:, _W1_CKV:_W1_CKV + B_KV_LORA]
        ckv = cr * lax.rsqrt(jnp.mean(cr * cr, axis=-1, keepdims=True) + EPS) * gkv_ref[...]
        ckv_ref[bs, rs] = out3(ckv)
        kmla_ref[bs, rs, 0:B_KV_LORA] = out3(ckv.astype(BF16))
        kb = u[:, _W1_KPE:_W1_KPE + LANES]
        rot = kb * cosr + pltpu.roll(kb, LANES - B_ROPE, axis=1) * sinr
        kpe_ref[bs, rs] = out3(rot[:, 0:B_ROPE])
        kmla_ref[bs, rs, B_KV_LORA:B_QK] = out3(rot[:, 0:B_ROPE].astype(BF16))
        lane1 = lax.broadcasted_iota(jnp.int32, (1, LANES), 1)
        dtr_ref[bs, rs] = out3(jnp.where(lane1 < C_HEADS, pltpu.roll(kb, LANES - 2 * B_ROPE, axis=1), 0.0))

        z_ref[bs, rs] = out3(u[:, _W1_Z:_W1_Z + C_D_INNER].astype(BF16))
        xbc_ref[bs, rs] = out3(u[:, _W1_XBC:_W1_XBC + C_CONV_DIM])

    us = [project(s) for s in range(nparts)]
    for s in range(nparts):
        finish(s, us[s])


def _inproj(x, mod, g_mix, w1, gq, wuq, bd, gkv, cos_t, sin_t, *, nb, r, nparts):
    B, S, D = x.shape
    nbt, nst = B // nb, S // r
    keep = min(S, A_WINDOW)
    assert keep == r and B % nb == 0 and S % r == 0
    grid = (nbt, nst)
    row = lambda c: pl.BlockSpec((nb, r, c), lambda b, i: (b, i, 0))
    full = lambda a: pl.BlockSpec(a.shape, lambda b, i: (0,) * a.ndim)
    in_specs = [row(D),
                pl.BlockSpec((nb, 1, D), lambda b, i: (b, 0, 0)),
                pl.BlockSpec((nb, 1, D), lambda b, i: (b, 0, 1)),
                full(g_mix), full(w1), full(gq), full(wuq), full(bd), full(gkv),
                pl.BlockSpec((r, LANES), lambda b, i: (i, 0)),
                pl.BlockSpec((r, LANES), lambda b, i: (i, 0))]
    out_shape = [jax.ShapeDtypeStruct((B, S, 3 * A_WIDTH), BF16),
                 jax.ShapeDtypeStruct((B, keep, A_WIDTH), F32),
                 jax.ShapeDtypeStruct((B, keep, A_WIDTH), F32),
                 jax.ShapeDtypeStruct((B, B_HEADS, S, B_QK), BF16),
                 jax.ShapeDtypeStruct((B, S, B_QK), BF16),
                 jax.ShapeDtypeStruct((B, S, B_KV_LORA), F32),
                 jax.ShapeDtypeStruct((B, S, B_ROPE), F32),
                 jax.ShapeDtypeStruct((B, S, C_D_INNER), BF16),
                 jax.ShapeDtypeStruct((B, S, C_CONV_DIM), F32),
                 jax.ShapeDtypeStruct((B, S, LANES), F32)]
    out_specs = [row(3 * A_WIDTH),
                 pl.BlockSpec((nb, keep, A_WIDTH), lambda b, i: (b, 0, 0)),
                 pl.BlockSpec((nb, keep, A_WIDTH), lambda b, i: (b, 0, 0)),
                 pl.BlockSpec((nb, B_HEADS, r, B_QK), lambda b, i: (b, 0, i, 0)),
                 row(B_QK), row(B_KV_LORA), row(B_ROPE), row(C_D_INNER), row(C_CONV_DIM), row(LANES)]
    return pl.pallas_call(
        functools.partial(_inproj_kernel, nb=nb, r=r, keep_tile=nst - 1, nparts=nparts),
        grid=grid, in_specs=in_specs, out_specs=out_specs, out_shape=out_shape,
        compiler_params=_cparams("arbitrary", "arbitrary"),
        name="inproj",
    )(x, mod, mod, g_mix, w1, gq, wuq, bd, gkv, cos_t, sin_t)


def _band_scores(q, k, hmask):
    qs = jnp.where(hmask, jnp.concatenate([q] * A_HEADS, axis=0), jnp.zeros((), BF16))
    return lax.dot_general(qs, k, (((1,), (1,)), ((), ())), preferred_element_type=F32)


def _band_finish(s, v, bias, valid, hmask):
    nq = s.shape[0] // A_HEADS
    s = s + bias
    if valid is not None:
        s = jnp.where(valid, s, NEG_INF)
    m = jnp.max(s, axis=-1, keepdims=True)
    e = jnp.exp(s - m)
    den = jnp.sum(e, axis=-1, keepdims=True)
    o = jnp.dot(e.astype(BF16), v, preferred_element_type=F32)
    o = jnp.where(hmask, o / den, 0.0)
    y = o[0:nq]
    for hd in range(1, A_HEADS):
        y = y + o[hd * nq:(hd + 1) * nq]
    return y


def _head_mask(nq):
    row = lax.broadcasted_iota(jnp.int32, (A_HEADS * nq, A_WIDTH), 0)
    col = lax.broadcasted_iota(jnp.int32, (A_HEADS * nq, A_WIDTH), 1)
    return (row // nq) == (col // A_HEAD_DIM)


def _band_prompt_kernel(cur_ref, prev_ref, bias_ref, o_ref, kwin, vwin, *, tq):
    i = pl.program_id(1)
    nch = tq // CHUNK
    kwin[0:A_WINDOW] = prev_ref[0, tq - A_WINDOW:tq, A_WIDTH:2 * A_WIDTH]
    kwin[A_WINDOW:A_WINDOW + tq] = cur_ref[0, :, A_WIDTH:2 * A_WIDTH]
    vwin[0:A_WINDOW] = prev_ref[0, tq - A_WINDOW:tq, 2 * A_WIDTH:3 * A_WIDTH]
    vwin[A_WINDOW:A_WINDOW + tq] = cur_ref[0, :, 2 * A_WIDTH:3 * A_WIDTH]
    hmask = _head_mask(CHUNK)
    ucol = lax.broadcasted_iota(jnp.int32, (1, A_BAND), 1)

    def scores(c):
        return _band_scores(cur_ref[0, c * CHUNK:(c + 1) * CHUNK, 0:A_WIDTH],
                            kwin[c * CHUNK:c * CHUNK + A_BAND], hmask)

    s_next = scores(0)
    for c in range(nch):
        s = s_next
        if c + 1 < nch:
            s_next = scores(c + 1)
        valid = (ucol // CHUNK + (i * nch + c - A_BACK)) >= 0
        y = _band_finish(s, vwin[c * CHUNK:c * CHUNK + A_BAND], bias_ref[...], valid, hmask)
        o_ref[0, c * CHUNK:(c + 1) * CHUNK, :] = y.astype(BF16)


def _band_prompt(qkv, bias_stack):
    B, S, _ = qkv.shape
    tq = A_WINDOW
    assert S % tq == 0
    return pl.pallas_call(
        functools.partial(_band_prompt_kernel, tq=tq),
        grid=(B, S // tq),
        in_specs=[pl.BlockSpec((1, tq, 3 * A_WIDTH), lambda b, i: (b, i, 0)),
                  pl.BlockSpec((1, tq, 3 * A_WIDTH), lambda b, i: (b, jnp.maximum(i - 1, 0), 0)),
                  pl.BlockSpec(bias_stack.shape, lambda b, i: (0, 0))],
        out_specs=pl.BlockSpec((1, tq, A_WIDTH), lambda b, i: (b, i, 0)),
        out_shape=jax.ShapeDtypeStruct((B, S, A_WIDTH), BF16),
        scratch_shapes=[pltpu.VMEM((A_WINDOW + tq, A_WIDTH), BF16), pltpu.VMEM((A_WINDOW + tq, A_WIDTH), BF16)],
        compiler_params=_cparams("arbitrary", "arbitrary"),
        name="band_prompt",
    )(qkv, qkv, bias_stack)


def _band_sample_kernel(qkv_ref, kt_ref, vt_ref, biasc_ref, biasn_ref, o_ref, *, nbb, ds):
    hmask = _head_mask(ds)
    nt_dims = (((1,), (1,)), ((), ()))
    for b in range(nbb):
        q = qkv_ref[b, :, 0:A_WIDTH]
        kn = qkv_ref[b, :, A_WIDTH:2 * A_WIDTH]
        vn = qkv_ref[b, :, 2 * A_WIDTH:3 * A_WIDTH]
        qs = jnp.where(hmask, jnp.concatenate([q] * A_HEADS, axis=0), jnp.zeros((), BF16))
        sc = jnp.dot(qs, kt_ref[0, b].astype(BF16), preferred_element_type=F32) + biasc_ref[...]
        sn = lax.dot_general(qs, kn, nt_dims, preferred_element_type=F32) + biasn_ref[...]
        m = jnp.maximum(jnp.max(sc, axis=-1, keepdims=True), jnp.max(sn, axis=-1, keepdims=True))
        ec = jnp.exp(sc - m)
        en = jnp.exp(sn - m)
        den = jnp.sum(ec, axis=-1, keepdims=True) + jnp.sum(en, axis=-1, keepdims=True)
        o = lax.dot_general(ec.astype(BF16), vt_ref[0, b].astype(BF16), nt_dims, preferred_element_type=F32)
        o = o + jnp.dot(en.astype(BF16), vn, preferred_element_type=F32)
        o = jnp.where(hmask, o / den, 0.0)
        y = o[0:ds]
        for hd in range(1, A_HEADS):
            y = y + o[hd * ds:(hd + 1) * ds]
        o_ref[b] = y.astype(BF16)


def _band_sample(qkv, cache_kt, cache_vt, layer, bias_stack):
    B, ds, _ = qkv.shape
    nr = cache_kt.shape[3]
    nbb = 4
    assert B % nbb == 0
    bias_c, bias_n = bias_stack[:, 0:nr], bias_stack[:, nr:]
    return pl.pallas_call(
        functools.partial(_band_sample_kernel, nbb=nbb, ds=ds),
        grid=(B // nbb,),
        in_specs=[pl.BlockSpec((nbb, ds, 3 * A_WIDTH), lambda b: (b, 0, 0)),
                  pl.BlockSpec((1, nbb, A_WIDTH, nr), lambda b: (layer, b, 0, 0)),
                  pl.BlockSpec((1, nbb, A_WIDTH, nr), lambda b: (layer, b, 0, 0)),
                  pl.BlockSpec(bias_c.shape, lambda b: (0, 0)),
                  pl.BlockSpec(bias_n.shape, lambda b: (0, 0))],
        out_specs=pl.BlockSpec((nbb, ds, A_WIDTH), lambda b: (b, 0, 0)),
        out_shape=jax.ShapeDtypeStruct((B, ds, A_WIDTH), BF16),
        compiler_params=_cparams("arbitrary"),
        name="band_sample",
    )(qkv, cache_kt, cache_vt, bias_c, bias_n)


def _mla_finish(acc, den, wuv_ref, bq):
    o = acc / den
    o_all = jnp.concatenate([o[hd * bq:(hd + 1) * bq] for hd in range(B_HEADS)], axis=1)
    return jnp.dot(o_all.astype(BF16), wuv_ref[...], preferred_element_type=F32)


def _mla_prompt_kernel(qi_ref, kj_ref, last_ref, q_ref, k_ref, wuv_ref, o_ref, m_sc, l_sc, acc_sc, *, bq, bk, sk):
    p = pl.program_id(1)
    i = qi_ref[p]
    j = kj_ref[p]
    M = B_HEADS * bq

    @pl.when(j == 0)
    def _():
        m_sc[...] = jnp.full(m_sc.shape, NEG_INF, F32)
        l_sc[...] = jnp.zeros(l_sc.shape, F32)
        acc_sc[...] = jnp.zeros(acc_sc.shape, F32)

    def step(masked):
        q = q_ref[0].reshape(M, B_QK)
        nt_dims = (((1,), (1,)), ((), ()))
        nsub = bk // sk

        def scores(t):
            return lax.dot_general(q, k_ref[0, t * sk:(t + 1) * sk, :], nt_dims, preferred_element_type=F32)

        m_run, l_run, acc = m_sc[...], l_sc[...], acc_sc[...]
        s_next = scores(0)
        for t in range(nsub):
            s = s_next
            if t + 1 < nsub:
                s_next = scores(t + 1)
            if masked:
                row = lax.broadcasted_iota(jnp.int32, (M, sk), 0)
                col = lax.broadcasted_iota(jnp.int32, (M, sk), 1)
                s = jnp.where((col + (j * bk + t * sk)) // CHUNK <= ((row % bq) + i * bq) // CHUNK, s, NEG_INF)
            m_new = jnp.maximum(m_run, jnp.max(s, axis=1, keepdims=True))
            alpha = jnp.exp2(m_run - m_new)
            pr = jnp.exp2(s - jnp.tile(m_new, (1, sk // LANES)))
            l_run = alpha * l_run + jnp.sum(pr, axis=1, keepdims=True)
            acc = acc * alpha + jnp.dot(pr.astype(BF16), k_ref[0, t * sk:(t + 1) * sk, 0:B_KV_LORA],
                                        preferred_element_type=F32)
            m_run = m_new
        m_sc[...] = m_run
        l_sc[...] = l_run
        acc_sc[...] = acc

    @pl.when(last_ref[p] == 0)
    def _():
        step(False)

    @pl.when(last_ref[p] == 1)
    def _():
        step(True)
        o_ref[0] = _mla_finish(acc_sc[...], l_sc[...], wuv_ref, bq).astype(BF16)


def _mla_prompt(qmla, kmla, wuv_bd, *, bq, bk):
    B, S = kmla.shape[0:2]
    bk = min(bk, S)
    assert S % bq == 0 and S % bk == 0 and bq % CHUNK == 0 and bk % bq == 0
    nq = S // bq
    n_kv = [(i * bq + bq - 1) // bk + 1 for i in range(nq)]
    qi = np.concatenate([np.full(n, i, np.int32) for i, n in enumerate(n_kv)])
    kj = np.concatenate([np.arange(n, dtype=np.int32) for n in n_kv])
    last = np.concatenate([(np.arange(n) == n - 1).astype(np.int32) for n in n_kv])
    grid_spec = pltpu.PrefetchScalarGridSpec(
        num_scalar_prefetch=3,
        grid=(B, len(qi)),
        in_specs=[pl.BlockSpec((1, B_HEADS, bq, B_QK), lambda b, p, qi, kj, last: (b, 0, qi[p], 0)),
                  pl.BlockSpec((1, bk, B_QK), lambda b, p, qi, kj, last: (b, kj[p], 0)),
                  pl.BlockSpec(wuv_bd.shape, lambda b, p, qi, kj, last: (0, 0))],
        out_specs=pl.BlockSpec((1, bq, B_WIDTH), lambda b, p, qi, kj, last: (b, qi[p], 0)),
        scratch_shapes=[pltpu.VMEM((B_HEADS * bq, LANES), F32), pltpu.VMEM((B_HEADS * bq, LANES), F32),
                        pltpu.VMEM((B_HEADS * bq, B_KV_LORA), F32)])
    return pl.pallas_call(
        functools.partial(_mla_prompt_kernel, bq=bq, bk=bk, sk=min(bk, bq)),
        grid_spec=grid_spec,
        out_shape=jax.ShapeDtypeStruct((B, S, B_WIDTH), BF16),
        compiler_params=_cparams("arbitrary", "arbitrary"),
        name="mla_prompt",
    )(jnp.asarray(qi), jnp.asarray(kj), jnp.asarray(last), qmla, kmla, wuv_bd)


def _mla_sample_kernel(q_ref, kn_ref, ckv_ref, kpet_ref, wuv_ref, o_ref, *, ds):
    M = B_HEADS * ds
    nt_dims = (((1,), (1,)), ((), ()))
    q = q_ref[0].reshape(M, B_QK)
    kn = kn_ref[0]
    ckv = ckv_ref[0, 0].astype(BF16)
    sc = (lax.dot_general(q[:, 0:B_KV_LORA], ckv, nt_dims, preferred_element_type=F32)
          + jnp.dot(q[:, B_KV_LORA:B_QK], kpet_ref[0, 0].astype(BF16), preferred_element_type=F32))
    sn = lax.dot_general(q, kn, nt_dims, preferred_element_type=F32)
    m = jnp.maximum(jnp.max(sc, axis=1, keepdims=True), jnp.max(sn, axis=1, keepdims=True))
    pc = jnp.exp2(sc - m)
    pn = jnp.exp2(sn - m)
    den = jnp.sum(pc, axis=1, keepdims=True) + jnp.sum(pn, axis=1, keepdims=True)
    acc = (jnp.dot(pc.astype(BF16), ckv, preferred_element_type=F32)
           + jnp.dot(pn.astype(BF16), kn[:, 0:B_KV_LORA], preferred_element_type=F32))
    o_ref[0] = _mla_finish(acc, den, wuv_ref, ds).astype(BF16)


def _mla_sample(qmla, kmla, cache_ckv, cache_kpet, layer, wuv_bd):
    B, ds = kmla.shape[0:2]
    past = cache_ckv.shape[2]
    return pl.pallas_call(
        functools.partial(_mla_sample_kernel, ds=ds),
        grid=(B,),
        in_specs=[pl.BlockSpec((1, B_HEADS, ds, B_QK), lambda b: (b, 0, 0, 0)),
                  pl.BlockSpec((1, ds, B_QK), lambda b: (b, 0, 0)),
                  pl.BlockSpec((1, 1, past, B_KV_LORA), lambda b: (layer, b, 0, 0)),
                  pl.BlockSpec((1, 1, B_ROPE, past), lambda b: (layer, b, 0, 0)),
                  pl.BlockSpec(wuv_bd.shape, lambda b: (0, 0))],
        out_specs=pl.BlockSpec((1, ds, B_WIDTH), lambda b: (b, 0, 0)),
        out_shape=jax.ShapeDtypeStruct((B, ds, B_WIDTH), BF16),
        compiler_params=_cparams("arbitrary"),
        name="mla_sample",
    )(qmla, kmla, cache_ckv, cache_kpet, wuv_bd)


def _ssd_kernel(xbc_ref, z_ref, dtr_ref, pre_ref, h0_ref, cw_ref, cb_ref, dtb_ref, alog_ref, dsk_ref, gn_ref,
                es_ref, ep_ref, tri_ref, y_ref, hout_ref,
                cbuf, xs_sc, b_sc, c_sc, ces_sc, cep_sc, dep_sc, st_sc, *, nbat, lt, lc, nt):
    t = pl.program_id(1)
    nch = lt // lc
    HS = C_HEADS * lc
    gw = C_HPG * C_HEAD_DIM

    @pl.when(t == 0)
    def _():
        st_sc[...] = jnp.zeros(st_sc.shape, F32)
        for b in range(nbat):
            cbuf[b, 8 - (C_CONV_W - 1):8] = pre_ref[b]
            for g in range(C_GROUPS):
                blk = jnp.concatenate([h0_ref[b, g * C_HPG + hl] for hl in range(C_HPG)], axis=1)
                st_sc[b, g * C_D_STATE:(g + 1) * C_D_STATE, g * gw:(g + 1) * gw] = blk

    hl = lax.broadcasted_iota(jnp.int32, (lt, LANES), 1)
    for b in range(nbat):
        cbuf[b, 8:8 + lt] = xbc_ref[b]
        acc = jnp.broadcast_to(cb_ref[...], (lt, C_CONV_DIM))
        for kk in range(C_CONV_W):
            off = 8 - (C_CONV_W - 1) + kk
            acc = acc + cbuf[b, off:off + lt] * cw_ref[kk:kk + 1, :]
        tail = cbuf[b, 8 + lt - (C_CONV_W - 1):8 + lt]
        cbuf[b, 8 - (C_CONV_W - 1):8] = tail
        xc = _silu(acc)
        xs_sc[b] = xc[:, 0:C_D_INNER]
        b_sc[b] = xc[:, C_D_INNER:C_D_INNER + C_GN]
        c_sc[b] = xc[:, C_D_INNER + C_GN:C_D_INNER + 2 * C_GN]

        dt = jnp.where(hl < C_HEADS, jax.nn.softplus(dtr_ref[b] + dtb_ref[...]), 0.0)
        da = dt * (-jnp.exp(alog_ref[...]))
        c3 = jnp.dot(tri_ref[...], _split3_lanes(da, hl), preferred_element_type=F32)
        cum = jnp.where(hl < C_HEADS, c3 + pltpu.roll(c3, LANES - C_HEADS, axis=1)
                        + pltpu.roll(c3, LANES - 2 * C_HEADS, axis=1), 0.0)
        cum3 = _split3_lanes(cum, hl)
        cep = jnp.dot(cum3, ep_ref[...], preferred_element_type=F32)
        cep_sc[b] = cep
        ces_sc[b] = cep if HS == C_D_INNER and lc == C_HEAD_DIM else jnp.dot(cum3, es_ref[...],
                                                                           preferred_element_type=F32)
        dep_sc[b] = jnp.dot(_split3_lanes(dt, hl), ep_ref[...], preferred_element_type=F32)

    r_s = lax.broadcasted_iota(jnp.int32, (lc, HS), 0)
    l_s = lax.broadcasted_iota(jnp.int32, (lc, HS), 1)
    eye_t = (l_s % lc) == r_s
    causal_t = (l_s % lc) <= r_s
    r_w = lax.broadcasted_iota(jnp.int32, (HS, C_GN), 0)
    l_w = lax.broadcasted_iota(jnp.int32, (HS, C_GN), 1)
    gmask = (r_w // (C_HPG * lc)) == (l_w // C_D_STATE)
    r_b =lax.broadcasted_iota(jnp.int32, (C_HPG * lc, gw), 0)
    l_b = lax.broadcasted_iota(jnp.int32, (C_HPG * lc, gw), 1)
    bmask = (r_b // lc) == (l_b // C_HEAD_DIM)
    r_g = lax.broadcasted_iota(jnp.int32, (C_GN, C_D_INNER), 0)
    l_g = lax.broadcasted_iota(jnp.int32, (C_GN, C_D_INNER), 1)
    smask = (r_g // C_D_STATE) == (l_g // gw)

    def stage_cb(b, rows):
        cm = c_sc[b, rows, :].astype(BF16)
        bm = b_sc[b, rows, :]
        w_nt = jnp.where(gmask, jnp.concatenate([bm] * C_HEADS, axis=0), 0.0).astype(BF16)
        cbcat = lax.dot_general(cm, w_nt, (((1,), (1,)), ((), ())), preferred_element_type=F32)
        st = st_sc[b]
        yoff = jnp.dot(cm, st.astype(BF16), preferred_element_type=F32)
        return dict(bm=bm, cbcat=cbcat, st=st, yoff=yoff)

    def stage_diag(b, rows, v):
        ce_s = ces_sc[b, rows, :]
        ce_p = cep_sc[b, rows, :]
        xs = xs_sc[b, rows, :]
        rflat = jnp.sum(jnp.where(eye_t, ce_s, 0.0), axis=0, keepdims=True)
        lcat = jnp.exp(jnp.where(causal_t, ce_s - rflat, NEG_INF))
        mcat = (v['cbcat'] * lcat).astype(BF16)
        xdt = xs * dep_sc[b, rows, :]
        xdt_b = xdt.astype(BF16)
        ydiag = []
        for g in range(C_GROUPS):
            bdx = jnp.where(bmask, jnp.concatenate([xdt_b[:, g * gw:(g + 1) * gw]] * C_HPG, axis=0),
                            jnp.zeros((), BF16))
            ydiag.append(jnp.dot(mcat[:, g * C_HPG * lc:(g + 1) * C_HPG * lc], bdx, preferred_element_type=F32))
        v.update(ce_p=ce_p, xs=xs, xdt=xdt, ydiag=jnp.concatenate(ydiag, axis=1))

    def stage_state(b, rows, v):
        ce_p = v['ce_p']
        last = ce_p[lc - 1:lc, :]
        xw = (v['xdt'] * jnp.exp(last - ce_p)).astype(BF16)
        upd = lax.dot_general(v['bm'].astype(BF16), xw, (((0,), (0,)), ((), ())), preferred_element_type=F32)
        st_sc[b] = jnp.exp(last) * v['st'] + jnp.where(smask, upd, 0.0)

    def stage_out(b, rows, v):
        y = v['ydiag'] + v['yoff'] * jnp.exp(v['ce_p']) + dsk_ref[...] * v['xs']
        y = y * _silu(z_ref[b, rows, :].astype(F32))
        y = y * lax.rsqrt(jnp.mean(y * y, axis=-1, keepdims=True) + EPS) * gn_ref[...]
        y_ref[b, rows, :] = y.astype(BF16)

    def chunk(c, carry):
        rows = pl.ds(pl.multiple_of(c * lc, lc), lc)
        vals = [stage_cb(b, rows) for b in range(nbat)]
        for stage in (stage_diag, stage_state, stage_out):
            for b in range(nbat):
                stage(b, rows, vals[b])
        return carry

    lax.fori_loop(0, nch, chunk, 0)

    @pl.when(t == nt - 1)
    def _():
        for b in range(nbat):
            for hd in range(C_HEADS):
                g = hd // C_HPG
                hout_ref[b, hd] = st_sc[b, g * C_D_STATE:(g + 1) * C_D_STATE,
                                        hd * C_HEAD_DIM:(hd + 1) * C_HEAD_DIM]


def _ssd(xbc, z, dtr, prefix, h0t, cw, cb, dtb, alog, dsk, gn, *, nbat, lt, lc):
    B, S, _ = xbc.shape
    nt = S // lt
    assert S % lt == 0 and lt % lc == 0 and B % nbat == 0
    hs = C_HEADS * lc

    def expand(width, per_head):
        e = np.zeros((LANES, width), np.float32)
        for part in range(3):
            e[part * C_HEADS:(part + 1) * C_HEADS] = (np.arange(width)[None, :] // per_head
                                                      == np.arange(C_HEADS)[:, None])
        return jnp.asarray(e, dtype=BF16)

    es, ep = expand(hs, lc), expand(C_D_INNER, C_HEAD_DIM)
    pos = np.arange(lt)
    tri = jnp.asarray((pos[None, :] <= pos[:, None]) & (pos[None, :] // lc == pos[:, None] // lc), dtype=BF16)
    row = lambda c: pl.BlockSpec((nbat, lt, c), lambda b, t: (b, t, 0))
    per_b = lambda a: pl.BlockSpec((nbat,) + a.shape[1:], lambda b, t: (b,) + (0,) * (a.ndim - 1))
    full = lambda a: pl.BlockSpec(a.shape, lambda b, t: (0,) * a.ndim)
    sc = lambda *shape: pltpu.VMEM((nbat,) + shape, F32)
    return pl.pallas_call(
        functools.partial(_ssd_kernel, nbat=nbat, lt=lt, lc=lc, nt=nt),
        grid=(B // nbat, nt),
        in_specs=[row(C_CONV_DIM), row(C_D_INNER), row(LANES), per_b(prefix), per_b(h0t),
                  full(cw), full(cb), full(dtb), full(alog), full(dsk), full(gn), full(es), full(ep), full(tri)],
        out_specs=[row(C_D_INNER), per_b(h0t)],
        out_shape=[jax.ShapeDtypeStruct((B, S, C_D_INNER), BF16),
                   jax.ShapeDtypeStruct(h0t.shape, F32)],
        scratch_shapes=[sc(lt + 8, C_CONV_DIM), sc(lt, C_D_INNER), sc(lt, C_GN), sc(lt, C_GN),
                        sc(lt, hs), sc(lt, C_D_INNER), sc(lt, C_D_INNER), sc(C_GN, C_D_INNER)],
        compiler_params=_cparams("arbitrary", "arbitrary"),
        name="ssd",
    )(xbc, z, dtr, prefix, h0t, cw, cb, dtb, alog, dsk, gn, es, ep, tri)


def _outproj_kernel(ya_ref, yb_ref, yc_ref, x_ref, g1_ref, sh_ref, sc_ref, gf_ref, wo_ref, wr_ref, br_ref,
                    x1_ref, h2_ref, gate_ref, *, nb, r, nparts):
    D = x_ref.shape[-1]
    pnb, pr = (nb // nparts, r) if nb > 1 else (1, r // nparts)
    PR = pnb * pr

    def sel_part(s):
        return (slice(s * pnb, (s + 1) * pnb), slice(None)) if nb > 1 else (slice(None), slice(s * pr, (s + 1) * pr))

    def in2(ref, s):
        bs, rs = sel_part(s)
        return ref[bs, rs].reshape(PR, ref.shape[-1])

    def per_batch(ref, s):
        v = ref[sel_part(s)[0]] if nb > 1 else ref[...]
        return jnp.broadcast_to(v, (pnb, pr, v.shape[-1])).reshape(PR, v.shape[-1])

    def project(s):
        o = jnp.dot(in2(ya_ref, s), wo_ref[0:A_WIDTH, :], preferred_element_type=F32)
        o = o + jnp.dot(in2(yb_ref, s), wo_ref[A_WIDTH:A_WIDTH + B_WIDTH, :], preferred_element_type=F32)
        return o + jnp.dot(in2(yc_ref, s), wo_ref[A_WIDTH + B_WIDTH:, :], preferred_element_type=F32)

    wr = wr_ref[...]
    w_hi = wr.astype(BF16)
    w_lo = (wr - w_hi.astype(F32)).astype(BF16)

    def logits(s, o):
        bs, rs = sel_part(s)
        x1 = in2(x_ref, s) + per_batch(g1_ref, s) * o
        x1_ref[bs, rs] = x1.reshape(pnb, pr, D)
        h2 = x1 * lax.rsqrt(jnp.mean(x1 * x1, axis=-1, keepdims=True) + EPS) * gf_ref[...]
        h2 = h2 * (1.0 + per_batch(sc_ref, s)) + per_batch(sh_ref, s)
        h_hi = h2.astype(BF16)
        h2_ref[bs, rs] = h_hi.reshape(pnb, pr, D)
        h_lo = (h2 - h_hi.astype(F32)).astype(BF16)
        return (jnp.dot(h_hi, w_hi, preferred_element_type=F32) + jnp.dot(h_hi, w_lo, preferred_element_type=F32)
                + jnp.dot(h_lo, w_hi, preferred_element_type=F32) + br_ref[...])

    def route(s, lg):
        bs, rs = sel_part(s)
        lane = lax.broadcasted_iota(jnp.int32, (PR, ROUTER_LANES), 1)
        big = jnp.int32(ROUTER_LANES)
        is_g = lane < E_GROUPS
        gl = jnp.where(is_g, lg, NEG_INF)
        gmax = jnp.max(gl, axis=-1, keepdims=True)
        p_top = 1.0 / jnp.sum(jnp.where(is_g, jnp.exp(gl - gmax), 0.0), axis=-1, keepdims=True)
        g_idx = jnp.min(jnp.where(is_g & (gl == gmax), lane, big), axis=-1, keepdims=True)
        e_lane = lane - ROUTER_E_OFF
        sel = (e_lane >= 0) & (e_lane < N_EXPERTS) & ((e_lane // E_PER_GROUP) == g_idx)
        l1 = jnp.where(sel, lg, NEG_INF)
        m1 = jnp.max(l1, axis=-1, keepdims=True)
        i1 = jnp.min(jnp.where(sel & (l1 == m1), lane, big), axis=-1, keepdims=True)
        sel2 = sel & (lane != i1)
        l2 = jnp.where(sel2, lg, NEG_INF)
        m2 = jnp.max(l2, axis=-1, keepdims=True)
        i2 = jnp.min(jnp.where(sel2 & (l2 == m2), lane, big), axis=-1, keepdims=True)
        e2 = jnp.exp(m2 - m1)
        w1 = p_top / (1.0 + e2)
        w2 = p_top * e2 / (1.0 + e2)
        gate = jnp.where(lane == i1, w1, 0.0) + jnp.where(lane == i2, w2, 0.0)
        gate_ref[bs, rs] = gate.reshape(pnb, pr, ROUTER_LANES)

    outs = [project(s) for s in range(nparts)]
    lgs = [logits(s, outs[s]) for s in range(nparts)]
    for s in range(nparts):
        route(s, lgs[s])


def _outproj(ya, yb, yc, x, mod, g_ffn, wo, wr, br, *, nb, r, nparts):
    B, S, D = x.shape
    row = lambda c: pl.BlockSpec((nb, r, c), lambda b, i: (b, i, 0))
    modc = lambda j: pl.BlockSpec((nb, 1, D), lambda b, i: (b, 0, j))
    full = lambda a: pl.BlockSpec(a.shape, lambda b, i: (0,) * a.ndim)
    return pl.pallas_call(
        functools.partial(_outproj_kernel, nb=nb, r=r, nparts=nparts),
        grid=(B // nb, S // r),
        in_specs=[row(A_WIDTH), row(B_WIDTH), row(C_D_INNER), row(D), modc(2), modc(3), modc(4),
                  full(g_ffn), full(wo), full(wr), full(br)],
        out_specs=[row(D), row(D), row(ROUTER_LANES)],
        out_shape=[jax.ShapeDtypeStruct((B, S, D), F32), jax.ShapeDtypeStruct((B, S, D), BF16),
                   jax.ShapeDtypeStruct((B, S, ROUTER_LANES), F32)],
        compiler_params=_cparams("arbitrary", "arbitrary"),
        name="outproj",
    )(ya, yb, yc, x, mod, mod, mod, g_ffn, wo, wr, br)


def _moe_kernel(h_ref, gate_ref, x_ref, g2_ref, wgu_ref, wd_ref, *rest, nb, r, nsteps, epb, final):
    if final:
        shf_ref, scf_ref, gfin_ref, o_ref, acc_sc = rest
    else:
        o_ref, acc_sc = rest
    e = pl.program_id(2)
    R = nb * r
    D = x_ref.shape[-1]

    @pl.when(e == 0)
    def _():
        acc_sc[...] = jnp.zeros(acc_sc.shape, F32)

    def as_bf16(w):
        return w if w.dtype == BF16 else w.astype(BF16)

    h = h_ref[...].reshape(R, D)
    gate = gate_ref[...].reshape(R, ROUTER_LANES)
    lane = lax.broadcasted_iota(jnp.int32, (R, ROUTER_LANES), 1)
    acts = []
    for k in range(epb):
        gu = jnp.dot(h, as_bf16(wgu_ref[0, k]), preferred_element_type=F32)
        gcol = jnp.sum(jnp.where(lane == e * epb + k + ROUTER_E_OFF, gate, 0.0), axis=-1, keepdims=True)
        acts.append((_silu(gu[:, 0:E_HIDDEN]) * gu[:, E_HIDDEN:] * gcol).astype(BF16))
    wd = as_bf16(wd_ref[0]).reshape(epb * E_HIDDEN, D)
    acc_sc[...] += jnp.dot(jnp.concatenate(acts, axis=1), wd, preferred_element_type=F32)

    @pl.when(e == nsteps - 1)
    def _():
        xo = x_ref[...].reshape(R, D) + _rows(g2_ref, nb, r) * acc_sc[...]
        if final:
            xo = xo * lax.rsqrt(jnp.mean(xo * xo, axis=-1, keepdims=True) + EPS) * gfin_ref[...]
            xo = xo * (1.0 + _rows(scf_ref, nb, r)) + _rows(shf_ref, nb, r)
        o_ref[...] = xo.reshape(nb, r, D)


def _moe(h2, gate, x1, mod, wgu, wd, layer, final, *, nb, r, epb):
    B, S, D = x1.shape
    ne = wgu.shape[1]
    assert ne % epb == 0
    nsteps = ne // epb
    row = lambda c: pl.BlockSpec((nb, r, c), lambda b, i, e: (b, i, 0))
    modc = lambda j: pl.BlockSpec((nb, 1, D), lambda b, i, e: (b, 0, j))
    in_specs = [row(D), row(ROUTER_LANES), row(D), modc(5),
                pl.BlockSpec((1, epb, D, 2 * E_HIDDEN), lambda b, i, e: (layer, e, 0, 0)),
                pl.BlockSpec((1, epb, E_HIDDEN, D), lambda b, i, e: (layer, e, 0, 0))]
    args = [h2, gate, x1, mod, wgu, wd]
    if final is not None:
        modf, gfin = final
        in_specs += [modc(0), modc(1), pl.BlockSpec(gfin.shape, lambda b, i, e: (0, 0))]
        args += [modf, modf, gfin]
    return pl.pallas_call(
        functools.partial(_moe_kernel, nb=nb, r=r, nsteps=nsteps, epb=epb, final=final is not None),
        grid=(B // nb, S // r, nsteps),
        in_specs=in_specs,
        out_specs=row(D),
        out_shape=jax.ShapeDtypeStruct((B, S, D), F32),
        scratch_shapes=[pltpu.VMEM((nb * r, D), F32)],
        compiler_params=_cparams("arbitrary", "arbitrary", "arbitrary"),
        name="moe",
    )(*args)


def _rope_tables(pos):
    half = B_ROPE // 2
    inv = 1.0 / (ROPE_THETA ** (jnp.arange(half, dtype=F32) / half))
    ang = pos.astype(F32)[:, None] * inv[None, :]
    cos, sin = jnp.cos(ang), jnp.sin(ang)
    return (jnp.tile(jnp.concatenate([cos, cos], axis=1), (1, B_HEADS)),
            jnp.tile(jnp.concatenate([-sin, sin], axis=1), (1, B_HEADS)))


def _swap_halves(w):
    half = w.shape[-1] // 2
    return jnp.concatenate([w[..., half:], w[..., :half]], axis=-1)


def _layer_weights(l, w_in, b_g_q, b_w_uq, b_g_kv, b_w_uk, b_w_uv, w_out, moe_w_rg, moe_b_rg, moe_w_re, moe_b_re):
    D = w_in.shape[1]
    w = w_in[l]
    o_cq = 3 * A_WIDTH
    o_ckv = o_cq + B_Q_LORA
    o_kpe = o_ckv + B_KV_LORA
    o_z = o_kpe + B_ROPE
    o_xbc = o_z + C_D_INNER
    o_dt = o_xbc + C_CONV_DIM
    wkpe = w[:, o_kpe:o_z]
    zc = lambda n: jnp.zeros((D, n), F32)
    w1 = jnp.concatenate([w[:, 0:o_cq], w[:, o_cq:o_ckv], zc(256 - B_Q_LORA), w[:, o_ckv:o_kpe],
                          wkpe, _swap_halves(wkpe), w[:, o_dt:o_dt + C_HEADS],
                          zc(LANES - 2 * B_ROPE - C_HEADS), w[:, o_z:o_xbc], w[:, o_xbc:o_dt]],
                         axis=1).astype(BF16)
    assert w1.shape[1] == _W1_N
    gq = jnp.pad(b_g_q[l], (0, 256 - B_Q_LORA))[None, :]
    uq = b_w_uq[l]
    pe = uq[:, :, B_NOPE:]
    wuq = jnp.concatenate([uq[:, :, :B_NOPE].reshape(B_Q_LORA, -1), pe.reshape(B_Q_LORA, -1),
                           _swap_halves(pe).reshape(B_Q_LORA, -1)], axis=1)
    wuq = jnp.pad(wuq, ((0, 256 - B_Q_LORA), (0, 0))).astype(BF16)
    eye = jnp.eye(B_HEADS, dtype=F32)
    bd = (jnp.transpose(b_w_uk[l], (1, 2, 0))[:, :, None, :] * eye[:, None, :, None]).reshape(
        B_HEADS * B_NOPE, B_HEADS * B_KV_LORA).astype(BF16)
    wuv = (jnp.transpose(b_w_uv[l], (1, 0, 2))[:, :, None, :] * eye[:, None, :, None]).reshape(
        B_HEADS * B_KV_LORA, B_WIDTH).astype(BF16)
    wr = jnp.zeros((D, ROUTER_LANES), F32)
    wr = wr.at[:, 0:E_GROUPS].set(moe_w_rg[l]).at[:, ROUTER_E_OFF:ROUTER_E_OFF + N_EXPERTS].set(moe_w_re[l])
    br = jnp.zeros((1, ROUTER_LANES), F32)
    br = br.at[0, 0:E_GROUPS].set(moe_b_rg[l]).at[0, ROUTER_E_OFF:ROUTER_E_OFF + N_EXPERTS].set(moe_b_re[l])
    return dict(w1=w1, gq=gq, wuq=wuq, bd=bd, gkv=b_g_kv[l][None, :], wuv=wuv, wo=w_out[l].astype(BF16), wr=wr, br=br)


def _band_bias(rel_bias, nq, nk, back):
    n = nk + nq - 1
    idx = np.clip(back + nq - 1 - np.arange(n + 1), -A_REL_CLIP, A_REL_CLIP) + A_REL_CLIP
    line = rel_bias[:, idx]
    skew = jnp.tile(line, (1, nq))[:, 0:nq * n].reshape(A_HEADS, nq, n)
    return skew[:, :, nq - 1:nq - 1 + nk].reshape(A_HEADS * nq, nk)


def _layer(x, mod, pos, lw, l, cache, prm, final, *, nb, r):
    B, S, D = x.shape
    cos_t, sin_t = _rope_tables(pos)
    (qkv, knew, vnew, qmla, kmla, ckv, kpe, z, xbc, dtr) = _inproj(
        x, mod, prm['g_mix'], lw['w1'], lw['gq'], lw['wuq'], lw['bd'], lw['gkv'], cos_t, sin_t, nb=nb, r=r,
        nparts=prm['in_parts'])
    if cache is None:
        ya = _band_prompt(qkv, _band_bias(prm['a_rel_bias'], CHUNK, A_BAND, A_WINDOW))
        yb = _mla_prompt(qmla, kmla, lw['wuv'], bq=512, bk=prm['mla_bk'])
        prefix = jnp.zeros((B, C_CONV_W - 1, C_CONV_DIM), F32)
        h0t = jnp.zeros((B, C_HEADS, C_D_STATE, C_HEAD_DIM), F32)
        lt, lc, nbat = min(S, 512), SSD_CHUNK, B
    else:
        nr = cache['a_kt'].shape[3]
        ya = _band_sample(qkv, cache['a_kt'], cache['a_vt'], l, _band_bias(prm['a_rel_bias'], S, nr + S, nr))
        yb = _mla_sample(qmla, kmla, cache['b_ckv'], cache['b_kpet'], l, lw['wuv'])
        prefix = cache['c_conv'][l]
        h0t = jnp.swapaxes(cache['c_ssm'][l], -1, -2)
        lt = lc = min(SSD_CHUNK, S)
        nbat = math.gcd(B, 4)
    yc, hft = _ssd(xbc, z, dtr, prefix, h0t, prm['c_conv_w'], prm['c_conv_b'], prm['c_dt_bias'], prm['c_a_log'],
                   prm['c_d_exp'], prm['c_g_norm'], nbat=nbat, lt=lt, lc=lc)
    conv_state = jnp.concatenate([prefix, xbc], axis=1)[:, -(C_CONV_W - 1):]
    x1, h2, gate = _outproj(ya, yb, yc, x, mod, prm['g_ffn'], lw['wo'], lw['wr'], lw['br'], nb=nb, r=r,
                            nparts=prm['out_parts'])
    r_moe = r if nb > 1 else min(S, 1024)
    xo = _moe(h2, gate, x1, mod, prm['moe_w_gu'], prm['moe_w_down'], l if prm['moe_w_gu'].shape[0] > 1 else 0,
              final, nb=nb, r=r_moe, epb=4 if prm['moe_w_gu'].dtype == BF16 else 2)
    states = (knew.reshape(B, -1, A_HEADS, A_HEAD_DIM), vnew.reshape(B, -1, A_HEADS, A_HEAD_DIM),
              ckv, kpe, conv_state, jnp.swapaxes(hft, -1, -2))
    return xo, states


def kernel(x_prompt, x_sample, c_prompt, c_sample, cache_a_k, cache_a_v, cache_b_ckv, cache_b_kpe,
           state_c_conv, state_c_ssm, w_ada, b_ada, g_mix, w_in, a_rel_bias, b_g_q, b_w_uq, b_g_kv,
           b_w_uk, b_w_uv, c_conv_w, c_conv_b, c_dt_bias, c_a_log, c_d, c_g_norm, w_out, g_ffn,
           moe_w_rg, moe_b_rg, moe_w_re, moe_b_re, moe_w_gu, moe_w_down, g_final, w_ada_f, b_ada_f):
    depth = w_in.shape[0]
    D = x_prompt.shape[-1]
    Bp, Sp, _ = x_prompt.shape
    Bs, Ss, _ = x_sample.shape
    past = cache_b_ckv.shape[2]
    pos_p = jnp.arange(Sp)
    pos_s = past + jnp.arange(Ss)

    nc = Bp + Bs
    ncp = -(-nc // 8) * 8
    c_all = jnp.pad(jnp.concatenate([c_prompt, c_sample], axis=0), ((0, ncp - nc), (0, 0)))
    mod = _ada(c_all, w_ada, b_ada[:, None, :])
    modf = _ada(c_all, w_ada_f[None], b_ada_f[None, None, :])[0]

    nr = cache_a_k.shape[2]
    to_t = lambda c: jnp.transpose(c, (0, 1, 3, 4, 2)).reshape(depth, Bs, A_WIDTH, nr)
    cache = {'a_kt': to_t(cache_a_k), 'a_vt': to_t(cache_a_v), 'b_ckv': cache_b_ckv,
             'b_kpet': jnp.swapaxes(cache_b_kpe, 2, 3), 'c_conv': state_c_conv, 'c_ssm': state_c_ssm}

    rp = min(Sp, A_WINDOW)
    moe_w_gu_b = moe_w_gu[0:1].astype(BF16)
    moe_w_down_b = moe_w_down[0:1].astype(BF16)
    xp, xs = x_prompt, x_sample
    st_p, st_s = [], []
    for l in range(depth):
        lw = _layer_weights(l, w_in, b_g_q, b_w_uq, b_g_kv, b_w_uk, b_w_uv, w_out,
                            moe_w_rg, moe_b_rg, moe_w_re, moe_b_re)
        prm = {'g_mix': g_mix[l][None, :], 'a_rel_bias': a_rel_bias[l], 'c_conv_w': c_conv_w[l],
               'c_conv_b': c_conv_b[l][None, :], 'c_dt_bias': jnp.pad(c_dt_bias[l], (0, LANES - C_HEADS))[None, :],
               'c_a_log': jnp.pad(c_a_log[l], (0, LANES - C_HEADS))[None, :],
               'c_d_exp': jnp.repeat(c_d[l], C_HEAD_DIM)[None, :], 'c_g_norm': c_g_norm[l][None, :],
               'g_ffn': g_ffn[l][None, :],
               'moe_w_gu': moe_w_gu_b if l == 0 else moe_w_gu,
               'moe_w_down': moe_w_down_b if l == 0 else moe_w_down,
               'mla_bk': 1024 if l == 0 else 2048,
               'in_parts': 1, 'out_parts': 2}
        fin_p = (modf[0:Bp][:, None, :], g_final[None, :]) if l == depth - 1 else None
        fin_s = (modf[Bp:nc][:, None, :], g_final[None, :]) if l == depth - 1 else None
        xp, sp = _layer(xp, mod[l, 0:Bp][:, None, :], pos_p, lw, l, None, prm, fin_p, nb=1, r=rp)
        xs, ss = _layer(xs, mod[l, Bp:nc][:, None, :], pos_s, lw, l, cache, prm, fin_s, nb=Bs, r=Ss)
        st_p.append(sp)
        st_s.append(ss)
    stack = lambda st, k: jnp.stack([t[k] for t in st])
    return ((xp, xs) + tuple(stack(st_p, k) for k in range(6))
            + tuple(stack(st_s, k) for k in range(6)))
```

```python
import functools
import math

import numpy as np
import jax
import jax.numpy as jnp
from jax import lax
from jax.experimental import pallas as pl
from jax.experimental.pallas import tpu as pltpu

F32 = jnp.float32
BF16 = jnp.bfloat16

D_MODEL = 1024
CHUNK = 64
EPS = 1e-6
NEG_INF = -1e30

A_HEADS = 4
A_HEAD_DIM = 64
A_WIDTH = A_HEADS * A_HEAD_DIM
A_BACK = 8
A_BAND = (A_BACK + 1) * CHUNK
A_WINDOW = A_BACK * CHUNK
A_REL_CLIP = 128
A_SCALE = A_HEAD_DIM ** -0.5

B_HEADS = 4
B_Q_LORA = 192
B_KV_LORA = 128
B_NOPE = 64
B_ROPE = 32
B_V = 64
B_WIDTH = B_HEADS * B_V
B_SCALE = (B_NOPE + B_ROPE) ** -0.5
B_QK = B_KV_LORA + B_ROPE
ROPE_THETA = 10000.0

C_D_INNER = 512
C_HEAD_DIM = 64
C_HEADS = C_D_INNER // C_HEAD_DIM
C_GROUPS = 2
C_D_STATE = 64
C_CONV_W = 4
C_CONV_DIM = C_D_INNER + 2 * C_GROUPS * C_D_STATE
C_GN = C_GROUPS * C_D_STATE
C_HPG = C_HEADS // C_GROUPS
SSD_CHUNK = 64

E_GROUPS = 4
E_PER_GROUP = 4
N_EXPERTS = E_GROUPS * E_PER_GROUP
E_HIDDEN = 256
ROUTER_LANES = 128
ROUTER_E_OFF = 16

LANES = 128
VMEM_LIMIT = 56 * 1024 * 1024

LOG2E = math.log2(math.e)
B_QSCALE = B_SCALE * LOG2E
A_QSCALE = A_SCALE * LOG2E

_W1_QKV = 0
_W1_CQ = 3 * A_WIDTH
_W1_CKV = _W1_CQ + 256
_W1_KPE = _W1_CKV + B_KV_LORA
_W1_Z = _W1_KPE + LANES
_W1_XBC = _W1_Z + C_D_INNER
_W1_N = _W1_XBC + C_CONV_DIM


def _cparams(*sem):
    return pltpu.CompilerParams(dimension_semantics=sem, vmem_limit_bytes=VMEM_LIMIT)


def _silu(v):
    return v * jax.nn.sigmoid(v)


def _split3_lanes(v, lane):
    x3 = v + pltpu.roll(v, C_HEADS, axis=1) + pltpu.roll(v, 2 * C_HEADS, axis=1)
    hi = x3.astype(BF16).astype(F32)
    r1 = x3 - hi
    mid = r1.astype(BF16).astype(F32)
    lo = r1 - mid
    return jnp.where(lane < C_HEADS, hi, jnp.where(lane < 2 * C_HEADS, mid, lo)).astype(BF16)


def _ada_kernel(c_ref, w_ref, b_ref, o_ref):
    s = _silu(c_ref[...]).astype(BF16)
    o_ref[0] = jnp.dot(s, w_ref[0].astype(BF16), preferred_element_type=F32) + b_ref[0]


def _ada(c_all, w, b):
    nl, d, n = w.shape
    bp = c_all.shape[0]
    tn = 1024
    return pl.pallas_call(
        _ada_kernel,
        grid=(nl, n // tn),
        in_specs=[pl.BlockSpec((bp, d), lambda l, j: (0, 0)),
                  pl.BlockSpec((1, d, tn), lambda l, j: (l, 0, j)),
                  pl.BlockSpec((1, 1, tn), lambda l, j: (l, 0, j))],
        out_specs=pl.BlockSpec((1, bp, tn), lambda l, j: (l, 0, j)),
        out_shape=jax.ShapeDtypeStruct((nl, bp, n), F32),
        compiler_params=_cparams("arbitrary", "arbitrary"),
        name="ada",
    )(c_all, w, b)


def _rows(ref_or_val, nb, r):
    v = ref_or_val[...]
    return jnp.broadcast_to(v, (nb, r, v.shape[-1])).reshape(nb * r, v.shape[-1])


def _inproj_kernel(x_ref, sh_ref, sc_ref, g_ref, w1_ref, gq_ref, wuq_ref, bd_ref, gkv_ref, cos_ref, sin_ref,
                   qkv_ref, knew_ref, vnew_ref, qmla_ref, kmla_ref, ckv_ref, kpe_ref, z_ref, xbc_ref, dtr_ref,
                   *, nb, r, keep_tile, nparts):
    D = x_ref.shape[-1]
    pnb, pr = (nb // nparts, r) if nb > 1 else (1, r // nparts)
    PR = pnb * pr

    def sel(s):
        return (slice(s * pnb, (s + 1) * pnb), slice(None)) if nb > 1 else (slice(None), slice(s * pr, (s + 1) * pr))

    def per_batch(ref, s):
        v = ref[sel(s)[0]] if nb > 1 else ref[...]
        return jnp.broadcast_to(v, (pnb, pr, v.shape[-1])).reshape(PR, v.shape[-1])

    def per_pos(ref, s):
        v = ref[...] if nb > 1 else ref[sel(s)[1]]
        return jnp.broadcast_to(v, (pnb, pr, v.shape[-1])).reshape(PR, v.shape[-1])

    def project(s):
        bs, rs = sel(s)
        x = x_ref[bs, rs].reshape(PR, D)
        h = x * lax.rsqrt(jnp.mean(x * x, axis=-1, keepdims=True) + EPS) * g_ref[...]
        h = h * (1.0 + per_batch(sc_ref, s)) + per_batch(sh_ref, s)
        return jnp.dot(h.astype(BF16), w1_ref[...], preferred_element_type=F32)

    def finish(s, u):
        bs, rs = sel(s)

        def out3(v):
            return v.reshape(pnb, pr, v.shape[-1])

        qkv_ref[bs, rs, 0:A_WIDTH] = out3((u[:, 0:A_WIDTH] * A_QSCALE).astype(BF16))
        qkv_ref[bs, rs, A_WIDTH:3 * A_WIDTH] = out3(u[:, A_WIDTH:3 * A_WIDTH].astype(BF16))

        @pl.when(pl.program_id(1) == keep_tile)
        def _():
            knew_ref[bs, rs] = out3(u[:, A_WIDTH:2 * A_WIDTH])
            vnew_ref[bs, rs] = out3(u[:, 2 * A_WIDTH:3 * A_WIDTH])

        blk = u[:, _W1_CQ:_W1_CQ + 256]
        lane = lax.broadcasted_iota(jnp.int32, (1, 256), 1)
        cq = jnp.where(lane < B_Q_LORA, blk, 0.0)
        cqn = cq * lax.rsqrt(jnp.sum(cq * cq, axis=-1, keepdims=True) * (1.0 / B_Q_LORA) + EPS) * gq_ref[...]
        qb = jnp.dot(cqn.astype(BF16), wuq_ref[...], preferred_element_type=F32)
        nope_w = B_HEADS * B_NOPE
        rope_w = B_HEADS * B_ROPE
        qlat = jnp.dot(qb[:, 0:nope_w].astype(BF16), bd_ref[...], preferred_element_type=F32) * B_QSCALE
        cosr = per_pos(cos_ref, s)
        sinr = per_pos(sin_ref, s)
        qpe = (qb[:, nope_w:nope_w + rope_w] * cosr
               + qb[:, nope_w + rope_w:nope_w + 2 * rope_w] * sinr) * B_QSCALE
        for hd in range(B_HEADS):
            qmla_ref[bs, hd, rs, 0:B_KV_LORA] = out3(qlat[:, hd * B_KV_LORA:(hd + 1) * B_KV_LORA].astype(BF16))
            qmla_ref[bs, hd, rs, B_KV_LORA:B_QK] = out3(qpe[:, hd * B_ROPE:(hd + 1) * B_ROPE].astype(BF16))

        cr = u[:, _W1_CKV:_W1_CKV + B_KV_LORA]
        ckv = cr * lax.rsqrt(jnp.mean(cr * cr, axis=-1, keepdims=True) + EPS) * gkv_ref[...]
        ckv_ref[bs, rs] = out3(ckv)
        kmla_ref[bs, rs, 0:B_KV_LORA] = out3(ckv.astype(BF16))
        kb = u[:, _W1_KPE:_W1_KPE + LANES]
        rot = kb * cosr + pltpu.roll(kb, LANES - B_ROPE, axis=1) * sinr
        kpe_ref[bs, rs] = out3(rot[:, 0:B_ROPE])
        kmla_ref[bs, rs, B_KV_LORA:B_QK] = out3(rot[:, 0:B_ROPE].astype(BF16))
        lane1 = lax.broadcasted_iota(jnp.int32, (1, LANES), 1)
        dtr_ref[bs, rs] = out3(jnp.where(lane1 < C_HEADS, pltpu.roll(kb, LANES - 2 * B_ROPE, axis=1), 0.0))

        z_ref[bs, rs] = out3(u[:, _W1_Z:_W1_Z + C_D_INNER].astype(BF16))
        xbc_ref[bs, rs] = out3(u[:, _W1_XBC:_W1_XBC + C_CONV_DIM])

    us = [project(s) for s in range(nparts)]
    for s in range(nparts):
        finish(s, us[s])


def _inproj(x, mod, g_mix, w1, gq, wuq, bd, gkv, cos_t, sin_t, *, nb, r, nparts):
    B, S, D = x.shape
    nbt, nst = B // nb, S // r
    keep = min(S, A_WINDOW)
    assert keep == r and B % nb == 0 and S % r == 0
    grid = (nbt, nst)
    row = lambda c: pl.BlockSpec((nb, r, c), lambda b, i: (b, i, 0))
    full = lambda a: pl.BlockSpec(a.shape, lambda b, i: (0,) * a.ndim)
    in_specs = [row(D),
                pl.BlockSpec((nb, 1, D), lambda b, i: (b, 0, 0)),
                pl.BlockSpec((nb, 1, D), lambda b, i: (b, 0, 1)),
                full(g_mix), full(w1), full(gq), full(wuq), full(bd), full(gkv),
                pl.BlockSpec((r, LANES), lambda b, i: (i, 0)),
                pl.BlockSpec((r, LANES), lambda b, i: (i, 0))]
    out_shape = [jax.ShapeDtypeStruct((B, S, 3 * A_WIDTH), BF16),
                 jax.ShapeDtypeStruct((B, keep, A_WIDTH), F32),
                 jax.ShapeDtypeStruct((B, keep, A_WIDTH), F32),
                 jax.ShapeDtypeStruct((B, B_HEADS, S, B_QK), BF16),
                 jax.ShapeDtypeStruct((B, S, B_QK), BF16),
                 jax.ShapeDtypeStruct((B, S, B_KV_LORA), F32),
                 jax.ShapeDtypeStruct((B, S, B_ROPE), F32),
                 jax.ShapeDtypeStruct((B, S, C_D_INNER), BF16),
                 jax.ShapeDtypeStruct((B, S, C_CONV_DIM), F32),
                 jax.ShapeDtypeStruct((B, S, LANES), F32)]
    out_specs = [row(3 * A_WIDTH),
                 pl.BlockSpec((nb, keep, A_WIDTH), lambda b, i: (b, 0, 0)),
                 pl.BlockSpec((nb, keep, A_WIDTH), lambda b, i: (b, 0, 0)),
                 pl.BlockSpec((nb, B_HEADS, r, B_QK), lambda b, i: (b, 0, i, 0)),
                 row(B_QK), row(B_KV_LORA), row(B_ROPE), row(C_D_INNER), row(C_CONV_DIM), row(LANES)]
    return pl.pallas_call(
        functools.partial(_inproj_kernel, nb=nb, r=r, keep_tile=nst - 1, nparts=nparts),
        grid=grid, in_specs=in_specs, out_specs=out_specs, out_shape=out_shape,
        compiler_params=_cparams("arbitrary", "arbitrary"),
        name="inproj",
    )(x, mod, mod, g_mix, w1, gq, wuq, bd, gkv, cos_t, sin_t)


def _band_scores(q, k, hmask):
    qs = jnp.where(hmask, jnp.concatenate([q] * A_HEADS, axis=0), jnp.zeros((), BF16))
    return lax.dot_general(qs, k, (((1,), (1,)), ((), ())), preferred_element_type=F32)


def _band_finish(s, v, bias, valid, hmask):
    nq = s.shape[0] // A_HEADS
    s = s + bias
    if valid is not None:
        s = jnp.where(valid, s, NEG_INF)
    m = jnp.max(s, axis=-1, keepdims=True)
    e = jnp.exp2(s - m)
    den = jnp.sum(e, axis=-1, keepdims=True)
    o = jnp.dot(e.astype(BF16), v, preferred_element_type=F32)
    o = jnp.where(hmask, o / den, 0.0)
    y = o[0:nq]
    for hd in range(1, A_HEADS):
        y = y + o[hd * nq:(hd + 1) * nq]
    return y


def _head_mask(nq):
    row = lax.broadcasted_iota(jnp.int32, (A_HEADS * nq, A_WIDTH), 0)
    col = lax.broadcasted_iota(jnp.int32, (A_HEADS * nq, A_WIDTH), 1)
    return (row // nq) == (col // A_HEAD_DIM)


def _band_prompt_kernel(cur_ref, prev_ref, bias_ref, o_ref, kwin, vwin, *, tq):
    i = pl.program_id(1)
    nch = tq // CHUNK
    kwin[0:A_WINDOW] = prev_ref[0, tq - A_WINDOW:tq, A_WIDTH:2 * A_WIDTH]
    kwin[A_WINDOW:A_WINDOW + tq] = cur_ref[0, :, A_WIDTH:2 * A_WIDTH]
    vwin[0:A_WINDOW] = prev_ref[0, tq - A_WINDOW:tq, 2 * A_WIDTH:3 * A_WIDTH]
    vwin[A_WINDOW:A_WINDOW + tq] = cur_ref[0, :, 2 * A_WIDTH:3 * A_WIDTH]
    hmask = _head_mask(CHUNK)
    ucol = lax.broadcasted_iota(jnp.int32, (1, A_BAND), 1)

    def scores(c):
        return _band_scores(cur_ref[0, c * CHUNK:(c + 1) * CHUNK, 0:A_WIDTH],
                            kwin[c * CHUNK:c * CHUNK + A_BAND], hmask)

    bias2 = bias_ref[...] * LOG2E
    s_next = scores(0)
    for c in range(nch):
        s = s_next
        if c + 1 < nch:
            s_next = scores(c + 1)
        valid = (ucol // CHUNK + (i * nch + c - A_BACK)) >= 0
        y = _band_finish(s, vwin[c * CHUNK:c * CHUNK + A_BAND], bias2, valid, hmask)
        o_ref[0, c * CHUNK:(c + 1) * CHUNK, :] = y.astype(BF16)


def _band_prompt(qkv, bias_stack):
    B, S, _ = qkv.shape
    tq = A_WINDOW
    assert S % tq == 0
    return pl.pallas_call(
        functools.partial(_band_prompt_kernel, tq=tq),
        grid=(B, S // tq),
        in_specs=[pl.BlockSpec((1, tq, 3 * A_WIDTH), lambda b, i: (b, i, 0)),
                  pl.BlockSpec((1, tq, 3 * A_WIDTH), lambda b, i: (b, jnp.maximum(i - 1, 0), 0)),
                  pl.BlockSpec(bias_stack.shape, lambda b, i: (0, 0))],
        out_specs=pl.BlockSpec((1, tq, A_WIDTH), lambda b, i: (b, i, 0)),
        out_shape=jax.ShapeDtypeStruct((B, S, A_WIDTH), BF16),
        scratch_shapes=[pltpu.VMEM((A_WINDOW + tq, A_WIDTH), BF16), pltpu.VMEM((A_WINDOW + tq, A_WIDTH), BF16)],
        compiler_params=_cparams("arbitrary", "arbitrary"),
        name="band_prompt",
    )(qkv, qkv, bias_stack)


def _band_sample_kernel(qkv_ref, kt_ref, vt_ref, biasc_ref, biasn_ref, o_ref, *, nbb, ds):
    hmask = _head_mask(ds)
    nt_dims = (((1,), (1,)), ((), ()))
    for b in range(nbb):
        q = qkv_ref[b, :, 0:A_WIDTH]
        kn = qkv_ref[b, :, A_WIDTH:2 * A_WIDTH]
        vn = qkv_ref[b, :, 2 * A_WIDTH:3 * A_WIDTH]
        qs = jnp.where(hmask, jnp.concatenate([q] * A_HEADS, axis=0), jnp.zeros((), BF16))
        sc = jnp.dot(qs, kt_ref[0, b].astype(BF16), preferred_element_type=F32) + biasc_ref[...] * LOG2E
        sn = lax.dot_general(qs, kn, nt_dims, preferred_element_type=F32) + biasn_ref[...] * LOG2E
        m = jnp.maximum(jnp.max(sc, axis=-1, keepdims=True), jnp.max(sn, axis=-1, keepdims=True))
        ec = jnp.exp2(sc - m)
        en = jnp.exp2(sn - m)
        den = jnp.sum(ec, axis=-1, keepdims=True) + jnp.sum(en, axis=-1, keepdims=True)
        o = lax.dot_general(ec.astype(BF16), vt_ref[0, b].astype(BF16), nt_dims, preferred_element_type=F32)
        o = o + jnp.dot(en.astype(BF16), vn, preferred_element_type=F32)
        o = jnp.where(hmask, o / den, 0.0)
        y = o[0:ds]
        for hd in range(1, A_HEADS):
            y = y + o[hd * ds:(hd + 1) * ds]
        o_ref[b] = y.astype(BF16)


def _band_sample(qkv, cache_kt, cache_vt, layer, bias_stack):
    B, ds, _ = qkv.shape
    nr = cache_kt.shape[3]
    nbb = 4
    assert B % nbb == 0
    bias_c, bias_n = bias_stack[:, 0:nr], bias_stack[:, nr:]
    return pl.pallas_call(
        functools.partial(_band_sample_kernel, nbb=nbb, ds=ds),
        grid=(B // nbb,),
        in_specs=[pl.BlockSpec((nbb, ds, 3 * A_WIDTH), lambda b: (b, 0, 0)),
                  pl.BlockSpec((1, nbb, A_WIDTH, nr), lambda b: (layer, b, 0, 0)),
                  pl.BlockSpec((1, nbb, A_WIDTH, nr), lambda b: (layer, b, 0, 0)),
                  pl.BlockSpec(bias_c.shape, lambda b: (0, 0)),
                  pl.BlockSpec(bias_n.shape, lambda b: (0, 0))],
        out_specs=pl.BlockSpec((nbb, ds, A_WIDTH), lambda b: (b, 0, 0)),
        out_shape=jax.ShapeDtypeStruct((B, ds, A_WIDTH), BF16),
        compiler_params=_cparams("arbitrary"),
        name="band_sample",
    )(qkv, cache_kt, cache_vt, bias_c, bias_n)


def _mla_finish(acc, den, wuv_ref, bq):
    o = acc / den
    o_all = jnp.concatenate([o[hd * bq:(hd + 1) * bq] for hd in range(B_HEADS)], axis=1)
    return jnp.dot(o_all.astype(BF16), wuv_ref[...], preferred_element_type=F32)


def _mla_prompt_kernel(qi_ref, kj_ref, last_ref, q_ref, k_ref, wuv_ref, o_ref, m_sc, acc_sc, *, bq, bk, sk):
    p = pl.program_id(1)
    i = qi_ref[p]
    j = kj_ref[p]
    M = B_HEADS * bq

    @pl.when(j == 0)
    def _():
        m_sc[...] = jnp.full(m_sc.shape, NEG_INF, F32)
        acc_sc[...] = jnp.zeros(acc_sc.shape, F32)

    def step(masked):
        q = q_ref[0].reshape(M, B_QK)
        nt_dims = (((1,), (1,)), ((), ()))
        nsub = bk // sk

        def scores(t):
            return lax.dot_general(q, k_ref[0, t * sk:(t + 1) * sk, :], nt_dims, preferred_element_type=F32)

        m_run, acc = m_sc[...], acc_sc[...]
        ones = jnp.ones((sk, LANES), BF16)
        s_next = scores(0)
        for t in range(nsub):
            s = s_next
            if t + 1 < nsub:
                s_next = scores(t + 1)
            if masked:
                row = lax.broadcasted_iota(jnp.int32, (M, sk), 0)
                col = lax.broadcasted_iota(jnp.int32, (M, sk), 1)
                s = jnp.where((col + (j * bk + t * sk)) // CHUNK <= ((row % bq) + i * bq) // CHUNK, s, NEG_INF)
            m_new = jnp.maximum(m_run, jnp.max(s, axis=1, keepdims=True))
            alpha = jnp.exp2(m_run - m_new)
            pr = jnp.exp2(s - jnp.tile(m_new, (1, sk // LANES)))
            v_ext = jnp.concatenate([k_ref[0, t * sk:(t + 1) * sk, 0:B_KV_LORA], ones], axis=1)
            acc = (acc * jnp.concatenate([alpha, alpha], axis=1)
                   + jnp.dot(pr.astype(BF16), v_ext, preferred_element_type=F32))
            m_run = m_new
        m_sc[...] = m_run
        acc_sc[...] = acc

    @pl.when(last_ref[p] == 0)
    def _():
        step(False)

    @pl.when(last_ref[p] == 1)
    def _():
        step(True)
        o_ref[0] = _mla_finish(acc_sc[:, 0:B_KV_LORA], acc_sc[:, B_KV_LORA:], wuv_ref, bq).astype(BF16)


def _mla_prompt(qmla, kmla, wuv_bd, *, bq, bk):
    B, S = kmla.shape[0:2]
    bk = min(bk, S)
    assert S % bq == 0 and S % bk == 0 and bq % CHUNK == 0 and bk % bq == 0
    nq = S // bq
    n_kv = [(i * bq + bq - 1) // bk + 1 for i in range(nq)]
    qi = np.concatenate([np.full(n, i, np.int32) for i, n in enumerate(n_kv)])
    kj = np.concatenate([np.arange(n, dtype=np.int32) for n in n_kv])
    last = np.concatenate([(np.arange(n) == n - 1).astype(np.int32) for n in n_kv])
    grid_spec = pltpu.PrefetchScalarGridSpec(
        num_scalar_prefetch=3,
        grid=(B, len(qi)),
        in_specs=[pl.BlockSpec((1, B_HEADS, bq, B_QK), lambda b, p, qi, kj, last: (b, 0, qi[p], 0)),
                  pl.BlockSpec((1, bk, B_QK), lambda b, p, qi, kj, last: (b, kj[p], 0)),
                  pl.BlockSpec(wuv_bd.shape, lambda b, p, qi, kj, last: (0, 0))],
        out_specs=pl.BlockSpec((1, bq, B_WIDTH), lambda b, p, qi, kj, last: (b, qi[p], 0)),
        scratch_shapes=[pltpu.VMEM((B_HEADS * bq, LANES), F32),
                        pltpu.VMEM((B_HEADS * bq, B_KV_LORA + LANES), F32)])
    return pl.pallas_call(
        functools.partial(_mla_prompt_kernel, bq=bq, bk=bk, sk=min(bk, bq)),
        grid_spec=grid_spec,
        out_shape=jax.ShapeDtypeStruct((B, S, B_WIDTH), BF16),
        compiler_params=_cparams("arbitrary", "arbitrary"),
        name="mla_prompt",
    )(jnp.asarray(qi), jnp.asarray(kj), jnp.asarray(last), qmla, kmla, wuv_bd)


def _mla_sample_kernel(q_ref, kn_ref, ckv_ref, kpet_ref, wuv_ref, o_ref, *, ds):
    M = B_HEADS * ds
    nt_dims = (((1,), (1,)), ((), ()))
    q = q_ref[0].reshape(M, B_QK)
    kn = kn_ref[0]
    ckv = ckv_ref[0, 0].astype(BF16)
    sc = (lax.dot_general(q[:, 0:B_KV_LORA], ckv, nt_dims, preferred_element_type=F32)
          + jnp.dot(q[:, B_KV_LORA:B_QK], kpet_ref[0, 0].astype(BF16), preferred_element_type=F32))
    sn = lax.dot_general(q, kn, nt_dims, preferred_element_type=F32)
    m = jnp.maximum(jnp.max(sc, axis=1, keepdims=True), jnp.max(sn, axis=1, keepdims=True))
    pc = jnp.exp2(sc - m)
    pn = jnp.exp2(sn - m)
    den = jnp.sum(pc, axis=1, keepdims=True) + jnp.sum(pn, axis=1, keepdims=True)
    acc = (jnp.dot(pc.astype(BF16), ckv, preferred_element_type=F32)
           + jnp.dot(pn.astype(BF16), kn[:, 0:B_KV_LORA], preferred_element_type=F32))
    o_ref[0] = _mla_finish(acc, den, wuv_ref, ds).astype(BF16)


def _mla_sample(qmla, kmla, cache_ckv, cache_kpet, layer, wuv_bd):
    B, ds = kmla.shape[0:2]
    past = cache_ckv.shape[2]
    return pl.pallas_call(
        functools.partial(_mla_sample_kernel, ds=ds),
        grid=(B,),
        in_specs=[pl.BlockSpec((1, B_HEADS, ds, B_QK), lambda b: (b, 0, 0, 0)),
                  pl.BlockSpec((1, ds, B_QK), lambda b: (b, 0, 0)),
                  pl.BlockSpec((1, 1, past, B_KV_LORA), lambda b: (layer, b, 0, 0)),
                  pl.BlockSpec((1, 1, B_ROPE, past), lambda b: (layer, b, 0, 0)),
                  pl.BlockSpec(wuv_bd.shape, lambda b: (0, 0))],
        out_specs=pl.BlockSpec((1, ds, B_WIDTH), lambda b: (b, 0, 0)),
        out_shape=jax.ShapeDtypeStruct((B, ds, B_WIDTH), BF16),
        compiler_params=_cparams("arbitrary"),
        name="mla_sample",
    )(qmla, kmla, cache_ckv, cache_kpet, wuv_bd)


def _ssd_kernel(xbc_ref, z_ref, dtr_ref, pre_ref, h0_ref, cw_ref, cb_ref, dtb_ref, alog_ref, dsk_ref, gn_ref,
                es_ref, ep_ref, tri_ref, y_ref, hout_ref,
                cbuf, xs_sc, b_sc, c_sc, ces_sc, cep_sc, dep_sc, st_sc, *, nbat, lt, lc, nt):
    t = pl.program_id(1)
    nch = lt // lc
    HS = C_HEADS * lc
    gw = C_HPG * C_HEAD_DIM

    @pl.when(t == 0)
    def _():
        st_sc[...] = jnp.zeros(st_sc.shape, F32)
        for b in range(nbat):
            cbuf[b, 8 - (C_CONV_W - 1):8] = pre_ref[b]
            for g in range(C_GROUPS):
                blk = jnp.concatenate([h0_ref[b, g * C_HPG + hl] for hl in range(C_HPG)], axis=1)
                st_sc[b, g * C_D_STATE:(g + 1) * C_D_STATE, g * gw:(g + 1) * gw] = blk

    hl = lax.broadcasted_iota(jnp.int32, (lt, LANES), 1)
    for b in range(nbat):
        cbuf[b, 8:8 + lt] = xbc_ref[b]
        acc = jnp.broadcast_to(cb_ref[...], (lt, C_CONV_DIM))
        for kk in range(C_CONV_W):
            off = 8 - (C_CONV_W - 1) + kk
            acc = acc + cbuf[b, off:off + lt] * cw_ref[kk:kk + 1, :]
        tail = cbuf[b, 8 + lt - (C_CONV_W - 1):8 + lt]
        cbuf[b, 8 - (C_CONV_W - 1):8] = tail
        xc = _silu(acc)
        xs_sc[b] = xc[:, 0:C_D_INNER]
        b_sc[b] = xc[:, C_D_INNER:C_D_INNER + C_GN]
        c_sc[b] = xc[:, C_D_INNER + C_GN:C_D_INNER + 2 * C_GN]

        dt = jnp.where(hl < C_HEADS, jax.nn.softplus(dtr_ref[b] + dtb_ref[...]), 0.0)
        da = dt * (-jnp.exp(alog_ref[...]))
        c3 = jnp.dot(tri_ref[...], _split3_lanes(da, hl), preferred_element_type=F32)
        cum = jnp.where(hl < C_HEADS, c3 + pltpu.roll(c3, LANES - C_HEADS, axis=1)
                        + pltpu.roll(c3, LANES - 2 * C_HEADS, axis=1), 0.0)
        cum3 = _split3_lanes(cum, hl)
        cep = jnp.dot(cum3, ep_ref[...], preferred_element_type=F32)
        cep_sc[b] = cep
        ces_sc[b] = cep if HS == C_D_INNER and lc == C_HEAD_DIM else jnp.dot(cum3, es_ref[...],
                                                                           preferred_element_type=F32)
        dep_sc[b] = jnp.dot(_split3_lanes(dt, hl), ep_ref[...], preferred_element_type=F32)

    r_s = lax.broadcasted_iota(jnp.int32, (lc, HS), 0)
    l_s = lax.broadcasted_iota(jnp.int32, (lc, HS), 1)
    eye_t = (l_s % lc) == r_s
    causal_t = (l_s % lc) <= r_s
    r_w = lax.broadcasted_iota(jnp.int32, (HS, C_GN), 0)
    l_w = lax.broadcasted_iota(jnp.int32, (HS, C_GN), 1)
    gmask = (r_w // (C_HPG * lc)) == (l_w // C_D_STATE)
    r_b =lax.broadcasted_iota(jnp.int32, (C_HPG * lc, gw), 0)
    l_b = lax.broadcasted_iota(jnp.int32, (C_HPG * lc, gw), 1)
    bmask = (r_b // lc) == (l_b // C_HEAD_DIM)
    r_g = lax.broadcasted_iota(jnp.int32, (C_GN, C_D_INNER), 0)
    l_g = lax.broadcasted_iota(jnp.int32, (C_GN, C_D_INNER), 1)
    smask = (r_g // C_D_STATE) == (l_g // gw)

    def stage_cb(b, rows):
        cm = c_sc[b, rows, :].astype(BF16)
        bm = b_sc[b, rows, :]
        w_nt = jnp.where(gmask, jnp.concatenate([bm] * C_HEADS, axis=0), 0.0).astype(BF16)
        cbcat = lax.dot_general(cm, w_nt, (((1,), (1,)), ((), ())), preferred_element_type=F32)
        st = st_sc[b]
        yoff = jnp.dot(cm, st.astype(BF16), preferred_element_type=F32)
        return dict(bm=bm, cbcat=cbcat, st=st, yoff=yoff)

    def stage_diag(b, rows, v):
        ce_s = ces_sc[b, rows, :]
        ce_p = cep_sc[b, rows, :]
        xs = xs_sc[b, rows, :]
        rflat = jnp.sum(jnp.where(eye_t, ce_s, 0.0), axis=0, keepdims=True)
        lcat = jnp.exp(jnp.where(causal_t, ce_s - rflat, NEG_INF))
        mcat = (v['cbcat'] * lcat).astype(BF16)
        xdt = xs * dep_sc[b, rows, :]
        xdt_b = xdt.astype(BF16)
        ydiag = []
        for g in range(C_GROUPS):
            bdx = jnp.where(bmask, jnp.concatenate([xdt_b[:, g * gw:(g + 1) * gw]] * C_HPG, axis=0),
                            jnp.zeros((), BF16))
            ydiag.append(jnp.dot(mcat[:, g * C_HPG * lc:(g + 1) * C_HPG * lc], bdx, preferred_element_type=F32))
        v.update(ce_p=ce_p, xs=xs, xdt=xdt, ydiag=jnp.concatenate(ydiag, axis=1))

    def stage_state(b, rows, v):
        ce_p = v['ce_p']
        last = ce_p[lc - 1:lc, :]
        xw = (v['xdt'] * jnp.exp(last - ce_p)).astype(BF16)
        upd = lax.dot_general(v['bm'].astype(BF16), xw, (((0,), (0,)), ((), ())), preferred_element_type=F32)
        st_sc[b] = jnp.exp(last) * v['st'] + jnp.where(smask, upd, 0.0)

    def stage_out(b, rows, v):
        y = v['ydiag'] + v['yoff'] * jnp.exp(v['ce_p']) + dsk_ref[...] * v['xs']
        y = y * _silu(z_ref[b, rows, :].astype(F32))
        y = y * lax.rsqrt(jnp.mean(y * y, axis=-1, keepdims=True) + EPS) * gn_ref[...]
        y_ref[b, rows, :] = y.astype(BF16)

    def chunk(c, carry):
        rows = pl.ds(pl.multiple_of(c * lc, lc), lc)
        vals = [stage_cb(b, rows) for b in range(nbat)]
        for stage in (stage_diag, stage_state, stage_out):
            for b in range(nbat):
                stage(b, rows, vals[b])
        return carry

    lax.fori_loop(0, nch, chunk, 0)

    @pl.when(t == nt - 1)
    def _():
        for b in range(nbat):
            for hd in range(C_HEADS):
                g = hd // C_HPG
                hout_ref[b, hd] = st_sc[b, g * C_D_STATE:(g + 1) * C_D_STATE,
                                        hd * C_HEAD_DIM:(hd + 1) * C_HEAD_DIM]


def _ssd(xbc, z, dtr, prefix, h0t, cw, cb, dtb, alog, dsk, gn, *, nbat, lt, lc):
    B, S, _ = xbc.shape
    nt = S // lt
    assert S % lt == 0 and lt % lc == 0 and B % nbat == 0
    hs = C_HEADS * lc

    def expand(width, per_head):
        e = np.zeros((LANES, width), np.float32)
        for part in range(3):
            e[part * C_HEADS:(part + 1) * C_HEADS] = (np.arange(width)[None, :] // per_head
                                                      == np.arange(C_HEADS)[:, None])
        return jnp.asarray(e, dtype=BF16)

    es, ep = expand(hs, lc), expand(C_D_INNER, C_HEAD_DIM)
    pos = np.arange(lt)
    tri = jnp.asarray((pos[None, :] <= pos[:, None]) & (pos[None, :] // lc == pos[:, None] // lc), dtype=BF16)
    row = lambda c: pl.BlockSpec((nbat, lt, c), lambda b, t: (b, t, 0))
    per_b = lambda a: pl.BlockSpec((nbat,) + a.shape[1:], lambda b, t: (b,) + (0,) * (a.ndim - 1))
    full = lambda a: pl.BlockSpec(a.shape, lambda b, t: (0,) * a.ndim)
    sc = lambda *shape: pltpu.VMEM((nbat,) + shape, F32)
    return pl.pallas_call(
        functools.partial(_ssd_kernel, nbat=nbat, lt=lt, lc=lc, nt=nt),
        grid=(B // nbat, nt),
        in_specs=[row(C_CONV_DIM), row(C_D_INNER), row(LANES), per_b(prefix), per_b(h0t),
                  full(cw), full(cb), full(dtb), full(alog), full(dsk), full(gn), full(es), full(ep), full(tri)],
        out_specs=[row(C_D_INNER), per_b(h0t)],
        out_shape=[jax.ShapeDtypeStruct((B, S, C_D_INNER), BF16),
                   jax.ShapeDtypeStruct(h0t.shape, F32)],
        scratch_shapes=[sc(lt + 8, C_CONV_DIM), sc(lt, C_D_INNER), sc(lt, C_GN), sc(lt, C_GN),
                        sc(lt, hs), sc(lt, C_D_INNER), sc(lt, C_D_INNER), sc(C_GN, C_D_INNER)],
        compiler_params=_cparams("arbitrary", "arbitrary"),
        name="ssd",
    )(xbc, z, dtr, prefix, h0t, cw, cb, dtb, alog, dsk, gn, es, ep, tri)


def _outproj_kernel(ya_ref, yb_ref, yc_ref, x_ref, g1_ref, sh_ref, sc_ref, gf_ref, wo_ref, wr_ref, br_ref,
                    x1_ref, h2_ref, gate_ref, *, nb, r, nparts):
    D = x_ref.shape[-1]
    pnb, pr = (nb // nparts, r) if nb > 1 else (1, r // nparts)
    PR = pnb * pr

    def sel_part(s):
        return (slice(s * pnb, (s + 1) * pnb), slice(None)) if nb > 1 else (slice(None), slice(s * pr, (s + 1) * pr))

    def in2(ref, s):
        bs, rs = sel_part(s)
        return ref[bs, rs].reshape(PR, ref.shape[-1])

    def per_batch(ref, s):
        v = ref[sel_part(s)[0]] if nb > 1 else ref[...]
        return jnp.broadcast_to(v, (pnb, pr, v.shape[-1])).reshape(PR, v.shape[-1])

    def project(s):
        o = jnp.dot(in2(ya_ref, s), wo_ref[0:A_WIDTH, :], preferred_element_type=F32)
        o = o + jnp.dot(in2(yb_ref, s), wo_ref[A_WIDTH:A_WIDTH + B_WIDTH, :], preferred_element_type=F32)
        return o + jnp.dot(in2(yc_ref, s), wo_ref[A_WIDTH + B_WIDTH:, :], preferred_element_type=F32)

    half = ROUTER_LANES // 4
    wr = wr_ref[...]
    w_hi = wr.astype(BF16).astype(F32)
    wl = lax.broadcasted_iota(jnp.int32, (1, ROUTER_LANES), 1)
    w_cat = jnp.where(wl < half, w_hi, pltpu.roll(wr - w_hi, half, axis=1)).astype(BF16)

    def logits(s, o):
        bs, rs = sel_part(s)
        x1 = in2(x_ref, s) + per_batch(g1_ref, s) * o
        x1_ref[bs, rs] = x1.reshape(pnb, pr, D)
        h2 = x1 * lax.rsqrt(jnp.mean(x1 * x1, axis=-1, keepdims=True) + EPS) * gf_ref[...]
        h2 = h2 * (1.0 + per_batch(sc_ref, s)) + per_batch(sh_ref, s)
        h_hi = h2.astype(BF16)
        h2_ref[bs, rs] = h_hi.reshape(pnb, pr, D)
        h_lo = (h2 - h_hi.astype(F32)).astype(BF16)
        prod = jnp.dot(jnp.concatenate([h_hi, h_lo], axis=0), w_cat, preferred_element_type=F32)
        both = prod[0:PR] + prod[PR:]
        return both + pltpu.roll(both, ROUTER_LANES - half, axis=1) + br_ref[...]

    def route(s, lg):
        bs, rs = sel_part(s)
        lane = lax.broadcasted_iota(jnp.int32, (PR, ROUTER_LANES), 1)
        big = jnp.int32(ROUTER_LANES)
        is_g = lane < E_GROUPS
        gl = jnp.where(is_g, lg, NEG_INF)
        gmax = jnp.max(gl, axis=-1, keepdims=True)
        p_top = 1.0 / jnp.sum(jnp.where(is_g, jnp.exp(gl - gmax), 0.0), axis=-1, keepdims=True)
        g_idx = jnp.min(jnp.where(is_g & (gl == gmax), lane, big), axis=-1, keepdims=True)
        e_lane = lane - ROUTER_E_OFF
        sel = (e_lane >= 0) & (e_lane < N_EXPERTS) & ((e_lane // E_PER_GROUP) == g_idx)
        l1 = jnp.where(sel, lg, NEG_INF)
        m1 = jnp.max(l1, axis=-1, keepdims=True)
        i1 = jnp.min(jnp.where(sel & (l1 == m1), lane, big), axis=-1, keepdims=True)
        sel2 = sel & (lane != i1)
        l2 = jnp.where(sel2, lg, NEG_INF)
        m2 = jnp.max(l2, axis=-1, keepdims=True)
        i2 = jnp.min(jnp.where(sel2 & (l2 == m2), lane, big), axis=-1, keepdims=True)
        e2 = jnp.exp(m2 - m1)
        w1 = p_top / (1.0 + e2)
        w2 = p_top * e2 / (1.0 + e2)
        gate = jnp.where(lane == i1, w1, 0.0) + jnp.where(lane == i2, w2, 0.0)
        gate_ref[bs, rs] = gate.reshape(pnb, pr, ROUTER_LANES)

    outs = [project(s) for s in range(nparts)]
    lgs = [logits(s, outs[s]) for s in range(nparts)]
    for s in range(nparts):
        route(s, lgs[s])


def _outproj(ya, yb, yc, x, mod, g_ffn, wo, wr, br, *, nb, r, nparts):
    B, S, D = x.shape
    row = lambda c: pl.BlockSpec((nb, r, c), lambda b, i: (b, i, 0))
    modc = lambda j: pl.BlockSpec((nb, 1, D), lambda b, i: (b, 0, j))
    full = lambda a: pl.BlockSpec(a.shape, lambda b, i: (0,) * a.ndim)
    return pl.pallas_call(
        functools.partial(_outproj_kernel, nb=nb, r=r, nparts=nparts),
        grid=(B // nb, S // r),
        in_specs=[row(A_WIDTH), row(B_WIDTH), row(C_D_INNER), row(D), modc(2), modc(3), modc(4),
                  full(g_ffn), full(wo), full(wr), full(br)],
        out_specs=[row(D), row(D), row(ROUTER_LANES)],
        out_shape=[jax.ShapeDtypeStruct((B, S, D), F32), jax.ShapeDtypeStruct((B, S, D), BF16),
                   jax.ShapeDtypeStruct((B, S, ROUTER_LANES), F32)],
        compiler_params=_cparams("arbitrary", "arbitrary"),
        name="outproj",
    )(ya, yb, yc, x, mod, mod, mod, g_ffn, wo, wr, br)


def _moe_kernel(h_ref, gate_ref, x_ref, g2_ref, wgu_ref, wd_ref, *rest, nb, r, nsteps, epb, final):
    if final:
        shf_ref, scf_ref, gfin_ref, o_ref, acc_sc = rest
    else:
        o_ref, acc_sc = rest
    e = pl.program_id(2)
    R = nb * r
    D = x_ref.shape[-1]

    @pl.when(e == 0)
    def _():
        acc_sc[...] = jnp.zeros(acc_sc.shape, F32)

    def as_bf16(w):
        return w if w.dtype == BF16 else w.astype(BF16)

    h = h_ref[...].reshape(R, D)
    gate = gate_ref[...].reshape(R, ROUTER_LANES)
    lane = lax.broadcasted_iota(jnp.int32, (R, ROUTER_LANES), 1)
    acts = []
    for k in range(epb):
        gu = jnp.dot(h, as_bf16(wgu_ref[0, k]), preferred_element_type=F32)
        gcol = jnp.sum(jnp.where(lane == e * epb + k + ROUTER_E_OFF, gate, 0.0), axis=-1, keepdims=True)
        acts.append((_silu(gu[:, 0:E_HIDDEN]) * gu[:, E_HIDDEN:] * gcol).astype(BF16))
    wd = as_bf16(wd_ref[0]).reshape(epb * E_HIDDEN, D)
    acc_sc[...] += jnp.dot(jnp.concatenate(acts, axis=1), wd, preferred_element_type=F32)

    @pl.when(e == nsteps - 1)
    def _():
        xo = x_ref[...].reshape(R, D) + _rows(g2_ref, nb, r) * acc_sc[...]
        if final:
            xo = xo * lax.rsqrt(jnp.mean(xo * xo, axis=-1, keepdims=True) + EPS) * gfin_ref[...]
            xo = xo * (1.0 + _rows(scf_ref, nb, r)) + _rows(shf_ref, nb, r)
        o_ref[...] = xo.reshape(nb, r, D)


def _moe(h2, gate, x1, mod, wgu, wd, layer, final, *, nb, r, epb):
    B, S, D = x1.shape
    ne = wgu.shape[1]
    assert ne % epb == 0
    nsteps = ne // epb
    row = lambda c: pl.BlockSpec((nb, r, c), lambda b, i, e: (b, i, 0))
    modc = lambda j: pl.BlockSpec((nb, 1, D), lambda b, i, e: (b, 0, j))
    in_specs = [row(D), row(ROUTER_LANES), row(D), modc(5),
                pl.BlockSpec((1, epb, D, 2 * E_HIDDEN), lambda b, i, e: (layer, e, 0, 0)),
                pl.BlockSpec((1, epb, E_HIDDEN, D), lambda b, i, e: (layer, e, 0, 0))]
    args = [h2, gate, x1, mod, wgu, wd]
    if final is not None:
        modf, gfin = final
        in_specs += [modc(0), modc(1), pl.BlockSpec(gfin.shape, lambda b, i, e: (0, 0))]
        args += [modf, modf, gfin]
    return pl.pallas_call(
        functools.partial(_moe_kernel, nb=nb, r=r, nsteps=nsteps, epb=epb, final=final is not None),
        grid=(B // nb, S // r, nsteps),
        in_specs=in_specs,
        out_specs=row(D),
        out_shape=jax.ShapeDtypeStruct((B, S, D), F32),
        scratch_shapes=[pltpu.VMEM((nb * r, D), F32)],
        compiler_params=_cparams("arbitrary", "arbitrary", "arbitrary"),
        name="moe",
    )(*args)


def _rope_tables(pos):
    half = B_ROPE // 2
    inv = 1.0 / (ROPE_THETA ** (jnp.arange(half, dtype=F32) / half))
    ang = pos.astype(F32)[:, None] * inv[None, :]
    cos, sin = jnp.cos(ang), jnp.sin(ang)
    return (jnp.tile(jnp.concatenate([cos, cos], axis=1), (1, B_HEADS)),
            jnp.tile(jnp.concatenate([-sin, sin], axis=1), (1, B_HEADS)))


def _swap_halves(w):
    half = w.shape[-1] // 2
    return jnp.concatenate([w[..., half:], w[..., :half]], axis=-1)


def _layer_weights(l, w_in, b_g_q, b_w_uq, b_g_kv, b_w_uk, b_w_uv, w_out, moe_w_rg, moe_b_rg, moe_w_re, moe_b_re):
    D = w_in.shape[1]
    w = w_in[l]
    o_cq = 3 * A_WIDTH
    o_ckv = o_cq + B_Q_LORA
    o_kpe = o_ckv + B_KV_LORA
    o_z = o_kpe + B_ROPE
    o_xbc = o_z + C_D_INNER
    o_dt = o_xbc + C_CONV_DIM
    wkpe = w[:, o_kpe:o_z]
    zc = lambda n: jnp.zeros((D, n), F32)
    w1 = jnp.concatenate([w[:, 0:o_cq], w[:, o_cq:o_ckv], zc(256 - B_Q_LORA), w[:, o_ckv:o_kpe],
                          wkpe, _swap_halves(wkpe), w[:, o_dt:o_dt + C_HEADS],
                          zc(LANES - 2 * B_ROPE - C_HEADS), w[:, o_z:o_xbc], w[:, o_xbc:o_dt]],
                         axis=1).astype(BF16)
    assert w1.shape[1] == _W1_N
    gq = jnp.pad(b_g_q[l], (0, 256 - B_Q_LORA))[None, :]
    uq = b_w_uq[l]
    pe = uq[:, :, B_NOPE:]
    wuq = jnp.concatenate([uq[:, :, :B_NOPE].reshape(B_Q_LORA, -1), pe.reshape(B_Q_LORA, -1),
                           _swap_halves(pe).reshape(B_Q_LORA, -1)], axis=1)
    wuq = jnp.pad(wuq, ((0, 256 - B_Q_LORA), (0, 0))).astype(BF16)
    eye = jnp.eye(B_HEADS, dtype=F32)
    bd = (jnp.transpose(b_w_uk[l], (1, 2, 0))[:, :, None, :] * eye[:, None, :, None]).reshape(
        B_HEADS * B_NOPE, B_HEADS * B_KV_LORA).astype(BF16)
    wuv = (jnp.transpose(b_w_uv[l], (1, 0, 2))[:, :, None, :] * eye[:, None, :, None]).reshape(
        B_HEADS * B_KV_LORA, B_WIDTH).astype(BF16)
    wr = jnp.zeros((D, ROUTER_LANES), F32)
    wr = wr.at[:, 0:E_GROUPS].set(moe_w_rg[l]).at[:, ROUTER_E_OFF:ROUTER_E_OFF + N_EXPERTS].set(moe_w_re[l])
    br = jnp.zeros((1, ROUTER_LANES), F32)
    br = br.at[0, 0:E_GROUPS].set(moe_b_rg[l]).at[0, ROUTER_E_OFF:ROUTER_E_OFF + N_EXPERTS].set(moe_b_re[l])
    return dict(w1=w1, gq=gq, wuq=wuq, bd=bd, gkv=b_g_kv[l][None, :], wuv=wuv, wo=w_out[l].astype(BF16), wr=wr, br=br)


def _band_bias(rel_bias, nq, nk, back):
    n = nk + nq - 1
    idx = np.clip(back + nq - 1 - np.arange(n + 1), -A_REL_CLIP, A_REL_CLIP) + A_REL_CLIP
    line = rel_bias[:, idx]
    skew = jnp.tile(line, (1, nq))[:, 0:nq * n].reshape(A_HEADS, nq, n)
    return skew[:, :, nq - 1:nq - 1 + nk].reshape(A_HEADS * nq, nk)


def _layer(x, mod, pos, lw, l, cache, prm, final, *, nb, r):
    B, S, D = x.shape
    cos_t, sin_t = _rope_tables(pos)
    (qkv, knew, vnew, qmla, kmla, ckv, kpe, z, xbc, dtr) = _inproj(
        x, mod, prm['g_mix'], lw['w1'], lw['gq'], lw['wuq'], lw['bd'], lw['gkv'], cos_t, sin_t, nb=nb, r=r,
        nparts=prm['in_parts'])
    if cache is None:
        ya = _band_prompt(qkv, _band_bias(prm['a_rel_bias'], CHUNK, A_BAND, A_WINDOW))
        yb = _mla_prompt(qmla, kmla, lw['wuv'], bq=512, bk=1024)
        prefix = jnp.zeros((B, C_CONV_W - 1, C_CONV_DIM), F32)
        h0t = jnp.zeros((B, C_HEADS, C_D_STATE, C_HEAD_DIM), F32)
        lt, lc, nbat = min(S, 512), SSD_CHUNK, B
    else:
        nr = cache['a_kt'].shape[3]
        ya = _band_sample(qkv, cache['a_kt'], cache['a_vt'], l, _band_bias(prm['a_rel_bias'], S, nr + S, nr))
        yb = _mla_sample(qmla, kmla, cache['b_ckv'], cache['b_kpet'], l, lw['wuv'])
        prefix = cache['c_conv'][l]
        h0t = jnp.swapaxes(cache['c_ssm'][l], -1, -2)
        lt = lc = min(SSD_CHUNK, S)
        nbat = math.gcd(B, 4)
    yc, hft = _ssd(xbc, z, dtr, prefix, h0t, prm['c_conv_w'], prm['c_conv_b'], prm['c_dt_bias'], prm['c_a_log'],
                   prm['c_d_exp'], prm['c_g_norm'], nbat=nbat, lt=lt, lc=lc)
    conv_state = jnp.concatenate([prefix, xbc], axis=1)[:, -(C_CONV_W - 1):]
    x1, h2, gate = _outproj(ya, yb, yc, x, mod, prm['g_ffn'], lw['wo'], lw['wr'], lw['br'], nb=nb, r=r,
                            nparts=prm['out_parts'])
    r_moe = r if nb > 1 else min(S, 1024)
    xo = _moe(h2, gate, x1, mod, prm['moe_w_gu'], prm['moe_w_down'], l, final, nb=nb, r=r_moe, epb=4)
    states = (knew.reshape(B, -1, A_HEADS, A_HEAD_DIM), vnew.reshape(B, -1, A_HEADS, A_HEAD_DIM),
              ckv, kpe, conv_state, jnp.swapaxes(hft, -1, -2))
    return xo, states


def kernel(x_prompt, x_sample, c_prompt, c_sample, cache_a_k, cache_a_v, cache_b_ckv, cache_b_kpe,
           state_c_conv, state_c_ssm, w_ada, b_ada, g_mix, w_in, a_rel_bias, b_g_q, b_w_uq, b_g_kv,
           b_w_uk, b_w_uv, c_conv_w, c_conv_b, c_dt_bias, c_a_log, c_d, c_g_norm, w_out, g_ffn,
           moe_w_rg, moe_b_rg, moe_w_re, moe_b_re, moe_w_gu, moe_w_down, g_final, w_ada_f, b_ada_f):
    depth = w_in.shape[0]
    D = x_prompt.shape[-1]
    Bp, Sp, _ = x_prompt.shape
    Bs, Ss, _ = x_sample.shape
    past = cache_b_ckv.shape[2]
    pos_p = jnp.arange(Sp)
    pos_s = past + jnp.arange(Ss)

    nc = Bp + Bs
    ncp = -(-nc // 8) * 8
    c_all = jnp.pad(jnp.concatenate([c_prompt, c_sample], axis=0), ((0, ncp - nc), (0, 0)))
    mod = _ada(c_all, w_ada, b_ada[:, None, :])
    modf = _ada(c_all, w_ada_f[None], b_ada_f[None, None, :])[0]

    nr = cache_a_k.shape[2]
    to_t = lambda c: jnp.transpose(c, (0, 1, 3, 4, 2)).reshape(depth, Bs, A_WIDTH, nr)
    cache = {'a_kt': to_t(cache_a_k), 'a_vt': to_t(cache_a_v), 'b_ckv': cache_b_ckv,
             'b_kpet': jnp.swapaxes(cache_b_kpe, 2, 3), 'c_conv': state_c_conv, 'c_ssm': state_c_ssm}

    rp = min(Sp, A_WINDOW)
    moe_w_gu = moe_w_gu.astype(BF16)
    moe_w_down = moe_w_down.astype(BF16)
    xp, xs = x_prompt, x_sample
    st_p, st_s = [], []
    for l in range(depth):
        lw = _layer_weights(l, w_in, b_g_q, b_w_uq, b_g_kv, b_w_uk, b_w_uv, w_out,
                            moe_w_rg, moe_b_rg, moe_w_re, moe_b_re)
        prm = {'g_mix': g_mix[l][None, :], 'a_rel_bias': a_rel_bias[l], 'c_conv_w': c_conv_w[l],
               'c_conv_b': c_conv_b[l][None, :], 'c_dt_bias': jnp.pad(c_dt_bias[l], (0, LANES - C_HEADS))[None, :],
               'c_a_log': jnp.pad(c_a_log[l], (0, LANES - C_HEADS))[None, :],
               'c_d_exp': jnp.repeat(c_d[l], C_HEAD_DIM)[None, :], 'c_g_norm': c_g_norm[l][None, :],
               'g_ffn': g_ffn[l][None, :], 'moe_w_gu': moe_w_gu, 'moe_w_down': moe_w_down,
               'in_parts': 1, 'out_parts': 2}
        fin_p = (modf[0:Bp][:, None, :], g_final[None, :]) if l == depth - 1 else None
        fin_s = (modf[Bp:nc][:, None, :], g_final[None, :]) if l == depth - 1 else None
        xp, sp = _layer(xp, mod[l, 0:Bp][:, None, :], pos_p, lw, l, None, prm, fin_p, nb=1, r=rp)
        xs, ss = _layer(xs, mod[l, Bp:nc][:, None, :], pos_s, lw, l, cache, prm, fin_s, nb=Bs, r=Ss)
        st_p.append(sp)
        st_s.append(ss)
    stack = lambda st, k: jnp.stack([t[k] for t in st])
    return ((xp, xs) + tuple(stack(st_p, k) for k in range(6))
            + tuple(stack(st_s, k) for k in range(6)))
```

```python
import functools
import math

import numpy as np
import jax
import jax.numpy as jnp
from jax import lax
from jax.experimental import pallas as pl
from jax.experimental.pallas import tpu as pltpu

F32 = jnp.float32
BF16 = jnp.bfloat16

D_MODEL = 1024
CHUNK = 64
EPS = 1e-6
NEG_INF = -1e30

A_HEADS = 4
A_HEAD_DIM = 64
A_WIDTH = A_HEADS * A_HEAD_DIM
A_BACK = 8
A_BAND = (A_BACK + 1) * CHUNK
A_WINDOW = A_BACK * CHUNK
A_REL_CLIP = 128
A_SCALE = A_HEAD_DIM ** -0.5

B_HEADS = 4
B_Q_LORA = 192
B_KV_LORA = 128
B_NOPE = 64
B_ROPE = 32
B_V = 64
B_WIDTH = B_HEADS * B_V
B_SCALE = (B_NOPE + B_ROPE) ** -0.5
B_QK = B_KV_LORA + B_ROPE
ROPE_THETA = 10000.0

C_D_INNER = 512
C_HEAD_DIM = 64
C_HEADS = C_D_INNER // C_HEAD_DIM
C_GROUPS = 2
C_D_STATE = 64
C_CONV_W = 4
C_CONV_DIM = C_D_INNER + 2 * C_GROUPS * C_D_STATE
C_GN = C_GROUPS * C_D_STATE
C_HPG = C_HEADS // C_GROUPS
SSD_CHUNK = 64

E_GROUPS = 4
E_PER_GROUP = 4
N_EXPERTS = E_GROUPS * E_PER_GROUP
E_HIDDEN = 256
ROUTER_LANES = 128
ROUTER_E_OFF = 16

LANES = 128
VMEM_LIMIT = 56 * 1024 * 1024

LOG2E = math.log2(math.e)
B_QSCALE = B_SCALE * LOG2E
A_QSCALE = A_SCALE * LOG2E

_W1_QKV = 0
_W1_CQ = 3 * A_WIDTH
_W1_CKV = _W1_CQ + 256
_W1_KPE = _W1_CKV + B_KV_LORA
_W1_Z = _W1_KPE + LANES
_W1_XBC = _W1_Z + C_D_INNER
_W1_N = _W1_XBC + C_CONV_DIM


def _cparams(*sem):
    return pltpu.CompilerParams(dimension_semantics=sem, vmem_limit_bytes=VMEM_LIMIT)


def _silu(v):
    return v * jax.nn.sigmoid(v)


def _split3_lanes(v, lane):
    x3 = v + pltpu.roll(v, C_HEADS, axis=1) + pltpu.roll(v, 2 * C_HEADS, axis=1)
    hi = x3.astype(BF16).astype(F32)
    r1 = x3 - hi
    mid = r1.astype(BF16).astype(F32)
    lo = r1 - mid
    return jnp.where(lane < C_HEADS, hi, jnp.where(lane < 2 * C_HEADS, mid, lo)).astype(BF16)


def _ada_kernel(c_ref, w_ref, b_ref, o_ref):
    s = _silu(c_ref[...]).astype(BF16)
    o_ref[0] = jnp.dot(s, w_ref[0].astype(BF16), preferred_element_type=F32) + b_ref[0]


def _ada(c_all, w, b):
    nl, d, n = w.shape
    bp = c_all.shape[0]
    tn = 1024
    return pl.pallas_call(
        _ada_kernel,
        grid=(nl, n // tn),
        in_specs=[pl.BlockSpec((bp, d), lambda l, j: (0, 0)),
                  pl.BlockSpec((1, d, tn), lambda l, j: (l, 0, j)),
                  pl.BlockSpec((1, 1, tn), lambda l, j: (l, 0, j))],
        out_specs=pl.BlockSpec((1, bp, tn), lambda l, j: (l, 0, j)),
        out_shape=jax.ShapeDtypeStruct((nl, bp, n), F32),
        compiler_params=_cparams("arbitrary", "arbitrary"),
        name="ada",
    )(c_all, w, b)


def _rows(ref_or_val, nb, r):
    v = ref_or_val[...]
    return jnp.broadcast_to(v, (nb, r, v.shape[-1])).reshape(nb * r, v.shape[-1])


def _inproj_kernel(x_ref, sh_ref, sc_ref, g_ref, w1_ref, gq_ref, wuq_ref, bd_ref, gkv_ref, cos_ref, sin_ref,
                   qkv_ref, knew_ref, vnew_ref, qmla_ref, kmla_ref, ckv_ref, kpe_ref, z_ref, xbc_ref, dtr_ref,
                   *, nb, r, keep_tile, nparts):
    D = x_ref.shape[-1]
    pnb, pr = (nb // nparts, r) if nb > 1 else (1, r // nparts)
    PR = pnb * pr

    def sel(s):
        return (slice(s * pnb, (s + 1) * pnb), slice(None)) if nb > 1 else (slice(None), slice(s * pr, (s + 1) * pr))

    def per_batch(ref, s):
        v = ref[sel(s)[0]] if nb > 1 else ref[...]
        return jnp.broadcast_to(v, (pnb, pr, v.shape[-1])).reshape(PR, v.shape[-1])

    def per_pos(ref, s):
        v = ref[...] if nb > 1 else ref[sel(s)[1]]
        return jnp.broadcast_to(v, (pnb, pr, v.shape[-1])).reshape(PR, v.shape[-1])

    def project(s):
        bs, rs = sel(s)
        x = x_ref[bs, rs].reshape(PR, D)
        h = x * lax.rsqrt(jnp.mean(x * x, axis=-1, keepdims=True) + EPS) * g_ref[...]
        h = h * (1.0 + per_batch(sc_ref, s)) + per_batch(sh_ref, s)
        return jnp.dot(h.astype(BF16), w1_ref[...], preferred_element_type=F32)

    def finish(s, u):
        bs, rs = sel(s)

        def out3(v):
            return v.reshape(pnb, pr, v.shape[-1])

        qkv_ref[bs, rs, 0:A_WIDTH] = out3((u[:, 0:A_WIDTH] * A_QSCALE).astype(BF16))
        qkv_ref[bs, rs, A_WIDTH:3 * A_WIDTH] = out3(u[:, A_WIDTH:3 * A_WIDTH].astype(BF16))

        @pl.when(pl.program_id(1) == keep_tile)
        def _():
            knew_ref[bs, rs] = out3(u[:, A_WIDTH:2 * A_WIDTH])
            vnew_ref[bs, rs] = out3(u[:, 2 * A_WIDTH:3 * A_WIDTH])

        blk = u[:, _W1_CQ:_W1_CQ + 256]
        lane = lax.broadcasted_iota(jnp.int32, (1, 256), 1)
        cq = jnp.where(lane < B_Q_LORA, blk, 0.0)
        cqn = cq * lax.rsqrt(jnp.sum(cq * cq, axis=-1, keepdims=True) * (1.0 / B_Q_LORA) + EPS) * gq_ref[...]
        qb = jnp.dot(cqn.astype(BF16), wuq_ref[...], preferred_element_type=F32)
        nope_w = B_HEADS * B_NOPE
        rope_w = B_HEADS * B_ROPE
        qlat = jnp.dot(qb[:, 0:nope_w].astype(BF16), bd_ref[...], preferred_element_type=F32) * B_QSCALE
        cosr = per_pos(cos_ref, s)
        sinr = per_pos(sin_ref, s)
        qpe = (qb[:, nope_w:nope_w + rope_w] * cosr
               + qb[:, nope_w + rope_w:nope_w + 2 * rope_w] * sinr) * B_QSCALE
        for hd in range(B_HEADS):
            qmla_ref[bs, hd, rs, 0:B_KV_LORA] = out3(qlat[:, hd * B_KV_LORA:(hd + 1) * B_KV_LORA].astype(BF16))
            qmla_ref[bs, hd, rs, B_KV_LORA:B_QK] = out3(qpe[:, hd * B_ROPE:(hd + 1) * B_ROPE].astype(BF16))

        cr = u[:, _W1_CKV:_W1_CKV + B_KV_LORA]
        ckv = cr * lax.rsqrt(jnp.mean(cr * cr, axis=-1, keepdims=True) + EPS) * gkv_ref[...]
        ckv_ref[bs, rs] = out3(ckv)
        kmla_ref[bs, rs, 0:B_KV_LORA] = out3(ckv.astype(BF16))
        kb = u[:, _W1_KPE:_W1_KPE + LANES]
        rot = kb * cosr + pltpu.roll(kb, LANES - B_ROPE, axis=1) * sinr
        kpe_ref[bs, rs] = out3(rot[:, 0:B_ROPE])
        kmla_ref[bs, rs, B_KV_LORA:B_QK] = out3(rot[:, 0:B_ROPE].astype(BF16))
        lane1 = lax.broadcasted_iota(jnp.int32, (1, LANES), 1)
        dtr_ref[bs, rs] = out3(jnp.where(lane1 < C_HEADS, pltpu.roll(kb, LANES - 2 * B_ROPE, axis=1), 0.0))

        z_ref[bs, rs] = out3(u[:, _W1_Z:_W1_Z + C_D_INNER].astype(BF16))
        xbc_ref[bs, rs] = out3(u[:, _W1_XBC:_W1_XBC + C_CONV_DIM])

    us = [project(s) for s in range(nparts)]
    for s in range(nparts):
        finish(s, us[s])


def _inproj(x, mod, g_mix, w1, gq, wuq, bd, gkv, cos_t, sin_t, *, nb, r, nparts):
    B, S, D = x.shape
    nbt, nst = B // nb, S // r
    keep = min(S, A_WINDOW)
    assert keep == r and B % nb == 0 and S % r == 0
    grid = (nbt, nst)
    row = lambda c: pl.BlockSpec((nb, r, c), lambda b, i: (b, i, 0))
    full = lambda a: pl.BlockSpec(a.shape, lambda b, i: (0,) * a.ndim)
    in_specs = [row(D),
                pl.BlockSpec((nb, 1, D), lambda b, i: (b, 0, 0)),
                pl.BlockSpec((nb, 1, D), lambda b, i: (b, 0, 1)),
                full(g_mix), full(w1), full(gq), full(wuq), full(bd), full(gkv),
                pl.BlockSpec((r, LANES), lambda b, i: (i, 0)),
                pl.BlockSpec((r, LANES), lambda b, i: (i, 0))]
    out_shape = [jax.ShapeDtypeStruct((B, S, 3 * A_WIDTH), BF16),
                 jax.ShapeDtypeStruct((B, keep, A_WIDTH), F32),
                 jax.ShapeDtypeStruct((B, keep, A_WIDTH), F32),
                 jax.ShapeDtypeStruct((B, B_HEADS, S, B_QK), BF16),
                 jax.ShapeDtypeStruct((B, S, B_QK), BF16),
                 jax.ShapeDtypeStruct((B, S, B_KV_LORA), F32),
                 jax.ShapeDtypeStruct((B, S, B_ROPE), F32),
                 jax.ShapeDtypeStruct((B, S, C_D_INNER), BF16),
                 jax.ShapeDtypeStruct((B, S, C_CONV_DIM), F32),
                 jax.ShapeDtypeStruct((B, S, LANES), F32)]
    out_specs = [row(3 * A_WIDTH),
                 pl.BlockSpec((nb, keep, A_WIDTH), lambda b, i: (b, 0, 0)),
                 pl.BlockSpec((nb, keep, A_WIDTH), lambda b, i: (b, 0, 0)),
                 pl.BlockSpec((nb, B_HEADS, r, B_QK), lambda b, i: (b, 0, i, 0)),
                 row(B_QK), row(B_KV_LORA), row(B_ROPE), row(C_D_INNER), row(C_CONV_DIM), row(LANES)]
    return pl.pallas_call(
        functools.partial(_inproj_kernel, nb=nb, r=r, keep_tile=nst - 1, nparts=nparts),
        grid=grid, in_specs=in_specs, out_specs=out_specs, out_shape=out_shape,
        compiler_params=_cparams("arbitrary", "arbitrary"),
        name="inproj",
    )(x, mod, mod, g_mix, w1, gq, wuq, bd, gkv, cos_t, sin_t)


def _band_scores(q, k, hmask):
    qs = jnp.where(hmask, jnp.concatenate([q] * A_HEADS, axis=0), jnp.zeros((), BF16))
    return lax.dot_general(qs, k, (((1,), (1,)), ((), ())), preferred_element_type=F32)


def _band_finish(s, v, bias, valid, hmask):
    nq = s.shape[0] // A_HEADS
    s = s + bias
    if valid is not None:
        s = jnp.where(valid, s, NEG_INF)
    m = jnp.max(s, axis=-1, keepdims=True)
    e = jnp.exp2(s - m)
    den = jnp.sum(e, axis=-1, keepdims=True)
    o = jnp.dot(e.astype(BF16), v, preferred_element_type=F32)
    o = jnp.where(hmask, o / den, 0.0)
    y = o[0:nq]
    for hd in range(1, A_HEADS):
        y = y + o[hd * nq:(hd + 1) * nq]
    return y


def _head_mask(nq):
    row = lax.broadcasted_iota(jnp.int32, (A_HEADS * nq, A_WIDTH), 0)
    col = lax.broadcasted_iota(jnp.int32, (A_HEADS * nq, A_WIDTH), 1)
    return (row // nq) == (col // A_HEAD_DIM)


def _band_prompt_kernel(cur_ref, prev_ref, bias_ref, o_ref, kwin, vwin, *, tq):
    i = pl.program_id(1)
    nch = tq // CHUNK
    kwin[0:A_WINDOW] = prev_ref[0, tq - A_WINDOW:tq, A_WIDTH:2 * A_WIDTH]
    kwin[A_WINDOW:A_WINDOW + tq] = cur_ref[0, :, A_WIDTH:2 * A_WIDTH]
    vwin[0:A_WINDOW] = prev_ref[0, tq - A_WINDOW:tq, 2 * A_WIDTH:3 * A_WIDTH]
    vwin[A_WINDOW:A_WINDOW + tq] = cur_ref[0, :, 2 * A_WIDTH:3 * A_WIDTH]
    hmask = _head_mask(CHUNK)
    ucol = lax.broadcasted_iota(jnp.int32, (1, A_BAND), 1)

    def scores(c):
        return _band_scores(cur_ref[0, c * CHUNK:(c + 1) * CHUNK, 0:A_WIDTH],
                            kwin[c * CHUNK:c * CHUNK + A_BAND], hmask)

    bias2 = bias_ref[...] * LOG2E
    s_next = scores(0)
    for c in range(nch):
        s = s_next
        if c + 1 < nch:
            s_next = scores(c + 1)
        valid = (ucol // CHUNK + (i * nch + c - A_BACK)) >= 0
        y = _band_finish(s, vwin[c * CHUNK:c * CHUNK + A_BAND], bias2, valid, hmask)
        o_ref[0, c * CHUNK:(c + 1) * CHUNK, :] = y.astype(BF16)


def _band_prompt(qkv, bias_stack):
    B, S, _ = qkv.shape
    tq = A_WINDOW
    assert S % tq == 0
    return pl.pallas_call(
        functools.partial(_band_prompt_kernel, tq=tq),
        grid=(B, S // tq),
        in_specs=[pl.BlockSpec((1, tq, 3 * A_WIDTH), lambda b, i: (b, i, 0)),
                  pl.BlockSpec((1, tq, 3 * A_WIDTH), lambda b, i: (b, jnp.maximum(i - 1, 0), 0)),
                  pl.BlockSpec(bias_stack.shape, lambda b, i: (0, 0))],
        out_specs=pl.BlockSpec((1, tq, A_WIDTH), lambda b, i: (b, i, 0)),
        out_shape=jax.ShapeDtypeStruct((B, S, A_WIDTH), BF16),
        scratch_shapes=[pltpu.VMEM((A_WINDOW + tq, A_WIDTH), BF16), pltpu.VMEM((A_WINDOW + tq, A_WIDTH), BF16)],
        compiler_params=_cparams("arbitrary", "arbitrary"),
        name="band_prompt",
    )(qkv, qkv, bias_stack)


def _band_sample_kernel(qkv_ref, kt_ref, vt_ref, biasc_ref, biasn_ref, o_ref, *, nbb, ds):
    hmask = _head_mask(ds)
    nt_dims = (((1,), (1,)), ((), ()))
    for b in range(nbb):
        q = qkv_ref[b, :, 0:A_WIDTH]
        kn = qkv_ref[b, :, A_WIDTH:2 * A_WIDTH]
        vn = qkv_ref[b, :, 2 * A_WIDTH:3 * A_WIDTH]
        qs = jnp.where(hmask, jnp.concatenate([q] * A_HEADS, axis=0), jnp.zeros((), BF16))
        sc = jnp.dot(qs, kt_ref[0, b].astype(BF16), preferred_element_type=F32) + biasc_ref[...] * LOG2E
        sn = lax.dot_general(qs, kn, nt_dims, preferred_element_type=F32) + biasn_ref[...] * LOG2E
        m = jnp.maximum(jnp.max(sc, axis=-1, keepdims=True), jnp.max(sn, axis=-1, keepdims=True))
        ec = jnp.exp2(sc - m)
        en = jnp.exp2(sn - m)
        den = jnp.sum(ec, axis=-1, keepdims=True) + jnp.sum(en, axis=-1, keepdims=True)
        o = lax.dot_general(ec.astype(BF16), vt_ref[0, b].astype(BF16), nt_dims, preferred_element_type=F32)
        o = o + jnp.dot(en.astype(BF16), vn, preferred_element_type=F32)
        o = jnp.where(hmask, o / den, 0.0)
        y = o[0:ds]
        for hd in range(1, A_HEADS):
            y = y + o[hd * ds:(hd + 1) * ds]
        o_ref[b] = y.astype(BF16)


def _band_sample(qkv, cache_kt, cache_vt, layer, bias_stack):
    B, ds, _ = qkv.shape
    nr = cache_kt.shape[3]
    nbb = 4
    assert B % nbb == 0
    bias_c, bias_n = bias_stack[:, 0:nr], bias_stack[:, nr:]
    return pl.pallas_call(
        functools.partial(_band_sample_kernel, nbb=nbb, ds=ds),
        grid=(B // nbb,),
        in_specs=[pl.BlockSpec((nbb, ds, 3 * A_WIDTH), lambda b: (b, 0, 0)),
                  pl.BlockSpec((1, nbb, A_WIDTH, nr), lambda b: (layer, b, 0, 0)),
                  pl.BlockSpec((1, nbb, A_WIDTH, nr), lambda b: (layer, b, 0, 0)),
                  pl.BlockSpec(bias_c.shape, lambda b: (0, 0)),
                  pl.BlockSpec(bias_n.shape, lambda b: (0, 0))],
        out_specs=pl.BlockSpec((nbb, ds, A_WIDTH), lambda b: (b, 0, 0)),
        out_shape=jax.ShapeDtypeStruct((B, ds, A_WIDTH), BF16),
        compiler_params=_cparams("arbitrary"),
        name="band_sample",
    )(qkv, cache_kt, cache_vt, bias_c, bias_n)


def _mla_finish(acc, den, wuv_ref, bq):
    o = acc / den
    o_all = jnp.concatenate([o[hd * bq:(hd + 1) * bq] for hd in range(B_HEADS)], axis=1)
    return jnp.dot(o_all.astype(BF16), wuv_ref[...], preferred_element_type=F32)


def _mla_prompt_kernel(qi_ref, kj_ref, last_ref, q_ref, k_ref, wuv_ref, o_ref, m_sc, acc_sc, *, bq, bk, sk):
    p = pl.program_id(1)
    i = qi_ref[p]
    j = kj_ref[p]
    M = B_HEADS * bq

    @pl.when(j == 0)
    def _():
        m_sc[...] = jnp.full(m_sc.shape, NEG_INF, F32)
        acc_sc[...] = jnp.zeros(acc_sc.shape, F32)

    nt_dims = (((1,), (1,)), ((), ()))
    nsub = bk // sk

    def scores(q, t):
        return lax.dot_general(q, k_ref[0, t * sk:(t + 1) * sk, :], nt_dims, preferred_element_type=F32)

    def absorb(s, t, m_run, acc):
        m_new = jnp.maximum(m_run, jnp.max(s, axis=1, keepdims=True))
        alpha = jnp.exp2(m_run - m_new)
        pr = jnp.exp2(s - jnp.tile(m_new, (1, sk // LANES)))
        v_ext = jnp.concatenate([k_ref[0, t * sk:(t + 1) * sk, 0:B_KV_LORA], jnp.ones((sk, LANES), BF16)], axis=1)
        acc = acc * jnp.concatenate([alpha, alpha], axis=1) + jnp.dot(pr.astype(BF16), v_ext,
                                                                      preferred_element_type=F32)
        return m_new, acc

    @pl.when(last_ref[p] == 0)
    def _():
        q = q_ref[0].reshape(M, B_QK)
        m_run, acc = m_sc[...], acc_sc[...]
        s_next = scores(q, 0)
        for t in range(nsub):
            s = s_next
            if t + 1 < nsub:
                s_next = scores(q, t + 1)
            m_run, acc = absorb(s, t, m_run, acc)
        m_sc[...] = m_run
        acc_sc[...] = acc

    @pl.when(last_ref[p] == 1)
    def _():
        for t in range(nsub):
            @pl.when(j * bk + t * sk <= i * bq + (bq - 1))
            def _():
                s = scores(q_ref[0].reshape(M, B_QK), t)
                row = lax.broadcasted_iota(jnp.int32, (M, sk), 0)
                col = lax.broadcasted_iota(jnp.int32, (M, sk), 1)
                s = jnp.where((col + (j * bk + t * sk)) // CHUNK <= ((row % bq) + i * bq) // CHUNK, s, NEG_INF)
                m_sc[...], acc_sc[...] = absorb(s, t, m_sc[...], acc_sc[...])
        o_ref[0] = _mla_finish(acc_sc[:, 0:B_KV_LORA], acc_sc[:, B_KV_LORA:], wuv_ref, bq).astype(BF16)


def _mla_prompt(qmla, kmla, wuv_bd, *, bq, bk):
    B, S = kmla.shape[0:2]
    bk = min(bk, S)
    assert S % bq == 0 and S % bk == 0 and bq % CHUNK == 0 and bk % bq == 0
    nq = S // bq
    n_kv = [(i * bq + bq - 1) // bk + 1 for i in range(nq)]
    qi = np.concatenate([np.full(n, i, np.int32) for i, n in enumerate(n_kv)])
    kj = np.concatenate([np.arange(n, dtype=np.int32) for n in n_kv])
    last = np.concatenate([(np.arange(n) == n - 1).astype(np.int32) for n in n_kv])
    grid_spec = pltpu.PrefetchScalarGridSpec(
        num_scalar_prefetch=3,
        grid=(B, len(qi)),
        in_specs=[pl.BlockSpec((1, B_HEADS, bq, B_QK), lambda b, p, qi, kj, last: (b, 0, qi[p], 0)),
                  pl.BlockSpec((1, bk, B_QK), lambda b, p, qi, kj, last: (b, kj[p], 0)),
                  pl.BlockSpec(wuv_bd.shape, lambda b, p, qi, kj, last: (0, 0))],
        out_specs=pl.BlockSpec((1, bq, B_WIDTH), lambda b, p, qi, kj, last: (b, qi[p], 0)),
        scratch_shapes=[pltpu.VMEM((B_HEADS * bq, LANES), F32),
                        pltpu.VMEM((B_HEADS * bq, B_KV_LORA + LANES), F32)])
    return pl.pallas_call(
        functools.partial(_mla_prompt_kernel, bq=bq, bk=bk, sk=min(bk, 512)),
        grid_spec=grid_spec,
        out_shape=jax.ShapeDtypeStruct((B, S, B_WIDTH), BF16),
        compiler_params=_cparams("arbitrary", "arbitrary"),
        name="mla_prompt",
    )(jnp.asarray(qi), jnp.asarray(kj), jnp.asarray(last), qmla, kmla, wuv_bd)


def _mla_sample_kernel(q_ref, kn_ref, ckv_ref, kpet_ref, wuv_ref, o_ref, *, ds):
    M = B_HEADS * ds
    nt_dims = (((1,), (1,)), ((), ()))
    q = q_ref[0].reshape(M, B_QK)
    kn = kn_ref[0]
    ckv = ckv_ref[0, 0].astype(BF16)
    sc = (lax.dot_general(q[:, 0:B_KV_LORA], ckv, nt_dims, preferred_element_type=F32)
          + jnp.dot(q[:, B_KV_LORA:B_QK], kpet_ref[0, 0].astype(BF16), preferred_element_type=F32))
    sn = lax.dot_general(q, kn, nt_dims, preferred_element_type=F32)
    m = jnp.maximum(jnp.max(sc, axis=1, keepdims=True), jnp.max(sn, axis=1, keepdims=True))
    pc = jnp.exp2(sc - m)
    pn = jnp.exp2(sn - m)
    den = jnp.sum(pc, axis=1, keepdims=True) + jnp.sum(pn, axis=1, keepdims=True)
    acc = (jnp.dot(pc.astype(BF16), ckv, preferred_element_type=F32)
           + jnp.dot(pn.astype(BF16), kn[:, 0:B_KV_LORA], preferred_element_type=F32))
    o_ref[0] = _mla_finish(acc, den, wuv_ref, ds).astype(BF16)


def _mla_sample(qmla, kmla, cache_ckv, cache_kpet, layer, wuv_bd):
    B, ds = kmla.shape[0:2]
    past = cache_ckv.shape[2]
    return pl.pallas_call(
        functools.partial(_mla_sample_kernel, ds=ds),
        grid=(B,),
        in_specs=[pl.BlockSpec((1, B_HEADS, ds, B_QK), lambda b: (b, 0, 0, 0)),
                  pl.BlockSpec((1, ds, B_QK), lambda b: (b, 0, 0)),
                  pl.BlockSpec((1, 1, past, B_KV_LORA), lambda b: (layer, b, 0, 0)),
                  pl.BlockSpec((1, 1, B_ROPE, past), lambda b: (layer, b, 0, 0)),
                  pl.BlockSpec(wuv_bd.shape, lambda b: (0, 0))],
        out_specs=pl.BlockSpec((1, ds, B_WIDTH), lambda b: (b, 0, 0)),
        out_shape=jax.ShapeDtypeStruct((B, ds, B_WIDTH), BF16),
        compiler_params=_cparams("arbitrary"),
        name="mla_sample",
    )(qmla, kmla, cache_ckv, cache_kpet, wuv_bd)


def _ssd_kernel(xbc_ref, z_ref, dtr_ref, pre_ref, h0_ref, cw_ref, cb_ref, dtb_ref, alog_ref, dsk_ref, gn_ref,
                es_ref, ep_ref, tri_ref, y_ref, hout_ref,
                cbuf, xs_sc, b_sc, c_sc, ces_sc, cep_sc, dep_sc, st_sc, *, nbat, lt, lc, nt):
    t = pl.program_id(1)
    nch = lt // lc
    HS = C_HEADS * lc
    gw = C_HPG * C_HEAD_DIM
    same_exp = lc == C_HEAD_DIM

    @pl.when(t == 0)
    def _():
        st_sc[...] = jnp.zeros(st_sc.shape, F32)
        for b in range(nbat):
            cbuf[b, 8 - (C_CONV_W - 1):8] = pre_ref[b]
            for g in range(C_GROUPS):
                blk = jnp.concatenate([h0_ref[b, g * C_HPG + hl] for hl in range(C_HPG)], axis=1)
                st_sc[b, g * C_D_STATE:(g + 1) * C_D_STATE, g * gw:(g + 1) * gw] = blk

    hl = lax.broadcasted_iota(jnp.int32, (lt, LANES), 1)
    for b in range(nbat):
        cbuf[b, 8:8 + lt] = xbc_ref[b]
        acc = jnp.broadcast_to(cb_ref[...], (lt, C_CONV_DIM))
        for kk in range(C_CONV_W):
            off = 8 - (C_CONV_W - 1) + kk
            acc = acc + cbuf[b, off:off + lt] * cw_ref[kk:kk + 1, :]
        tail = cbuf[b, 8 + lt - (C_CONV_W - 1):8 + lt]
        cbuf[b, 8 - (C_CONV_W - 1):8] = tail
        xc = _silu(acc)
        xs_sc[b] = xc[:, 0:C_D_INNER]
        b_sc[b] = xc[:, C_D_INNER:C_D_INNER + C_GN]
        c_sc[b] = xc[:, C_D_INNER + C_GN:C_D_INNER + 2 * C_GN]

        dt = jnp.where(hl < C_HEADS, jax.nn.softplus(dtr_ref[b] + dtb_ref[...]), 0.0)
        da = dt * (-jnp.exp(alog_ref[...]))
        c3 = jnp.dot(tri_ref[...], _split3_lanes(da, hl), preferred_element_type=F32)
        cum = jnp.where(hl < C_HEADS, c3 + pltpu.roll(c3, LANES - C_HEADS, axis=1)
                        + pltpu.roll(c3, LANES - 2 * C_HEADS, axis=1), 0.0)
        cum3 = _split3_lanes(cum, hl)
        cep = jnp.dot(cum3, ep_ref[...], preferred_element_type=F32)
        cep_sc[b] = cep
        if not same_exp:
            ces_sc[b] = jnp.dot(cum3, es_ref[...], preferred_element_type=F32)
        dep_sc[b] = jnp.dot(_split3_lanes(dt, hl), ep_ref[...], preferred_element_type=F32)

    r_s = lax.broadcasted_iota(jnp.int32, (lc, HS), 0)
    l_s = lax.broadcasted_iota(jnp.int32, (lc, HS), 1)
    eye_t = (l_s % lc) == r_s
    causal_t = (l_s % lc) <= r_s
    r_w = lax.broadcasted_iota(jnp.int32, (HS, C_GN), 0)
    l_w = lax.broadcasted_iota(jnp.int32, (HS, C_GN), 1)
    gmask = (r_w // (C_HPG * lc)) == (l_w // C_D_STATE)
    r_b =lax.broadcasted_iota(jnp.int32, (C_HPG * lc, gw), 0)
    l_b = lax.broadcasted_iota(jnp.int32, (C_HPG * lc, gw), 1)
    bmask = (r_b // lc) == (l_b // C_HEAD_DIM)
    r_g = lax.broadcasted_iota(jnp.int32, (C_GN, C_D_INNER), 0)
    l_g = lax.broadcasted_iota(jnp.int32, (C_GN, C_D_INNER), 1)
    smask = (r_g // C_D_STATE) == (l_g // gw)

    def stage_cb(b, rows):
        cm = c_sc[b, rows, :].astype(BF16)
        bm = b_sc[b, rows, :]
        w_nt = jnp.where(gmask, jnp.concatenate([bm] * C_HEADS, axis=0), 0.0).astype(BF16)
        cbcat = lax.dot_general(cm, w_nt, (((1,), (1,)), ((), ())), preferred_element_type=F32)
        st = st_sc[b]
        yoff = jnp.dot(cm, st.astype(BF16), preferred_element_type=F32)
        return dict(bm=bm, cbcat=cbcat, st=st, yoff=yoff)

    def stage_diag(b, rows, v):
        ce_p = cep_sc[b, rows, :]
        ce_s = ce_p if same_exp else ces_sc[b, rows, :]
        xs = xs_sc[b, rows, :]
        rflat = jnp.sum(jnp.where(eye_t, ce_s, 0.0), axis=0, keepdims=True)
        lcat = jnp.exp(jnp.where(causal_t, ce_s - rflat, NEG_INF))
        mcat = (v['cbcat'] * lcat).astype(BF16)
        xdt = xs * dep_sc[b, rows, :]
        xdt_b = xdt.astype(BF16)
        ydiag = []
        for g in range(C_GROUPS):
            bdx = jnp.where(bmask, jnp.concatenate([xdt_b[:, g * gw:(g + 1) * gw]] * C_HPG, axis=0),
                            jnp.zeros((), BF16))
            ydiag.append(jnp.dot(mcat[:, g * C_HPG * lc:(g + 1) * C_HPG * lc], bdx, preferred_element_type=F32))
        v.update(ce_p=ce_p, xs=xs, xdt=xdt, ydiag=jnp.concatenate(ydiag, axis=1))

    def stage_state(b, rows, v):
        ce_p = v['ce_p']
        last = ce_p[lc - 1:lc, :]
        xw = (v['xdt'] * jnp.exp(last - ce_p)).astype(BF16)
        upd = lax.dot_general(v['bm'].astype(BF16), xw, (((0,), (0,)), ((), ())), preferred_element_type=F32)
        st_sc[b] = jnp.exp(last) * v['st'] + jnp.where(smask, upd, 0.0)

    def stage_out(b, rows, v):
        y = v['ydiag'] + v['yoff'] * jnp.exp(v['ce_p']) + dsk_ref[...] * v['xs']
        y = y * _silu(z_ref[b, rows, :].astype(F32))
        y = y * lax.rsqrt(jnp.mean(y * y, axis=-1, keepdims=True) + EPS) * gn_ref[...]
        y_ref[b, rows, :] = y.astype(BF16)

    def chunk(c, carry):
        rows = pl.ds(pl.multiple_of(c * lc, lc), lc)
        vals = [stage_cb(b, rows) for b in range(nbat)]
        for stage in (stage_diag, stage_state, stage_out):
            for b in range(nbat):
                stage(b, rows, vals[b])
        return carry

    lax.fori_loop(0, nch, chunk, 0)

    @pl.when(t == nt - 1)
    def _():
        for b in range(nbat):
            for hd in range(C_HEADS):
                g = hd // C_HPG
                hout_ref[b, hd] = st_sc[b, g * C_D_STATE:(g + 1) * C_D_STATE,
                                        hd * C_HEAD_DIM:(hd + 1) * C_HEAD_DIM]


def _ssd(xbc, z, dtr, prefix, h0t, cw, cb, dtb, alog, dsk, gn, *, nbat, lt, lc):
    B, S, _ = xbc.shape
    nt = S // lt
    assert S % lt == 0 and lt % lc == 0 and B % nbat == 0
    hs = C_HEADS * lc

    def expand(width, per_head):
        e = np.zeros((LANES, width), np.float32)
        for part in range(3):
            e[part * C_HEADS:(part + 1) * C_HEADS] = (np.arange(width)[None, :] // per_head
                                                      == np.arange(C_HEADS)[:, None])
        return jnp.asarray(e, dtype=BF16)

    es, ep = expand(hs, lc), expand(C_D_INNER, C_HEAD_DIM)
    pos = np.arange(lt)
    tri = jnp.asarray((pos[None, :] <= pos[:, None]) & (pos[None, :] // lc == pos[:, None] // lc), dtype=BF16)
    row = lambda c: pl.BlockSpec((nbat, lt, c), lambda b, t: (b, t, 0))
    per_b = lambda a: pl.BlockSpec((nbat,) + a.shape[1:], lambda b, t: (b,) + (0,) * (a.ndim - 1))
    full = lambda a: pl.BlockSpec(a.shape, lambda b, t: (0,) * a.ndim)
    sc = lambda *shape: pltpu.VMEM((nbat,) + shape, F32)
    return pl.pallas_call(
        functools.partial(_ssd_kernel, nbat=nbat, lt=lt, lc=lc, nt=nt),
        grid=(B // nbat, nt),
        in_specs=[row(C_CONV_DIM), row(C_D_INNER), row(LANES), per_b(prefix), per_b(h0t),
                  full(cw), full(cb), full(dtb), full(alog), full(dsk), full(gn), full(es), full(ep), full(tri)],
        out_specs=[row(C_D_INNER), per_b(h0t)],
        out_shape=[jax.ShapeDtypeStruct((B, S, C_D_INNER), BF16),
                   jax.ShapeDtypeStruct(h0t.shape, F32)],
        scratch_shapes=[sc(lt + 8, C_CONV_DIM), sc(lt, C_D_INNER), sc(lt, C_GN), sc(lt, C_GN),
                        sc(lt, hs) if lc != C_HEAD_DIM else sc(8, LANES),
                        sc(lt, C_D_INNER), sc(lt, C_D_INNER), sc(C_GN, C_D_INNER)],
        compiler_params=_cparams("arbitrary", "arbitrary"),
        name="ssd",
    )(xbc, z, dtr, prefix, h0t, cw, cb, dtb, alog, dsk, gn, es, ep, tri)


def _outproj_kernel(ya_ref, yb_ref, yc_ref, x_ref, g1_ref, sh_ref, sc_ref, gf_ref, wo_ref, wr_ref, br_ref,
                    x1_ref, h2_ref, gate_ref, *, nb, r, nparts):
    D = x_ref.shape[-1]
    pnb, pr = (nb // nparts, r) if nb > 1 else (1, r // nparts)
    PR = pnb * pr

    def sel_part(s):
        return (slice(s * pnb, (s + 1) * pnb), slice(None)) if nb > 1 else (slice(None), slice(s * pr, (s + 1) * pr))

    def in2(ref, s):
        bs, rs = sel_part(s)
        return ref[bs, rs].reshape(PR, ref.shape[-1])

    def per_batch(ref, s):
        v = ref[sel_part(s)[0]] if nb > 1 else ref[...]
        return jnp.broadcast_to(v, (pnb, pr, v.shape[-1])).reshape(PR, v.shape[-1])

    def project(s):
        o = jnp.dot(in2(ya_ref, s), wo_ref[0:A_WIDTH, :], preferred_element_type=F32)
        o = o + jnp.dot(in2(yb_ref, s), wo_ref[A_WIDTH:A_WIDTH + B_WIDTH, :], preferred_element_type=F32)
        return o + jnp.dot(in2(yc_ref, s), wo_ref[A_WIDTH + B_WIDTH:, :], preferred_element_type=F32)

    wr = wr_ref[...]
    w_hi = wr.astype(BF16)
    w_lo = (wr - w_hi.astype(F32)).astype(BF16)

    def logits(s, o):
        bs, rs = sel_part(s)
        x1 = in2(x_ref, s) + per_batch(g1_ref, s) * o
        x1_ref[bs, rs] = x1.reshape(pnb, pr, D)
        h2 = x1 * lax.rsqrt(jnp.mean(x1 * x1, axis=-1, keepdims=True) + EPS) * gf_ref[...]
        h2 = h2 * (1.0 + per_batch(sc_ref, s)) + per_batch(sh_ref, s)
        h_hi = h2.astype(BF16)
        h2_ref[bs, rs] = h_hi.reshape(pnb, pr, D)
        h_lo = (h2 - h_hi.astype(F32)).astype(BF16)
        return (jnp.dot(h_hi, w_hi, preferred_element_type=F32) + jnp.dot(h_hi, w_lo, preferred_element_type=F32)
                + jnp.dot(h_lo, w_hi, preferred_element_type=F32) + br_ref[...])

    def route(s, lg):
        bs, rs = sel_part(s)
        lane = lax.broadcasted_iota(jnp.int32, (PR, ROUTER_LANES), 1)
        big = jnp.int32(ROUTER_LANES)
        is_g = lane < E_GROUPS
        gl = jnp.where(is_g, lg, NEG_INF)
        gmax = jnp.max(gl, axis=-1, keepdims=True)
        p_top = 1.0 / jnp.sum(jnp.where(is_g, jnp.exp(gl - gmax), 0.0), axis=-1, keepdims=True)
        g_idx = jnp.min(jnp.where(is_g & (gl == gmax), lane, big), axis=-1, keepdims=True)
        e_lane = lane - ROUTER_E_OFF
        sel = (e_lane >= 0) & (e_lane < N_EXPERTS) & ((e_lane // E_PER_GROUP) == g_idx)
        l1 = jnp.where(sel, lg, NEG_INF)
        m1 = jnp.max(l1, axis=-1, keepdims=True)
        i1 = jnp.min(jnp.where(sel & (l1 == m1), lane, big), axis=-1, keepdims=True)
        sel2 = sel & (lane != i1)
        l2 = jnp.where(sel2, lg, NEG_INF)
        m2 = jnp.max(l2, axis=-1, keepdims=True)
        i2 = jnp.min(jnp.where(sel2 & (l2 == m2), lane, big), axis=-1, keepdims=True)
        e2 = jnp.exp(m2 - m1)
        w1 = p_top / (1.0 + e2)
        w2 = p_top * e2 / (1.0 + e2)
        gate = jnp.where(lane == i1, w1, 0.0) + jnp.where(lane == i2, w2, 0.0)
        gate_ref[bs, rs] = gate.reshape(pnb, pr, ROUTER_LANES)

    outs = [project(s) for s in range(nparts)]
    lgs = [logits(s, outs[s]) for s in range(nparts)]
    for s in range(nparts):
        route(s, lgs[s])


def _outproj(ya, yb, yc, x, mod, g_ffn, wo, wr, br, *, nb, r, nparts):
    B, S, D = x.shape
    row = lambda c: pl.BlockSpec((nb, r, c), lambda b, i: (b, i, 0))
    modc = lambda j: pl.BlockSpec((nb, 1, D), lambda b, i: (b, 0, j))
    full = lambda a: pl.BlockSpec(a.shape, lambda b, i: (0,) * a.ndim)
    return pl.pallas_call(
        functools.partial(_outproj_kernel, nb=nb, r=r, nparts=nparts),
        grid=(B // nb, S // r),
        in_specs=[row(A_WIDTH), row(B_WIDTH), row(C_D_INNER), row(D), modc(2), modc(3), modc(4),
                  full(g_ffn), full(wo), full(wr), full(br)],
        out_specs=[row(D), row(D), row(ROUTER_LANES)],
        out_shape=[jax.ShapeDtypeStruct((B, S, D), F32), jax.ShapeDtypeStruct((B, S, D), BF16),
                   jax.ShapeDtypeStruct((B, S, ROUTER_LANES), F32)],
        compiler_params=_cparams("arbitrary", "arbitrary"),
        name="outproj",
    )(ya, yb, yc, x, mod, mod, mod, g_ffn, wo, wr, br)


def _moe_kernel(h_ref, gate_ref, x_ref, g2_ref, wgu_ref, wd_ref, *rest, nb, r, nsteps, epb, final):
    if final:
        shf_ref, scf_ref, gfin_ref, o_ref, acc_sc = rest
    else:
        o_ref, acc_sc = rest
    e = pl.program_id(2)
    R = nb * r
    D = x_ref.shape[-1]

    @pl.when(e == 0)
    def _():
        acc_sc[...] = jnp.zeros(acc_sc.shape, F32)

    def as_bf16(w):
        return w if w.dtype == BF16 else w.astype(BF16)

    h = h_ref[...].reshape(R, D)
    gate = gate_ref[...].reshape(R, ROUTER_LANES)
    lane = lax.broadcasted_iota(jnp.int32, (R, ROUTER_LANES), 1)
    acts = []
    for k in range(epb):
        gu = jnp.dot(h, as_bf16(wgu_ref[0, k]), preferred_element_type=F32)
        gcol = jnp.sum(jnp.where(lane == e * epb + k + ROUTER_E_OFF, gate, 0.0), axis=-1, keepdims=True)
        acts.append((_silu(gu[:, 0:E_HIDDEN]) * gu[:, E_HIDDEN:] * gcol).astype(BF16))
    wd = as_bf16(wd_ref[0]).reshape(epb * E_HIDDEN, D)
    acc_sc[...] += jnp.dot(jnp.concatenate(acts, axis=1), wd, preferred_element_type=F32)

    @pl.when(e == nsteps - 1)
    def _():
        xo = x_ref[...].reshape(R, D) + _rows(g2_ref, nb, r) * acc_sc[...]
        if final:
            xo = xo * lax.rsqrt(jnp.mean(xo * xo, axis=-1, keepdims=True) + EPS) * gfin_ref[...]
            xo = xo * (1.0 + _rows(scf_ref, nb, r)) + _rows(shf_ref, nb, r)
        o_ref[...] = xo.reshape(nb, r, D)


def _moe(h2, gate, x1, mod, wgu, wd, layer, final, *, nb, r, epb):
    B, S, D = x1.shape
    ne = wgu.shape[1]
    assert ne % epb == 0
    nsteps = ne // epb
    row = lambda c: pl.BlockSpec((nb, r, c), lambda b, i, e: (b, i, 0))
    modc = lambda j: pl.BlockSpec((nb, 1, D), lambda b, i, e: (b, 0, j))
    in_specs = [row(D), row(ROUTER_LANES), row(D), modc(5),
                pl.BlockSpec((1, epb, D, 2 * E_HIDDEN), lambda b, i, e: (layer, e, 0, 0)),
                pl.BlockSpec((1, epb, E_HIDDEN, D), lambda b, i, e: (layer, e, 0, 0))]
    args = [h2, gate, x1, mod, wgu, wd]
    if final is not None:
        modf, gfin = final
        in_specs += [modc(0), modc(1), pl.BlockSpec(gfin.shape, lambda b, i, e: (0, 0))]
        args += [modf, modf, gfin]
    return pl.pallas_call(
        functools.partial(_moe_kernel, nb=nb, r=r, nsteps=nsteps, epb=epb, final=final is not None),
        grid=(B // nb, S // r, nsteps),
        in_specs=in_specs,
        out_specs=row(D),
        out_shape=jax.ShapeDtypeStruct((B, S, D), F32),
        scratch_shapes=[pltpu.VMEM((nb * r, D), F32)],
        compiler_params=_cparams("arbitrary", "arbitrary", "arbitrary"),
        name="moe",
    )(*args)


def _rope_tables(pos):
    half = B_ROPE // 2
    inv = 1.0 / (ROPE_THETA ** (jnp.arange(half, dtype=F32) / half))
    ang = pos.astype(F32)[:, None] * inv[None, :]
    cos, sin = jnp.cos(ang), jnp.sin(ang)
    return (jnp.tile(jnp.concatenate([cos, cos], axis=1), (1, B_HEADS)),
            jnp.tile(jnp.concatenate([-sin, sin], axis=1), (1, B_HEADS)))


def _swap_halves(w):
    half = w.shape[-1] // 2
    return jnp.concatenate([w[..., half:], w[..., :half]], axis=-1)


def _layer_weights(l, w_in, b_g_q, b_w_uq, b_g_kv, b_w_uk, b_w_uv, w_out, moe_w_rg, moe_b_rg, moe_w_re, moe_b_re):
    D = w_in.shape[1]
    w = w_in[l]
    o_cq = 3 * A_WIDTH
    o_ckv = o_cq + B_Q_LORA
    o_kpe = o_ckv + B_KV_LORA
    o_z = o_kpe + B_ROPE
    o_xbc = o_z + C_D_INNER
    o_dt = o_xbc + C_CONV_DIM
    wkpe = w[:, o_kpe:o_z]
    zc = lambda n: jnp.zeros((D, n), F32)
    w1 = jnp.concatenate([w[:, 0:o_cq], w[:, o_cq:o_ckv], zc(256 - B_Q_LORA), w[:, o_ckv:o_kpe],
                          wkpe, _swap_halves(wkpe), w[:, o_dt:o_dt + C_HEADS],
                          zc(LANES - 2 * B_ROPE - C_HEADS), w[:, o_z:o_xbc], w[:, o_xbc:o_dt]],
                         axis=1).astype(BF16)
    assert w1.shape[1] == _W1_N
    gq = jnp.pad(b_g_q[l], (0, 256 - B_Q_LORA))[None, :]
    uq = b_w_uq[l]
    pe = uq[:, :, B_NOPE:]
    wuq = jnp.concatenate([uq[:, :, :B_NOPE].reshape(B_Q_LORA, -1), pe.reshape(B_Q_LORA, -1),
                           _swap_halves(pe).reshape(B_Q_LORA, -1)], axis=1)
    wuq = jnp.pad(wuq, ((0, 256 - B_Q_LORA), (0, 0))).astype(BF16)
    eye = jnp.eye(B_HEADS, dtype=F32)
    bd = (jnp.transpose(b_w_uk[l], (1, 2, 0))[:, :, None, :] * eye[:, None, :, None]).reshape(
        B_HEADS * B_NOPE, B_HEADS * B_KV_LORA).astype(BF16)
    wuv = (jnp.transpose(b_w_uv[l], (1, 0, 2))[:, :, None, :] * eye[:, None, :, None]).reshape(
        B_HEADS * B_KV_LORA, B_WIDTH).astype(BF16)
    wr = jnp.zeros((D, ROUTER_LANES), F32)
    wr = wr.at[:, 0:E_GROUPS].set(moe_w_rg[l]).at[:, ROUTER_E_OFF:ROUTER_E_OFF + N_EXPERTS].set(moe_w_re[l])
    br = jnp.zeros((1, ROUTER_LANES), F32)
    br = br.at[0, 0:E_GROUPS].set(moe_b_rg[l]).at[0, ROUTER_E_OFF:ROUTER_E_OFF + N_EXPERTS].set(moe_b_re[l])
    return dict(w1=w1, gq=gq, wuq=wuq, bd=bd, gkv=b_g_kv[l][None, :], wuv=wuv, wo=w_out[l].astype(BF16), wr=wr, br=br)


def _band_bias(rel_bias, nq, nk, back):
    n = nk + nq - 1
    idx = np.clip(back + nq - 1 - np.arange(n + 1), -A_REL_CLIP, A_REL_CLIP) + A_REL_CLIP
    line = rel_bias[:, idx]
    skew = jnp.tile(line, (1, nq))[:, 0:nq * n].reshape(A_HEADS, nq, n)
    return skew[:, :, nq - 1:nq - 1 + nk].reshape(A_HEADS * nq, nk)


def _layer(x, mod, pos, lw, l, cache, prm, final, *, nb, r):
    B, S, D = x.shape
    cos_t, sin_t = _rope_tables(pos)
    (qkv, knew, vnew, qmla, kmla, ckv, kpe, z, xbc, dtr) = _inproj(
        x, mod, prm['g_mix'], lw['w1'], lw['gq'], lw['wuq'], lw['bd'], lw['gkv'], cos_t, sin_t, nb=nb, r=r,
        nparts=prm['in_parts'])
    if cache is None:
        ya = _band_prompt(qkv, _band_bias(prm['a_rel_bias'], CHUNK, A_BAND, A_WINDOW))
        yb = _mla_prompt(qmla, kmla, lw['wuv'], bq=min(prm['mla_bq'], S), bk=1024)
        prefix = jnp.zeros((B, C_CONV_W - 1, C_CONV_DIM), F32)
        h0t = jnp.zeros((B, C_HEADS, C_D_STATE, C_HEAD_DIM), F32)
        lt, lc, nbat = min(S, 512), SSD_CHUNK, B
    else:
        nr = cache['a_kt'].shape[3]
        ya = _band_sample(qkv, cache['a_kt'], cache['a_vt'], l, _band_bias(prm['a_rel_bias'], S, nr + S, nr))
        yb = _mla_sample(qmla, kmla, cache['b_ckv'], cache['b_kpet'], l, lw['wuv'])
        prefix = cache['c_conv'][l]
        h0t = jnp.swapaxes(cache['c_ssm'][l], -1, -2)
        lt = lc = min(SSD_CHUNK, S)
        nbat = math.gcd(B, 4)
    yc, hft = _ssd(xbc, z, dtr, prefix, h0t, prm['c_conv_w'], prm['c_conv_b'], prm['c_dt_bias'], prm['c_a_log'],
                   prm['c_d_exp'], prm['c_g_norm'], nbat=nbat, lt=lt, lc=lc)
    conv_state = jnp.concatenate([prefix, xbc], axis=1)[:, -(C_CONV_W - 1):]
    x1, h2, gate = _outproj(ya, yb, yc, x, mod, prm['g_ffn'], lw['wo'], lw['wr'], lw['br'], nb=nb, r=r,
                            nparts=prm['out_parts'])
    r_moe = r if nb > 1 else min(S, 1024)
    xo = _moe(h2, gate, x1, mod, prm['moe_w_gu'], prm['moe_w_down'], l, final, nb=nb, r=r_moe, epb=4)
    states = (knew.reshape(B, -1, A_HEADS, A_HEAD_DIM), vnew.reshape(B, -1, A_HEADS, A_HEAD_DIM),
              ckv, kpe, conv_state, jnp.swapaxes(hft, -1, -2))
    return xo, states


def kernel(x_prompt, x_sample, c_prompt, c_sample, cache_a_k, cache_a_v, cache_b_ckv, cache_b_kpe,
           state_c_conv, state_c_ssm, w_ada, b_ada, g_mix, w_in, a_rel_bias, b_g_q, b_w_uq, b_g_kv,
           b_w_uk, b_w_uv, c_conv_w, c_conv_b, c_dt_bias, c_a_log, c_d, c_g_norm, w_out, g_ffn,
           moe_w_rg, moe_b_rg, moe_w_re, moe_b_re, moe_w_gu, moe_w_down, g_final, w_ada_f, b_ada_f):
    depth = w_in.shape[0]
    D = x_prompt.shape[-1]
    Bp, Sp, _ = x_prompt.shape
    Bs, Ss, _ = x_sample.shape
    past = cache_b_ckv.shape[2]
    pos_p = jnp.arange(Sp)
    pos_s = past + jnp.arange(Ss)

    nc = Bp + Bs
    ncp = -(-nc // 8) * 8
    c_all = jnp.pad(jnp.concatenate([c_prompt, c_sample], axis=0), ((0, ncp - nc), (0, 0)))
    mod = _ada(c_all, w_ada, b_ada[:, None, :])
    modf = _ada(c_all, w_ada_f[None], b_ada_f[None, None, :])[0]

    nr = cache_a_k.shape[2]
    to_t = lambda c: jnp.transpose(c, (0, 1, 3, 4, 2)).reshape(depth, Bs, A_WIDTH, nr)
    cache = {'a_kt': to_t(cache_a_k), 'a_vt': to_t(cache_a_v), 'b_ckv': cache_b_ckv,
             'b_kpet': jnp.swapaxes(cache_b_kpe, 2, 3), 'c_conv': state_c_conv, 'c_ssm': state_c_ssm}

    rp = min(Sp, A_WINDOW)
    moe_w_gu = moe_w_gu.astype(BF16)
    moe_w_down = moe_w_down.astype(BF16)
    xp, xs = x_prompt, x_sample
    st_p, st_s = [], []
    for l in range(depth):
        lw = _layer_weights(l, w_in, b_g_q, b_w_uq, b_g_kv, b_w_uk, b_w_uv, w_out,
                            moe_w_rg, moe_b_rg, moe_w_re, moe_b_re)
        prm = {'g_mix': g_mix[l][None, :], 'a_rel_bias': a_rel_bias[l], 'c_conv_w': c_conv_w[l],
               'c_conv_b': c_conv_b[l][None, :], 'c_dt_bias': jnp.pad(c_dt_bias[l], (0, LANES - C_HEADS))[None, :],
               'c_a_log': jnp.pad(c_a_log[l], (0, LANES - C_HEADS))[None, :],
               'c_d_exp': jnp.repeat(c_d[l], C_HEAD_DIM)[None, :], 'c_g_norm': c_g_norm[l][None, :],
               'g_ffn': g_ffn[l][None, :], 'moe_w_gu': moe_w_gu, 'moe_w_down': moe_w_down,
               'in_parts': 1, 'out_parts': 2, 'mla_bq': 512 if l == 0 else 1024}
        fin_p = (modf[0:Bp][:, None, :], g_final[None, :]) if l == depth - 1 else None
        fin_s = (modf[Bp:nc][:, None, :], g_final[None, :]) if l == depth - 1 else None
        xp, sp = _layer(xp, mod[l, 0:Bp][:, None, :], pos_p, lw, l, None, prm, fin_p, nb=1, r=rp)
        xs, ss = _layer(xs, mod[l, Bp:nc][:, None, :], pos_s, lw, l, cache, prm, fin_s, nb=Bs, r=Ss)
        st_p.append(sp)
        st_s.append(ss)
    stack = lambda st, k: jnp.stack([t[k] for t in st])
    return ((xp, xs) + tuple(stack(st_p, k) for k in range(6))
            + tuple(stack(st_s, k) for k in range(6)))
```

```python
import functools
import math

import numpy as np
import jax
import jax.numpy as jnp
from jax import lax
from jax.experimental import pallas as pl
from jax.experimental.pallas import tpu as pltpu

F32 = jnp.float32
BF16 = jnp.bfloat16

D_MODEL = 1024
CHUNK = 64
EPS = 1e-6
NEG_INF = -1e30

A_HEADS = 4
A_HEAD_DIM = 64
A_WIDTH = A_HEADS * A_HEAD_DIM
A_BACK = 8
A_BAND = (A_BACK + 1) * CHUNK
A_WINDOW = A_BACK * CHUNK
A_REL_CLIP = 128
A_SCALE = A_HEAD_DIM ** -0.5

B_HEADS = 4
B_Q_LORA = 192
B_KV_LORA = 128
B_NOPE = 64
B_ROPE = 32
B_V = 64
B_WIDTH = B_HEADS * B_V
B_SCALE = (B_NOPE + B_ROPE) ** -0.5
B_QK = B_KV_LORA + B_ROPE
ROPE_THETA = 10000.0

C_D_INNER = 512
C_HEAD_DIM = 64
C_HEADS = C_D_INNER // C_HEAD_DIM
C_GROUPS = 2
C_D_STATE = 64
C_CONV_W = 4
C_CONV_DIM = C_D_INNER + 2 * C_GROUPS * C_D_STATE
C_GN = C_GROUPS * C_D_STATE
C_HPG = C_HEADS // C_GROUPS
SSD_CHUNK = 64

E_GROUPS = 4
E_PER_GROUP = 4
N_EXPERTS = E_GROUPS * E_PER_GROUP
E_HIDDEN = 256
ROUTER_LANES = 128
ROUTER_E_OFF = 16

LANES = 128
VMEM_LIMIT = 56 * 1024 * 1024

LOG2E = math.log2(math.e)
B_QSCALE = B_SCALE * LOG2E
A_QSCALE = A_SCALE * LOG2E

_W1_QKV = 0
_W1_CQ = 3 * A_WIDTH
_W1_CKV = _W1_CQ + 256
_W1_KPE = _W1_CKV + B_KV_LORA
_W1_Z = _W1_KPE + LANES
_W1_XBC = _W1_Z + C_D_INNER
_W1_N = _W1_XBC + C_CONV_DIM


def _cparams(*sem):
    return pltpu.CompilerParams(dimension_semantics=sem, vmem_limit_bytes=VMEM_LIMIT)


def _silu(v):
    return v * jax.nn.sigmoid(v)


def _split3_lanes(v, lane):
    x3 = v + pltpu.roll(v, C_HEADS, axis=1) + pltpu.roll(v, 2 * C_HEADS, axis=1)
    hi = x3.astype(BF16).astype(F32)
    r1 = x3 - hi
    mid = r1.astype(BF16).astype(F32)
    lo = r1 - mid
    return jnp.where(lane < C_HEADS, hi, jnp.where(lane < 2 * C_HEADS, mid, lo)).astype(BF16)


def _ada_kernel(c_ref, w_ref, b_ref, o_ref):
    s = _silu(c_ref[...]).astype(BF16)
    o_ref[0] = jnp.dot(s, w_ref[0].astype(BF16), preferred_element_type=F32) + b_ref[0]


def _ada(c_all, w, b):
    nl, d, n = w.shape
    bp = c_all.shape[0]
    tn = 1024
    return pl.pallas_call(
        _ada_kernel,
        grid=(nl, n // tn),
        in_specs=[pl.BlockSpec((bp, d), lambda l, j: (0, 0)),
                  pl.BlockSpec((1, d, tn), lambda l, j: (l, 0, j)),
                  pl.BlockSpec((1, 1, tn), lambda l, j: (l, 0, j))],
        out_specs=pl.BlockSpec((1, bp, tn), lambda l, j: (l, 0, j)),
        out_shape=jax.ShapeDtypeStruct((nl, bp, n), F32),
        compiler_params=_cparams("arbitrary", "arbitrary"),
        name="ada",
    )(c_all, w, b)


def _rows(ref_or_val, nb, r):
    v = ref_or_val[...]
    return jnp.broadcast_to(v, (nb, r, v.shape[-1])).reshape(nb * r, v.shape[-1])


def _inproj_kernel(x_ref, sh_ref, sc_ref, g_ref, w1_ref, gq_ref, wuq_ref, bd_ref, gkv_ref, cos_ref, sin_ref,
                   qkv_ref, knew_ref, vnew_ref, qmla_ref, kmla_ref, ckv_ref, kpe_ref, z_ref, xbc_ref, dtr_ref,
                   *, nb, r, keep_tile, nparts):
    D = x_ref.shape[-1]
    pnb, pr = (nb // nparts, r) if nb > 1 else (1, r // nparts)
    PR = pnb * pr

    def sel(s):
        return (slice(s * pnb, (s + 1) * pnb), slice(None)) if nb > 1 else (slice(None), slice(s * pr, (s + 1) * pr))

    def per_batch(ref, s):
        v = ref[sel(s)[0]] if nb > 1 else ref[...]
        return jnp.broadcast_to(v, (pnb, pr, v.shape[-1])).reshape(PR, v.shape[-1])

    def per_pos(ref, s):
        v = ref[...] if nb > 1 else ref[sel(s)[1]]
        return jnp.broadcast_to(v, (pnb, pr, v.shape[-1])).reshape(PR, v.shape[-1])

    def project(s):
        bs, rs = sel(s)
        x = x_ref[bs, rs].reshape(PR, D)
        h = x * lax.rsqrt(jnp.mean(x * x, axis=-1, keepdims=True) + EPS) * g_ref[...]
        h = h * (1.0 + per_batch(sc_ref, s)) + per_batch(sh_ref, s)
        return jnp.dot(h.astype(BF16), w1_ref[...], preferred_element_type=F32)

    def finish(s, u):
        bs, rs = sel(s)

        def out3(v):
            return v.reshape(pnb, pr, v.shape[-1])

        qkv_ref[bs, rs, 0:A_WIDTH] = out3((u[:, 0:A_WIDTH] * A_QSCALE).astype(BF16))
        qkv_ref[bs, rs, A_WIDTH:3 * A_WIDTH] = out3(u[:, A_WIDTH:3 * A_WIDTH].astype(BF16))

        @pl.when(pl.program_id(1) == keep_tile)
        def _():
            knew_ref[bs, rs] = out3(u[:, A_WIDTH:2 * A_WIDTH])
            vnew_ref[bs, rs] = out3(u[:, 2 * A_WIDTH:3 * A_WIDTH])

        blk = u[:, _W1_CQ:_W1_CQ + 256]
        lane = lax.broadcasted_iota(jnp.int32, (1, 256), 1)
        cq = jnp.where(lane < B_Q_LORA, blk, 0.0)
        cqn = cq * lax.rsqrt(jnp.sum(cq * cq, axis=-1, keepdims=True) * (1.0 / B_Q_LORA) + EPS) * gq_ref[...]
        qb = jnp.dot(cqn.astype(BF16), wuq_ref[...], preferred_element_type=F32)
        nope_w = B_HEADS * B_NOPE
        rope_w = B_HEADS * B_ROPE
        qlat = jnp.dot(qb[:, 0:nope_w].astype(BF16), bd_ref[...], preferred_element_type=F32) * B_QSCALE
        cosr = per_pos(cos_ref, s)
        sinr = per_pos(sin_ref, s)
        qpe = (qb[:, nope_w:nope_w + rope_w] * cosr
               + qb[:, nope_w + rope_w:nope_w + 2 * rope_w] * sinr) * B_QSCALE
        for hd in range(B_HEADS):
            qmla_ref[bs, hd, rs, 0:B_KV_LORA] = out3(qlat[:, hd * B_KV_LORA:(hd + 1) * B_KV_LORA].astype(BF16))
            qmla_ref[bs, hd, rs, B_KV_LORA:B_QK] = out3(qpe[:, hd * B_ROPE:(hd + 1) * B_ROPE].astype(BF16))

        cr = u[:, _W1_CKV:_W1_CKV + B_KV_LORA]
        ckv = cr * lax.rsqrt(jnp.mean(cr * cr, axis=-1, keepdims=True) + EPS) * gkv_ref[...]
        ckv_ref[bs, rs] = out3(ckv)
        kmla_ref[bs, rs, 0:B_KV_LORA] = out3(ckv.astype(BF16))
        kb = u[:, _W1_KPE:_W1_KPE + LANES]
        rot = kb * cosr + pltpu.roll(kb, LANES - B_ROPE, axis=1) * sinr
        kpe_ref[bs, rs] = out3(rot[:, 0:B_ROPE])
        kmla_ref[bs, rs, B_KV_LORA:B_QK] = out3(rot[:, 0:B_ROPE].astype(BF16))
        lane1 = lax.broadcasted_iota(jnp.int32, (1, LANES), 1)
        dtr_ref[bs, rs] = out3(jnp.where(lane1 < C_HEADS, pltpu.roll(kb, LANES - 2 * B_ROPE, axis=1), 0.0))

        z_ref[bs, rs] = out3(u[:, _W1_Z:_W1_Z + C_D_INNER].astype(BF16))
        xbc_ref[bs, rs] = out3(u[:, _W1_XBC:_W1_XBC + C_CONV_DIM])

    us = [project(s) for s in range(nparts)]
    for s in range(nparts):
        finish(s, us[s])


def _inproj(x, mod, g_mix, w1, gq, wuq, bd, gkv, cos_t, sin_t, *, nb, r, nparts):
    B, S, D = x.shape
    nbt, nst = B // nb, S // r
    keep = min(S, A_WINDOW)
    assert keep == r and B % nb == 0 and S % r == 0
    grid = (nbt, nst)
    row = lambda c: pl.BlockSpec((nb, r, c), lambda b, i: (b, i, 0))
    full = lambda a: pl.BlockSpec(a.shape, lambda b, i: (0,) * a.ndim)
    in_specs = [row(D),
                pl.BlockSpec((nb, 1, D), lambda b, i: (b, 0, 0)),
                pl.BlockSpec((nb, 1, D), lambda b, i: (b, 0, 1)),
                full(g_mix), full(w1), full(gq), full(wuq), full(bd), full(gkv),
                pl.BlockSpec((r, LANES), lambda b, i: (i, 0)),
                pl.BlockSpec((r, LANES), lambda b, i: (i, 0))]
    out_shape = [jax.ShapeDtypeStruct((B, S, 3 * A_WIDTH), BF16),
                 jax.ShapeDtypeStruct((B, keep, A_WIDTH), F32),
                 jax.ShapeDtypeStruct((B, keep, A_WIDTH), F32),
                 jax.ShapeDtypeStruct((B, B_HEADS, S, B_QK), BF16),
                 jax.ShapeDtypeStruct((B, S, B_QK), BF16),
                 jax.ShapeDtypeStruct((B, S, B_KV_LORA), F32),
                 jax.ShapeDtypeStruct((B, S, B_ROPE), F32),
                 jax.ShapeDtypeStruct((B, S, C_D_INNER), BF16),
                 jax.ShapeDtypeStruct((B, S, C_CONV_DIM), F32),
                 jax.ShapeDtypeStruct((B, S, LANES), F32)]
    out_specs = [row(3 * A_WIDTH),
                 pl.BlockSpec((nb, keep, A_WIDTH), lambda b, i: (b, 0, 0)),
                 pl.BlockSpec((nb, keep, A_WIDTH), lambda b, i: (b, 0, 0)),
                 pl.BlockSpec((nb, B_HEADS, r, B_QK), lambda b, i: (b, 0, i, 0)),
                 row(B_QK), row(B_KV_LORA), row(B_ROPE), row(C_D_INNER), row(C_CONV_DIM), row(LANES)]
    return pl.pallas_call(
        functools.partial(_inproj_kernel, nb=nb, r=r, keep_tile=nst - 1, nparts=nparts),
        grid=grid, in_specs=in_specs, out_specs=out_specs, out_shape=out_shape,
        compiler_params=_cparams("arbitrary", "arbitrary"),
        name="inproj",
    )(x, mod, mod, g_mix, w1, gq, wuq, bd, gkv, cos_t, sin_t)


def _band_scores(q, k, hmask):
    qs = jnp.where(hmask, jnp.concatenate([q] * A_HEADS, axis=0), jnp.zeros((), BF16))
    return lax.dot_general(qs, k, (((1,), (1,)), ((), ())), preferred_element_type=F32)


def _band_finish(s, v, bias, valid, hmask):
    nq = s.shape[0] // A_HEADS
    s = s + bias
    if valid is not None:
        s = jnp.where(valid, s, NEG_INF)
    m = jnp.max(s, axis=-1, keepdims=True)
    e = jnp.exp2(s - m)
    den = jnp.sum(e, axis=-1, keepdims=True)
    o = jnp.dot(e.astype(BF16), v, preferred_element_type=F32)
    o = jnp.where(hmask, o / den, 0.0)
    y = o[0:nq]
    for hd in range(1, A_HEADS):
        y = y + o[hd * nq:(hd + 1) * nq]
    return y


def _head_mask(nq):
    row = lax.broadcasted_iota(jnp.int32, (A_HEADS * nq, A_WIDTH), 0)
    col = lax.broadcasted_iota(jnp.int32, (A_HEADS * nq, A_WIDTH), 1)
    return (row // nq) == (col // A_HEAD_DIM)


def _band_prompt_kernel(cur_ref, prev_ref, bias_ref, o_ref, kwin, vwin, *, tq):
    i = pl.program_id(1)
    nch = tq // CHUNK
    kwin[0:A_WINDOW] = prev_ref[0, tq - A_WINDOW:tq, A_WIDTH:2 * A_WIDTH]
    kwin[A_WINDOW:A_WINDOW + tq] = cur_ref[0, :, A_WIDTH:2 * A_WIDTH]
    vwin[0:A_WINDOW] = prev_ref[0, tq - A_WINDOW:tq, 2 * A_WIDTH:3 * A_WIDTH]
    vwin[A_WINDOW:A_WINDOW + tq] = cur_ref[0, :, 2 * A_WIDTH:3 * A_WIDTH]
    hmask = _head_mask(CHUNK)
    ucol = lax.broadcasted_iota(jnp.int32, (1, A_BAND), 1)

    def scores(c):
        return _band_scores(cur_ref[0, c * CHUNK:(c + 1) * CHUNK, 0:A_WIDTH],
                            kwin[c * CHUNK:c * CHUNK + A_BAND], hmask)

    bias2 = bias_ref[...] * LOG2E
    s_next = scores(0)
    for c in range(nch):
        s = s_next
        if c + 1 < nch:
            s_next = scores(c + 1)
        valid = (ucol // CHUNK + (i * nch + c - A_BACK)) >= 0
        y = _band_finish(s, vwin[c * CHUNK:c * CHUNK + A_BAND], bias2, valid, hmask)
        o_ref[0, c * CHUNK:(c + 1) * CHUNK, :] = y.astype(BF16)


def _band_prompt(qkv, bias_stack):
    B, S, _ = qkv.shape
    tq = A_WINDOW
    assert S % tq == 0
    return pl.pallas_call(
        functools.partial(_band_prompt_kernel, tq=tq),
        grid=(B, S // tq),
        in_specs=[pl.BlockSpec((1, tq, 3 * A_WIDTH), lambda b, i: (b, i, 0)),
                  pl.BlockSpec((1, tq, 3 * A_WIDTH), lambda b, i: (b, jnp.maximum(i - 1, 0), 0)),
                  pl.BlockSpec(bias_stack.shape, lambda b, i: (0, 0))],
        out_specs=pl.BlockSpec((1, tq, A_WIDTH), lambda b, i: (b, i, 0)),
        out_shape=jax.ShapeDtypeStruct((B, S, A_WIDTH), BF16),
        scratch_shapes=[pltpu.VMEM((A_WINDOW + tq, A_WIDTH), BF16), pltpu.VMEM((A_WINDOW + tq, A_WIDTH), BF16)],
        compiler_params=_cparams("arbitrary", "arbitrary"),
        name="band_prompt",
    )(qkv, qkv, bias_stack)


def _band_sample_kernel(qkv_ref, kt_ref, vt_ref, biasc_ref, biasn_ref, o_ref, *, nbb, ds):
    hmask = _head_mask(ds)
    nt_dims = (((1,), (1,)), ((), ()))
    for b in range(nbb):
        q = qkv_ref[b, :, 0:A_WIDTH]
        kn = qkv_ref[b, :, A_WIDTH:2 * A_WIDTH]
        vn = qkv_ref[b, :, 2 * A_WIDTH:3 * A_WIDTH]
        qs = jnp.where(hmask, jnp.concatenate([q] * A_HEADS, axis=0), jnp.zeros((), BF16))
        sc = jnp.dot(qs, kt_ref[0, b].astype(BF16), preferred_element_type=F32) + biasc_ref[...] * LOG2E
        sn = lax.dot_general(qs, kn, nt_dims, preferred_element_type=F32) + biasn_ref[...] * LOG2E
        m = jnp.maximum(jnp.max(sc, axis=-1, keepdims=True), jnp.max(sn, axis=-1, keepdims=True))
        ec = jnp.exp2(sc - m)
        en = jnp.exp2(sn - m)
        den = jnp.sum(ec, axis=-1, keepdims=True) + jnp.sum(en, axis=-1, keepdims=True)
        o = lax.dot_general(ec.astype(BF16), vt_ref[0, b].astype(BF16), nt_dims, preferred_element_type=F32)
        o = o + jnp.dot(en.astype(BF16), vn, preferred_element_type=F32)
        o = jnp.where(hmask, o / den, 0.0)
        y = o[0:ds]
        for hd in range(1, A_HEADS):
            y = y + o[hd * ds:(hd + 1) * ds]
        o_ref[b] = y.astype(BF16)


def _band_sample(qkv, cache_kt, cache_vt, layer, bias_stack):
    B, ds, _ = qkv.shape
    nr = cache_kt.shape[3]
    nbb = 4
    assert B % nbb == 0
    bias_c, bias_n = bias_stack[:, 0:nr], bias_stack[:, nr:]
    return pl.pallas_call(
        functools.partial(_band_sample_kernel, nbb=nbb, ds=ds),
        grid=(B // nbb,),
        in_specs=[pl.BlockSpec((nbb, ds, 3 * A_WIDTH), lambda b: (b, 0, 0)),
                  pl.BlockSpec((1, nbb, A_WIDTH, nr), lambda b: (layer, b, 0, 0)),
                  pl.BlockSpec((1, nbb, A_WIDTH, nr), lambda b: (layer, b, 0, 0)),
                  pl.BlockSpec(bias_c.shape, lambda b: (0, 0)),
                  pl.BlockSpec(bias_n.shape, lambda b: (0, 0))],
        out_specs=pl.BlockSpec((nbb, ds, A_WIDTH), lambda b: (b, 0, 0)),
        out_shape=jax.ShapeDtypeStruct((B, ds, A_WIDTH), BF16),
        compiler_params=_cparams("arbitrary"),
        name="band_sample",
    )(qkv, cache_kt, cache_vt, bias_c, bias_n)


def _mla_finish(acc, den, wuv_ref, bq):
    o = acc / den
    o_all = jnp.concatenate([o[hd * bq:(hd + 1) * bq] for hd in range(B_HEADS)], axis=1)
    return jnp.dot(o_all.astype(BF16), wuv_ref[...], preferred_element_type=F32)


def _mla_prompt_kernel(qi_ref, kj_ref, last_ref, q_ref, k_ref, wuv_ref, o_ref, m_sc, acc_sc, *, bq, bk, sk):
    p = pl.program_id(1)
    i = qi_ref[p]
    j = kj_ref[p]
    M = B_HEADS * bq

    @pl.when(j == 0)
    def _():
        m_sc[...] = jnp.full(m_sc.shape, NEG_INF, F32)
        acc_sc[...] = jnp.zeros(acc_sc.shape, F32)

    nt_dims = (((1,), (1,)), ((), ()))
    nsub = bk // sk

    def scores(q, t):
        return lax.dot_general(q, k_ref[0, t * sk:(t + 1) * sk, :], nt_dims, preferred_element_type=F32)

    def absorb(s, t, m_run, acc):
        m_new = jnp.maximum(m_run, jnp.max(s, axis=1, keepdims=True))
        alpha = jnp.exp2(m_run - m_new)
        pr = jnp.exp2(s - jnp.tile(m_new, (1, sk // LANES)))
        v_ext = jnp.concatenate([k_ref[0, t * sk:(t + 1) * sk, 0:B_KV_LORA], jnp.ones((sk, LANES), BF16)], axis=1)
        acc = acc * jnp.concatenate([alpha, alpha], axis=1) + jnp.dot(pr.astype(BF16), v_ext,
                                                                      preferred_element_type=F32)
        return m_new, acc

    @pl.when(last_ref[p] == 0)
    def _():
        q = q_ref[0].reshape(M, B_QK)
        m_run, acc = m_sc[...], acc_sc[...]
        s_next = scores(q, 0)
        for t in range(nsub):
            s = s_next
            if t + 1 < nsub:
                s_next = scores(q, t + 1)
            m_run, acc = absorb(s, t, m_run, acc)
        m_sc[...] = m_run
        acc_sc[...] = acc

    @pl.when(last_ref[p] == 1)
    def _():
        for t in range(nsub):
            @pl.when(j * bk + t * sk <= i * bq + (bq - 1))
            def _():
                s = scores(q_ref[0].reshape(M, B_QK), t)
                row = lax.broadcasted_iota(jnp.int32, (M, sk), 0)
                col = lax.broadcasted_iota(jnp.int32, (M, sk), 1)
                s = jnp.where((col + (j * bk + t * sk)) // CHUNK <= ((row % bq) + i * bq) // CHUNK, s, NEG_INF)
                m_sc[...], acc_sc[...] = absorb(s, t, m_sc[...], acc_sc[...])
        o_ref[0] = _mla_finish(acc_sc[:, 0:B_KV_LORA], acc_sc[:, B_KV_LORA:], wuv_ref, bq).astype(BF16)


def _mla_prompt(qmla, kmla, wuv_bd, *, bq, bk):
    B, S = kmla.shape[0:2]
    bk = min(bk, S)
    assert S % bq == 0 and S % bk == 0 and bq % CHUNK == 0 and bk % bq == 0
    nq = S // bq
    n_kv = [(i * bq + bq - 1) // bk + 1 for i in range(nq)]
    qi = np.concatenate([np.full(n, i, np.int32) for i, n in enumerate(n_kv)])
    kj = np.concatenate([np.arange(n, dtype=np.int32) for n in n_kv])
    last = np.concatenate([(np.arange(n) == n - 1).astype(np.int32) for n in n_kv])
    grid_spec = pltpu.PrefetchScalarGridSpec(
        num_scalar_prefetch=3,
        grid=(B, len(qi)),
        in_specs=[pl.BlockSpec((1, B_HEADS, bq, B_QK), lambda b, p, qi, kj, last: (b, 0, qi[p], 0)),
                  pl.BlockSpec((1, bk, B_QK), lambda b, p, qi, kj, last: (b, kj[p], 0)),
                  pl.BlockSpec(wuv_bd.shape, lambda b, p, qi, kj, last: (0, 0))],
        out_specs=pl.BlockSpec((1, bq, B_WIDTH), lambda b, p, qi, kj, last: (b, qi[p], 0)),
        scratch_shapes=[pltpu.VMEM((B_HEADS * bq, LANES), F32),
                        pltpu.VMEM((B_HEADS * bq, B_KV_LORA + LANES), F32)])
    return pl.pallas_call(
        functools.partial(_mla_prompt_kernel, bq=bq, bk=bk, sk=min(bk, 512)),
        grid_spec=grid_spec,
        out_shape=jax.ShapeDtypeStruct((B, S, B_WIDTH), BF16),
        compiler_params=_cparams("arbitrary", "arbitrary"),
        name="mla_prompt",
    )(jnp.asarray(qi), jnp.asarray(kj), jnp.asarray(last), qmla, kmla, wuv_bd)


def _mla_sample_kernel(q_ref, kn_ref, ckv_ref, kpet_ref, wuv_ref, o_ref, *, ds):
    M = B_HEADS * ds
    nt_dims = (((1,), (1,)), ((), ()))
    q = q_ref[0].reshape(M, B_QK)
    kn = kn_ref[0]
    ckv = ckv_ref[0, 0].astype(BF16)
    sc = (lax.dot_general(q[:, 0:B_KV_LORA], ckv, nt_dims, preferred_element_type=F32)
          + jnp.dot(q[:, B_KV_LORA:B_QK], kpet_ref[0, 0].astype(BF16), preferred_element_type=F32))
    sn = lax.dot_general(q, kn, nt_dims, preferred_element_type=F32)
    m = jnp.maximum(jnp.max(sc, axis=1, keepdims=True), jnp.max(sn, axis=1, keepdims=True))
    pc = jnp.exp2(sc - m)
    pn = jnp.exp2(sn - m)
    den = jnp.sum(pc, axis=1, keepdims=True) + jnp.sum(pn, axis=1, keepdims=True)
    acc = (jnp.dot(pc.astype(BF16), ckv, preferred_element_type=F32)
           + jnp.dot(pn.astype(BF16), kn[:, 0:B_KV_LORA], preferred_element_type=F32))
    o_ref[0] = _mla_finish(acc, den, wuv_ref, ds).astype(BF16)


def _mla_sample(qmla, kmla, cache_ckv, cache_kpet, layer, wuv_bd):
    B, ds = kmla.shape[0:2]
    past = cache_ckv.shape[2]
    return pl.pallas_call(
        functools.partial(_mla_sample_kernel, ds=ds),
        grid=(B,),
        in_specs=[pl.BlockSpec((1, B_HEADS, ds, B_QK), lambda b: (b, 0, 0, 0)),
                  pl.BlockSpec((1, ds, B_QK), lambda b: (b, 0, 0)),
                  pl.BlockSpec((1, 1, past, B_KV_LORA), lambda b: (layer, b, 0, 0)),
                  pl.BlockSpec((1, 1, B_ROPE, past), lambda b: (layer, b, 0, 0)),
                  pl.BlockSpec(wuv_bd.shape, lambda b: (0, 0))],
        out_specs=pl.BlockSpec((1, ds, B_WIDTH), lambda b: (b, 0, 0)),
        out_shape=jax.ShapeDtypeStruct((B, ds, B_WIDTH), BF16),
        compiler_params=_cparams("arbitrary"),
        name="mla_sample",
    )(qmla, kmla, cache_ckv, cache_kpet, wuv_bd)


def _ssd_kernel(xbc_ref, z_ref, dtr_ref, pre_ref, h0_ref, cw_ref, cb_ref, dtb_ref, alog_ref, dsk_ref, gn_ref,
                es_ref, ep_ref, tri_ref, y_ref, hout_ref,
                cbuf, xs_sc, b_sc, c_sc, ces_sc, cep_sc, dep_sc, st_sc, *, nbat, lt, lc, nt):
    t = pl.program_id(1)
    nch = lt // lc
    HS = C_HEADS * lc
    gw = C_HPG * C_HEAD_DIM
    same_exp = lc == C_HEAD_DIM

    @pl.when(t == 0)
    def _():
        st_sc[...] = jnp.zeros(st_sc.shape, F32)
        for b in range(nbat):
            cbuf[b, 8 - (C_CONV_W - 1):8] = pre_ref[b]
            for g in range(C_GROUPS):
                blk = jnp.concatenate([h0_ref[b, g * C_HPG + hl] for hl in range(C_HPG)], axis=1)
                st_sc[b, g * C_D_STATE:(g + 1) * C_D_STATE, g * gw:(g + 1) * gw] = blk

    hl = lax.broadcasted_iota(jnp.int32, (lt, LANES), 1)
    for b in range(nbat):
        cbuf[b, 8:8 + lt] = xbc_ref[b]
        acc = jnp.broadcast_to(cb_ref[...], (lt, C_CONV_DIM))
        for kk in range(C_CONV_W):
            off = 8 - (C_CONV_W - 1) + kk
            acc = acc + cbuf[b, off:off + lt] * cw_ref[kk:kk + 1, :]
        tail = cbuf[b, 8 + lt - (C_CONV_W - 1):8 + lt]
        cbuf[b, 8 - (C_CONV_W - 1):8] = tail
        xc = _silu(acc)
        xs_sc[b] = xc[:, 0:C_D_INNER]
        b_sc[b] = xc[:, C_D_INNER:C_D_INNER + C_GN]
        c_sc[b] = xc[:, C_D_INNER + C_GN:C_D_INNER + 2 * C_GN]

        dt = jnp.where(hl < C_HEADS, jax.nn.softplus(dtr_ref[b] + dtb_ref[...]), 0.0)
        da = dt * (-jnp.exp(alog_ref[...]))
        c3 = jnp.dot(tri_ref[...], _split3_lanes(da, hl), preferred_element_type=F32)
        cum = jnp.where(hl < C_HEADS, c3 + pltpu.roll(c3, LANES - C_HEADS, axis=1)
                        + pltpu.roll(c3, LANES - 2 * C_HEADS, axis=1), 0.0)
        cum3 = _split3_lanes(cum, hl)
        cep = jnp.dot(cum3, ep_ref[...], preferred_element_type=F32)
        cep_sc[b] = cep
        if not same_exp:
            ces_sc[b] = jnp.dot(cum3, es_ref[...], preferred_element_type=F32)
        dep_sc[b] = jnp.dot(_split3_lanes(dt, hl), ep_ref[...], preferred_element_type=F32)

    r_s = lax.broadcasted_iota(jnp.int32, (lc, HS), 0)
    l_s = lax.broadcasted_iota(jnp.int32, (lc, HS), 1)
    eye_t = (l_s % lc) == r_s
    causal_t = (l_s % lc) <= r_s
    r_w = lax.broadcasted_iota(jnp.int32, (HS, C_GN), 0)
    l_w = lax.broadcasted_iota(jnp.int32, (HS, C_GN), 1)
    gmask = (r_w // (C_HPG * lc)) == (l_w // C_D_STATE)
    r_b =lax.broadcasted_iota(jnp.int32, (C_HPG * lc, gw), 0)
    l_b = lax.broadcasted_iota(jnp.int32, (C_HPG * lc, gw), 1)
    bmask = (r_b // lc) == (l_b // C_HEAD_DIM)
    r_g = lax.broadcasted_iota(jnp.int32, (C_GN, C_D_INNER), 0)
    l_g = lax.broadcasted_iota(jnp.int32, (C_GN, C_D_INNER), 1)
    smask = (r_g // C_D_STATE) == (l_g // gw)

    def stage_cb(b, rows):
        cm = c_sc[b, rows, :].astype(BF16)
        bm = b_sc[b, rows, :]
        w_nt = jnp.where(gmask, jnp.concatenate([bm] * C_HEADS, axis=0), 0.0).astype(BF16)
        cbcat = lax.dot_general(cm, w_nt, (((1,), (1,)), ((), ())), preferred_element_type=F32)
        st = st_sc[b]
        yoff = jnp.dot(cm, st.astype(BF16), preferred_element_type=F32)
        return dict(bm=bm, cbcat=cbcat, st=st, yoff=yoff)

    def stage_diag(b, rows, v):
        ce_p = cep_sc[b, rows, :]
        ce_s = ce_p if same_exp else ces_sc[b, rows, :]
        xs = xs_sc[b, rows, :]
        rflat = jnp.sum(jnp.where(eye_t, ce_s, 0.0), axis=0, keepdims=True)
        lcat = jnp.exp(jnp.where(causal_t, ce_s - rflat, NEG_INF))
        mcat = (v['cbcat'] * lcat).astype(BF16)
        xdt = xs * dep_sc[b, rows, :]
        xdt_b = xdt.astype(BF16)
        ydiag = []
        for g in range(C_GROUPS):
            bdx = jnp.where(bmask, jnp.concatenate([xdt_b[:, g * gw:(g + 1) * gw]] * C_HPG, axis=0),
                            jnp.zeros((), BF16))
            ydiag.append(jnp.dot(mcat[:, g * C_HPG * lc:(g + 1) * C_HPG * lc], bdx, preferred_element_type=F32))
        v.update(ce_p=ce_p, xs=xs, xdt=xdt, ydiag=jnp.concatenate(ydiag, axis=1))

    def stage_state(b, rows, v):
        ce_p = v['ce_p']
        last = ce_p[lc - 1:lc, :]
        xw = (v['xdt'] * jnp.exp(last - ce_p)).astype(BF16)
        upd = lax.dot_general(v['bm'].astype(BF16), xw, (((0,), (0,)), ((), ())), preferred_element_type=F32)
        st_sc[b] = jnp.exp(last) * v['st'] + jnp.where(smask, upd, 0.0)

    def stage_out(b, rows, v):
        y = v['ydiag'] + v['yoff'] * jnp.exp(v['ce_p']) + dsk_ref[...] * v['xs']
        y = y * _silu(z_ref[b, rows, :].astype(F32))
        y = y * lax.rsqrt(jnp.mean(y * y, axis=-1, keepdims=True) + EPS) * gn_ref[...]
        y_ref[b, rows, :] = y.astype(BF16)

    def chunk(c, carry):
        rows = pl.ds(pl.multiple_of(c * lc, lc), lc)
        vals = [stage_cb(b, rows) for b in range(nbat)]
        for stage in (stage_diag, stage_state, stage_out):
            for b in range(nbat):
                stage(b, rows, vals[b])
        return carry

    lax.fori_loop(0, nch, chunk, 0)

    @pl.when(t == nt - 1)
    def _():
        for b in range(nbat):
            for hd in range(C_HEADS):
                g = hd // C_HPG
                hout_ref[b, hd] = st_sc[b, g * C_D_STATE:(g + 1) * C_D_STATE,
                                        hd * C_HEAD_DIM:(hd + 1) * C_HEAD_DIM]


def _ssd(xbc, z, dtr, prefix, h0t, cw, cb, dtb, alog, dsk, gn, *, nbat, lt, lc):
    B, S, _ = xbc.shape
    nt = S // lt
    assert S % lt == 0 and lt % lc == 0 and B % nbat == 0
    hs = C_HEADS * lc

    def expand(width, per_head):
        e = np.zeros((LANES, width), np.float32)
        for part in range(3):
            e[part * C_HEADS:(part + 1) * C_HEADS] = (np.arange(width)[None, :] // per_head
                                                      == np.arange(C_HEADS)[:, None])
        return jnp.asarray(e, dtype=BF16)

    es, ep = expand(hs, lc), expand(C_D_INNER, C_HEAD_DIM)
    pos = np.arange(lt)
    tri = jnp.asarray((pos[None, :] <= pos[:, None]) & (pos[None, :] // lc == pos[:, None] // lc), dtype=BF16)
    row = lambda c: pl.BlockSpec((nbat, lt, c), lambda b, t: (b, t, 0))
    per_b = lambda a: pl.BlockSpec((nbat,) + a.shape[1:], lambda b, t: (b,) + (0,) * (a.ndim - 1))
    full = lambda a: pl.BlockSpec(a.shape, lambda b, t: (0,) * a.ndim)
    sc = lambda *shape: pltpu.VMEM((nbat,) + shape, F32)
    return pl.pallas_call(
        functools.partial(_ssd_kernel, nbat=nbat, lt=lt, lc=lc, nt=nt),
        grid=(B // nbat, nt),
        in_specs=[row(C_CONV_DIM), row(C_D_INNER), row(LANES), per_b(prefix), per_b(h0t),
                  full(cw), full(cb), full(dtb), full(alog), full(dsk), full(gn), full(es), full(ep), full(tri)],
        out_specs=[row(C_D_INNER), per_b(h0t)],
        out_shape=[jax.ShapeDtypeStruct((B, S, C_D_INNER), BF16),
                   jax.ShapeDtypeStruct(h0t.shape, F32)],
        scratch_shapes=[sc(lt + 8, C_CONV_DIM), sc(lt, C_D_INNER), sc(lt, C_GN), sc(lt, C_GN),
                        sc(lt, hs) if lc != C_HEAD_DIM else sc(8, LANES),
                        sc(lt, C_D_INNER), sc(lt, C_D_INNER), sc(C_GN, C_D_INNER)],
        compiler_params=_cparams("arbitrary", "arbitrary"),
        name="ssd",
    )(xbc, z, dtr, prefix, h0t, cw, cb, dtb, alog, dsk, gn, es, ep, tri)


def _outproj_kernel(ya_ref, yb_ref, yc_ref, x_ref, g1_ref, sh_ref, sc_ref, gf_ref, wo_ref, wr_ref, br_ref,
                    x1_ref, h2_ref, gate_ref, *, nb, r, nparts):
    D = x_ref.shape[-1]
    pnb, pr = (nb // nparts, r) if nb > 1 else (1, r // nparts)
    PR = pnb * pr

    def sel_part(s):
        return (slice(s * pnb, (s + 1) * pnb), slice(None)) if nb > 1 else (slice(None), slice(s * pr, (s + 1) * pr))

    def in2(ref, s):
        bs, rs = sel_part(s)
        return ref[bs, rs].reshape(PR, ref.shape[-1])

    def per_batch(ref, s):
        v = ref[sel_part(s)[0]] if nb > 1 else ref[...]
        return jnp.broadcast_to(v, (pnb, pr, v.shape[-1])).reshape(PR, v.shape[-1])

    def project(s):
        o = jnp.dot(in2(ya_ref, s), wo_ref[0:A_WIDTH, :], preferred_element_type=F32)
        o = o + jnp.dot(in2(yb_ref, s), wo_ref[A_WIDTH:A_WIDTH + B_WIDTH, :], preferred_element_type=F32)
        return o + jnp.dot(in2(yc_ref, s), wo_ref[A_WIDTH + B_WIDTH:, :], preferred_element_type=F32)

    wr = wr_ref[...]
    w_hi = wr.astype(BF16)
    w_lo = (wr - w_hi.astype(F32)).astype(BF16)

    def logits(s, o):
        bs, rs = sel_part(s)
        x1 = in2(x_ref, s) + per_batch(g1_ref, s) * o
        x1_ref[bs, rs] = x1.reshape(pnb, pr, D)
        h2 = x1 * lax.rsqrt(jnp.mean(x1 * x1, axis=-1, keepdims=True) + EPS) * gf_ref[...]
        h2 = h2 * (1.0 + per_batch(sc_ref, s)) + per_batch(sh_ref, s)
        h_hi = h2.astype(BF16)
        h2_ref[bs, rs] = h_hi.reshape(pnb, pr, D)
        h_lo = (h2 - h_hi.astype(F32)).astype(BF16)
        return (jnp.dot(h_hi, w_hi, preferred_element_type=F32) + jnp.dot(h_hi, w_lo, preferred_element_type=F32)
                + jnp.dot(h_lo, w_hi, preferred_element_type=F32) + br_ref[...])

    def route(s, lg):
        bs, rs = sel_part(s)
        lane = lax.broadcasted_iota(jnp.int32, (PR, ROUTER_LANES), 1)
        big = jnp.int32(ROUTER_LANES)
        is_g = lane < E_GROUPS
        gl = jnp.where(is_g, lg, NEG_INF)
        gmax = jnp.max(gl, axis=-1, keepdims=True)
        p_top = 1.0 / jnp.sum(jnp.where(is_g, jnp.exp(gl - gmax), 0.0), axis=-1, keepdims=True)
        g_idx = jnp.min(jnp.where(is_g & (gl == gmax), lane, big), axis=-1, keepdims=True)
        e_lane = lane - ROUTER_E_OFF
        sel = (e_lane >= 0) & (e_lane < N_EXPERTS) & ((e_lane // E_PER_GROUP) == g_idx)
        l1 = jnp.where(sel, lg, NEG_INF)
        m1 = jnp.max(l1, axis=-1, keepdims=True)
        i1 = jnp.min(jnp.where(sel & (l1 == m1), lane, big), axis=-1, keepdims=True)
        sel2 = sel & (lane != i1)
        l2 = jnp.where(sel2, lg, NEG_INF)
        m2 = jnp.max(l2, axis=-1, keepdims=True)
        i2 = jnp.min(jnp.where(sel2 & (l2 == m2), lane, big), axis=-1, keepdims=True)
        e2 = jnp.exp(m2 - m1)
        w1 = p_top / (1.0 + e2)
        w2 = p_top * e2 / (1.0 + e2)
        gate = jnp.where(lane == i1, w1, 0.0) + jnp.where(lane == i2, w2, 0.0)
        gate = gate + jnp.where(lane == 0, g_idx.astype(F32), 0.0)
        gate_ref[bs, rs] = gate.reshape(pnb, pr, ROUTER_LANES)

    outs = [project(s) for s in range(nparts)]
    lgs = [logits(s, outs[s]) for s in range(nparts)]
    for s in range(nparts):
        route(s, lgs[s])


def _outproj(ya, yb, yc, x, mod, g_ffn, wo, wr, br, *, nb, r, nparts):
    B, S, D = x.shape
    row = lambda c: pl.BlockSpec((nb, r, c), lambda b, i: (b, i, 0))
    modc = lambda j: pl.BlockSpec((nb, 1, D), lambda b, i: (b, 0, j))
    full = lambda a: pl.BlockSpec(a.shape, lambda b, i: (0,) * a.ndim)
    return pl.pallas_call(
        functools.partial(_outproj_kernel, nb=nb, r=r, nparts=nparts),
        grid=(B // nb, S // r),
        in_specs=[row(A_WIDTH), row(B_WIDTH), row(C_D_INNER), row(D), modc(2), modc(3), modc(4),
                  full(g_ffn), full(wo), full(wr), full(br)],
        out_specs=[row(D), row(D), row(ROUTER_LANES)],
        out_shape=[jax.ShapeDtypeStruct((B, S, D), F32), jax.ShapeDtypeStruct((B, S, D), BF16),
                   jax.ShapeDtypeStruct((B, S, ROUTER_LANES), F32)],
        compiler_params=_cparams("arbitrary", "arbitrary"),
        name="outproj",
    )(ya, yb, yc, x, mod, mod, mod, g_ffn, wo, wr, br)


def _moe_kernel(h_ref, gate_ref, x_ref, g2_ref, wgu_ref, wd_ref, *rest, nb, r, nsteps, epb, final):
    if final:
        shf_ref, scf_ref, gfin_ref, o_ref, acc_sc = rest
    else:
        o_ref, acc_sc = rest
    e = pl.program_id(2)
    R = nb * r
    D = x_ref.shape[-1]

    @pl.when(e == 0)
    def _():
        acc_sc[...] = jnp.zeros(acc_sc.shape, F32)

    def as_bf16(w):
        return w if w.dtype == BF16 else w.astype(BF16)

    h = h_ref[...].reshape(R, D)
    gate = gate_ref[...].reshape(R, ROUTER_LANES)
    lane = lax.broadcasted_iota(jnp.int32, (R, ROUTER_LANES), 1)
    acts = []
    for k in range(epb):
        gu = jnp.dot(h, as_bf16(wgu_ref[0, k]), preferred_element_type=F32)
        gcol = jnp.sum(jnp.where(lane == e * epb + k + ROUTER_E_OFF, gate, 0.0), axis=-1, keepdims=True)
        acts.append((_silu(gu[:, 0:E_HIDDEN]) * gu[:, E_HIDDEN:] * gcol).astype(BF16))
    wd = as_bf16(wd_ref[0]).reshape(epb * E_HIDDEN, D)
    acc_sc[...] += jnp.dot(jnp.concatenate(acts, axis=1), wd, preferred_element_type=F32)

    @pl.when(e == nsteps - 1)
    def _():
        xo = x_ref[...].reshape(R, D) + _rows(g2_ref, nb, r) * acc_sc[...]
        if final:
            xo = xo * lax.rsqrt(jnp.mean(xo * xo, axis=-1, keepdims=True) + EPS) * gfin_ref[...]
            xo = xo * (1.0 + _rows(scf_ref, nb, r)) + _rows(shf_ref, nb, r)
        o_ref[...] = xo.reshape(nb, r, D)


def _moe(h2, gate, x1, mod, wgu, wd, layer, final, *, nb, r, epb):
    B, S, D = x1.shape
    ne = wgu.shape[1]
    assert ne % epb == 0
    nsteps = ne // epb
    row = lambda c: pl.BlockSpec((nb, r, c), lambda b, i, e: (b, i, 0))
    modc = lambda j: pl.BlockSpec((nb, 1, D), lambda b, i, e: (b, 0, j))
    in_specs = [row(D), row(ROUTER_LANES), row(D), modc(5),
                pl.BlockSpec((1, epb, D, 2 * E_HIDDEN), lambda b, i, e: (layer, e, 0, 0)),
                pl.BlockSpec((1, epb, E_HIDDEN, D), lambda b, i, e: (layer, e, 0, 0))]
    args = [h2, gate, x1, mod, wgu, wd]
    if final is not None:
        modf, gfin = final
        in_specs += [modc(0), modc(1), pl.BlockSpec(gfin.shape, lambda b, i, e: (0, 0))]
        args += [modf, modf, gfin]
    return pl.pallas_call(
        functools.partial(_moe_kernel, nb=nb, r=r, nsteps=nsteps, epb=epb, final=final is not None),
        grid=(B // nb, S // r, nsteps),
        in_specs=in_specs,
        out_specs=row(D),
        out_shape=jax.ShapeDtypeStruct((B, S, D), F32),
        scratch_shapes=[pltpu.VMEM((nb * r, D), F32)],
        compiler_params=_cparams("arbitrary", "arbitrary", "arbitrary"),
        name="moe",
    )(*args)


MOE_BLK = 128


def _moe_sorted_kernel(h_ref, gate_ref, x_ref, g2_ref, wgu_ref, wd_ref, tri_ref, *rest, nb, r, final):
    if final:
        shf_ref, scf_ref, gfin_ref, o_ref, xc_sc, yc_sc, p_sc, meta = rest
    else:
        o_ref, xc_sc, yc_sc, p_sc, meta = rest
    g = pl.program_id(2)
    R = nb * r
    D = x_ref.shape[-1]
    RC = xc_sc.shape[0]

    @pl.when(g == 0)
    def _():
        gate = gate_ref[...].reshape(R, ROUTER_LANES)
        lane = lax.broadcasted_iota(jnp.int32, (R, ROUTER_LANES), 1)
        onehot = jnp.where((lane < E_GROUPS) & (gate[:, 0:1] == lane.astype(F32)), 1.0, 0.0)
        pos = jnp.dot(tri_ref[...], onehot.astype(BF16), preferred_element_type=F32)
        rank = jnp.sum(onehot * pos, axis=1, keepdims=True)
        counts = jnp.sum(onehot, axis=0, keepdims=True)
        lane1 = lax.broadcasted_iota(jnp.int32, (1, ROUTER_LANES), 1)
        base = jnp.int32(0)
        baserow = jnp.zeros((1, ROUTER_LANES), F32)
        for gg in range(E_GROUPS):
            cnt = jnp.sum(jnp.where(lane1 == gg, counts, 0.0)).astype(jnp.int32)
            nblk = (cnt + (MOE_BLK - 1)) // MOE_BLK
            meta[gg] = base
            meta[E_GROUPS + gg] = nblk
            baserow = jnp.where(lane1 == gg, base.astype(F32), baserow)
            base = base + nblk * MOE_BLK
        cpos = jnp.sum(onehot * baserow, axis=1, keepdims=True) + rank
        colc = lax.broadcasted_iota(jnp.int32, (1, RC), 1).astype(F32)
        pm = jnp.where(cpos == colc, 1.0, 0.0).astype(BF16)
        p_sc[...] = pm
        g_hi = gate.astype(BF16)
        g_lo = (gate - g_hi.astype(F32)).astype(BF16)
        xext = jnp.concatenate([h_ref[...].reshape(R, D), g_hi, g_lo], axis=1)
        xc_sc[...] = lax.dot_general(pm, xext, (((0,), (0,)), ((), ())),
                                     preferred_element_type=F32).astype(BF16)
        yc_sc[...] = jnp.zeros(yc_sc.shape, BF16)

    base_g = meta[g]
    nblk_g = meta[E_GROUPS + g]
    lane_b = lax.broadcasted_iota(jnp.int32, (MOE_BLK, ROUTER_LANES), 1)

    def block(bi, carry):
        rows = pl.ds(pl.multiple_of(base_g + bi * MOE_BLK, MOE_BLK), MOE_BLK)
        xg = xc_sc[rows, 0:D]
        gg = (xc_sc[rows, D:D + ROUTER_LANES].astype(F32)
              + xc_sc[rows, D + ROUTER_LANES:D + 2 * ROUTER_LANES].astype(F32))
        acts = []
        for k in range(E_PER_GROUP):
            gu = jnp.dot(xg, wgu_ref[0, k], preferred_element_type=F32)
            gcol = jnp.sum(jnp.where(lane_b == ROUTER_E_OFF + g * E_PER_GROUP + k, gg, 0.0),
                           axis=-1, keepdims=True)
            acts.append((_silu(gu[:, 0:E_HIDDEN]) * gu[:, E_HIDDEN:] * gcol).astype(BF16))
        wd = wd_ref[0].reshape(E_PER_GROUP * E_HIDDEN, D)
        yc_sc[rows, :] = jnp.dot(jnp.concatenate(acts, axis=1), wd, preferred_element_type=F32).astype(BF16)
        return carry

    lax.fori_loop(0, nblk_g, block, 0)

    @pl.when(g == E_GROUPS - 1)
    def _():
        moe = jnp.dot(p_sc[...], yc_sc[...], preferred_element_type=F32)
        xo = x_ref[...].reshape(R, D) + _rows(g2_ref, nb, r) * moe
        if final:
            xo = xo * lax.rsqrt(jnp.mean(xo * xo, axis=-1, keepdims=True) + EPS) * gfin_ref[...]
            xo = xo * (1.0 + _rows(scf_ref, nb, r)) + _rows(shf_ref, nb, r)
        o_ref[...] = xo.reshape(nb, r, D)


def _moe_sorted(h2, gate, x1, mod, wgu, wd, layer, final, *, nb, r):
    B, S, D = x1.shape
    R = nb * r
    rc = R + E_GROUPS * MOE_BLK
    pos = np.arange(R)
    tri = jnp.asarray(pos[None, :] < pos[:, None], dtype=BF16)
    row = lambda c: pl.BlockSpec((nb, r, c), lambda b, i, g: (b, i, 0))
    modc = lambda j: pl.BlockSpec((nb, 1, D), lambda b, i, g: (b, 0, j))
    in_specs = [row(D), row(ROUTER_LANES), row(D), modc(5),
                pl.BlockSpec((1, E_PER_GROUP, D, 2 * E_HIDDEN), lambda b, i, g: (layer, g, 0, 0)),
                pl.BlockSpec((1, E_PER_GROUP, E_HIDDEN, D), lambda b, i, g: (layer, g, 0, 0)),
                pl.BlockSpec(tri.shape, lambda b, i, g: (0, 0))]
    args = [h2, gate, x1, mod, wgu, wd, tri]
    if final is not None:
        modf, gfin = final
        in_specs += [modc(0), modc(1), pl.BlockSpec(gfin.shape, lambda b, i, g: (0, 0))]
        args += [modf, modf, gfin]
    return pl.pallas_call(
        functools.partial(_moe_sorted_kernel, nb=nb, r=r, final=final is not None),
        grid=(B // nb, S // r, E_GROUPS),
        in_specs=in_specs,
        out_specs=row(D),
        out_shape=jax.ShapeDtypeStruct((B, S, D), F32),
        scratch_shapes=[pltpu.VMEM((rc, D + 2 * ROUTER_LANES), BF16), pltpu.VMEM((rc, D), BF16),
                        pltpu.VMEM((R, rc), BF16), pltpu.SMEM((2 * E_GROUPS,), jnp.int32)],
        compiler_params=_cparams("arbitrary", "arbitrary", "arbitrary"),
        name="moe_sorted",
    )(*args)


def _rope_tables(pos):
    half = B_ROPE // 2
    inv = 1.0 / (ROPE_THETA ** (jnp.arange(half, dtype=F32) / half))
    ang = pos.astype(F32)[:, None] * inv[None, :]
    cos, sin = jnp.cos(ang), jnp.sin(ang)
    return (jnp.tile(jnp.concatenate([cos, cos], axis=1), (1, B_HEADS)),
            jnp.tile(jnp.concatenate([-sin, sin], axis=1), (1, B_HEADS)))


def _swap_halves(w):
    half = w.shape[-1] // 2
    return jnp.concatenate([w[..., half:], w[..., :half]], axis=-1)


def _layer_weights(l, w_in, b_g_q, b_w_uq, b_g_kv, b_w_uk, b_w_uv, w_out, moe_w_rg, moe_b_rg, moe_w_re, moe_b_re):
    D = w_in.shape[1]
    w = w_in[l]
    o_cq = 3 * A_WIDTH
    o_ckv = o_cq + B_Q_LORA
    o_kpe = o_ckv + B_KV_LORA
    o_z = o_kpe + B_ROPE
    o_xbc = o_z + C_D_INNER
    o_dt = o_xbc + C_CONV_DIM
    wkpe = w[:, o_kpe:o_z]
    zc = lambda n: jnp.zeros((D, n), F32)
    w1 = jnp.concatenate([w[:, 0:o_cq], w[:, o_cq:o_ckv], zc(256 - B_Q_LORA), w[:, o_ckv:o_kpe],
                          wkpe, _swap_halves(wkpe), w[:, o_dt:o_dt + C_HEADS],
                          zc(LANES - 2 * B_ROPE - C_HEADS), w[:, o_z:o_xbc], w[:, o_xbc:o_dt]],
                         axis=1).astype(BF16)
    assert w1.shape[1] == _W1_N
    gq = jnp.pad(b_g_q[l], (0, 256 - B_Q_LORA))[None, :]
    uq = b_w_uq[l]
    pe = uq[:, :, B_NOPE:]
    wuq = jnp.concatenate([uq[:, :, :B_NOPE].reshape(B_Q_LORA, -1), pe.reshape(B_Q_LORA, -1),
                           _swap_halves(pe).reshape(B_Q_LORA, -1)], axis=1)
    wuq = jnp.pad(wuq, ((0, 256 - B_Q_LORA), (0, 0))).astype(BF16)
    eye = jnp.eye(B_HEADS, dtype=F32)
    bd = (jnp.transpose(b_w_uk[l], (1, 2, 0))[:, :, None, :] * eye[:, None, :, None]).reshape(
        B_HEADS * B_NOPE, B_HEADS * B_KV_LORA).astype(BF16)
    wuv = (jnp.transpose(b_w_uv[l], (1, 0, 2))[:, :, None, :] * eye[:, None, :, None]).reshape(
        B_HEADS * B_KV_LORA, B_WIDTH).astype(BF16)
    wr = jnp.zeros((D, ROUTER_LANES), F32)
    wr = wr.at[:, 0:E_GROUPS].set(moe_w_rg[l]).at[:, ROUTER_E_OFF:ROUTER_E_OFF + N_EXPERTS].set(moe_w_re[l])
    br = jnp.zeros((1, ROUTER_LANES), F32)
    br = br.at[0, 0:E_GROUPS].set(moe_b_rg[l]).at[0, ROUTER_E_OFF:ROUTER_E_OFF + N_EXPERTS].set(moe_b_re[l])
    return dict(w1=w1, gq=gq, wuq=wuq, bd=bd, gkv=b_g_kv[l][None, :], wuv=wuv, wo=w_out[l].astype(BF16), wr=wr, br=br)


def _band_bias(rel_bias, nq, nk, back):
    n = nk + nq - 1
    idx = np.clip(back + nq - 1 - np.arange(n + 1), -A_REL_CLIP, A_REL_CLIP) + A_REL_CLIP
    line = rel_bias[:, idx]
    skew = jnp.tile(line, (1, nq))[:, 0:nq * n].reshape(A_HEADS, nq, n)
    return skew[:, :, nq - 1:nq - 1 + nk].reshape(A_HEADS * nq, nk)


def _layer(x, mod, pos, lw, l, cache, prm, final, *, nb, r):
    B, S, D = x.shape
    cos_t, sin_t = _rope_tables(pos)
    (qkv, knew, vnew, qmla, kmla, ckv, kpe, z, xbc, dtr) = _inproj(
        x, mod, prm['g_mix'], lw['w1'], lw['gq'], lw['wuq'], lw['bd'], lw['gkv'], cos_t, sin_t, nb=nb, r=r,
        nparts=prm['in_parts'])
    if cache is None:
        ya = _band_prompt(qkv, _band_bias(prm['a_rel_bias'], CHUNK, A_BAND, A_WINDOW))
        yb = _mla_prompt(qmla, kmla, lw['wuv'], bq=min(prm['mla_bq'], S), bk=1024)
        prefix = jnp.zeros((B, C_CONV_W - 1, C_CONV_DIM), F32)
        h0t = jnp.zeros((B, C_HEADS, C_D_STATE, C_HEAD_DIM), F32)
        lt, lc, nbat = min(S, 512), SSD_CHUNK, B
    else:
        nr = cache['a_kt'].shape[3]
        ya = _band_sample(qkv, cache['a_kt'], cache['a_vt'], l, _band_bias(prm['a_rel_bias'], S, nr + S, nr))
        yb = _mla_sample(qmla, kmla, cache['b_ckv'], cache['b_kpet'], l, lw['wuv'])
        prefix = cache['c_conv'][l]
        h0t = jnp.swapaxes(cache['c_ssm'][l], -1, -2)
        lt = lc = min(SSD_CHUNK, S)
        nbat = math.gcd(B, 4)
    yc, hft = _ssd(xbc, z, dtr, prefix, h0t, prm['c_conv_w'], prm['c_conv_b'], prm['c_dt_bias'], prm['c_a_log'],
                   prm['c_d_exp'], prm['c_g_norm'], nbat=nbat, lt=lt, lc=lc)
    conv_state = jnp.concatenate([prefix, xbc], axis=1)[:, -(C_CONV_W - 1):]
    x1, h2, gate = _outproj(ya, yb, yc, x, mod, prm['g_ffn'], lw['wo'], lw['wr'], lw['br'], nb=nb, r=r,
                            nparts=prm['out_parts'])
    if prm['moe_sorted']:
        xo = _moe_sorted(h2, gate, x1, mod, prm['moe_w_gu'], prm['moe_w_down'], l, final, nb=nb, r=r)
    else:
        r_moe = r if nb > 1 else min(S, 1024)
        xo = _moe(h2, gate, x1, mod, prm['moe_w_gu'], prm['moe_w_down'], l, final, nb=nb, r=r_moe, epb=4)
    states = (knew.reshape(B, -1, A_HEADS, A_HEAD_DIM), vnew.reshape(B, -1, A_HEADS, A_HEAD_DIM),
              ckv, kpe, conv_state, jnp.swapaxes(hft, -1, -2))
    return xo, states


def kernel(x_prompt, x_sample, c_prompt, c_sample, cache_a_k, cache_a_v, cache_b_ckv, cache_b_kpe,
           state_c_conv, state_c_ssm, w_ada, b_ada, g_mix, w_in, a_rel_bias, b_g_q, b_w_uq, b_g_kv,
           b_w_uk, b_w_uv, c_conv_w, c_conv_b, c_dt_bias, c_a_log, c_d, c_g_norm, w_out, g_ffn,
           moe_w_rg, moe_b_rg, moe_w_re, moe_b_re, moe_w_gu, moe_w_down, g_final, w_ada_f, b_ada_f):
    depth = w_in.shape[0]
    D = x_prompt.shape[-1]
    Bp, Sp, _ = x_prompt.shape
    Bs, Ss, _ = x_sample.shape
    past = cache_b_ckv.shape[2]
    pos_p = jnp.arange(Sp)
    pos_s = past + jnp.arange(Ss)

    nc = Bp + Bs
    ncp = -(-nc // 8) * 8
    c_all = jnp.pad(jnp.concatenate([c_prompt, c_sample], axis=0), ((0, ncp - nc), (0, 0)))
    mod = _ada(c_all, w_ada, b_ada[:, None, :])
    modf = _ada(c_all, w_ada_f[None], b_ada_f[None, None, :])[0]

    nr = cache_a_k.shape[2]
    to_t = lambda c: jnp.transpose(c, (0, 1, 3, 4, 2)).reshape(depth, Bs, A_WIDTH, nr)
    cache = {'a_kt': to_t(cache_a_k), 'a_vt': to_t(cache_a_v), 'b_ckv': cache_b_ckv,
             'b_kpet': jnp.swapaxes(cache_b_kpe, 2, 3), 'c_conv': state_c_conv, 'c_ssm': state_c_ssm}

    rp = min(Sp, A_WINDOW)
    moe_w_gu = moe_w_gu.astype(BF16)
    moe_w_down = moe_w_down.astype(BF16)
    xp, xs = x_prompt, x_sample
    st_p, st_s = [], []
    for l in range(depth):
        lw = _layer_weights(l, w_in, b_g_q, b_w_uq, b_g_kv, b_w_uk, b_w_uv, w_out,
                            moe_w_rg, moe_b_rg, moe_w_re, moe_b_re)
        prm = {'g_mix': g_mix[l][None, :], 'a_rel_bias': a_rel_bias[l], 'c_conv_w': c_conv_w[l],
               'c_conv_b': c_conv_b[l][None, :], 'c_dt_bias': jnp.pad(c_dt_bias[l], (0, LANES - C_HEADS))[None, :],
               'c_a_log': jnp.pad(c_a_log[l], (0, LANES - C_HEADS))[None, :],
               'c_d_exp': jnp.repeat(c_d[l], C_HEAD_DIM)[None, :], 'c_g_norm': c_g_norm[l][None, :],
               'g_ffn': g_ffn[l][None, :], 'moe_w_gu': moe_w_gu, 'moe_w_down': moe_w_down,
               'in_parts': 1, 'out_parts': 2, 'mla_bq': 512, 'moe_sorted': l == 1}
        fin_p = (modf[0:Bp][:, None, :], g_final[None, :]) if l == depth - 1 else None
        fin_s = (modf[Bp:nc][:, None, :], g_final[None, :]) if l == depth - 1 else None
        xp, sp = _layer(xp, mod[l, 0:Bp][:, None, :], pos_p, lw, l, None, prm, fin_p, nb=1, r=rp)
        xs, ss = _layer(xs, mod[l, Bp:nc][:, None, :], pos_s, lw, l, cache, prm, fin_s, nb=Bs, r=Ss)
        st_p.append(sp)
        st_s.append(ss)
    stack = lambda st, k: jnp.stack([t[k] for t in st])
    return ((xp, xs) + tuple(stack(st_p, k) for k in range(6))
            + tuple(stack(st_s, k) for k in range(6)))
```

```python
import functools
import math

import numpy as np
import jax
import jax.numpy as jnp
from jax import lax
from jax.experimental import pallas as pl
from jax.experimental.pallas import tpu as pltpu

F32 = jnp.float32
BF16 = jnp.bfloat16

D_MODEL = 1024
CHUNK = 64
EPS = 1e-6
NEG_INF = -1e30

A_HEADS = 4
A_HEAD_DIM = 64
A_WIDTH = A_HEADS * A_HEAD_DIM
A_BACK = 8
A_BAND = (A_BACK + 1) * CHUNK
A_WINDOW = A_BACK * CHUNK
A_REL_CLIP = 128
A_SCALE = A_HEAD_DIM ** -0.5

B_HEADS = 4
B_Q_LORA = 192
B_KV_LORA = 128
B_NOPE = 64
B_ROPE = 32
B_V = 64
B_WIDTH = B_HEADS * B_V
B_SCALE = (B_NOPE + B_ROPE) ** -0.5
B_QK = B_KV_LORA + B_ROPE
ROPE_THETA = 10000.0

C_D_INNER = 512
C_HEAD_DIM = 64
C_HEADS = C_D_INNER // C_HEAD_DIM
C_GROUPS = 2
C_D_STATE = 64
C_CONV_W = 4
C_CONV_DIM = C_D_INNER + 2 * C_GROUPS * C_D_STATE
C_GN = C_GROUPS * C_D_STATE
C_HPG = C_HEADS // C_GROUPS
SSD_CHUNK = 64

E_GROUPS = 4
E_PER_GROUP = 4
N_EXPERTS = E_GROUPS * E_PER_GROUP
E_HIDDEN = 256
ROUTER_LANES = 128
ROUTER_E_OFF = 16

LANES = 128
VMEM_LIMIT = 56 * 1024 * 1024

LOG2E = math.log2(math.e)
B_QSCALE = B_SCALE * LOG2E
A_QSCALE = A_SCALE * LOG2E

_W1_QKV = 0
_W1_CQ = 3 * A_WIDTH
_W1_CKV = _W1_CQ + 256
_W1_KPE = _W1_CKV + B_KV_LORA
_W1_Z = _W1_KPE + LANES
_W1_XBC = _W1_Z + C_D_INNER
_W1_N = _W1_XBC + C_CONV_DIM


def _cparams(*sem):
    return pltpu.CompilerParams(dimension_semantics=sem, vmem_limit_bytes=VMEM_LIMIT)


def _silu(v):
    return v * jax.nn.sigmoid(v)


def _split3_lanes(v, lane):
    x3 = v + pltpu.roll(v, C_HEADS, axis=1) + pltpu.roll(v, 2 * C_HEADS, axis=1)
    hi = x3.astype(BF16).astype(F32)
    r1 = x3 - hi
    mid = r1.astype(BF16).astype(F32)
    lo = r1 - mid
    return jnp.where(lane < C_HEADS, hi, jnp.where(lane < 2 * C_HEADS, mid, lo)).astype(BF16)


def _ada_kernel(c_ref, w_ref, b_ref, o_ref):
    s = _silu(c_ref[...]).astype(BF16)
    o_ref[0] = jnp.dot(s, w_ref[0].astype(BF16), preferred_element_type=F32) + b_ref[0]


def _ada(c_all, w, b):
    nl, d, n = w.shape
    bp = c_all.shape[0]
    tn = 1024
    return pl.pallas_call(
        _ada_kernel,
        grid=(nl, n // tn),
        in_specs=[pl.BlockSpec((bp, d), lambda l, j: (0, 0)),
                  pl.BlockSpec((1, d, tn), lambda l, j: (l, 0, j)),
                  pl.BlockSpec((1, 1, tn), lambda l, j: (l, 0, j))],
        out_specs=pl.BlockSpec((1, bp, tn), lambda l, j: (l, 0, j)),
        out_shape=jax.ShapeDtypeStruct((nl, bp, n), F32),
        compiler_params=_cparams("arbitrary", "arbitrary"),
        name="ada",
    )(c_all, w, b)


def _rows(ref_or_val, nb, r):
    v = ref_or_val[...]
    return jnp.broadcast_to(v, (nb, r, v.shape[-1])).reshape(nb * r, v.shape[-1])


def _inproj_kernel(x_ref, sh_ref, sc_ref, g_ref, w1_ref, gq_ref, wuq_ref, bd_ref, gkv_ref, cos_ref, sin_ref,
                   qkv_ref, knew_ref, vnew_ref, qmla_ref, kmla_ref, ckv_ref, kpe_ref, z_ref, xbc_ref, dtr_ref,
                   *, nb, r, keep_tile, nparts):
    D = x_ref.shape[-1]
    pnb, pr = (nb // nparts, r) if nb > 1 else (1, r // nparts)
    PR = pnb * pr

    def sel(s):
        return (slice(s * pnb, (s + 1) * pnb), slice(None)) if nb > 1 else (slice(None), slice(s * pr, (s + 1) * pr))

    def per_batch(ref, s):
        v = ref[sel(s)[0]] if nb > 1 else ref[...]
        return jnp.broadcast_to(v, (pnb, pr, v.shape[-1])).reshape(PR, v.shape[-1])

    def per_pos(ref, s):
        v = ref[...] if nb > 1 else ref[sel(s)[1]]
        return jnp.broadcast_to(v, (pnb, pr, v.shape[-1])).reshape(PR, v.shape[-1])

    def project(s):
        bs, rs = sel(s)
        x = x_ref[bs, rs].reshape(PR, D)
        h = x * lax.rsqrt(jnp.mean(x * x, axis=-1, keepdims=True) + EPS) * g_ref[...]
        h = h * (1.0 + per_batch(sc_ref, s)) + per_batch(sh_ref, s)
        return jnp.dot(h.astype(BF16), w1_ref[...], preferred_element_type=F32)

    def finish(s, u):
        bs, rs = sel(s)

        def out3(v):
            return v.reshape(pnb, pr, v.shape[-1])

        qkv_ref[bs, rs, 0:A_WIDTH] = out3((u[:, 0:A_WIDTH] * A_QSCALE).astype(BF16))
        qkv_ref[bs, rs, A_WIDTH:3 * A_WIDTH] = out3(u[:, A_WIDTH:3 * A_WIDTH].astype(BF16))

        @pl.when(pl.program_id(1) == keep_tile)
        def _():
            knew_ref[bs, rs] = out3(u[:, A_WIDTH:2 * A_WIDTH])
            vnew_ref[bs, rs] = out3(u[:, 2 * A_WIDTH:3 * A_WIDTH])

        blk = u[:, _W1_CQ:_W1_CQ + 256]
        lane = lax.broadcasted_iota(jnp.int32, (1, 256), 1)
        cq = jnp.where(lane < B_Q_LORA, blk, 0.0)
        cqn = cq * lax.rsqrt(jnp.sum(cq * cq, axis=-1, keepdims=True) * (1.0 / B_Q_LORA) + EPS) * gq_ref[...]
        qb = jnp.dot(cqn.astype(BF16), wuq_ref[...], preferred_element_type=F32)
        nope_w = B_HEADS * B_NOPE
        rope_w = B_HEADS * B_ROPE
        qlat = jnp.dot(qb[:, 0:nope_w].astype(BF16), bd_ref[...], preferred_element_type=F32) * B_QSCALE
        cosr = per_pos(cos_ref, s)
        sinr = per_pos(sin_ref, s)
        qpe = (qb[:, nope_w:nope_w + rope_w] * cosr
               + qb[:, nope_w + rope_w:nope_w + 2 * rope_w] * sinr) * B_QSCALE
        for hd in range(B_HEADS):
            qmla_ref[bs, hd, rs, 0:B_KV_LORA] = out3(qlat[:, hd * B_KV_LORA:(hd + 1) * B_KV_LORA].astype(BF16))
            qmla_ref[bs, hd, rs, B_KV_LORA:B_QK] = out3(qpe[:, hd * B_ROPE:(hd + 1) * B_ROPE].astype(BF16))

        cr = u[:, _W1_CKV:_W1_CKV + B_KV_LORA]
        ckv = cr * lax.rsqrt(jnp.mean(cr * cr, axis=-1, keepdims=True) + EPS) * gkv_ref[...]
        ckv_ref[bs, rs] = out3(ckv)
        kmla_ref[bs, rs, 0:B_KV_LORA] = out3(ckv.astype(BF16))
        kb = u[:, _W1_KPE:_W1_KPE + LANES]
        rot = kb * cosr + pltpu.roll(kb, LANES - B_ROPE, axis=1) * sinr
        kpe_ref[bs, rs] = out3(rot[:, 0:B_ROPE])
        kmla_ref[bs, rs, B_KV_LORA:B_QK] = out3(rot[:, 0:B_ROPE].astype(BF16))
        lane1 = lax.broadcasted_iota(jnp.int32, (1, LANES), 1)
        dtr_ref[bs, rs] = out3(jnp.where(lane1 < C_HEADS, pltpu.roll(kb, LANES - 2 * B_ROPE, axis=1), 0.0))

        z_ref[bs, rs] = out3(u[:, _W1_Z:_W1_Z + C_D_INNER].astype(BF16))
        xbc_ref[bs, rs] = out3(u[:, _W1_XBC:_W1_XBC + C_CONV_DIM])

    us = [project(s) for s in range(nparts)]
    for s in range(nparts):
        finish(s, us[s])


def _inproj(x, mod, g_mix, w1, gq, wuq, bd, gkv, cos_t, sin_t, *, nb, r, nparts):
    B, S, D = x.shape
    nbt, nst = B // nb, S // r
    keep = min(S, A_WINDOW)
    assert keep == r and B % nb == 0 and S % r == 0
    grid = (nbt, nst)
    row = lambda c: pl.BlockSpec((nb, r, c), lambda b, i: (b, i, 0))
    full = lambda a: pl.BlockSpec(a.shape, lambda b, i: (0,) * a.ndim)
    in_specs = [row(D),
                pl.BlockSpec((nb, 1, D), lambda b, i: (b, 0, 0)),
                pl.BlockSpec((nb, 1, D), lambda b, i: (b, 0, 1)),
                full(g_mix), full(w1), full(gq), full(wuq), full(bd), full(gkv),
                pl.BlockSpec((r, LANES), lambda b, i: (i, 0)),
                pl.BlockSpec((r, LANES), lambda b, i: (i, 0))]
    out_shape = [jax.ShapeDtypeStruct((B, S, 3 * A_WIDTH), BF16),
                 jax.ShapeDtypeStruct((B, keep, A_WIDTH), F32),
                 jax.ShapeDtypeStruct((B, keep, A_WIDTH), F32),
                 jax.ShapeDtypeStruct((B, B_HEADS, S, B_QK), BF16),
                 jax.ShapeDtypeStruct((B, S, B_QK), BF16),
                 jax.ShapeDtypeStruct((B, S, B_KV_LORA), F32),
                 jax.ShapeDtypeStruct((B, S, B_ROPE), F32),
                 jax.ShapeDtypeStruct((B, S, C_D_INNER), BF16),
                 jax.ShapeDtypeStruct((B, S, C_CONV_DIM), F32),
                 jax.ShapeDtypeStruct((B, S, LANES), F32)]
    out_specs = [row(3 * A_WIDTH),
                 pl.BlockSpec((nb, keep, A_WIDTH), lambda b, i: (b, 0, 0)),
                 pl.BlockSpec((nb, keep, A_WIDTH), lambda b, i: (b, 0, 0)),
                 pl.BlockSpec((nb, B_HEADS, r, B_QK), lambda b, i: (b, 0, i, 0)),
                 row(B_QK), row(B_KV_LORA), row(B_ROPE), row(C_D_INNER), row(C_CONV_DIM), row(LANES)]
    return pl.pallas_call(
        functools.partial(_inproj_kernel, nb=nb, r=r, keep_tile=nst - 1, nparts=nparts),
        grid=grid, in_specs=in_specs, out_specs=out_specs, out_shape=out_shape,
        compiler_params=_cparams("arbitrary", "arbitrary"),
        name="inproj",
    )(x, mod, mod, g_mix, w1, gq, wuq, bd, gkv, cos_t, sin_t)


def _band_scores(q, k, hmask):
    qs = jnp.where(hmask, jnp.concatenate([q] * A_HEADS, axis=0), jnp.zeros((), BF16))
    return lax.dot_general(qs, k, (((1,), (1,)), ((), ())), preferred_element_type=F32)


def _band_finish(s, v, bias, valid, hmask):
    nq = s.shape[0] // A_HEADS
    s = s + bias
    if valid is not None:
        s = jnp.where(valid, s, NEG_INF)
    m = jnp.max(s, axis=-1, keepdims=True)
    e = jnp.exp2(s - m)
    den = jnp.sum(e, axis=-1, keepdims=True)
    o = jnp.dot(e.astype(BF16), v, preferred_element_type=F32)
    o = jnp.where(hmask, o / den, 0.0)
    y = o[0:nq]
    for hd in range(1, A_HEADS):
        y = y + o[hd * nq:(hd + 1) * nq]
    return y


def _head_mask(nq):
    row = lax.broadcasted_iota(jnp.int32, (A_HEADS * nq, A_WIDTH), 0)
    col = lax.broadcasted_iota(jnp.int32, (A_HEADS * nq, A_WIDTH), 1)
    return (row // nq) == (col // A_HEAD_DIM)


def _band_prompt_kernel(cur_ref, prev_ref, bias_ref, o_ref, kwin, vwin, *, tq):
    i = pl.program_id(1)
    nch = tq // CHUNK
    kwin[0:A_WINDOW] = prev_ref[0, tq - A_WINDOW:tq, A_WIDTH:2 * A_WIDTH]
    kwin[A_WINDOW:A_WINDOW + tq] = cur_ref[0, :, A_WIDTH:2 * A_WIDTH]
    vwin[0:A_WINDOW] = prev_ref[0, tq - A_WINDOW:tq, 2 * A_WIDTH:3 * A_WIDTH]
    vwin[A_WINDOW:A_WINDOW + tq] = cur_ref[0, :, 2 * A_WIDTH:3 * A_WIDTH]
    hmask = _head_mask(CHUNK)
    ucol = lax.broadcasted_iota(jnp.int32, (1, A_BAND), 1)

    def scores(c):
        return _band_scores(cur_ref[0, c * CHUNK:(c + 1) * CHUNK, 0:A_WIDTH],
                            kwin[c * CHUNK:c * CHUNK + A_BAND], hmask)

    bias2 = bias_ref[...] * LOG2E
    s_next = scores(0)
    for c in range(nch):
        s = s_next
        if c + 1 < nch:
            s_next = scores(c + 1)
        valid = (ucol // CHUNK + (i * nch + c - A_BACK)) >= 0
        y = _band_finish(s, vwin[c * CHUNK:c * CHUNK + A_BAND], bias2, valid, hmask)
        o_ref[0, c * CHUNK:(c + 1) * CHUNK, :] = y.astype(BF16)


def _band_prompt(qkv, bias_stack):
    B, S, _ = qkv.shape
    tq = A_WINDOW
    assert S % tq == 0
    return pl.pallas_call(
        functools.partial(_band_prompt_kernel, tq=tq),
        grid=(B, S // tq),
        in_specs=[pl.BlockSpec((1, tq, 3 * A_WIDTH), lambda b, i: (b, i, 0)),
                  pl.BlockSpec((1, tq, 3 * A_WIDTH), lambda b, i: (b, jnp.maximum(i - 1, 0), 0)),
                  pl.BlockSpec(bias_stack.shape, lambda b, i: (0, 0))],
        out_specs=pl.BlockSpec((1, tq, A_WIDTH), lambda b, i: (b, i, 0)),
        out_shape=jax.ShapeDtypeStruct((B, S, A_WIDTH), BF16),
        scratch_shapes=[pltpu.VMEM((A_WINDOW + tq, A_WIDTH), BF16), pltpu.VMEM((A_WINDOW + tq, A_WIDTH), BF16)],
        compiler_params=_cparams("arbitrary", "arbitrary"),
        name="band_prompt",
    )(qkv, qkv, bias_stack)


def _band_sample_kernel(qkv_ref, kt_ref, vt_ref, biasc_ref, biasn_ref, o_ref, *, nbb, ds):
    hmask = _head_mask(ds)
    nt_dims = (((1,), (1,)), ((), ()))
    for b in range(nbb):
        q = qkv_ref[b, :, 0:A_WIDTH]
        kn = qkv_ref[b, :, A_WIDTH:2 * A_WIDTH]
        vn = qkv_ref[b, :, 2 * A_WIDTH:3 * A_WIDTH]
        qs = jnp.where(hmask, jnp.concatenate([q] * A_HEADS, axis=0), jnp.zeros((), BF16))
        sc = jnp.dot(qs, kt_ref[0, b].astype(BF16), preferred_element_type=F32) + biasc_ref[...] * LOG2E
        sn = lax.dot_general(qs, kn, nt_dims, preferred_element_type=F32) + biasn_ref[...] * LOG2E
        m = jnp.maximum(jnp.max(sc, axis=-1, keepdims=True), jnp.max(sn, axis=-1, keepdims=True))
        ec = jnp.exp2(sc - m)
        en = jnp.exp2(sn - m)
        den = jnp.sum(ec, axis=-1, keepdims=True) + jnp.sum(en, axis=-1, keepdims=True)
        o = lax.dot_general(ec.astype(BF16), vt_ref[0, b].astype(BF16), nt_dims, preferred_element_type=F32)
        o = o + jnp.dot(en.astype(BF16), vn, preferred_element_type=F32)
        o = jnp.where(hmask, o / den, 0.0)
        y = o[0:ds]
        for hd in range(1, A_HEADS):
            y = y + o[hd * ds:(hd + 1) * ds]
        o_ref[b] = y.astype(BF16)


def _band_sample(qkv, cache_kt, cache_vt, layer, bias_stack):
    B, ds, _ = qkv.shape
    nr = cache_kt.shape[3]
    nbb = 4
    assert B % nbb == 0
    bias_c, bias_n = bias_stack[:, 0:nr], bias_stack[:, nr:]
    return pl.pallas_call(
        functools.partial(_band_sample_kernel, nbb=nbb, ds=ds),
        grid=(B // nbb,),
        in_specs=[pl.BlockSpec((nbb, ds, 3 * A_WIDTH), lambda b: (b, 0, 0)),
                  pl.BlockSpec((1, nbb, A_WIDTH, nr), lambda b: (layer, b, 0, 0)),
                  pl.BlockSpec((1, nbb, A_WIDTH, nr), lambda b: (layer, b, 0, 0)),
                  pl.BlockSpec(bias_c.shape, lambda b: (0, 0)),
                  pl.BlockSpec(bias_n.shape, lambda b: (0, 0))],
        out_specs=pl.BlockSpec((nbb, ds, A_WIDTH), lambda b: (b, 0, 0)),
        out_shape=jax.ShapeDtypeStruct((B, ds, A_WIDTH), BF16),
        compiler_params=_cparams("arbitrary"),
        name="band_sample",
    )(qkv, cache_kt, cache_vt, bias_c, bias_n)


def _mla_finish(acc, den, wuv_ref, bq):
    o = acc / den
    o_all = jnp.concatenate([o[hd * bq:(hd + 1) * bq] for hd in range(B_HEADS)], axis=1)
    return jnp.dot(o_all.astype(BF16), wuv_ref[...], preferred_element_type=F32)


def _mla_prompt_kernel(qi_ref, kj_ref, last_ref, q_ref, k_ref, wuv_ref, o_ref, m_sc, acc_sc, *, bq, bk, sk):
    p = pl.program_id(1)
    i = qi_ref[p]
    j = kj_ref[p]
    M = B_HEADS * bq

    @pl.when(j == 0)
    def _():
        m_sc[...] = jnp.full(m_sc.shape, NEG_INF, F32)
        acc_sc[...] = jnp.zeros(acc_sc.shape, F32)

    nt_dims = (((1,), (1,)), ((), ()))
    nsub = bk // sk

    def scores(q, t):
        return lax.dot_general(q, k_ref[0, t * sk:(t + 1) * sk, :], nt_dims, preferred_element_type=F32)

    def absorb(s, t, m_run, acc):
        m_new = jnp.maximum(m_run, jnp.max(s, axis=1, keepdims=True))
        alpha = jnp.exp2(m_run - m_new)
        pr = jnp.exp2(s - jnp.tile(m_new, (1, sk // LANES)))
        v_ext = jnp.concatenate([k_ref[0, t * sk:(t + 1) * sk, 0:B_KV_LORA], jnp.ones((sk, LANES), BF16)], axis=1)
        acc = acc * jnp.concatenate([alpha, alpha], axis=1) + jnp.dot(pr.astype(BF16), v_ext,
                                                                      preferred_element_type=F32)
        return m_new, acc

    @pl.when(last_ref[p] == 0)
    def _():
        q = q_ref[0].reshape(M, B_QK)
        m_run, acc = m_sc[...], acc_sc[...]
        s_next = scores(q, 0)
        for t in range(nsub):
            s = s_next
            if t + 1 < nsub:
                s_next = scores(q, t + 1)
            m_run, acc = absorb(s, t, m_run, acc)
        m_sc[...] = m_run
        acc_sc[...] = acc

    @pl.when(last_ref[p] == 1)
    def _():
        for t in range(nsub):
            @pl.when(j * bk + t * sk <= i * bq + (bq - 1))
            def _():
                s = scores(q_ref[0].reshape(M, B_QK), t)
                row = lax.broadcasted_iota(jnp.int32, (M, sk), 0)
                col = lax.broadcasted_iota(jnp.int32, (M, sk), 1)
                s = jnp.where((col + (j * bk + t * sk)) // CHUNK <= ((row % bq) + i * bq) // CHUNK, s, NEG_INF)
                m_sc[...], acc_sc[...] = absorb(s, t, m_sc[...], acc_sc[...])
        o_ref[0] = _mla_finish(acc_sc[:, 0:B_KV_LORA], acc_sc[:, B_KV_LORA:], wuv_ref, bq).astype(BF16)


def _mla_prompt(qmla, kmla, wuv_bd, *, bq, bk):
    B, S = kmla.shape[0:2]
    bk = min(bk, S)
    assert S % bq == 0 and S % bk == 0 and bq % CHUNK == 0 and bk % bq == 0
    nq = S // bq
    n_kv = [(i * bq + bq - 1) // bk + 1 for i in range(nq)]
    qi = np.concatenate([np.full(n, i, np.int32) for i, n in enumerate(n_kv)])
    kj = np.concatenate([np.arange(n, dtype=np.int32) for n in n_kv])
    last = np.concatenate([(np.arange(n) == n - 1).astype(np.int32) for n in n_kv])
    grid_spec = pltpu.PrefetchScalarGridSpec(
        num_scalar_prefetch=3,
        grid=(B, len(qi)),
        in_specs=[pl.BlockSpec((1, B_HEADS, bq, B_QK), lambda b, p, qi, kj, last: (b, 0, qi[p], 0)),
                  pl.BlockSpec((1, bk, B_QK), lambda b, p, qi, kj, last: (b, kj[p], 0)),
                  pl.BlockSpec(wuv_bd.shape, lambda b, p, qi, kj, last: (0, 0))],
        out_specs=pl.BlockSpec((1, bq, B_WIDTH), lambda b, p, qi, kj, last: (b, qi[p], 0)),
        scratch_shapes=[pltpu.VMEM((B_HEADS * bq, LANES), F32),
                        pltpu.VMEM((B_HEADS * bq, B_KV_LORA + LANES), F32)])
    return pl.pallas_call(
        functools.partial(_mla_prompt_kernel, bq=bq, bk=bk, sk=min(bk, 512)),
        grid_spec=grid_spec,
        out_shape=jax.ShapeDtypeStruct((B, S, B_WIDTH), BF16),
        compiler_params=_cparams("arbitrary", "arbitrary"),
        name="mla_prompt",
    )(jnp.asarray(qi), jnp.asarray(kj), jnp.asarray(last), qmla, kmla, wuv_bd)


def _mla_sample_kernel(q_ref, kn_ref, ckv_ref, kpet_ref, wuv_ref, o_ref, *, ds):
    M = B_HEADS * ds
    nt_dims = (((1,), (1,)), ((), ()))
    q = q_ref[0].reshape(M, B_QK)
    kn = kn_ref[0]
    ckv = ckv_ref[0, 0].astype(BF16)
    sc = (lax.dot_general(q[:, 0:B_KV_LORA], ckv, nt_dims, preferred_element_type=F32)
          + jnp.dot(q[:, B_KV_LORA:B_QK], kpet_ref[0, 0].astype(BF16), preferred_element_type=F32))
    sn = lax.dot_general(q, kn, nt_dims, preferred_element_type=F32)
    m = jnp.maximum(jnp.max(sc, axis=1, keepdims=True), jnp.max(sn, axis=1, keepdims=True))
    pc = jnp.exp2(sc - m)
    pn = jnp.exp2(sn - m)
    den = jnp.sum(pc, axis=1, keepdims=True) + jnp.sum(pn, axis=1, keepdims=True)
    acc = (jnp.dot(pc.astype(BF16), ckv, preferred_element_type=F32)
           + jnp.dot(pn.astype(BF16), kn[:, 0:B_KV_LORA], preferred_element_type=F32))
    o_ref[0] = _mla_finish(acc, den, wuv_ref, ds).astype(BF16)


def _mla_sample(qmla, kmla, cache_ckv, cache_kpet, layer, wuv_bd):
    B, ds = kmla.shape[0:2]
    past = cache_ckv.shape[2]
    return pl.pallas_call(
        functools.partial(_mla_sample_kernel, ds=ds),
        grid=(B,),
        in_specs=[pl.BlockSpec((1, B_HEADS, ds, B_QK), lambda b: (b, 0, 0, 0)),
                  pl.BlockSpec((1, ds, B_QK), lambda b: (b, 0, 0)),
                  pl.BlockSpec((1, 1, past, B_KV_LORA), lambda b: (layer, b, 0, 0)),
                  pl.BlockSpec((1, 1, B_ROPE, past), lambda b: (layer, b, 0, 0)),
                  pl.BlockSpec(wuv_bd.shape, lambda b: (0, 0))],
        out_specs=pl.BlockSpec((1, ds, B_WIDTH), lambda b: (b, 0, 0)),
        out_shape=jax.ShapeDtypeStruct((B, ds, B_WIDTH), BF16),
        compiler_params=_cparams("arbitrary"),
        name="mla_sample",
    )(qmla, kmla, cache_ckv, cache_kpet, wuv_bd)


def _ssd_kernel(xbc_ref, z_ref, dtr_ref, pre_ref, h0_ref, cw_ref, cb_ref, dtb_ref, alog_ref, dsk_ref, gn_ref,
                es_ref, ep_ref, tri_ref, y_ref, hout_ref,
                cbuf, xs_sc, b_sc, c_sc, ces_sc, cep_sc, dep_sc, st_sc, *, nbat, lt, lc, nt):
    t = pl.program_id(1)
    nch = lt // lc
    HS = C_HEADS * lc
    gw = C_HPG * C_HEAD_DIM
    same_exp = lc == C_HEAD_DIM

    @pl.when(t == 0)
    def _():
        st_sc[...] = jnp.zeros(st_sc.shape, F32)
        for b in range(nbat):
            cbuf[b, 8 - (C_CONV_W - 1):8] = pre_ref[b]
            for g in range(C_GROUPS):
                blk = jnp.concatenate([h0_ref[b, g * C_HPG + hl] for hl in range(C_HPG)], axis=1)
                st_sc[b, g * C_D_STATE:(g + 1) * C_D_STATE, g * gw:(g + 1) * gw] = blk

    hl = lax.broadcasted_iota(jnp.int32, (lt, LANES), 1)
    for b in range(nbat):
        cbuf[b, 8:8 + lt] = xbc_ref[b]
        acc = jnp.broadcast_to(cb_ref[...], (lt, C_CONV_DIM))
        for kk in range(C_CONV_W):
            off = 8 - (C_CONV_W - 1) + kk
            acc = acc + cbuf[b, off:off + lt] * cw_ref[kk:kk + 1, :]
        tail = cbuf[b, 8 + lt - (C_CONV_W - 1):8 + lt]
        cbuf[b, 8 - (C_CONV_W - 1):8] = tail
        xc = _silu(acc)
        xs_sc[b] = xc[:, 0:C_D_INNER]
        b_sc[b] = xc[:, C_D_INNER:C_D_INNER + C_GN]
        c_sc[b] = xc[:, C_D_INNER + C_GN:C_D_INNER + 2 * C_GN]

        dt = jnp.where(hl < C_HEADS, jax.nn.softplus(dtr_ref[b] + dtb_ref[...]), 0.0)
        da = dt * (-jnp.exp(alog_ref[...]))
        c3 = jnp.dot(tri_ref[...], _split3_lanes(da, hl), preferred_element_type=F32)
        cum = jnp.where(hl < C_HEADS, c3 + pltpu.roll(c3, LANES - C_HEADS, axis=1)
                        + pltpu.roll(c3, LANES - 2 * C_HEADS, axis=1), 0.0)
        cum3 = _split3_lanes(cum, hl)
        cep = jnp.dot(cum3, ep_ref[...], preferred_element_type=F32)
        cep_sc[b] = cep
        if not same_exp:
            ces_sc[b] = jnp.dot(cum3, es_ref[...], preferred_element_type=F32)
        dep_sc[b] = jnp.dot(_split3_lanes(dt, hl), ep_ref[...], preferred_element_type=F32)

    r_s = lax.broadcasted_iota(jnp.int32, (lc, HS), 0)
    l_s = lax.broadcasted_iota(jnp.int32, (lc, HS), 1)
    eye_t = (l_s % lc) == r_s
    causal_t = (l_s % lc) <= r_s
    r_w = lax.broadcasted_iota(jnp.int32, (HS, C_GN), 0)
    l_w = lax.broadcasted_iota(jnp.int32, (HS, C_GN), 1)
    gmask = (r_w // (C_HPG * lc)) == (l_w // C_D_STATE)
    r_b =lax.broadcasted_iota(jnp.int32, (C_HPG * lc, gw), 0)
    l_b = lax.broadcasted_iota(jnp.int32, (C_HPG * lc, gw), 1)
    bmask = (r_b // lc) == (l_b // C_HEAD_DIM)
    r_g = lax.broadcasted_iota(jnp.int32, (C_GN, C_D_INNER), 0)
    l_g = lax.broadcasted_iota(jnp.int32, (C_GN, C_D_INNER), 1)
    smask = (r_g // C_D_STATE) == (l_g // gw)

    def stage_cb(b, rows):
        cm = c_sc[b, rows, :].astype(BF16)
        bm = b_sc[b, rows, :]
        w_nt = jnp.where(gmask, jnp.concatenate([bm] * C_HEADS, axis=0), 0.0).astype(BF16)
        cbcat = lax.dot_general(cm, w_nt, (((1,), (1,)), ((), ())), preferred_element_type=F32)
        st = st_sc[b]
        yoff = jnp.dot(cm, st.astype(BF16), preferred_element_type=F32)
        return dict(bm=bm, cbcat=cbcat, st=st, yoff=yoff)

    def stage_diag(b, rows, v):
        ce_p = cep_sc[b, rows, :]
        ce_s = ce_p if same_exp else ces_sc[b, rows, :]
        xs = xs_sc[b, rows, :]
        rflat = jnp.sum(jnp.where(eye_t, ce_s, 0.0), axis=0, keepdims=True)
        lcat = jnp.exp(jnp.where(causal_t, ce_s - rflat, NEG_INF))
        mcat = (v['cbcat'] * lcat).astype(BF16)
        xdt = xs * dep_sc[b, rows, :]
        xdt_b = xdt.astype(BF16)
        ydiag = []
        for g in range(C_GROUPS):
            bdx = jnp.where(bmask, jnp.concatenate([xdt_b[:, g * gw:(g + 1) * gw]] * C_HPG, axis=0),
                            jnp.zeros((), BF16))
            ydiag.append(jnp.dot(mcat[:, g * C_HPG * lc:(g + 1) * C_HPG * lc], bdx, preferred_element_type=F32))
        v.update(ce_p=ce_p, xs=xs, xdt=xdt, ydiag=jnp.concatenate(ydiag, axis=1))

    def stage_state(b, rows, v):
        ce_p = v['ce_p']
        last = ce_p[lc - 1:lc, :]
        xw = (v['xdt'] * jnp.exp(last - ce_p)).astype(BF16)
        upd = lax.dot_general(v['bm'].astype(BF16), xw, (((0,), (0,)), ((), ())), preferred_element_type=F32)
        st_sc[b] = jnp.exp(last) * v['st'] + jnp.where(smask, upd, 0.0)

    def stage_out(b, rows, v):
        y = v['ydiag'] + v['yoff'] * jnp.exp(v['ce_p']) + dsk_ref[...] * v['xs']
        y = y * _silu(z_ref[b, rows, :].astype(F32))
        y = y * lax.rsqrt(jnp.mean(y * y, axis=-1, keepdims=True) + EPS) * gn_ref[...]
        y_ref[b, rows, :] = y.astype(BF16)

    def chunk(c, carry):
        rows = pl.ds(pl.multiple_of(c * lc, lc), lc)
        vals = [stage_cb(b, rows) for b in range(nbat)]
        for stage in (stage_diag, stage_state, stage_out):
            for b in range(nbat):
                stage(b, rows, vals[b])
        return carry

    lax.fori_loop(0, nch, chunk, 0)

    @pl.when(t == nt - 1)
    def _():
        for b in range(nbat):
            for hd in range(C_HEADS):
                g = hd // C_HPG
                hout_ref[b, hd] = st_sc[b, g * C_D_STATE:(g + 1) * C_D_STATE,
                                        hd * C_HEAD_DIM:(hd + 1) * C_HEAD_DIM]


def _ssd(xbc, z, dtr, prefix, h0t, cw, cb, dtb, alog, dsk, gn, *, nbat, lt, lc):
    B, S, _ = xbc.shape
    nt = S // lt
    assert S % lt == 0 and lt % lc == 0 and B % nbat == 0
    hs = C_HEADS * lc

    def expand(width, per_head):
        e = np.zeros((LANES, width), np.float32)
        for part in range(3):
            e[part * C_HEADS:(part + 1) * C_HEADS] = (np.arange(width)[None, :] // per_head
                                                      == np.arange(C_HEADS)[:, None])
        return jnp.asarray(e, dtype=BF16)

    es, ep = expand(hs, lc), expand(C_D_INNER, C_HEAD_DIM)
    pos = np.arange(lt)
    tri = jnp.asarray((pos[None, :] <= pos[:, None]) & (pos[None, :] // lc == pos[:, None] // lc), dtype=BF16)
    row = lambda c: pl.BlockSpec((nbat, lt, c), lambda b, t: (b, t, 0))
    per_b = lambda a: pl.BlockSpec((nbat,) + a.shape[1:], lambda b, t: (b,) + (0,) * (a.ndim - 1))
    full = lambda a: pl.BlockSpec(a.shape, lambda b, t: (0,) * a.ndim)
    sc = lambda *shape: pltpu.VMEM((nbat,) + shape, F32)
    return pl.pallas_call(
        functools.partial(_ssd_kernel, nbat=nbat, lt=lt, lc=lc, nt=nt),
        grid=(B // nbat, nt),
        in_specs=[row(C_CONV_DIM), row(C_D_INNER), row(LANES), per_b(prefix), per_b(h0t),
                  full(cw), full(cb), full(dtb), full(alog), full(dsk), full(gn), full(es), full(ep), full(tri)],
        out_specs=[row(C_D_INNER), per_b(h0t)],
        out_shape=[jax.ShapeDtypeStruct((B, S, C_D_INNER), BF16),
                   jax.ShapeDtypeStruct(h0t.shape, F32)],
        scratch_shapes=[sc(lt + 8, C_CONV_DIM), sc(lt, C_D_INNER), sc(lt, C_GN), sc(lt, C_GN),
                        sc(lt, hs) if lc != C_HEAD_DIM else sc(8, LANES),
                        sc(lt, C_D_INNER), sc(lt, C_D_INNER), sc(C_GN, C_D_INNER)],
        compiler_params=_cparams("arbitrary", "arbitrary"),
        name="ssd",
    )(xbc, z, dtr, prefix, h0t, cw, cb, dtb, alog, dsk, gn, es, ep, tri)


def _outproj_kernel(ya_ref, yb_ref, yc_ref, x_ref, g1_ref, sh_ref, sc_ref, gf_ref, wo_ref, wr_ref, br_ref,
                    x1_ref, h2_ref, gate_ref, *, nb, r, nparts):
    D = x_ref.shape[-1]
    pnb, pr = (nb // nparts, r) if nb > 1 else (1, r // nparts)
    PR = pnb * pr

    def sel_part(s):
        return (slice(s * pnb, (s + 1) * pnb), slice(None)) if nb > 1 else (slice(None), slice(s * pr, (s + 1) * pr))

    def in2(ref, s):
        bs, rs = sel_part(s)
        return ref[bs, rs].reshape(PR, ref.shape[-1])

    def per_batch(ref, s):
        v = ref[sel_part(s)[0]] if nb > 1 else ref[...]
        return jnp.broadcast_to(v, (pnb, pr, v.shape[-1])).reshape(PR, v.shape[-1])

    def project(s):
        o = jnp.dot(in2(ya_ref, s), wo_ref[0:A_WIDTH, :], preferred_element_type=F32)
        o = o + jnp.dot(in2(yb_ref, s), wo_ref[A_WIDTH:A_WIDTH + B_WIDTH, :], preferred_element_type=F32)
        return o + jnp.dot(in2(yc_ref, s), wo_ref[A_WIDTH + B_WIDTH:, :], preferred_element_type=F32)

    wr = wr_ref[...]
    w_hi = wr.astype(BF16)
    w_lo = (wr - w_hi.astype(F32)).astype(BF16)

    def logits(s, o):
        bs, rs = sel_part(s)
        x1 = in2(x_ref, s) + per_batch(g1_ref, s) * o
        x1_ref[bs, rs] = x1.reshape(pnb, pr, D)
        h2 = x1 * lax.rsqrt(jnp.mean(x1 * x1, axis=-1, keepdims=True) + EPS) * gf_ref[...]
        h2 = h2 * (1.0 + per_batch(sc_ref, s)) + per_batch(sh_ref, s)
        h_hi = h2.astype(BF16)
        h2_ref[bs, rs] = h_hi.reshape(pnb, pr, D)
        h_lo = (h2 - h_hi.astype(F32)).astype(BF16)
        return (jnp.dot(h_hi, w_hi, preferred_element_type=F32) + jnp.dot(h_hi, w_lo, preferred_element_type=F32)
                + jnp.dot(h_lo, w_hi, preferred_element_type=F32) + br_ref[...])

    def route(s, lg):
        bs, rs = sel_part(s)
        lane = lax.broadcasted_iota(jnp.int32, (PR, ROUTER_LANES), 1)
        big = jnp.int32(ROUTER_LANES)
        is_g = lane < E_GROUPS
        gl = jnp.where(is_g, lg, NEG_INF)
        gmax = jnp.max(gl, axis=-1, keepdims=True)
        p_top = 1.0 / jnp.sum(jnp.where(is_g, jnp.exp(gl - gmax), 0.0), axis=-1, keepdims=True)
        g_idx = jnp.min(jnp.where(is_g & (gl == gmax), lane, big), axis=-1, keepdims=True)
        e_lane = lane - ROUTER_E_OFF
        sel = (e_lane >= 0) & (e_lane < N_EXPERTS) & ((e_lane // E_PER_GROUP) == g_idx)
        l1 = jnp.where(sel, lg, NEG_INF)
        m1 = jnp.max(l1, axis=-1, keepdims=True)
        i1 = jnp.min(jnp.where(sel & (l1 == m1), lane, big), axis=-1, keepdims=True)
        sel2 = sel & (lane != i1)
        l2 = jnp.where(sel2, lg, NEG_INF)
        m2 = jnp.max(l2, axis=-1, keepdims=True)
        i2 = jnp.min(jnp.where(sel2 & (l2 == m2), lane, big), axis=-1, keepdims=True)
        e2 = jnp.exp(m2 - m1)
        w1 = p_top / (1.0 + e2)
        w2 = p_top * e2 / (1.0 + e2)
        gate = jnp.where(lane == i1, w1, 0.0) + jnp.where(lane == i2, w2, 0.0)
        gate = gate + jnp.where(lane == 0, g_idx.astype(F32), 0.0)
        gate_ref[bs, rs] = gate.reshape(pnb, pr, ROUTER_LANES)

    outs = [project(s) for s in range(nparts)]
    lgs = [logits(s, outs[s]) for s in range(nparts)]
    for s in range(nparts):
        route(s, lgs[s])


def _outproj(ya, yb, yc, x, mod, g_ffn, wo, wr, br, *, nb, r, nparts):
    B, S, D = x.shape
    row = lambda c: pl.BlockSpec((nb, r, c), lambda b, i: (b, i, 0))
    modc = lambda j: pl.BlockSpec((nb, 1, D), lambda b, i: (b, 0, j))
    full = lambda a: pl.BlockSpec(a.shape, lambda b, i: (0,) * a.ndim)
    return pl.pallas_call(
        functools.partial(_outproj_kernel, nb=nb, r=r, nparts=nparts),
        grid=(B // nb, S // r),
        in_specs=[row(A_WIDTH), row(B_WIDTH), row(C_D_INNER), row(D), modc(2), modc(3), modc(4),
                  full(g_ffn), full(wo), full(wr), full(br)],
        out_specs=[row(D), row(D), row(ROUTER_LANES)],
        out_shape=[jax.ShapeDtypeStruct((B, S, D), F32), jax.ShapeDtypeStruct((B, S, D), BF16),
                   jax.ShapeDtypeStruct((B, S, ROUTER_LANES), F32)],
        compiler_params=_cparams("arbitrary", "arbitrary"),
        name="outproj",
    )(ya, yb, yc, x, mod, mod, mod, g_ffn, wo, wr, br)


def _moe_kernel(h_ref, gate_ref, x_ref, g2_ref, wgu_ref, wd_ref, *rest, nb, r, nsteps, epb, final):
    if final:
        shf_ref, scf_ref, gfin_ref, o_ref, acc_sc = rest
    else:
        o_ref, acc_sc = rest
    e = pl.program_id(2)
    R = nb * r
    D = x_ref.shape[-1]

    @pl.when(e == 0)
    def _():
        acc_sc[...] = jnp.zeros(acc_sc.shape, F32)

    def as_bf16(w):
        return w if w.dtype == BF16 else w.astype(BF16)

    h = h_ref[...].reshape(R, D)
    gate = gate_ref[...].reshape(R, ROUTER_LANES)
    lane = lax.broadcasted_iota(jnp.int32, (R, ROUTER_LANES), 1)
    acts = []
    for k in range(epb):
        gu = jnp.dot(h, as_bf16(wgu_ref[0, k]), preferred_element_type=F32)
        gcol = jnp.sum(jnp.where(lane == e * epb + k + ROUTER_E_OFF, gate, 0.0), axis=-1, keepdims=True)
        acts.append((_silu(gu[:, 0:E_HIDDEN]) * gu[:, E_HIDDEN:] * gcol).astype(BF16))
    wd = as_bf16(wd_ref[0]).reshape(epb * E_HIDDEN, D)
    acc_sc[...] += jnp.dot(jnp.concatenate(acts, axis=1), wd, preferred_element_type=F32)

    @pl.when(e == nsteps - 1)
    def _():
        xo = x_ref[...].reshape(R, D) + _rows(g2_ref, nb, r) * acc_sc[...]
        if final:
            xo = xo * lax.rsqrt(jnp.mean(xo * xo, axis=-1, keepdims=True) + EPS) * gfin_ref[...]
            xo = xo * (1.0 + _rows(scf_ref, nb, r)) + _rows(shf_ref, nb, r)
        o_ref[...] = xo.reshape(nb, r, D)


def _moe(h2, gate, x1, mod, wgu, wd, layer, final, *, nb, r, epb):
    B, S, D = x1.shape
    ne = wgu.shape[1]
    assert ne % epb == 0
    nsteps = ne // epb
    row = lambda c: pl.BlockSpec((nb, r, c), lambda b, i, e: (b, i, 0))
    modc = lambda j: pl.BlockSpec((nb, 1, D), lambda b, i, e: (b, 0, j))
    in_specs = [row(D), row(ROUTER_LANES), row(D), modc(5),
                pl.BlockSpec((1, epb, D, 2 * E_HIDDEN), lambda b, i, e: (layer, e, 0, 0)),
                pl.BlockSpec((1, epb, E_HIDDEN, D), lambda b, i, e: (layer, e, 0, 0))]
    args = [h2, gate, x1, mod, wgu, wd]
    if final is not None:
        modf, gfin = final
        in_specs += [modc(0), modc(1), pl.BlockSpec(gfin.shape, lambda b, i, e: (0, 0))]
        args += [modf, modf, gfin]
    return pl.pallas_call(
        functools.partial(_moe_kernel, nb=nb, r=r, nsteps=nsteps, epb=epb, final=final is not None),
        grid=(B // nb, S // r, nsteps),
        in_specs=in_specs,
        out_specs=row(D),
        out_shape=jax.ShapeDtypeStruct((B, S, D), F32),
        scratch_shapes=[pltpu.VMEM((nb * r, D), F32)],
        compiler_params=_cparams("arbitrary", "arbitrary", "arbitrary"),
        name="moe",
    )(*args)


MOE_BLK = 160


def _moe_sorted_kernel(h_ref, gate_ref, x_ref, g2_ref, wgu_ref, wd_ref, tri_ref, *rest, nb, r, final):
    if final:
        shf_ref, scf_ref, gfin_ref, o_ref, xc_sc, yc_sc, p_sc, meta = rest
    else:
        o_ref, xc_sc, yc_sc, p_sc, meta = rest
    g = pl.program_id(2)
    R = nb * r
    D = x_ref.shape[-1]
    RC = xc_sc.shape[0]

    @pl.when(g == 0)
    def _():
        gate = gate_ref[...].reshape(R, ROUTER_LANES)
        lane = lax.broadcasted_iota(jnp.int32, (R, ROUTER_LANES), 1)
        onehot = jnp.where((lane < E_GROUPS) & (gate[:, 0:1] == lane.astype(F32)), 1.0, 0.0)
        pos = jnp.dot(tri_ref[...], onehot.astype(BF16), preferred_element_type=F32)
        rank = jnp.sum(onehot * pos, axis=1, keepdims=True)
        counts = jnp.sum(onehot, axis=0, keepdims=True)
        lane1 = lax.broadcasted_iota(jnp.int32, (1, ROUTER_LANES), 1)
        base = jnp.int32(0)
        baserow = jnp.zeros((1, ROUTER_LANES), F32)
        for gg in range(E_GROUPS):
            cnt = jnp.sum(jnp.where(lane1 == gg, counts, 0.0)).astype(jnp.int32)
            nblk = (cnt + (MOE_BLK - 1)) // MOE_BLK
            meta[gg] = base
            meta[E_GROUPS + gg] = nblk
            baserow = jnp.where(lane1 == gg, base.astype(F32), baserow)
            base = base + nblk * MOE_BLK
        cpos = jnp.sum(onehot * baserow, axis=1, keepdims=True) + rank
        colc = lax.broadcasted_iota(jnp.int32, (1, RC), 1).astype(F32)
        pm = jnp.where(cpos == colc, 1.0, 0.0).astype(BF16)
        p_sc[...] = pm
        g_hi = gate.astype(BF16)
        g_lo = (gate - g_hi.astype(F32)).astype(BF16)
        xext = jnp.concatenate([h_ref[...].reshape(R, D), g_hi, g_lo], axis=1)
        xc_sc[...] = lax.dot_general(pm, xext, (((0,), (0,)), ((), ())),
                                     preferred_element_type=F32).astype(BF16)
        yc_sc[...] = jnp.zeros(yc_sc.shape, BF16)

    base_g = meta[g]
    nblk_g = meta[E_GROUPS + g]
    lane_b = lax.broadcasted_iota(jnp.int32, (MOE_BLK, ROUTER_LANES), 1)

    def block(bi, carry):
        rows = pl.ds(pl.multiple_of(base_g + bi * MOE_BLK, MOE_BLK), MOE_BLK)
        xg = xc_sc[rows, 0:D]
        gg = (xc_sc[rows, D:D + ROUTER_LANES].astype(F32)
              + xc_sc[rows, D + ROUTER_LANES:D + 2 * ROUTER_LANES].astype(F32))
        acts = []
        for k in range(E_PER_GROUP):
            gu = jnp.dot(xg, wgu_ref[0, k], preferred_element_type=F32)
            gcol = jnp.sum(jnp.where(lane_b == ROUTER_E_OFF + g * E_PER_GROUP + k, gg, 0.0),
                           axis=-1, keepdims=True)
            acts.append((_silu(gu[:, 0:E_HIDDEN]) * gu[:, E_HIDDEN:] * gcol).astype(BF16))
        wd = wd_ref[0].reshape(E_PER_GROUP * E_HIDDEN, D)
        yc_sc[rows, :] = jnp.dot(jnp.concatenate(acts, axis=1), wd, preferred_element_type=F32).astype(BF16)
        return carry

    lax.fori_loop(0, nblk_g, block, 0)

    @pl.when(g == E_GROUPS - 1)
    def _():
        moe = jnp.dot(p_sc[...], yc_sc[...], preferred_element_type=F32)
        xo = x_ref[...].reshape(R, D) + _rows(g2_ref, nb, r) * moe
        if final:
            xo = xo * lax.rsqrt(jnp.mean(xo * xo, axis=-1, keepdims=True) + EPS) * gfin_ref[...]
            xo = xo * (1.0 + _rows(scf_ref, nb, r)) + _rows(shf_ref, nb, r)
        o_ref[...] = xo.reshape(nb, r, D)


def _moe_sorted(h2, gate, x1, mod, wgu, wd, layer, final, *, nb, r):
    B, S, D = x1.shape
    R = nb * r
    rc = R + E_GROUPS * MOE_BLK
    pos = np.arange(R)
    tri = jnp.asarray(pos[None, :] < pos[:, None], dtype=BF16)
    row = lambda c: pl.BlockSpec((nb, r, c), lambda b, i, g: (b, i, 0))
    modc = lambda j: pl.BlockSpec((nb, 1, D), lambda b, i, g: (b, 0, j))
    in_specs = [row(D), row(ROUTER_LANES), row(D), modc(5),
                pl.BlockSpec((1, E_PER_GROUP, D, 2 * E_HIDDEN), lambda b, i, g: (layer, g, 0, 0)),
                pl.BlockSpec((1, E_PER_GROUP, E_HIDDEN, D), lambda b, i, g: (layer, g, 0, 0)),
                pl.BlockSpec(tri.shape, lambda b, i, g: (0, 0))]
    args = [h2, gate, x1, mod, wgu, wd, tri]
    if final is not None:
        modf, gfin = final
        in_specs += [modc(0), modc(1), pl.BlockSpec(gfin.shape, lambda b, i, g: (0, 0))]
        args += [modf, modf, gfin]
    return pl.pallas_call(
        functools.partial(_moe_sorted_kernel, nb=nb, r=r, final=final is not None),
        grid=(B // nb, S // r, E_GROUPS),
        in_specs=in_specs,
        out_specs=row(D),
        out_shape=jax.ShapeDtypeStruct((B, S, D), F32),
        scratch_shapes=[pltpu.VMEM((rc, D + 2 * ROUTER_LANES), BF16), pltpu.VMEM((rc, D), BF16),
                        pltpu.VMEM((R, rc), BF16), pltpu.SMEM((2 * E_GROUPS,), jnp.int32)],
        compiler_params=_cparams("arbitrary", "arbitrary", "arbitrary"),
        name="moe_sorted",
    )(*args)


def _rope_tables(pos):
    half = B_ROPE // 2
    inv = 1.0 / (ROPE_THETA ** (jnp.arange(half, dtype=F32) / half))
    ang = pos.astype(F32)[:, None] * inv[None, :]
    cos, sin = jnp.cos(ang), jnp.sin(ang)
    return (jnp.tile(jnp.concatenate([cos, cos], axis=1), (1, B_HEADS)),
            jnp.tile(jnp.concatenate([-sin, sin], axis=1), (1, B_HEADS)))


def _swap_halves(w):
    half = w.shape[-1] // 2
    return jnp.concatenate([w[..., half:], w[..., :half]], axis=-1)


def _layer_weights(l, w_in, b_g_q, b_w_uq, b_g_kv, b_w_uk, b_w_uv, w_out, moe_w_rg, moe_b_rg, moe_w_re, moe_b_re):
    D = w_in.shape[1]
    w = w_in[l]
    o_cq = 3 * A_WIDTH
    o_ckv = o_cq + B_Q_LORA
    o_kpe = o_ckv + B_KV_LORA
    o_z = o_kpe + B_ROPE
    o_xbc = o_z + C_D_INNER
    o_dt = o_xbc + C_CONV_DIM
    wkpe = w[:, o_kpe:o_z]
    zc = lambda n: jnp.zeros((D, n), F32)
    w1 = jnp.concatenate([w[:, 0:o_cq], w[:, o_cq:o_ckv], zc(256 - B_Q_LORA), w[:, o_ckv:o_kpe],
                          wkpe, _swap_halves(wkpe), w[:, o_dt:o_dt + C_HEADS],
                          zc(LANES - 2 * B_ROPE - C_HEADS), w[:, o_z:o_xbc], w[:, o_xbc:o_dt]],
                         axis=1).astype(BF16)
    assert w1.shape[1] == _W1_N
    gq = jnp.pad(b_g_q[l], (0, 256 - B_Q_LORA))[None, :]
    uq = b_w_uq[l]
    pe = uq[:, :, B_NOPE:]
    wuq = jnp.concatenate([uq[:, :, :B_NOPE].reshape(B_Q_LORA, -1), pe.reshape(B_Q_LORA, -1),
                           _swap_halves(pe).reshape(B_Q_LORA, -1)], axis=1)
    wuq = jnp.pad(wuq, ((0, 256 - B_Q_LORA), (0, 0))).astype(BF16)
    eye = jnp.eye(B_HEADS, dtype=F32)
    bd = (jnp.transpose(b_w_uk[l], (1, 2, 0))[:, :, None, :] * eye[:, None, :, None]).reshape(
        B_HEADS * B_NOPE, B_HEADS * B_KV_LORA).astype(BF16)
    wuv = (jnp.transpose(b_w_uv[l], (1, 0, 2))[:, :, None, :] * eye[:, None, :, None]).reshape(
        B_HEADS * B_KV_LORA, B_WIDTH).astype(BF16)
    wr = jnp.zeros((D, ROUTER_LANES), F32)
    wr = wr.at[:, 0:E_GROUPS].set(moe_w_rg[l]).at[:, ROUTER_E_OFF:ROUTER_E_OFF + N_EXPERTS].set(moe_w_re[l])
    br = jnp.zeros((1, ROUTER_LANES), F32)
    br = br.at[0, 0:E_GROUPS].set(moe_b_rg[l]).at[0, ROUTER_E_OFF:ROUTER_E_OFF + N_EXPERTS].set(moe_b_re[l])
    return dict(w1=w1, gq=gq, wuq=wuq, bd=bd, gkv=b_g_kv[l][None, :], wuv=wuv, wo=w_out[l].astype(BF16), wr=wr, br=br)


def _band_bias(rel_bias, nq, nk, back):
    n = nk + nq - 1
    idx = np.clip(back + nq - 1 - np.arange(n + 1), -A_REL_CLIP, A_REL_CLIP) + A_REL_CLIP
    line = rel_bias[:, idx]
    skew = jnp.tile(line, (1, nq))[:, 0:nq * n].reshape(A_HEADS, nq, n)
    return skew[:, :, nq - 1:nq - 1 + nk].reshape(A_HEADS * nq, nk)


def _layer(x, mod, pos, lw, l, cache, prm, final, *, nb, r):
    B, S, D = x.shape
    cos_t, sin_t = _rope_tables(pos)
    (qkv, knew, vnew, qmla, kmla, ckv, kpe, z, xbc, dtr) = _inproj(
        x, mod, prm['g_mix'], lw['w1'], lw['gq'], lw['wuq'], lw['bd'], lw['gkv'], cos_t, sin_t, nb=nb, r=r,
        nparts=prm['in_parts'])
    if cache is None:
        ya = _band_prompt(qkv, _band_bias(prm['a_rel_bias'], CHUNK, A_BAND, A_WINDOW))
        yb = _mla_prompt(qmla, kmla, lw['wuv'], bq=min(prm['mla_bq'], S), bk=1024)
        prefix = jnp.zeros((B, C_CONV_W - 1, C_CONV_DIM), F32)
        h0t = jnp.zeros((B, C_HEADS, C_D_STATE, C_HEAD_DIM), F32)
        lt, lc, nbat = min(S, 512), SSD_CHUNK, B
    else:
        nr = cache['a_kt'].shape[3]
        ya = _band_sample(qkv, cache['a_kt'], cache['a_vt'], l, _band_bias(prm['a_rel_bias'], S, nr + S, nr))
        yb = _mla_sample(qmla, kmla, cache['b_ckv'], cache['b_kpet'], l, lw['wuv'])
        prefix = cache['c_conv'][l]
        h0t = jnp.swapaxes(cache['c_ssm'][l], -1, -2)
        lt = lc = min(SSD_CHUNK, S)
        nbat = math.gcd(B, 4)
    yc, hft = _ssd(xbc, z, dtr, prefix, h0t, prm['c_conv_w'], prm['c_conv_b'], prm['c_dt_bias'], prm['c_a_log'],
                   prm['c_d_exp'], prm['c_g_norm'], nbat=nbat, lt=lt, lc=lc)
    conv_state = jnp.concatenate([prefix, xbc], axis=1)[:, -(C_CONV_W - 1):]
    x1, h2, gate = _outproj(ya, yb, yc, x, mod, prm['g_ffn'], lw['wo'], lw['wr'], lw['br'], nb=nb, r=r,
                            nparts=prm['out_parts'])
    if prm['moe_sorted']:
        xo = _moe_sorted(h2, gate, x1, mod, prm['moe_w_gu'], prm['moe_w_down'], l, final, nb=nb, r=r)
    else:
        r_moe = r if nb > 1 else min(S, 1024)
        xo = _moe(h2, gate, x1, mod, prm['moe_w_gu'], prm['moe_w_down'], l, final, nb=nb, r=r_moe, epb=4)
    states = (knew.reshape(B, -1, A_HEADS, A_HEAD_DIM), vnew.reshape(B, -1, A_HEADS, A_HEAD_DIM),
              ckv, kpe, conv_state, jnp.swapaxes(hft, -1, -2))
    return xo, states


def kernel(x_prompt, x_sample, c_prompt, c_sample, cache_a_k, cache_a_v, cache_b_ckv, cache_b_kpe,
           state_c_conv, state_c_ssm, w_ada, b_ada, g_mix, w_in, a_rel_bias, b_g_q, b_w_uq, b_g_kv,
           b_w_uk, b_w_uv, c_conv_w, c_conv_b, c_dt_bias, c_a_log, c_d, c_g_norm, w_out, g_ffn,
           moe_w_rg, moe_b_rg, moe_w_re, moe_b_re, moe_w_gu, moe_w_down, g_final, w_ada_f, b_ada_f):
    depth = w_in.shape[0]
    D = x_prompt.shape[-1]
    Bp, Sp, _ = x_prompt.shape
    Bs, Ss, _ = x_sample.shape
    past = cache_b_ckv.shape[2]
    pos_p = jnp.arange(Sp)
    pos_s = past + jnp.arange(Ss)

    nc = Bp + Bs
    ncp = -(-nc // 8) * 8
    c_all = jnp.pad(jnp.concatenate([c_prompt, c_sample], axis=0), ((0, ncp - nc), (0, 0)))
    mod = _ada(c_all, w_ada, b_ada[:, None, :])
    modf = _ada(c_all, w_ada_f[None], b_ada_f[None, None, :])[0]

    nr = cache_a_k.shape[2]
    to_t = lambda c: jnp.transpose(c, (0, 1, 3, 4, 2)).reshape(depth, Bs, A_WIDTH, nr)
    cache = {'a_kt': to_t(cache_a_k), 'a_vt': to_t(cache_a_v), 'b_ckv': cache_b_ckv,
             'b_kpet': jnp.swapaxes(cache_b_kpe, 2, 3), 'c_conv': state_c_conv, 'c_ssm': state_c_ssm}

    rp = min(Sp, A_WINDOW)
    moe_w_gu = moe_w_gu.astype(BF16)
    moe_w_down = moe_w_down.astype(BF16)
    xp, xs = x_prompt, x_sample
    st_p, st_s = [], []
    for l in range(depth):
        lw = _layer_weights(l, w_in, b_g_q, b_w_uq, b_g_kv, b_w_uk, b_w_uv, w_out,
                            moe_w_rg, moe_b_rg, moe_w_re, moe_b_re)
        prm = {'g_mix': g_mix[l][None, :], 'a_rel_bias': a_rel_bias[l], 'c_conv_w': c_conv_w[l],
               'c_conv_b': c_conv_b[l][None, :], 'c_dt_bias': jnp.pad(c_dt_bias[l], (0, LANES - C_HEADS))[None, :],
               'c_a_log': jnp.pad(c_a_log[l], (0, LANES - C_HEADS))[None, :],
               'c_d_exp': jnp.repeat(c_d[l], C_HEAD_DIM)[None, :], 'c_g_norm': c_g_norm[l][None, :],
               'g_ffn': g_ffn[l][None, :], 'moe_w_gu': moe_w_gu, 'moe_w_down': moe_w_down,
               'in_parts': 1, 'out_parts': 2, 'mla_bq': 512, 'moe_sorted': True}
        fin_p = (modf[0:Bp][:, None, :], g_final[None, :]) if l == depth - 1 else None
        fin_s = (modf[Bp:nc][:, None, :], g_final[None, :]) if l == depth - 1 else None
        xp, sp = _layer(xp, mod[l, 0:Bp][:, None, :], pos_p, lw, l, None, prm, fin_p, nb=1, r=rp)
        xs, ss = _layer(xs, mod[l, Bp:nc][:, None, :], pos_s, lw, l, cache, prm, fin_s, nb=Bs, r=Ss)
        st_p.append(sp)
        st_s.append(ss)
    stack = lambda st, k: jnp.stack([t[k] for t in st])
    return ((xp, xs) + tuple(stack(st_p, k) for k in range(6))
            + tuple(stack(st_s, k) for k in range(6)))
```

```python
import functools
import math

import numpy as np
import jax
import jax.numpy as jnp
from jax import lax
from jax.experimental import pallas as pl
from jax.experimental.pallas import tpu as pltpu

F32 = jnp.float32
BF16 = jnp.bfloat16

D_MODEL = 1024
CHUNK = 64
EPS = 1e-6
NEG_INF = -1e30

A_HEADS = 4
A_HEAD_DIM = 64
A_WIDTH = A_HEADS * A_HEAD_DIM
A_BACK = 8
A_BAND = (A_BACK + 1) * CHUNK
A_WINDOW = A_BACK * CHUNK
A_REL_CLIP = 128
A_SCALE = A_HEAD_DIM ** -0.5

B_HEADS = 4
B_Q_LORA = 192
B_KV_LORA = 128
B_NOPE = 64
B_ROPE = 32
B_V = 64
B_WIDTH = B_HEADS * B_V
B_SCALE = (B_NOPE + B_ROPE) ** -0.5
B_QK = B_KV_LORA + B_ROPE
ROPE_THETA = 10000.0

C_D_INNER = 512
C_HEAD_DIM = 64
C_HEADS = C_D_INNER // C_HEAD_DIM
C_GROUPS = 2
C_D_STATE = 64
C_CONV_W = 4
C_CONV_DIM = C_D_INNER + 2 * C_GROUPS * C_D_STATE
C_GN = C_GROUPS * C_D_STATE
C_HPG = C_HEADS // C_GROUPS
SSD_CHUNK = 64

E_GROUPS = 4
E_PER_GROUP = 4
N_EXPERTS = E_GROUPS * E_PER_GROUP
E_HIDDEN = 256
ROUTER_LANES = 128
ROUTER_E_OFF = 16

LANES = 128
VMEM_LIMIT = 56 * 1024 * 1024

LOG2E = math.log2(math.e)
B_QSCALE = B_SCALE * LOG2E
A_QSCALE = A_SCALE * LOG2E

_W1_QKV = 0
_W1_CQ = 3 * A_WIDTH
_W1_CKV = _W1_CQ + 256
_W1_KPE = _W1_CKV + B_KV_LORA
_W1_Z = _W1_KPE + LANES
_W1_XBC = _W1_Z + C_D_INNER
_W1_N = _W1_XBC + C_CONV_DIM


def _cparams(*sem):
    return pltpu.CompilerParams(dimension_semantics=sem, vmem_limit_bytes=VMEM_LIMIT)


def _silu(v):
    return v * jax.nn.sigmoid(v)


def _split3_lanes(v, lane):
    x3 = v + pltpu.roll(v, C_HEADS, axis=1) + pltpu.roll(v, 2 * C_HEADS, axis=1)
    hi = x3.astype(BF16).astype(F32)
    r1 = x3 - hi
    mid = r1.astype(BF16).astype(F32)
    lo = r1 - mid
    return jnp.where(lane < C_HEADS, hi, jnp.where(lane < 2 * C_HEADS, mid, lo)).astype(BF16)


def _ada_kernel(c_ref, w_ref, b_ref, o_ref):
    s = _silu(c_ref[...]).astype(BF16)
    o_ref[0] = jnp.dot(s, w_ref[0].astype(BF16), preferred_element_type=F32) + b_ref[0]


def _ada(c_all, w, b):
    nl, d, n = w.shape
    bp = c_all.shape[0]
    tn = 1024
    return pl.pallas_call(
        _ada_kernel,
        grid=(nl, n // tn),
        in_specs=[pl.BlockSpec((bp, d), lambda l, j: (0, 0)),
                  pl.BlockSpec((1, d, tn), lambda l, j: (l, 0, j)),
                  pl.BlockSpec((1, 1, tn), lambda l, j: (l, 0, j))],
        out_specs=pl.BlockSpec((1, bp, tn), lambda l, j: (l, 0, j)),
        out_shape=jax.ShapeDtypeStruct((nl, bp, n), F32),
        compiler_params=_cparams("arbitrary", "arbitrary"),
        name="ada",
    )(c_all, w, b)


def _rows(ref_or_val, nb, r):
    v = ref_or_val[...]
    return jnp.broadcast_to(v, (nb, r, v.shape[-1])).reshape(nb * r, v.shape[-1])


def _inproj_kernel(x_ref, sh_ref, sc_ref, g_ref, w1_ref, gq_ref, wuq_ref, bd_ref, gkv_ref, cos_ref, sin_ref,
                   qkv_ref, knew_ref, vnew_ref, qmla_ref, kmla_ref, ckv_ref, kpe_ref, z_ref, xbc_ref, dtr_ref,
                   *, nb, r, keep_tile, nparts):
    D = x_ref.shape[-1]
    pnb, pr = (nb // nparts, r) if nb > 1 else (1, r // nparts)
    PR = pnb * pr

    def sel(s):
        return (slice(s * pnb, (s + 1) * pnb), slice(None)) if nb > 1 else (slice(None), slice(s * pr, (s + 1) * pr))

    def per_batch(ref, s):
        v = ref[sel(s)[0]] if nb > 1 else ref[...]
        return jnp.broadcast_to(v, (pnb, pr, v.shape[-1])).reshape(PR, v.shape[-1])

    def per_pos(ref, s):
        v = ref[...] if nb > 1 else ref[sel(s)[1]]
        return jnp.broadcast_to(v, (pnb, pr, v.shape[-1])).reshape(PR, v.shape[-1])

    def project(s):
        bs, rs = sel(s)
        x = x_ref[bs, rs].reshape(PR, D)
        h = x * lax.rsqrt(jnp.mean(x * x, axis=-1, keepdims=True) + EPS) * g_ref[...]
        h = h * (1.0 + per_batch(sc_ref, s)) + per_batch(sh_ref, s)
        return jnp.dot(h.astype(BF16), w1_ref[...], preferred_element_type=F32)

    def finish(s, u):
        bs, rs = sel(s)

        def out3(v):
            return v.reshape(pnb, pr, v.shape[-1])

        qkv_ref[bs, rs, 0:A_WIDTH] = out3((u[:, 0:A_WIDTH] * A_QSCALE).astype(BF16))
        qkv_ref[bs, rs, A_WIDTH:3 * A_WIDTH] = out3(u[:, A_WIDTH:3 * A_WIDTH].astype(BF16))

        @pl.when(pl.program_id(1) == keep_tile)
        def _():
            knew_ref[bs, rs] = out3(u[:, A_WIDTH:2 * A_WIDTH])
            vnew_ref[bs, rs] = out3(u[:, 2 * A_WIDTH:3 * A_WIDTH])

        blk = u[:, _W1_CQ:_W1_CQ + 256]
        lane = lax.broadcasted_iota(jnp.int32, (1, 256), 1)
        cq = jnp.where(lane < B_Q_LORA, blk, 0.0)
        cqn = cq * lax.rsqrt(jnp.sum(cq * cq, axis=-1, keepdims=True) * (1.0 / B_Q_LORA) + EPS) * gq_ref[...]
        qb = jnp.dot(cqn.astype(BF16), wuq_ref[...], preferred_element_type=F32)
        nope_w = B_HEADS * B_NOPE
        rope_w = B_HEADS * B_ROPE
        qlat = jnp.dot(qb[:, 0:nope_w].astype(BF16), bd_ref[...], preferred_element_type=F32) * B_QSCALE
        cosr = per_pos(cos_ref, s)
        sinr = per_pos(sin_ref, s)
        qpe = (qb[:, nope_w:nope_w + rope_w] * cosr
               + qb[:, nope_w + rope_w:nope_w + 2 * rope_w] * sinr) * B_QSCALE
        for hd in range(B_HEADS):
            qmla_ref[bs, hd, rs, 0:B_KV_LORA] = out3(qlat[:, hd * B_KV_LORA:(hd + 1) * B_KV_LORA].astype(BF16))
            qmla_ref[bs, hd, rs, B_KV_LORA:B_QK] = out3(qpe[:, hd * B_ROPE:(hd + 1) * B_ROPE].astype(BF16))

        cr = u[:, _W1_CKV:_W1_CKV + B_KV_LORA]
        ckv = cr * lax.rsqrt(jnp.mean(cr * cr, axis=-1, keepdims=True) + EPS) * gkv_ref[...]
        ckv_ref[bs, rs] = out3(ckv)
        kmla_ref[bs, rs, 0:B_KV_LORA] = out3(ckv.astype(BF16))
        kb = u[:, _W1_KPE:_W1_KPE + LANES]
        rot = kb * cosr + pltpu.roll(kb, LANES - B_ROPE, axis=1) * sinr
        kpe_ref[bs, rs] = out3(rot[:, 0:B_ROPE])
        kmla_ref[bs, rs, B_KV_LORA:B_QK] = out3(rot[:, 0:B_ROPE].astype(BF16))
        lane1 = lax.broadcasted_iota(jnp.int32, (1, LANES), 1)
        dtr_ref[bs, rs] = out3(jnp.where(lane1 < C_HEADS, pltpu.roll(kb, LANES - 2 * B_ROPE, axis=1), 0.0))

        z_ref[bs, rs] = out3(u[:, _W1_Z:_W1_Z + C_D_INNER].astype(BF16))
        xbc_ref[bs, rs] = out3(u[:, _W1_XBC:_W1_XBC + C_CONV_DIM])

    us = [project(s) for s in range(nparts)]
    for s in range(nparts):
        finish(s, us[s])


def _inproj(x, mod, g_mix, w1, gq, wuq, bd, gkv, cos_t, sin_t, *, nb, r, nparts):
    B, S, D = x.shape
    nbt, nst = B // nb, S // r
    keep = min(S, A_WINDOW)
    assert keep == r and B % nb == 0 and S % r == 0
    grid = (nbt, nst)
    row = lambda c: pl.BlockSpec((nb, r, c), lambda b, i: (b, i, 0))
    full = lambda a: pl.BlockSpec(a.shape, lambda b, i: (0,) * a.ndim)
    in_specs = [row(D),
                pl.BlockSpec((nb, 1, D), lambda b, i: (b, 0, 0)),
                pl.BlockSpec((nb, 1, D), lambda b, i: (b, 0, 1)),
                full(g_mix), full(w1), full(gq), full(wuq), full(bd), full(gkv),
                pl.BlockSpec((r, LANES), lambda b, i: (i, 0)),
                pl.BlockSpec((r, LANES), lambda b, i: (i, 0))]
    out_shape = [jax.ShapeDtypeStruct((B, S, 3 * A_WIDTH), BF16),
                 jax.ShapeDtypeStruct((B, keep, A_WIDTH), F32),
                 jax.ShapeDtypeStruct((B, keep, A_WIDTH), F32),
                 jax.ShapeDtypeStruct((B, B_HEADS, S, B_QK), BF16),
                 jax.ShapeDtypeStruct((B, S, B_QK), BF16),
                 jax.ShapeDtypeStruct((B, S, B_KV_LORA), F32),
                 jax.ShapeDtypeStruct((B, S, B_ROPE), F32),
                 jax.ShapeDtypeStruct((B, S, C_D_INNER), BF16),
                 jax.ShapeDtypeStruct((B, S, C_CONV_DIM), F32),
                 jax.ShapeDtypeStruct((B, S, LANES), F32)]
    out_specs = [row(3 * A_WIDTH),
                 pl.BlockSpec((nb, keep, A_WIDTH), lambda b, i: (b, 0, 0)),
                 pl.BlockSpec((nb, keep, A_WIDTH), lambda b, i: (b, 0, 0)),
                 pl.BlockSpec((nb, B_HEADS, r, B_QK), lambda b, i: (b, 0, i, 0)),
                 row(B_QK), row(B_KV_LORA), row(B_ROPE), row(C_D_INNER), row(C_CONV_DIM), row(LANES)]
    return pl.pallas_call(
        functools.partial(_inproj_kernel, nb=nb, r=r, keep_tile=nst - 1, nparts=nparts),
        grid=grid, in_specs=in_specs, out_specs=out_specs, out_shape=out_shape,
        compiler_params=_cparams("arbitrary", "arbitrary"),
        name="inproj",
    )(x, mod, mod, g_mix, w1, gq, wuq, bd, gkv, cos_t, sin_t)


def _band_scores(q, k, hmask):
    qs = jnp.where(hmask, jnp.concatenate([q] * A_HEADS, axis=0), jnp.zeros((), BF16))
    return lax.dot_general(qs, k, (((1,), (1,)), ((), ())), preferred_element_type=F32)


def _band_finish(s, v, bias, valid, hmask):
    nq = s.shape[0] // A_HEADS
    s = s + bias
    if valid is not None:
        s = jnp.where(valid, s, NEG_INF)
    m = jnp.max(s, axis=-1, keepdims=True)
    e = jnp.exp2(s - m)
    den = jnp.sum(e, axis=-1, keepdims=True)
    o = jnp.dot(e.astype(BF16), v, preferred_element_type=F32)
    o = jnp.where(hmask, o / den, 0.0)
    y = o[0:nq]
    for hd in range(1, A_HEADS):
        y = y + o[hd * nq:(hd + 1) * nq]
    return y


def _head_mask(nq):
    row = lax.broadcasted_iota(jnp.int32, (A_HEADS * nq, A_WIDTH), 0)
    col = lax.broadcasted_iota(jnp.int32, (A_HEADS * nq, A_WIDTH), 1)
    return (row // nq) == (col // A_HEAD_DIM)


def _band_prompt_kernel(cur_ref, prev_ref, bias_ref, o_ref, kwin, vwin, *, tq):
    i = pl.program_id(1)
    nch = tq // CHUNK
    kwin[0:A_WINDOW] = prev_ref[0, tq - A_WINDOW:tq, A_WIDTH:2 * A_WIDTH]
    kwin[A_WINDOW:A_WINDOW + tq] = cur_ref[0, :, A_WIDTH:2 * A_WIDTH]
    vwin[0:A_WINDOW] = prev_ref[0, tq - A_WINDOW:tq, 2 * A_WIDTH:3 * A_WIDTH]
    vwin[A_WINDOW:A_WINDOW + tq] = cur_ref[0, :, 2 * A_WIDTH:3 * A_WIDTH]
    hmask = _head_mask(CHUNK)
    ucol = lax.broadcasted_iota(jnp.int32, (1, A_BAND), 1)

    def scores(c):
        return _band_scores(cur_ref[0, c * CHUNK:(c + 1) * CHUNK, 0:A_WIDTH],
                            kwin[c * CHUNK:c * CHUNK + A_BAND], hmask)

    def run(first_tile):
        bias2 = bias_ref[...] * LOG2E
        s_next = scores(0)
        for c in range(nch):
            s = s_next
            if c + 1 < nch:
                s_next = scores(c + 1)
            valid = (ucol // CHUNK + (c - A_BACK)) >= 0 if first_tile else None
            y = _band_finish(s, vwin[c * CHUNK:c * CHUNK + A_BAND], bias2, valid, hmask)
            o_ref[0, c * CHUNK:(c + 1) * CHUNK, :] = y.astype(BF16)

    @pl.when(i == 0)
    def _():
        run(True)

    @pl.when(i > 0)
    def _():
        run(False)


def _band_prompt(qkv, bias_stack):
    B, S, _ = qkv.shape
    tq = A_WINDOW
    assert S % tq == 0 and tq // CHUNK >= A_BACK
    return pl.pallas_call(
        functools.partial(_band_prompt_kernel, tq=tq),
        grid=(B, S // tq),
        in_specs=[pl.BlockSpec((1, tq, 3 * A_WIDTH), lambda b, i: (b, i, 0)),
                  pl.BlockSpec((1, tq, 3 * A_WIDTH), lambda b, i: (b, jnp.maximum(i - 1, 0), 0)),
                  pl.BlockSpec(bias_stack.shape, lambda b, i: (0, 0))],
        out_specs=pl.BlockSpec((1, tq, A_WIDTH), lambda b, i: (b, i, 0)),
        out_shape=jax.ShapeDtypeStruct((B, S, A_WIDTH), BF16),
        scratch_shapes=[pltpu.VMEM((A_WINDOW + tq, A_WIDTH), BF16), pltpu.VMEM((A_WINDOW + tq, A_WIDTH), BF16)],
        compiler_params=_cparams("arbitrary", "arbitrary"),
        name="band_prompt",
    )(qkv, qkv, bias_stack)


def _band_sample_kernel(qkv_ref, kt_ref, vt_ref, biasc_ref, biasn_ref, o_ref, *, nbb, ds):
    hmask = _head_mask(ds)
    nt_dims = (((1,), (1,)), ((), ()))
    for b in range(nbb):
        q = qkv_ref[b, :, 0:A_WIDTH]
        kn = qkv_ref[b, :, A_WIDTH:2 * A_WIDTH]
        vn = qkv_ref[b, :, 2 * A_WIDTH:3 * A_WIDTH]
        qs = jnp.where(hmask, jnp.concatenate([q] * A_HEADS, axis=0), jnp.zeros((), BF16))
        sc = jnp.dot(qs, kt_ref[0, b].astype(BF16), preferred_element_type=F32) + biasc_ref[...] * LOG2E
        sn = lax.dot_general(qs, kn, nt_dims, preferred_element_type=F32) + biasn_ref[...] * LOG2E
        m = jnp.maximum(jnp.max(sc, axis=-1, keepdims=True), jnp.max(sn, axis=-1, keepdims=True))
        ec = jnp.exp2(sc - m)
        en = jnp.exp2(sn - m)
        den = jnp.sum(ec, axis=-1, keepdims=True) + jnp.sum(en, axis=-1, keepdims=True)
        o = lax.dot_general(ec.astype(BF16), vt_ref[0, b].astype(BF16), nt_dims, preferred_element_type=F32)
        o = o + jnp.dot(en.astype(BF16), vn, preferred_element_type=F32)
        o = jnp.where(hmask, o / den, 0.0)
        y = o[0:ds]
        for hd in range(1, A_HEADS):
            y = y + o[hd * ds:(hd + 1) * ds]
        o_ref[b] = y.astype(BF16)


def _band_sample(qkv, cache_kt, cache_vt, layer, bias_stack):
    B, ds, _ = qkv.shape
    nr = cache_kt.shape[3]
    nbb = 4
    assert B % nbb == 0
    bias_c, bias_n = bias_stack[:, 0:nr], bias_stack[:, nr:]
    return pl.pallas_call(
        functools.partial(_band_sample_kernel, nbb=nbb, ds=ds),
        grid=(B // nbb,),
        in_specs=[pl.BlockSpec((nbb, ds, 3 * A_WIDTH), lambda b: (b, 0, 0)),
                  pl.BlockSpec((1, nbb, A_WIDTH, nr), lambda b: (layer, b, 0, 0)),
                  pl.BlockSpec((1, nbb, A_WIDTH, nr), lambda b: (layer, b, 0, 0)),
                  pl.BlockSpec(bias_c.shape, lambda b: (0, 0)),
                  pl.BlockSpec(bias_n.shape, lambda b: (0, 0))],
        out_specs=pl.BlockSpec((nbb, ds, A_WIDTH), lambda b: (b, 0, 0)),
        out_shape=jax.ShapeDtypeStruct((B, ds, A_WIDTH), BF16),
        compiler_params=_cparams("arbitrary"),
        name="band_sample",
    )(qkv, cache_kt, cache_vt, bias_c, bias_n)


def _mla_finish(acc, den, wuv_ref, bq):
    o = acc / den
    o_all = jnp.concatenate([o[hd * bq:(hd + 1) * bq] for hd in range(B_HEADS)], axis=1)
    return jnp.dot(o_all.astype(BF16), wuv_ref[...], preferred_element_type=F32)


def _mla_prompt_kernel(qi_ref, kj_ref, last_ref, q_ref, k_ref, wuv_ref, o_ref, m_sc, acc_sc, *, bq, bk, sk):
    p = pl.program_id(1)
    i = qi_ref[p]
    j = kj_ref[p]
    M = B_HEADS * bq

    @pl.when(j == 0)
    def _():
        m_sc[...] = jnp.full(m_sc.shape, NEG_INF, F32)
        acc_sc[...] = jnp.zeros(acc_sc.shape, F32)

    nt_dims = (((1,), (1,)), ((), ()))
    nsub = bk // sk

    def scores(q, t):
        return lax.dot_general(q, k_ref[0, t * sk:(t + 1) * sk, :], nt_dims, preferred_element_type=F32)

    def absorb(s, t, m_run, acc):
        m_new = jnp.maximum(m_run, jnp.max(s, axis=1, keepdims=True))
        alpha = jnp.exp2(m_run - m_new)
        pr = jnp.exp2(s - jnp.tile(m_new, (1, sk // LANES)))
        v_ext = jnp.concatenate([k_ref[0, t * sk:(t + 1) * sk, 0:B_KV_LORA], jnp.ones((sk, LANES), BF16)], axis=1)
        acc = acc * jnp.concatenate([alpha, alpha], axis=1) + jnp.dot(pr.astype(BF16), v_ext,
                                                                      preferred_element_type=F32)
        return m_new, acc

    @pl.when(last_ref[p] == 0)
    def _():
        q = q_ref[0].reshape(M, B_QK)
        m_run, acc = m_sc[...], acc_sc[...]
        s_next = scores(q, 0)
        for t in range(nsub):
            s = s_next
            if t + 1 < nsub:
                s_next = scores(q, t + 1)
            m_run, acc = absorb(s, t, m_run, acc)
        m_sc[...] = m_run
        acc_sc[...] = acc

    @pl.when(last_ref[p] == 1)
    def _():
        for t in range(nsub):
            @pl.when(j * bk + t * sk <= i * bq + (bq - 1))
            def _():
                s = scores(q_ref[0].reshape(M, B_QK), t)
                row = lax.broadcasted_iota(jnp.int32, (M, sk), 0)
                col = lax.broadcasted_iota(jnp.int32, (M, sk), 1)
                s = jnp.where((col + (j * bk + t * sk)) // CHUNK <= ((row % bq) + i * bq) // CHUNK, s, NEG_INF)
                m_sc[...], acc_sc[...] = absorb(s, t, m_sc[...], acc_sc[...])
        o_ref[0] = _mla_finish(acc_sc[:, 0:B_KV_LORA], acc_sc[:, B_KV_LORA:], wuv_ref, bq).astype(BF16)


def _mla_prompt(qmla, kmla, wuv_bd, *, bq, bk):
    B, S = kmla.shape[0:2]
    bk = min(bk, S)
    assert S % bq == 0 and S % bk == 0 and bq % CHUNK == 0 and bk % bq == 0
    nq = S // bq
    n_kv = [(i * bq + bq - 1) // bk + 1 for i in range(nq)]
    qi = np.concatenate([np.full(n, i, np.int32) for i, n in enumerate(n_kv)])
    kj = np.concatenate([np.arange(n, dtype=np.int32) for n in n_kv])
    last = np.concatenate([(np.arange(n) == n - 1).astype(np.int32) for n in n_kv])
    grid_spec = pltpu.PrefetchScalarGridSpec(
        num_scalar_prefetch=3,
        grid=(B, len(qi)),
        in_specs=[pl.BlockSpec((1, B_HEADS, bq, B_QK), lambda b, p, qi, kj, last: (b, 0, qi[p], 0)),
                  pl.BlockSpec((1, bk, B_QK), lambda b, p, qi, kj, last: (b, kj[p], 0)),
                  pl.BlockSpec(wuv_bd.shape, lambda b, p, qi, kj, last: (0, 0))],
        out_specs=pl.BlockSpec((1, bq, B_WIDTH), lambda b, p, qi, kj, last: (b, qi[p], 0)),
        scratch_shapes=[pltpu.VMEM((B_HEADS * bq, LANES), F32),
                        pltpu.VMEM((B_HEADS * bq, B_KV_LORA + LANES), F32)])
    return pl.pallas_call(
        functools.partial(_mla_prompt_kernel, bq=bq, bk=bk, sk=min(bk, 512)),
        grid_spec=grid_spec,
        out_shape=jax.ShapeDtypeStruct((B, S, B_WIDTH), BF16),
        compiler_params=_cparams("arbitrary", "arbitrary"),
        name="mla_prompt",
    )(jnp.asarray(qi), jnp.asarray(kj), jnp.asarray(last), qmla, kmla, wuv_bd)


def _mla_sample_kernel(q_ref, kn_ref, ckv_ref, kpet_ref, wuv_ref, o_ref, *, ds):
    M = B_HEADS * ds
    nt_dims = (((1,), (1,)), ((), ()))
    q = q_ref[0].reshape(M, B_QK)
    kn = kn_ref[0]
    ckv = ckv_ref[0, 0].astype(BF16)
    sc = (lax.dot_general(q[:, 0:B_KV_LORA], ckv, nt_dims, preferred_element_type=F32)
          + jnp.dot(q[:, B_KV_LORA:B_QK], kpet_ref[0, 0].astype(BF16), preferred_element_type=F32))
    sn = lax.dot_general(q, kn, nt_dims, preferred_element_type=F32)
    m = jnp.maximum(jnp.max(sc, axis=1, keepdims=True), jnp.max(sn, axis=1, keepdims=True))
    pc = jnp.exp2(sc - m)
    pn = jnp.exp2(sn - m)
    den = jnp.sum(pc, axis=1, keepdims=True) + jnp.sum(pn, axis=1, keepdims=True)
    acc = (jnp.dot(pc.astype(BF16), ckv, preferred_element_type=F32)
           + jnp.dot(pn.astype(BF16), kn[:, 0:B_KV_LORA], preferred_element_type=F32))
    o_ref[0] = _mla_finish(acc, den, wuv_ref, ds).astype(BF16)


def _mla_sample(qmla, kmla, cache_ckv, cache_kpet, layer, wuv_bd):
    B, ds = kmla.shape[0:2]
    past = cache_ckv.shape[2]
    return pl.pallas_call(
        functools.partial(_mla_sample_kernel, ds=ds),
        grid=(B,),
        in_specs=[pl.BlockSpec((1, B_HEADS, ds, B_QK), lambda b: (b, 0, 0, 0)),
                  pl.BlockSpec((1, ds, B_QK), lambda b: (b, 0, 0)),
                  pl.BlockSpec((1, 1, past, B_KV_LORA), lambda b: (layer, b, 0, 0)),
                  pl.BlockSpec((1, 1, B_ROPE, past), lambda b: (layer, b, 0, 0)),
                  pl.BlockSpec(wuv_bd.shape, lambda b: (0, 0))],
        out_specs=pl.BlockSpec((1, ds, B_WIDTH), lambda b: (b, 0, 0)),
        out_shape=jax.ShapeDtypeStruct((B, ds, B_WIDTH), BF16),
        compiler_params=_cparams("arbitrary"),
        name="mla_sample",
    )(qmla, kmla, cache_ckv, cache_kpet, wuv_bd)


def _ssd_kernel(xbc_ref, z_ref, dtr_ref, pre_ref, h0_ref, cw_ref, cb_ref, dtb_ref, alog_ref, dsk_ref, gn_ref,
                es_ref, ep_ref, tri_ref, y_ref, hout_ref,
                cbuf, xs_sc, b_sc, c_sc, ces_sc, cep_sc, dep_sc, st_sc, *, nbat, lt, lc, nt):
    t = pl.program_id(1)
    nch = lt // lc
    HS = C_HEADS * lc
    gw = C_HPG * C_HEAD_DIM
    same_exp = lc == C_HEAD_DIM

    @pl.when(t == 0)
    def _():
        st_sc[...] = jnp.zeros(st_sc.shape, F32)
        for b in range(nbat):
            cbuf[b, 8 - (C_CONV_W - 1):8] = pre_ref[b]
            for g in range(C_GROUPS):
                blk = jnp.concatenate([h0_ref[b, g * C_HPG + hl] for hl in range(C_HPG)], axis=1)
                st_sc[b, g * C_D_STATE:(g + 1) * C_D_STATE, g * gw:(g + 1) * gw] = blk

    hl = lax.broadcasted_iota(jnp.int32, (lt, LANES), 1)
    for b in range(nbat):
        cbuf[b, 8:8 + lt] = xbc_ref[b]
        acc = jnp.broadcast_to(cb_ref[...], (lt, C_CONV_DIM))
        for kk in range(C_CONV_W):
            off = 8 - (C_CONV_W - 1) + kk
            acc = acc + cbuf[b, off:off + lt] * cw_ref[kk:kk + 1, :]
        tail = cbuf[b, 8 + lt - (C_CONV_W - 1):8 + lt]
        cbuf[b, 8 - (C_CONV_W - 1):8] = tail
        xc = _silu(acc)
        xs_sc[b] = xc[:, 0:C_D_INNER]
        b_sc[b] = xc[:, C_D_INNER:C_D_INNER + C_GN]
        c_sc[b] = xc[:, C_D_INNER + C_GN:C_D_INNER + 2 * C_GN]

        dt = jnp.where(hl < C_HEADS, jax.nn.softplus(dtr_ref[b] + dtb_ref[...]), 0.0)
        da = dt * (-jnp.exp(alog_ref[...]))
        c3 = jnp.dot(tri_ref[...], _split3_lanes(da, hl), preferred_element_type=F32)
        cum = jnp.where(hl < C_HEADS, c3 + pltpu.roll(c3, LANES - C_HEADS, axis=1)
                        + pltpu.roll(c3, LANES - 2 * C_HEADS, axis=1), 0.0)
        cum3 = _split3_lanes(cum, hl)
        cep = jnp.dot(cum3, ep_ref[...], preferred_element_type=F32)
        cep_sc[b] = cep
        if not same_exp:
            ces_sc[b] = jnp.dot(cum3, es_ref[...], preferred_element_type=F32)
        dep_sc[b] = jnp.dot(_split3_lanes(dt, hl), ep_ref[...], preferred_element_type=F32)

    r_s = lax.broadcasted_iota(jnp.int32, (lc, HS), 0)
    l_s = lax.broadcasted_iota(jnp.int32, (lc, HS), 1)
    eye_t = (l_s % lc) == r_s
    causal_t = (l_s % lc) <= r_s
    r_w = lax.broadcasted_iota(jnp.int32, (HS, C_GN), 0)
    l_w = lax.broadcasted_iota(jnp.int32, (HS, C_GN), 1)
    gmask = (r_w // (C_HPG * lc)) == (l_w // C_D_STATE)
    r_b =lax.broadcasted_iota(jnp.int32, (C_HPG * lc, gw), 0)
    l_b = lax.broadcasted_iota(jnp.int32, (C_HPG * lc, gw), 1)
    bmask = (r_b // lc) == (l_b // C_HEAD_DIM)
    r_g = lax.broadcasted_iota(jnp.int32, (C_GN, C_D_INNER), 0)
    l_g = lax.broadcasted_iota(jnp.int32, (C_GN, C_D_INNER), 1)
    smask = (r_g // C_D_STATE) == (l_g // gw)

    def stage_cb(b, rows):
        cm = c_sc[b, rows, :].astype(BF16)
        bm = b_sc[b, rows, :]
        w_nt = jnp.where(gmask, jnp.concatenate([bm] * C_HEADS, axis=0), 0.0).astype(BF16)
        cbcat = lax.dot_general(cm, w_nt, (((1,), (1,)), ((), ())), preferred_element_type=F32)
        st = st_sc[b]
        yoff = jnp.dot(cm, st.astype(BF16), preferred_element_type=F32)
        return dict(bm=bm, cbcat=cbcat, st=st, yoff=yoff)

    def stage_diag(b, rows, v):
        ce_p = cep_sc[b, rows, :]
        ce_s = ce_p if same_exp else ces_sc[b, rows, :]
        xs = xs_sc[b, rows, :]
        rflat = jnp.sum(jnp.where(eye_t, ce_s, 0.0), axis=0, keepdims=True)
        lcat = jnp.exp(jnp.where(causal_t, ce_s - rflat, NEG_INF))
        mcat = (v['cbcat'] * lcat).astype(BF16)
        xdt = xs * dep_sc[b, rows, :]
        xdt_b = xdt.astype(BF16)
        ydiag = []
        for g in range(C_GROUPS):
            bdx = jnp.where(bmask, jnp.concatenate([xdt_b[:, g * gw:(g + 1) * gw]] * C_HPG, axis=0),
                            jnp.zeros((), BF16))
            ydiag.append(jnp.dot(mcat[:, g * C_HPG * lc:(g + 1) * C_HPG * lc], bdx, preferred_element_type=F32))
        v.update(ce_p=ce_p, xs=xs, xdt=xdt, ydiag=jnp.concatenate(ydiag, axis=1))

    def stage_state(b, rows, v):
        ce_p = v['ce_p']
        last = ce_p[lc - 1:lc, :]
        xw = (v['xdt'] * jnp.exp(last - ce_p)).astype(BF16)
        upd = lax.dot_general(v['bm'].astype(BF16), xw, (((0,), (0,)), ((), ())), preferred_element_type=F32)
        st_sc[b] = jnp.exp(last) * v['st'] + jnp.where(smask, upd, 0.0)

    def stage_out(b, rows, v):
        y = v['ydiag'] + v['yoff'] * jnp.exp(v['ce_p']) + dsk_ref[...] * v['xs']
        y = y * _silu(z_ref[b, rows, :].astype(F32))
        y = y * lax.rsqrt(jnp.mean(y * y, axis=-1, keepdims=True) + EPS) * gn_ref[...]
        y_ref[b, rows, :] = y.astype(BF16)

    def chunk(c, carry):
        rows = pl.ds(pl.multiple_of(c * lc, lc), lc)
        vals = [stage_cb(b, rows) for b in range(nbat)]
        for stage in (stage_diag, stage_state, stage_out):
            for b in range(nbat):
                stage(b, rows, vals[b])
        return carry

    lax.fori_loop(0, nch, chunk, 0)

    @pl.when(t == nt - 1)
    def _():
        for b in range(nbat):
            for hd in range(C_HEADS):
                g = hd // C_HPG
                hout_ref[b, hd] = st_sc[b, g * C_D_STATE:(g + 1) * C_D_STATE,
                                        hd * C_HEAD_DIM:(hd + 1) * C_HEAD_DIM]


def _ssd(xbc, z, dtr, prefix, h0t, cw, cb, dtb, alog, dsk, gn, *, nbat, lt, lc):
    B, S, _ = xbc.shape
    nt = S // lt
    assert S % lt == 0 and lt % lc == 0 and B % nbat == 0
    hs = C_HEADS * lc

    def expand(width, per_head):
        e = np.zeros((LANES, width), np.float32)
        for part in range(3):
            e[part * C_HEADS:(part + 1) * C_HEADS] = (np.arange(width)[None, :] // per_head
                                                      == np.arange(C_HEADS)[:, None])
        return jnp.asarray(e, dtype=BF16)

    es, ep = expand(hs, lc), expand(C_D_INNER, C_HEAD_DIM)
    pos = np.arange(lt)
    tri = jnp.asarray((pos[None, :] <= pos[:, None]) & (pos[None, :] // lc == pos[:, None] // lc), dtype=BF16)
    row = lambda c: pl.BlockSpec((nbat, lt, c), lambda b, t: (b, t, 0))
    per_b = lambda a: pl.BlockSpec((nbat,) + a.shape[1:], lambda b, t: (b,) + (0,) * (a.ndim - 1))
    full = lambda a: pl.BlockSpec(a.shape, lambda b, t: (0,) * a.ndim)
    sc = lambda *shape: pltpu.VMEM((nbat,) + shape, F32)
    return pl.pallas_call(
        functools.partial(_ssd_kernel, nbat=nbat, lt=lt, lc=lc, nt=nt),
        grid=(B // nbat, nt),
        in_specs=[row(C_CONV_DIM), row(C_D_INNER), row(LANES), per_b(prefix), per_b(h0t),
                  full(cw), full(cb), full(dtb), full(alog), full(dsk), full(gn), full(es), full(ep), full(tri)],
        out_specs=[row(C_D_INNER), per_b(h0t)],
        out_shape=[jax.ShapeDtypeStruct((B, S, C_D_INNER), BF16),
                   jax.ShapeDtypeStruct(h0t.shape, F32)],
        scratch_shapes=[sc(lt + 8, C_CONV_DIM), sc(lt, C_D_INNER), sc(lt, C_GN), sc(lt, C_GN),
                        sc(lt, hs) if lc != C_HEAD_DIM else sc(8, LANES),
                        sc(lt, C_D_INNER), sc(lt, C_D_INNER), sc(C_GN, C_D_INNER)],
        compiler_params=_cparams("arbitrary", "arbitrary"),
        name="ssd",
    )(xbc, z, dtr, prefix, h0t, cw, cb, dtb, alog, dsk, gn, es, ep, tri)


def _outproj_kernel(ya_ref, yb_ref, yc_ref, x_ref, g1_ref, sh_ref, sc_ref, gf_ref, wo_ref, wr_ref, br_ref,
                    x1_ref, h2_ref, gate_ref, *, nb, r, nparts):
    D = x_ref.shape[-1]
    pnb, pr = (nb // nparts, r) if nb > 1 else (1, r // nparts)
    PR = pnb * pr

    def sel_part(s):
        return (slice(s * pnb, (s + 1) * pnb), slice(None)) if nb > 1 else (slice(None), slice(s * pr, (s + 1) * pr))

    def in2(ref, s):
        bs, rs = sel_part(s)
        return ref[bs, rs].reshape(PR, ref.shape[-1])

    def per_batch(ref, s):
        v = ref[sel_part(s)[0]] if nb > 1 else ref[...]
        return jnp.broadcast_to(v, (pnb, pr, v.shape[-1])).reshape(PR, v.shape[-1])

    def project(s):
        o = jnp.dot(in2(ya_ref, s), wo_ref[0:A_WIDTH, :], preferred_element_type=F32)
        o = o + jnp.dot(in2(yb_ref, s), wo_ref[A_WIDTH:A_WIDTH + B_WIDTH, :], preferred_element_type=F32)
        return o + jnp.dot(in2(yc_ref, s), wo_ref[A_WIDTH + B_WIDTH:, :], preferred_element_type=F32)

    wr = wr_ref[...]
    w_hi = wr.astype(BF16)
    w_lo = (wr - w_hi.astype(F32)).astype(BF16)

    def logits(s, o):
        bs, rs = sel_part(s)
        x1 = in2(x_ref, s) + per_batch(g1_ref, s) * o
        x1_ref[bs, rs] = x1.reshape(pnb, pr, D)
        h2 = x1 * lax.rsqrt(jnp.mean(x1 * x1, axis=-1, keepdims=True) + EPS) * gf_ref[...]
        h2 = h2 * (1.0 + per_batch(sc_ref, s)) + per_batch(sh_ref, s)
        h_hi = h2.astype(BF16)
        h2_ref[bs, rs] = h_hi.reshape(pnb, pr, D)
        h_lo = (h2 - h_hi.astype(F32)).astype(BF16)
        return (jnp.dot(h_hi, w_hi, preferred_element_type=F32) + jnp.dot(h_hi, w_lo, preferred_element_type=F32)
                + jnp.dot(h_lo, w_hi, preferred_element_type=F32) + br_ref[...])

    def route(s, lg):
        bs, rs = sel_part(s)
        lane = lax.broadcasted_iota(jnp.int32, (PR, ROUTER_LANES), 1)
        big = jnp.int32(ROUTER_LANES)
        is_g = lane < E_GROUPS
        gl = jnp.where(is_g, lg, NEG_INF)
        gmax = jnp.max(gl, axis=-1, keepdims=True)
        p_top = 1.0 / jnp.sum(jnp.where(is_g, jnp.exp(gl - gmax), 0.0), axis=-1, keepdims=True)
        g_idx = jnp.min(jnp.where(is_g & (gl == gmax), lane, big), axis=-1, keepdims=True)
        e_lane = lane - ROUTER_E_OFF
        sel = (e_lane >= 0) & (e_lane < N_EXPERTS) & ((e_lane // E_PER_GROUP) == g_idx)
        l1 = jnp.where(sel, lg, NEG_INF)
        m1 = jnp.max(l1, axis=-1, keepdims=True)
        i1 = jnp.min(jnp.where(sel & (l1 == m1), lane, big), axis=-1, keepdims=True)
        sel2 = sel & (lane != i1)
        l2 = jnp.where(sel2, lg, NEG_INF)
        m2 = jnp.max(l2, axis=-1, keepdims=True)
        i2 = jnp.min(jnp.where(sel2 & (l2 == m2), lane, big), axis=-1, keepdims=True)
        e2 = jnp.exp(m2 - m1)
        w1 = p_top / (1.0 + e2)
        w2 = p_top * e2 / (1.0 + e2)
        gate = jnp.where(lane == i1, w1, 0.0) + jnp.where(lane == i2, w2, 0.0)
        gate = gate + jnp.where(lane == 0, g_idx.astype(F32), 0.0)
        gate_ref[bs, rs] = gate.reshape(pnb, pr, ROUTER_LANES)

    outs = [project(s) for s in range(nparts)]
    lgs = [logits(s, outs[s]) for s in range(nparts)]
    for s in range(nparts):
        route(s, lgs[s])


def _outproj(ya, yb, yc, x, mod, g_ffn, wo, wr, br, *, nb, r, nparts):
    B, S, D = x.shape
    row = lambda c: pl.BlockSpec((nb, r, c), lambda b, i: (b, i, 0))
    modc = lambda j: pl.BlockSpec((nb, 1, D), lambda b, i: (b, 0, j))
    full = lambda a: pl.BlockSpec(a.shape, lambda b, i: (0,) * a.ndim)
    return pl.pallas_call(
        functools.partial(_outproj_kernel, nb=nb, r=r, nparts=nparts),
        grid=(B // nb, S // r),
        in_specs=[row(A_WIDTH), row(B_WIDTH), row(C_D_INNER), row(D), modc(2), modc(3), modc(4),
                  full(g_ffn), full(wo), full(wr), full(br)],
        out_specs=[row(D), row(D), row(ROUTER_LANES)],
        out_shape=[jax.ShapeDtypeStruct((B, S, D), F32), jax.ShapeDtypeStruct((B, S, D), BF16),
                   jax.ShapeDtypeStruct((B, S, ROUTER_LANES), F32)],
        compiler_params=_cparams("arbitrary", "arbitrary"),
        name="outproj",
    )(ya, yb, yc, x, mod, mod, mod, g_ffn, wo, wr, br)


MOE_BLK = 160


def _moe_sorted_kernel(h_ref, gate_ref, x_ref, g2_ref, wgu_ref, wd_ref, tri_ref, *rest, nb, r, final):
    if final:
        shf_ref, scf_ref, gfin_ref, o_ref, xc_sc, yc_sc, p_sc, meta = rest
    else:
        o_ref, xc_sc, yc_sc, p_sc, meta = rest
    g = pl.program_id(2)
    R = nb * r
    D = x_ref.shape[-1]
    RC = xc_sc.shape[0]

    @pl.when(g == 0)
    def _():
        gate = gate_ref[...].reshape(R, ROUTER_LANES)
        lane = lax.broadcasted_iota(jnp.int32, (R, ROUTER_LANES), 1)
        onehot = jnp.where((lane < E_GROUPS) & (gate[:, 0:1] == lane.astype(F32)), 1.0, 0.0)
        pos = jnp.dot(tri_ref[...], onehot.astype(BF16), preferred_element_type=F32)
        rank = jnp.sum(onehot * pos, axis=1, keepdims=True)
        counts = jnp.sum(onehot, axis=0, keepdims=True)
        lane1 = lax.broadcasted_iota(jnp.int32, (1, ROUTER_LANES), 1)
        base = jnp.int32(0)
        baserow = jnp.zeros((1, ROUTER_LANES), F32)
        for gg in range(E_GROUPS):
            cnt = jnp.sum(jnp.where(lane1 == gg, counts, 0.0)).astype(jnp.int32)
            nblk = (cnt + (MOE_BLK - 1)) // MOE_BLK
            meta[gg] = base
            meta[E_GROUPS + gg] = nblk
            baserow = jnp.where(lane1 == gg, base.astype(F32), baserow)
            base = base + nblk * MOE_BLK
        cpos = jnp.sum(onehot * baserow, axis=1, keepdims=True) + rank
        colc = lax.broadcasted_iota(jnp.int32, (1, RC), 1).astype(F32)
        pm = jnp.where(cpos == colc, 1.0, 0.0).astype(BF16)
        p_sc[...] = pm
        g_hi = gate.astype(BF16)
        g_lo = (gate - g_hi.astype(F32)).astype(BF16)
        xext = jnp.concatenate([h_ref[...].reshape(R, D), g_hi, g_lo], axis=1)
        xc_sc[...] = lax.dot_general(pm, xext, (((0,), (0,)), ((), ())),
                                     preferred_element_type=F32).astype(BF16)
        yc_sc[...] = jnp.zeros(yc_sc.shape, BF16)

    base_g = meta[g]
    nblk_g = meta[E_GROUPS + g]
    lane_b = lax.broadcasted_iota(jnp.int32, (MOE_BLK, ROUTER_LANES), 1)

    def block(bi, carry):
        rows = pl.ds(pl.multiple_of(base_g + bi * MOE_BLK, MOE_BLK), MOE_BLK)
        xg = xc_sc[rows, 0:D]
        gg = (xc_sc[rows, D:D + ROUTER_LANES].astype(F32)
              + xc_sc[rows, D + ROUTER_LANES:D + 2 * ROUTER_LANES].astype(F32))
        acts = []
        for k in range(E_PER_GROUP):
            gu = jnp.dot(xg, wgu_ref[0, k], preferred_element_type=F32)
            gcol = jnp.sum(jnp.where(lane_b == ROUTER_E_OFF + g * E_PER_GROUP + k, gg, 0.0),
                           axis=-1, keepdims=True)
            acts.append((_silu(gu[:, 0:E_HIDDEN]) * gu[:, E_HIDDEN:] * gcol).astype(BF16))
        wd = wd_ref[0].reshape(E_PER_GROUP * E_HIDDEN, D)
        yc_sc[rows, :] = jnp.dot(jnp.concatenate(acts, axis=1), wd, preferred_element_type=F32).astype(BF16)
        return carry

    lax.fori_loop(0, nblk_g, block, 0)

    @pl.when(g == E_GROUPS - 1)
    def _():
        moe = jnp.dot(p_sc[...], yc_sc[...], preferred_element_type=F32)
        xo = x_ref[...].reshape(R, D) + _rows(g2_ref, nb, r) * moe
        if final:
            xo = xo * lax.rsqrt(jnp.mean(xo * xo, axis=-1, keepdims=True) + EPS) * gfin_ref[...]
            xo = xo * (1.0 + _rows(scf_ref, nb, r)) + _rows(shf_ref, nb, r)
        o_ref[...] = xo.reshape(nb, r, D)


def _moe_sorted(h2, gate, x1, mod, wgu, wd, layer, final, *, nb, r):
    B, S, D = x1.shape
    R = nb * r
    rc = R + E_GROUPS * MOE_BLK
    pos = np.arange(R)
    tri = jnp.asarray(pos[None, :] < pos[:, None], dtype=BF16)
    row = lambda c: pl.BlockSpec((nb, r, c), lambda b, i, g: (b, i, 0))
    modc = lambda j: pl.BlockSpec((nb, 1, D), lambda b, i, g: (b, 0, j))
    in_specs = [row(D), row(ROUTER_LANES), row(D), modc(5),
                pl.BlockSpec((1, E_PER_GROUP, D, 2 * E_HIDDEN), lambda b, i, g: (layer, g, 0, 0)),
                pl.BlockSpec((1, E_PER_GROUP, E_HIDDEN, D), lambda b, i, g: (layer, g, 0, 0)),
                pl.BlockSpec(tri.shape, lambda b, i, g: (0, 0))]
    args = [h2, gate, x1, mod, wgu, wd, tri]
    if final is not None:
        modf, gfin = final
        in_specs += [modc(0), modc(1), pl.BlockSpec(gfin.shape, lambda b, i, g: (0, 0))]
        args += [modf, modf, gfin]
    return pl.pallas_call(
        functools.partial(_moe_sorted_kernel, nb=nb, r=r, final=final is not None),
        grid=(B // nb, S // r, E_GROUPS),
        in_specs=in_specs,
        out_specs=row(D),
        out_shape=jax.ShapeDtypeStruct((B, S, D), F32),
        scratch_shapes=[pltpu.VMEM((rc, D + 2 * ROUTER_LANES), BF16), pltpu.VMEM((rc, D), BF16),
                        pltpu.VMEM((R, rc), BF16), pltpu.SMEM((2 * E_GROUPS,), jnp.int32)],
        compiler_params=_cparams("arbitrary", "arbitrary", "arbitrary"),
        name="moe_sorted",
    )(*args)


def _rope_tables(pos):
    half = B_ROPE // 2
    inv = 1.0 / (ROPE_THETA ** (jnp.arange(half, dtype=F32) / half))
    ang = pos.astype(F32)[:, None] * inv[None, :]
    cos, sin = jnp.cos(ang), jnp.sin(ang)
    return (jnp.tile(jnp.concatenate([cos, cos], axis=1), (1, B_HEADS)),
            jnp.tile(jnp.concatenate([-sin, sin], axis=1), (1, B_HEADS)))


def _swap_halves(w):
    half = w.shape[-1] // 2
    return jnp.concatenate([w[..., half:], w[..., :half]], axis=-1)


def _layer_weights(l, w_in, b_g_q, b_w_uq, b_g_kv, b_w_uk, b_w_uv, w_out, moe_w_rg, moe_b_rg, moe_w_re, moe_b_re):
    D = w_in.shape[1]
    w = w_in[l]
    o_cq = 3 * A_WIDTH
    o_ckv = o_cq + B_Q_LORA
    o_kpe = o_ckv + B_KV_LORA
    o_z = o_kpe + B_ROPE
    o_xbc = o_z + C_D_INNER
    o_dt = o_xbc + C_CONV_DIM
    wkpe = w[:, o_kpe:o_z]
    zc = lambda n: jnp.zeros((D, n), F32)
    w1 = jnp.concatenate([w[:, 0:o_cq], w[:, o_cq:o_ckv], zc(256 - B_Q_LORA), w[:, o_ckv:o_kpe],
                          wkpe, _swap_halves(wkpe), w[:, o_dt:o_dt + C_HEADS],
                          zc(LANES - 2 * B_ROPE - C_HEADS), w[:, o_z:o_xbc], w[:, o_xbc:o_dt]],
                         axis=1).astype(BF16)
    assert w1.shape[1] == _W1_N
    gq = jnp.pad(b_g_q[l], (0, 256 - B_Q_LORA))[None, :]
    uq = b_w_uq[l]
    pe = uq[:, :, B_NOPE:]
    wuq = jnp.concatenate([uq[:, :, :B_NOPE].reshape(B_Q_LORA, -1), pe.reshape(B_Q_LORA, -1),
                           _swap_halves(pe).reshape(B_Q_LORA, -1)], axis=1)
    wuq = jnp.pad(wuq, ((0, 256 - B_Q_LORA), (0, 0))).astype(BF16)
    eye = jnp.eye(B_HEADS, dtype=F32)
    bd = (jnp.transpose(b_w_uk[l], (1, 2, 0))[:, :, None, :] * eye[:, None, :, None]).reshape(
        B_HEADS * B_NOPE, B_HEADS * B_KV_LORA).astype(BF16)
    wuv = (jnp.transpose(b_w_uv[l], (1, 0, 2))[:, :, None, :] * eye[:, None, :, None]).reshape(
        B_HEADS * B_KV_LORA, B_WIDTH).astype(BF16)
    wr = jnp.zeros((D, ROUTER_LANES), F32)
    wr = wr.at[:, 0:E_GROUPS].set(moe_w_rg[l]).at[:, ROUTER_E_OFF:ROUTER_E_OFF + N_EXPERTS].set(moe_w_re[l])
    br = jnp.zeros((1, ROUTER_LANES), F32)
    br = br.at[0, 0:E_GROUPS].set(moe_b_rg[l]).at[0, ROUTER_E_OFF:ROUTER_E_OFF + N_EXPERTS].set(moe_b_re[l])
    return dict(w1=w1, gq=gq, wuq=wuq, bd=bd, gkv=b_g_kv[l][None, :], wuv=wuv, wo=w_out[l].astype(BF16), wr=wr, br=br)


def _band_bias(rel_bias, nq, nk, back):
    n = nk + nq - 1
    idx = np.clip(back + nq - 1 - np.arange(n + 1), -A_REL_CLIP, A_REL_CLIP) + A_REL_CLIP
    line = rel_bias[:, idx]
    skew = jnp.tile(line, (1, nq))[:, 0:nq * n].reshape(A_HEADS, nq, n)
    return skew[:, :, nq - 1:nq - 1 + nk].reshape(A_HEADS * nq, nk)


def _layer(x, mod, pos, lw, l, cache, prm, final, *, nb, r):
    B, S, D = x.shape
    cos_t, sin_t = _rope_tables(pos)
    (qkv, knew, vnew, qmla, kmla, ckv, kpe, z, xbc, dtr) = _inproj(
        x, mod, prm['g_mix'], lw['w1'], lw['gq'], lw['wuq'], lw['bd'], lw['gkv'], cos_t, sin_t, nb=nb, r=r,
        nparts=1)
    if cache is None:
        ya = _band_prompt(qkv, _band_bias(prm['a_rel_bias'], CHUNK, A_BAND, A_WINDOW))
        yb = _mla_prompt(qmla, kmla, lw['wuv'], bq=min(512, S), bk=1024)
        prefix = jnp.zeros((B, C_CONV_W - 1, C_CONV_DIM), F32)
        h0t = jnp.zeros((B, C_HEADS, C_D_STATE, C_HEAD_DIM), F32)
        lt, lc, nbat = min(S, 512), SSD_CHUNK, B
    else:
        nr = cache['a_kt'].shape[3]
        ya = _band_sample(qkv, cache['a_kt'], cache['a_vt'], l, _band_bias(prm['a_rel_bias'], S, nr + S, nr))
        yb = _mla_sample(qmla, kmla, cache['b_ckv'], cache['b_kpet'], l, lw['wuv'])
        prefix = cache['c_conv'][l]
        h0t = jnp.swapaxes(cache['c_ssm'][l], -1, -2)
        lt = lc = min(SSD_CHUNK, S)
        nbat = math.gcd(B, 4)
    yc, hft = _ssd(xbc, z, dtr, prefix, h0t, prm['c_conv_w'], prm['c_conv_b'], prm['c_dt_bias'], prm['c_a_log'],
                   prm['c_d_exp'], prm['c_g_norm'], nbat=nbat, lt=lt, lc=lc)
    conv_state = jnp.concatenate([prefix, xbc], axis=1)[:, -(C_CONV_W - 1):]
    x1, h2, gate = _outproj(ya, yb, yc, x, mod, prm['g_ffn'], lw['wo'], lw['wr'], lw['br'], nb=nb, r=r,
                            nparts=2)
    xo = _moe_sorted(h2, gate, x1, mod, prm['moe_w_gu'], prm['moe_w_down'], l, final, nb=nb, r=r)
    states = (knew.reshape(B, -1, A_HEADS, A_HEAD_DIM), vnew.reshape(B, -1, A_HEADS, A_HEAD_DIM),
              ckv, kpe, conv_state, jnp.swapaxes(hft, -1, -2))
    return xo, states


def kernel(x_prompt, x_sample, c_prompt, c_sample, cache_a_k, cache_a_v, cache_b_ckv, cache_b_kpe,
           state_c_conv, state_c_ssm, w_ada, b_ada, g_mix, w_in, a_rel_bias, b_g_q, b_w_uq, b_g_kv,
           b_w_uk, b_w_uv, c_conv_w, c_conv_b, c_dt_bias, c_a_log, c_d, c_g_norm, w_out, g_ffn,
           moe_w_rg, moe_b_rg, moe_w_re, moe_b_re, moe_w_gu, moe_w_down, g_final, w_ada_f, b_ada_f):
    depth = w_in.shape[0]
    D = x_prompt.shape[-1]
    Bp, Sp, _ = x_prompt.shape
    Bs, Ss, _ = x_sample.shape
    past = cache_b_ckv.shape[2]
    pos_p = jnp.arange(Sp)
    pos_s = past + jnp.arange(Ss)

    nc = Bp + Bs
    ncp = -(-nc // 8) * 8
    c_all = jnp.pad(jnp.concatenate([c_prompt, c_sample], axis=0), ((0, ncp - nc), (0, 0)))
    mod = _ada(c_all, w_ada, b_ada[:, None, :])
    modf = _ada(c_all, w_ada_f[None], b_ada_f[None, None, :])[0]

    nr = cache_a_k.shape[2]
    to_t = lambda c: jnp.transpose(c, (0, 1, 3, 4, 2)).reshape(depth, Bs, A_WIDTH, nr)
    cache = {'a_kt': to_t(cache_a_k), 'a_vt': to_t(cache_a_v), 'b_ckv': cache_b_ckv,
             'b_kpet': jnp.swapaxes(cache_b_kpe, 2, 3), 'c_conv': state_c_conv, 'c_ssm': state_c_ssm}

    rp = min(Sp, A_WINDOW)
    moe_w_gu = moe_w_gu.astype(BF16)
    moe_w_down = moe_w_down.astype(BF16)
    xp, xs = x_prompt, x_sample
    st_p, st_s = [], []
    for l in range(depth):
        lw = _layer_weights(l, w_in, b_g_q, b_w_uq, b_g_kv, b_w_uk, b_w_uv, w_out,
                            moe_w_rg, moe_b_rg, moe_w_re, moe_b_re)
        prm = {'g_mix': g_mix[l][None, :], 'a_rel_bias': a_rel_bias[l], 'c_conv_w': c_conv_w[l],
               'c_conv_b': c_conv_b[l][None, :], 'c_dt_bias': jnp.pad(c_dt_bias[l], (0, LANES - C_HEADS))[None, :],
               'c_a_log': jnp.pad(c_a_log[l], (0, LANES - C_HEADS))[None, :],
               'c_d_exp': jnp.repeat(c_d[l], C_HEAD_DIM)[None, :], 'c_g_norm': c_g_norm[l][None, :],
               'g_ffn': g_ffn[l][None, :], 'moe_w_gu': moe_w_gu, 'moe_w_down': moe_w_down}
        fin_p = (modf[0:Bp][:, None, :], g_final[None, :]) if l == depth - 1 else None
        fin_s = (modf[Bp:nc][:, None, :], g_final[None, :]) if l == depth - 1 else None
        xp, sp = _layer(xp, mod[l, 0:Bp][:, None, :], pos_p, lw, l, None, prm, fin_p, nb=1, r=rp)
        xs, ss = _layer(xs, mod[l, Bp:nc][:, None, :], pos_s, lw, l, cache, prm, fin_s, nb=Bs, r=Ss)
        st_p.append(sp)
        st_s.append(ss)
    stack = lambda st, k: jnp.stack([t[k] for t in st])
    return ((xp, xs) + tuple(stack(st_p, k) for k in range(6))
            + tuple(stack(st_s, k) for k in range(6)))
```

```python
import functools
import math

import numpy as np
import jax
import jax.numpy as jnp
from jax import lax
from jax.experimental import pallas as pl
from jax.experimental.pallas import tpu as pltpu

F32 = jnp.float32
BF16 = jnp.bfloat16

D_MODEL = 1024
CHUNK = 64
EPS = 1e-6
NEG_INF = -1e30

A_HEADS = 4
A_HEAD_DIM = 64
A_WIDTH = A_HEADS * A_HEAD_DIM
A_BACK = 8
A_BAND = (A_BACK + 1) * CHUNK
A_WINDOW = A_BACK * CHUNK
A_REL_CLIP = 128
A_SCALE = A_HEAD_DIM ** -0.5

B_HEADS = 4
B_Q_LORA = 192
B_KV_LORA = 128
B_NOPE = 64
B_ROPE = 32
B_V = 64
B_WIDTH = B_HEADS * B_V
B_SCALE = (B_NOPE + B_ROPE) ** -0.5
B_QK = B_KV_LORA + B_ROPE
ROPE_THETA = 10000.0

C_D_INNER = 512
C_HEAD_DIM = 64
C_HEADS = C_D_INNER // C_HEAD_DIM
C_GROUPS = 2
C_D_STATE = 64
C_CONV_W = 4
C_CONV_DIM = C_D_INNER + 2 * C_GROUPS * C_D_STATE
C_GN = C_GROUPS * C_D_STATE
C_HPG = C_HEADS // C_GROUPS
SSD_CHUNK = 64

E_GROUPS = 4
E_PER_GROUP = 4
N_EXPERTS = E_GROUPS * E_PER_GROUP
E_HIDDEN = 256
ROUTER_LANES = 128
ROUTER_E_OFF = 16

LANES = 128
VMEM_LIMIT = 56 * 1024 * 1024

LOG2E = math.log2(math.e)
B_QSCALE = B_SCALE * LOG2E
A_QSCALE = A_SCALE * LOG2E

_W1_QKV = 0
_W1_CQ = 3 * A_WIDTH
_W1_CKV = _W1_CQ + 256
_W1_KPE = _W1_CKV + B_KV_LORA
_W1_Z = _W1_KPE + LANES
_W1_XBC = _W1_Z + C_D_INNER
_W1_N = _W1_XBC + C_CONV_DIM


def _cparams(*sem):
    return pltpu.CompilerParams(dimension_semantics=sem, vmem_limit_bytes=VMEM_LIMIT)


def _silu(v):
    return v * jax.nn.sigmoid(v)


def _split3_lanes(v, lane):
    x3 = v + pltpu.roll(v, C_HEADS, axis=1) + pltpu.roll(v, 2 * C_HEADS, axis=1)
    hi = x3.astype(BF16).astype(F32)
    r1 = x3 - hi
    mid = r1.astype(BF16).astype(F32)
    lo = r1 - mid
    return jnp.where(lane < C_HEADS, hi, jnp.where(lane < 2 * C_HEADS, mid, lo)).astype(BF16)


def _ada_kernel(c_ref, w_ref, b_ref, o_ref):
    s = _silu(c_ref[...]).astype(BF16)
    o_ref[0] = jnp.dot(s, w_ref[0].astype(BF16), preferred_element_type=F32) + b_ref[0]


def _ada(c_all, w, b):
    nl, d, n = w.shape
    bp = c_all.shape[0]
    tn = 1024
    return pl.pallas_call(
        _ada_kernel,
        grid=(nl, n // tn),
        in_specs=[pl.BlockSpec((bp, d), lambda l, j: (0, 0)),
                  pl.BlockSpec((1, d, tn), lambda l, j: (l, 0, j)),
                  pl.BlockSpec((1, 1, tn), lambda l, j: (l, 0, j))],
        out_specs=pl.BlockSpec((1, bp, tn), lambda l, j: (l, 0, j)),
        out_shape=jax.ShapeDtypeStruct((nl, bp, n), F32),
        compiler_params=_cparams("arbitrary", "arbitrary"),
        name="ada",
    )(c_all, w, b)


def _rows(ref_or_val, nb, r):
    v = ref_or_val[...]
    return jnp.broadcast_to(v, (nb, r, v.shape[-1])).reshape(nb * r, v.shape[-1])


def _inproj_kernel(x_ref, sh_ref, sc_ref, g_ref, w1_ref, gq_ref, wuq_ref, bd_ref, gkv_ref, cos_ref, sin_ref,
                   qkv_ref, knew_ref, vnew_ref, qmla_ref, kmla_ref, ckv_ref, kpe_ref, z_ref, xbc_ref, dtr_ref,
                   *, nb, r, keep_tile, nparts):
    D = x_ref.shape[-1]
    pnb, pr = (nb // nparts, r) if nb > 1 else (1, r // nparts)
    PR = pnb * pr

    def sel(s):
        return (slice(s * pnb, (s + 1) * pnb), slice(None)) if nb > 1 else (slice(None), slice(s * pr, (s + 1) * pr))

    def per_batch(ref, s):
        v = ref[sel(s)[0]] if nb > 1 else ref[...]
        return jnp.broadcast_to(v, (pnb, pr, v.shape[-1])).reshape(PR, v.shape[-1])

    def per_pos(ref, s):
        v = ref[...] if nb > 1 else ref[sel(s)[1]]
        return jnp.broadcast_to(v, (pnb, pr, v.shape[-1])).reshape(PR, v.shape[-1])

    def project(s):
        bs, rs = sel(s)
        x = x_ref[bs, rs].reshape(PR, D)
        h = x * lax.rsqrt(jnp.mean(x * x, axis=-1, keepdims=True) + EPS) * g_ref[...]
        h = h * (1.0 + per_batch(sc_ref, s)) + per_batch(sh_ref, s)
        return jnp.dot(h.astype(BF16), w1_ref[...], preferred_element_type=F32)

    def finish(s, u):
        bs, rs = sel(s)

        def out3(v):
            return v.reshape(pnb, pr, v.shape[-1])

        qkv_ref[bs, rs, 0:A_WIDTH] = out3((u[:, 0:A_WIDTH] * A_QSCALE).astype(BF16))
        qkv_ref[bs, rs, A_WIDTH:3 * A_WIDTH] = out3(u[:, A_WIDTH:3 * A_WIDTH].astype(BF16))

        @pl.when(pl.program_id(1) == keep_tile)
        def _():
            knew_ref[bs, rs] = out3(u[:, A_WIDTH:2 * A_WIDTH])
            vnew_ref[bs, rs] = out3(u[:, 2 * A_WIDTH:3 * A_WIDTH])

        blk = u[:, _W1_CQ:_W1_CQ + 256]
        lane = lax.broadcasted_iota(jnp.int32, (1, 256), 1)
        cq = jnp.where(lane < B_Q_LORA, blk, 0.0)
        cqn = cq * lax.rsqrt(jnp.sum(cq * cq, axis=-1, keepdims=True) * (1.0 / B_Q_LORA) + EPS) * gq_ref[...]
        qb = jnp.dot(cqn.astype(BF16), wuq_ref[...], preferred_element_type=F32)
        nope_w = B_HEADS * B_NOPE
        rope_w = B_HEADS * B_ROPE
        qlat = jnp.dot(qb[:, 0:nope_w].astype(BF16), bd_ref[...], preferred_element_type=F32) * B_QSCALE
        cosr = per_pos(cos_ref, s)
        sinr = per_pos(sin_ref, s)
        qpe = (qb[:, nope_w:nope_w + rope_w] * cosr
               + qb[:, nope_w + rope_w:nope_w + 2 * rope_w] * sinr) * B_QSCALE
        for hd in range(B_HEADS):
            qmla_ref[bs, hd, rs, 0:B_KV_LORA] = out3(qlat[:, hd * B_KV_LORA:(hd + 1) * B_KV_LORA].astype(BF16))
            qmla_ref[bs, hd, rs, B_KV_LORA:B_QK] = out3(qpe[:, hd * B_ROPE:(hd + 1) * B_ROPE].astype(BF16))

        cr = u[:, _W1_CKV:_W1_CKV + B_KV_LORA]
        ckv = cr * lax.rsqrt(jnp.mean(cr * cr, axis=-1, keepdims=True) + EPS) * gkv_ref[...]
        ckv_ref[bs, rs] = out3(ckv)
        kmla_ref[bs, rs, 0:B_KV_LORA] = out3(ckv.astype(BF16))
        kb = u[:, _W1_KPE:_W1_KPE + LANES]
        rot = kb * cosr + pltpu.roll(kb, LANES - B_ROPE, axis=1) * sinr
        kpe_ref[bs, rs] = out3(rot[:, 0:B_ROPE])
        kmla_ref[bs, rs, B_KV_LORA:B_QK] = out3(rot[:, 0:B_ROPE].astype(BF16))
        lane1 = lax.broadcasted_iota(jnp.int32, (1, LANES), 1)
        dtr_ref[bs, rs] = out3(jnp.where(lane1 < C_HEADS, pltpu.roll(kb, LANES - 2 * B_ROPE, axis=1), 0.0))

        z_ref[bs, rs] = out3(u[:, _W1_Z:_W1_Z + C_D_INNER].astype(BF16))
        xbc_ref[bs, rs] = out3(u[:, _W1_XBC:_W1_XBC + C_CONV_DIM])

    us = [project(s) for s in range(nparts)]
    for s in range(nparts):
        finish(s, us[s])


def _inproj(x, mod, g_mix, w1, gq, wuq, bd, gkv, cos_t, sin_t, *, nb, r, nparts):
    B, S, D = x.shape
    nbt, nst = B // nb, S // r
    keep = min(S, A_WINDOW)
    assert keep == r and B % nb == 0 and S % r == 0
    grid = (nbt, nst)
    row = lambda c: pl.BlockSpec((nb, r, c), lambda b, i: (b, i, 0))
    full = lambda a: pl.BlockSpec(a.shape, lambda b, i: (0,) * a.ndim)
    in_specs = [row(D),
                pl.BlockSpec((nb, 1, D), lambda b, i: (b, 0, 0)),
                pl.BlockSpec((nb, 1, D), lambda b, i: (b, 0, 1)),
                full(g_mix), full(w1), full(gq), full(wuq), full(bd), full(gkv),
                pl.BlockSpec((r, LANES), lambda b, i: (i, 0)),
                pl.BlockSpec((r, LANES), lambda b, i: (i, 0))]
    out_shape = [jax.ShapeDtypeStruct((B, S, 3 * A_WIDTH), BF16),
                 jax.ShapeDtypeStruct((B, keep, A_WIDTH), F32),
                 jax.ShapeDtypeStruct((B, keep, A_WIDTH), F32),
                 jax.ShapeDtypeStruct((B, B_HEADS, S, B_QK), BF16),
                 jax.ShapeDtypeStruct((B, S, B_QK), BF16),
                 jax.ShapeDtypeStruct((B, S, B_KV_LORA), F32),
                 jax.ShapeDtypeStruct((B, S, B_ROPE), F32),
                 jax.ShapeDtypeStruct((B, S, C_D_INNER), BF16),
                 jax.ShapeDtypeStruct((B, S, C_CONV_DIM), F32),
                 jax.ShapeDtypeStruct((B, S, LANES), F32)]
    out_specs = [row(3 * A_WIDTH),
                 pl.BlockSpec((nb, keep, A_WIDTH), lambda b, i: (b, 0, 0)),
                 pl.BlockSpec((nb, keep, A_WIDTH), lambda b, i: (b, 0, 0)),
                 pl.BlockSpec((nb, B_HEADS, r, B_QK), lambda b, i: (b, 0, i, 0)),
                 row(B_QK), row(B_KV_LORA), row(B_ROPE), row(C_D_INNER), row(C_CONV_DIM), row(LANES)]
    return pl.pallas_call(
        functools.partial(_inproj_kernel, nb=nb, r=r, keep_tile=nst - 1, nparts=nparts),
        grid=grid, in_specs=in_specs, out_specs=out_specs, out_shape=out_shape,
        compiler_params=_cparams("arbitrary", "arbitrary"),
        name="inproj",
    )(x, mod, mod, g_mix, w1, gq, wuq, bd, gkv, cos_t, sin_t)


def _band_scores(q, k, hmask):
    qs = jnp.where(hmask, jnp.concatenate([q] * A_HEADS, axis=0), jnp.zeros((), BF16))
    return lax.dot_general(qs, k, (((1,), (1,)), ((), ())), preferred_element_type=F32)


def _band_finish(s, v, bias, valid, hmask):
    nq = s.shape[0] // A_HEADS
    s = s + bias
    if valid is not None:
        s = jnp.where(valid, s, NEG_INF)
    m = jnp.max(s, axis=-1, keepdims=True)
    e = jnp.exp2(s - m)
    den = jnp.sum(e, axis=-1, keepdims=True)
    o = jnp.dot(e.astype(BF16), v, preferred_element_type=F32)
    o = jnp.where(hmask, o / den, 0.0)
    y = o[0:nq]
    for hd in range(1, A_HEADS):
        y = y + o[hd * nq:(hd + 1) * nq]
    return y


def _head_mask(nq):
    row = lax.broadcasted_iota(jnp.int32, (A_HEADS * nq, A_WIDTH), 0)
    col = lax.broadcasted_iota(jnp.int32, (A_HEADS * nq, A_WIDTH), 1)
    return (row // nq) == (col // A_HEAD_DIM)


def _band_prompt_kernel(cur_ref, prev_ref, bias_ref, o_ref, kwin, vwin, *, tq):
    i = pl.program_id(1)
    nch = tq // CHUNK
    kwin[0:A_WINDOW] = prev_ref[0, tq - A_WINDOW:tq, A_WIDTH:2 * A_WIDTH]
    kwin[A_WINDOW:A_WINDOW + tq] = cur_ref[0, :, A_WIDTH:2 * A_WIDTH]
    vwin[0:A_WINDOW] = prev_ref[0, tq - A_WINDOW:tq, 2 * A_WIDTH:3 * A_WIDTH]
    vwin[A_WINDOW:A_WINDOW + tq] = cur_ref[0, :, 2 * A_WIDTH:3 * A_WIDTH]
    hmask = _head_mask(CHUNK)
    ucol = lax.broadcasted_iota(jnp.int32, (1, A_BAND), 1)

    def scores(c):
        return _band_scores(cur_ref[0, c * CHUNK:(c + 1) * CHUNK, 0:A_WIDTH],
                            kwin[c * CHUNK:c * CHUNK + A_BAND], hmask)

    def run(first_tile):
        bias2 = bias_ref[...] * LOG2E
        s_next = scores(0)
        for c in range(nch):
            s = s_next
            if c + 1 < nch:
                s_next = scores(c + 1)
            valid = (ucol // CHUNK + (c - A_BACK)) >= 0 if first_tile else None
            y = _band_finish(s, vwin[c * CHUNK:c * CHUNK + A_BAND], bias2, valid, hmask)
            o_ref[0, c * CHUNK:(c + 1) * CHUNK, :] = y.astype(BF16)

    @pl.when(i == 0)
    def _():
        run(True)

    @pl.when(i > 0)
    def _():
        run(False)


def _band_prompt(qkv, bias_stack):
    B, S, _ = qkv.shape
    tq = A_WINDOW
    assert S % tq == 0 and tq // CHUNK >= A_BACK
    return pl.pallas_call(
        functools.partial(_band_prompt_kernel, tq=tq),
        grid=(B, S // tq),
        in_specs=[pl.BlockSpec((1, tq, 3 * A_WIDTH), lambda b, i: (b, i, 0)),
                  pl.BlockSpec((1, tq, 3 * A_WIDTH), lambda b, i: (b, jnp.maximum(i - 1, 0), 0)),
                  pl.BlockSpec(bias_stack.shape, lambda b, i: (0, 0))],
        out_specs=pl.BlockSpec((1, tq, A_WIDTH), lambda b, i: (b, i, 0)),
        out_shape=jax.ShapeDtypeStruct((B, S, A_WIDTH), BF16),
        scratch_shapes=[pltpu.VMEM((A_WINDOW + tq, A_WIDTH), BF16), pltpu.VMEM((A_WINDOW + tq, A_WIDTH), BF16)],
        compiler_params=_cparams("arbitrary", "arbitrary"),
        name="band_prompt",
    )(qkv, qkv, bias_stack)


def _band_sample_kernel(qkv_ref, kt_ref, vt_ref, biasc_ref, biasn_ref, o_ref, *, nbb, ds):
    hmask = _head_mask(ds)
    nt_dims = (((1,), (1,)), ((), ()))
    bias_c = biasc_ref[...] * LOG2E
    bias_n = biasn_ref[...] * LOG2E

    def scores(b):
        qs = jnp.where(hmask, jnp.concatenate([qkv_ref[b, :, 0:A_WIDTH]] * A_HEADS, axis=0), jnp.zeros((), BF16))
        sc = jnp.dot(qs, kt_ref[0, b].astype(BF16), preferred_element_type=F32)
        sn = lax.dot_general(qs, qkv_ref[b, :, A_WIDTH:2 * A_WIDTH], nt_dims, preferred_element_type=F32)
        return sc, sn

    raw = [scores(b) for b in range(nbb)]
    for b in range(nbb):
        vn = qkv_ref[b, :, 2 * A_WIDTH:3 * A_WIDTH]
        sc = raw[b][0] + bias_c
        sn = raw[b][1] + bias_n
        m = jnp.maximum(jnp.max(sc, axis=-1, keepdims=True), jnp.max(sn, axis=-1, keepdims=True))
        ec = jnp.exp2(sc - m)
        en = jnp.exp2(sn - m)
        den = jnp.sum(ec, axis=-1, keepdims=True) + jnp.sum(en, axis=-1, keepdims=True)
        o = lax.dot_general(ec.astype(BF16), vt_ref[0, b].astype(BF16), nt_dims, preferred_element_type=F32)
        o = o + jnp.dot(en.astype(BF16), vn, preferred_element_type=F32)
        o = jnp.where(hmask, o / den, 0.0)
        y = o[0:ds]
        for hd in range(1, A_HEADS):
            y = y + o[hd * ds:(hd + 1) * ds]
        o_ref[b] = y.astype(BF16)


def _band_sample(qkv, cache_kt, cache_vt, layer, bias_stack):
    B, ds, _ = qkv.shape
    nr = cache_kt.shape[3]
    nbb = 4
    assert B % nbb == 0
    bias_c, bias_n = bias_stack[:, 0:nr], bias_stack[:, nr:]
    return pl.pallas_call(
        functools.partial(_band_sample_kernel, nbb=nbb, ds=ds),
        grid=(B // nbb,),
        in_specs=[pl.BlockSpec((nbb, ds, 3 * A_WIDTH), lambda b: (b, 0, 0)),
                  pl.BlockSpec((1, nbb, A_WIDTH, nr), lambda b: (layer, b, 0, 0)),
                  pl.BlockSpec((1, nbb, A_WIDTH, nr), lambda b: (layer, b, 0, 0)),
                  pl.BlockSpec(bias_c.shape, lambda b: (0, 0)),
                  pl.BlockSpec(bias_n.shape, lambda b: (0, 0))],
        out_specs=pl.BlockSpec((nbb, ds, A_WIDTH), lambda b: (b, 0, 0)),
        out_shape=jax.ShapeDtypeStruct((B, ds, A_WIDTH), BF16),
        compiler_params=_cparams("arbitrary"),
        name="band_sample",
    )(qkv, cache_kt, cache_vt, bias_c, bias_n)


def _mla_finish(acc, den, wuv_ref, bq):
    o = acc / den
    o_all = jnp.concatenate([o[hd * bq:(hd + 1) * bq] for hd in range(B_HEADS)], axis=1)
    return jnp.dot(o_all.astype(BF16), wuv_ref[...], preferred_element_type=F32)


def _mla_prompt_kernel(qi_ref, kj_ref, last_ref, q_ref, k_ref, wuv_ref, o_ref, m_sc, acc_sc, *, bq, bk, sk):
    p = pl.program_id(1)
    i = qi_ref[p]
    j = kj_ref[p]
    M = B_HEADS * bq

    @pl.when(j == 0)
    def _():
        m_sc[...] = jnp.full(m_sc.shape, NEG_INF, F32)
        acc_sc[...] = jnp.zeros(acc_sc.shape, F32)

    nt_dims = (((1,), (1,)), ((), ()))
    nsub = bk // sk

    def scores(q, t):
        return lax.dot_general(q, k_ref[0, t * sk:(t + 1) * sk, :], nt_dims, preferred_element_type=F32)

    def absorb(s, t, m_run, acc):
        m_new = jnp.maximum(m_run, jnp.max(s, axis=1, keepdims=True))
        alpha = jnp.exp2(m_run - m_new)
        pr = jnp.exp2(s - jnp.tile(m_new, (1, sk // LANES)))
        v_ext = jnp.concatenate([k_ref[0, t * sk:(t + 1) * sk, 0:B_KV_LORA], jnp.ones((sk, LANES), BF16)], axis=1)
        acc = acc * jnp.concatenate([alpha, alpha], axis=1) + jnp.dot(pr.astype(BF16), v_ext,
                                                                      preferred_element_type=F32)
        return m_new, acc

    @pl.when(last_ref[p] == 0)
    def _():
        q = q_ref[0].reshape(M, B_QK)
        m_run, acc = m_sc[...], acc_sc[...]
        s_next = scores(q, 0)
        for t in range(nsub):
            s = s_next
            if t + 1 < nsub:
                s_next = scores(q, t + 1)
            m_run, acc = absorb(s, t, m_run, acc)
        m_sc[...] = m_run
        acc_sc[...] = acc

    @pl.when(last_ref[p] == 1)
    def _():
        for t in range(nsub):
            @pl.when(j * bk + t * sk <= i * bq + (bq - 1))
            def _():
                s = scores(q_ref[0].reshape(M, B_QK), t)
                row = lax.broadcasted_iota(jnp.int32, (M, sk), 0)
                col = lax.broadcasted_iota(jnp.int32, (M, sk), 1)
                s = jnp.where((col + (j * bk + t * sk)) // CHUNK <= ((row % bq) + i * bq) // CHUNK, s, NEG_INF)
                m_sc[...], acc_sc[...] = absorb(s, t, m_sc[...], acc_sc[...])
        o_ref[0] = _mla_finish(acc_sc[:, 0:B_KV_LORA], acc_sc[:, B_KV_LORA:], wuv_ref, bq).astype(BF16)


def _mla_prompt(qmla, kmla, wuv_bd, *, bq, bk):
    B, S = kmla.shape[0:2]
    bk = min(bk, S)
    assert S % bq == 0 and S % bk == 0 and bq % CHUNK == 0 and bk % bq == 0
    nq = S // bq
    n_kv = [(i * bq + bq - 1) // bk + 1 for i in range(nq)]
    qi = np.concatenate([np.full(n, i, np.int32) for i, n in enumerate(n_kv)])
    kj = np.concatenate([np.arange(n, dtype=np.int32) for n in n_kv])
    last = np.concatenate([(np.arange(n) == n - 1).astype(np.int32) for n in n_kv])
    grid_spec = pltpu.PrefetchScalarGridSpec(
        num_scalar_prefetch=3,
        grid=(B, len(qi)),
        in_specs=[pl.BlockSpec((1, B_HEADS, bq, B_QK), lambda b, p, qi, kj, last: (b, 0, qi[p], 0)),
                  pl.BlockSpec((1, bk, B_QK), lambda b, p, qi, kj, last: (b, kj[p], 0)),
                  pl.BlockSpec(wuv_bd.shape, lambda b, p, qi, kj, last: (0, 0))],
        out_specs=pl.BlockSpec((1, bq, B_WIDTH), lambda b, p, qi, kj, last: (b, qi[p], 0)),
        scratch_shapes=[pltpu.VMEM((B_HEADS * bq, LANES), F32),
                        pltpu.VMEM((B_HEADS * bq, B_KV_LORA + LANES), F32)])
    return pl.pallas_call(
        functools.partial(_mla_prompt_kernel, bq=bq, bk=bk, sk=min(bk, 512)),
        grid_spec=grid_spec,
        out_shape=jax.ShapeDtypeStruct((B, S, B_WIDTH), BF16),
        compiler_params=_cparams("arbitrary", "arbitrary"),
        name="mla_prompt",
    )(jnp.asarray(qi), jnp.asarray(kj), jnp.asarray(last), qmla, kmla, wuv_bd)


def _mla_sample_kernel(q_ref, kn_ref, ckv_ref, kpet_ref, wuv_ref, o_ref, *, ds, nbb):
    M = B_HEADS * ds
    nt_dims = (((1,), (1,)), ((), ()))

    def scores(b):
        q = q_ref[b].reshape(M, B_QK)
        kn = kn_ref[b]
        ckv = ckv_ref[0, b].astype(BF16)
        sc = (lax.dot_general(q[:, 0:B_KV_LORA], ckv, nt_dims, preferred_element_type=F32)
              + jnp.dot(q[:, B_KV_LORA:B_QK], kpet_ref[0, b].astype(BF16), preferred_element_type=F32))
        sn = lax.dot_general(q, kn, nt_dims, preferred_element_type=F32)
        return sc, sn, ckv, kn

    def finish(b, sc, sn, ckv, kn):
        m = jnp.maximum(jnp.max(sc, axis=1, keepdims=True), jnp.max(sn, axis=1, keepdims=True))
        pc = jnp.exp2(sc - m)
        pn = jnp.exp2(sn - m)
        den = jnp.sum(pc, axis=1, keepdims=True) + jnp.sum(pn, axis=1, keepdims=True)
        acc = (jnp.dot(pc.astype(BF16), ckv, preferred_element_type=F32)
               + jnp.dot(pn.astype(BF16), kn[:, 0:B_KV_LORA], preferred_element_type=F32))
        o_ref[b] = _mla_finish(acc, den, wuv_ref, ds).astype(BF16)

    vals = [scores(b) for b in range(nbb)]
    for b in range(nbb):
        finish(b, *vals[b])


def _mla_sample(qmla, kmla, cache_ckv, cache_kpet, layer, wuv_bd):
    B, ds = kmla.shape[0:2]
    past = cache_ckv.shape[2]
    nbb = math.gcd(B, 2)
    return pl.pallas_call(
        functools.partial(_mla_sample_kernel, ds=ds, nbb=nbb),
        grid=(B // nbb,),
        in_specs=[pl.BlockSpec((nbb, B_HEADS, ds, B_QK), lambda b: (b, 0, 0, 0)),
                  pl.BlockSpec((nbb, ds, B_QK), lambda b: (b, 0, 0)),
                  pl.BlockSpec((1, nbb, past, B_KV_LORA), lambda b: (layer, b, 0, 0)),
                  pl.BlockSpec((1, nbb, B_ROPE, past), lambda b: (layer, b, 0, 0)),
                  pl.BlockSpec(wuv_bd.shape, lambda b: (0, 0))],
        out_specs=pl.BlockSpec((nbb, ds, B_WIDTH), lambda b: (b, 0, 0)),
        out_shape=jax.ShapeDtypeStruct((B, ds, B_WIDTH), BF16),
        compiler_params=_cparams("arbitrary"),
        name="mla_sample",
    )(qmla, kmla, cache_ckv, cache_kpet, wuv_bd)


def _ssd_kernel(xbc_ref, z_ref, dtr_ref, pre_ref, h0_ref, cw_ref, cb_ref, dtb_ref, alog_ref, dsk_ref, gn_ref,
                es_ref, ep_ref, tri_ref, y_ref, hout_ref,
                cbuf, xs_sc, b_sc, c_sc, ces_sc, cep_sc, dep_sc, st_sc, *, nbat, lt, lc, nt):
    t = pl.program_id(1)
    nch = lt // lc
    HS = C_HEADS * lc
    gw = C_HPG * C_HEAD_DIM
    same_exp = lc == C_HEAD_DIM

    @pl.when(t == 0)
    def _():
        st_sc[...] = jnp.zeros(st_sc.shape, F32)
        for b in range(nbat):
            cbuf[b, 8 - (C_CONV_W - 1):8] = pre_ref[b]
            for g in range(C_GROUPS):
                blk = jnp.concatenate([h0_ref[b, g * C_HPG + hl] for hl in range(C_HPG)], axis=1)
                st_sc[b, g * C_D_STATE:(g + 1) * C_D_STATE, g * gw:(g + 1) * gw] = blk

    hl = lax.broadcasted_iota(jnp.int32, (lt, LANES), 1)
    for b in range(nbat):
        cbuf[b, 8:8 + lt] = xbc_ref[b]
        acc = jnp.broadcast_to(cb_ref[...], (lt, C_CONV_DIM))
        for kk in range(C_CONV_W):
            off = 8 - (C_CONV_W - 1) + kk
            acc = acc + cbuf[b, off:off + lt] * cw_ref[kk:kk + 1, :]
        tail = cbuf[b, 8 + lt - (C_CONV_W - 1):8 + lt]
        cbuf[b, 8 - (C_CONV_W - 1):8] = tail
        xc = _silu(acc)
        xs_sc[b] = xc[:, 0:C_D_INNER]
        b_sc[b] = xc[:, C_D_INNER:C_D_INNER + C_GN]
        c_sc[b] = xc[:, C_D_INNER + C_GN:C_D_INNER + 2 * C_GN]

        dt = jnp.where(hl < C_HEADS, jax.nn.softplus(dtr_ref[b] + dtb_ref[...]), 0.0)
        da = dt * (-jnp.exp(alog_ref[...]))
        c3 = jnp.dot(tri_ref[...], _split3_lanes(da, hl), preferred_element_type=F32)
        cum = jnp.where(hl < C_HEADS, c3 + pltpu.roll(c3, LANES - C_HEADS, axis=1)
                        + pltpu.roll(c3, LANES - 2 * C_HEADS, axis=1), 0.0)
        cum3 = _split3_lanes(cum, hl)
        cep = jnp.dot(cum3, ep_ref[...], preferred_element_type=F32)
        cep_sc[b] = cep
        if not same_exp:
            ces_sc[b] = jnp.dot(cum3, es_ref[...], preferred_element_type=F32)
        dep_sc[b] = jnp.dot(_split3_lanes(dt, hl), ep_ref[...], preferred_element_type=F32)

    r_s = lax.broadcasted_iota(jnp.int32, (lc, HS), 0)
    l_s = lax.broadcasted_iota(jnp.int32, (lc, HS), 1)
    eye_t = (l_s % lc) == r_s
    causal_t = (l_s % lc) <= r_s
    r_w = lax.broadcasted_iota(jnp.int32, (HS, C_GN), 0)
    l_w = lax.broadcasted_iota(jnp.int32, (HS, C_GN), 1)
    gmask = (r_w // (C_HPG * lc)) == (l_w // C_D_STATE)
    r_b =lax.broadcasted_iota(jnp.int32, (C_HPG * lc, gw), 0)
    l_b = lax.broadcasted_iota(jnp.int32, (C_HPG * lc, gw), 1)
    bmask = (r_b // lc) == (l_b // C_HEAD_DIM)
    r_g = lax.broadcasted_iota(jnp.int32, (C_GN, C_D_INNER), 0)
    l_g = lax.broadcasted_iota(jnp.int32, (C_GN, C_D_INNER), 1)
    smask = (r_g // C_D_STATE) == (l_g // gw)

    def stage_cb(b, rows):
        cm = c_sc[b, rows, :].astype(BF16)
        bm = b_sc[b, rows, :]
        w_nt = jnp.where(gmask, jnp.concatenate([bm] * C_HEADS, axis=0), 0.0).astype(BF16)
        cbcat = lax.dot_general(cm, w_nt, (((1,), (1,)), ((), ())), preferred_element_type=F32)
        st = st_sc[b]
        yoff = jnp.dot(cm, st.astype(BF16), preferred_element_type=F32)
        return dict(bm=bm, cbcat=cbcat, st=st, yoff=yoff)

    def stage_diag(b, rows, v):
        ce_p = cep_sc[b, rows, :]
        ce_s = ce_p if same_exp else ces_sc[b, rows, :]
        xs = xs_sc[b, rows, :]
        rflat = jnp.sum(jnp.where(eye_t, ce_s, 0.0), axis=0, keepdims=True)
        lcat = jnp.exp(jnp.where(causal_t, ce_s - rflat, NEG_INF))
        mcat = (v['cbcat'] * lcat).astype(BF16)
        xdt = xs * dep_sc[b, rows, :]
        xdt_b = xdt.astype(BF16)
        ydiag = []
        for g in range(C_GROUPS):
            bdx = jnp.where(bmask, jnp.concatenate([xdt_b[:, g * gw:(g + 1) * gw]] * C_HPG, axis=0),
                            jnp.zeros((), BF16))
            ydiag.append(jnp.dot(mcat[:, g * C_HPG * lc:(g + 1) * C_HPG * lc], bdx, preferred_element_type=F32))
        v.update(ce_p=ce_p, xs=xs, xdt=xdt, ydiag=jnp.concatenate(ydiag, axis=1))

    def stage_state(b, rows, v):
        ce_p = v['ce_p']
        last = ce_p[lc - 1:lc, :]
        xw = (v['xdt'] * jnp.exp(last - ce_p)).astype(BF16)
        upd = lax.dot_general(v['bm'].astype(BF16), xw, (((0,), (0,)), ((), ())), preferred_element_type=F32)
        st_sc[b] = jnp.exp(last) * v['st'] + jnp.where(smask, upd, 0.0)

    def stage_out(b, rows, v):
        y = v['ydiag'] + v['yoff'] * jnp.exp(v['ce_p']) + dsk_ref[...] * v['xs']
        y = y * _silu(z_ref[b, rows, :].astype(F32))
        y = y * lax.rsqrt(jnp.mean(y * y, axis=-1, keepdims=True) + EPS) * gn_ref[...]
        y_ref[b, rows, :] = y.astype(BF16)

    def chunk(c, carry):
        rows = pl.ds(pl.multiple_of(c * lc, lc), lc)
        vals = [stage_cb(b, rows) for b in range(nbat)]
        for stage in (stage_diag, stage_state, stage_out):
            for b in range(nbat):
                stage(b, rows, vals[b])
        return carry

    lax.fori_loop(0, nch, chunk, 0)

    @pl.when(t == nt - 1)
    def _():
        for b in range(nbat):
            for hd in range(C_HEADS):
                g = hd // C_HPG
                hout_ref[b, hd] = st_sc[b, g * C_D_STATE:(g + 1) * C_D_STATE,
                                        hd * C_HEAD_DIM:(hd + 1) * C_HEAD_DIM]


def _ssd(xbc, z, dtr, prefix, h0t, cw, cb, dtb, alog, dsk, gn, *, nbat, lt, lc):
    B, S, _ = xbc.shape
    nt = S // lt
    assert S % lt == 0 and lt % lc == 0 and B % nbat == 0
    hs = C_HEADS * lc

    def expand(width, per_head):
        e = np.zeros((LANES, width), np.float32)
        for part in range(3):
            e[part * C_HEADS:(part + 1) * C_HEADS] = (np.arange(width)[None, :] // per_head
                                                      == np.arange(C_HEADS)[:, None])
        return jnp.asarray(e, dtype=BF16)

    es, ep = expand(hs, lc), expand(C_D_INNER, C_HEAD_DIM)
    pos = np.arange(lt)
    tri = jnp.asarray((pos[None, :] <= pos[:, None]) & (pos[None, :] // lc == pos[:, None] // lc), dtype=BF16)
    row = lambda c: pl.BlockSpec((nbat, lt, c), lambda b, t: (b, t, 0))
    per_b = lambda a: pl.BlockSpec((nbat,) + a.shape[1:], lambda b, t: (b,) + (0,) * (a.ndim - 1))
    full = lambda a: pl.BlockSpec(a.shape, lambda b, t: (0,) * a.ndim)
    sc = lambda *shape: pltpu.VMEM((nbat,) + shape, F32)
    return pl.pallas_call(
        functools.partial(_ssd_kernel, nbat=nbat, lt=lt, lc=lc, nt=nt),
        grid=(B // nbat, nt),
        in_specs=[row(C_CONV_DIM), row(C_D_INNER), row(LANES), per_b(prefix), per_b(h0t),
                  full(cw), full(cb), full(dtb), full(alog), full(dsk), full(gn), full(es), full(ep), full(tri)],
        out_specs=[row(C_D_INNER), per_b(h0t)],
        out_shape=[jax.ShapeDtypeStruct((B, S, C_D_INNER), BF16),
                   jax.ShapeDtypeStruct(h0t.shape, F32)],
        scratch_shapes=[sc(lt + 8, C_CONV_DIM), sc(lt, C_D_INNER), sc(lt, C_GN), sc(lt, C_GN),
                        sc(lt, hs) if lc != C_HEAD_DIM else sc(8, LANES),
                        sc(lt, C_D_INNER), sc(lt, C_D_INNER), sc(C_GN, C_D_INNER)],
        compiler_params=_cparams("arbitrary", "arbitrary"),
        name="ssd",
    )(xbc, z, dtr, prefix, h0t, cw, cb, dtb, alog, dsk, gn, es, ep, tri)


def _outproj_kernel(ya_ref, yb_ref, yc_ref, x_ref, g1_ref, sh_ref, sc_ref, gf_ref, wo_ref, wr_ref, br_ref,
                    x1_ref, h2_ref, gate_ref, *, nb, r, nparts):
    D = x_ref.shape[-1]
    pnb, pr = (nb // nparts, r) if nb > 1 else (1, r // nparts)
    PR = pnb * pr

    def sel_part(s):
        return (slice(s * pnb, (s + 1) * pnb), slice(None)) if nb > 1 else (slice(None), slice(s * pr, (s + 1) * pr))

    def in2(ref, s):
        bs, rs = sel_part(s)
        return ref[bs, rs].reshape(PR, ref.shape[-1])

    def per_batch(ref, s):
        v = ref[sel_part(s)[0]] if nb > 1 else ref[...]
        return jnp.broadcast_to(v, (pnb, pr, v.shape[-1])).reshape(PR, v.shape[-1])

    def project(s):
        o = jnp.dot(in2(ya_ref, s), wo_ref[0:A_WIDTH, :], preferred_element_type=F32)
        o = o + jnp.dot(in2(yb_ref, s), wo_ref[A_WIDTH:A_WIDTH + B_WIDTH, :], preferred_element_type=F32)
        return o + jnp.dot(in2(yc_ref, s), wo_ref[A_WIDTH + B_WIDTH:, :], preferred_element_type=F32)

    wr = wr_ref[...]
    w_hi = wr.astype(BF16)
    w_lo = (wr - w_hi.astype(F32)).astype(BF16)

    def logits(s, o):
        bs, rs = sel_part(s)
        x1 = in2(x_ref, s) + per_batch(g1_ref, s) * o
        x1_ref[bs, rs] = x1.reshape(pnb, pr, D)
        h2 = x1 * lax.rsqrt(jnp.mean(x1 * x1, axis=-1, keepdims=True) + EPS) * gf_ref[...]
        h2 = h2 * (1.0 + per_batch(sc_ref, s)) + per_batch(sh_ref, s)
        h_hi = h2.astype(BF16)
        h2_ref[bs, rs] = h_hi.reshape(pnb, pr, D)
        h_lo = (h2 - h_hi.astype(F32)).astype(BF16)
        return (jnp.dot(h_hi, w_hi, preferred_element_type=F32) + jnp.dot(h_hi, w_lo, preferred_element_type=F32)
                + jnp.dot(h_lo, w_hi, preferred_element_type=F32) + br_ref[...])

    def route(s, lg):
        bs, rs = sel_part(s)
        lane = lax.broadcasted_iota(jnp.int32, (PR, ROUTER_LANES), 1)
        big = jnp.int32(ROUTER_LANES)
        is_g = lane < E_GROUPS
        gl = jnp.where(is_g, lg, NEG_INF)
        gmax = jnp.max(gl, axis=-1, keepdims=True)
        p_top = 1.0 / jnp.sum(jnp.where(is_g, jnp.exp(gl - gmax), 0.0), axis=-1, keepdims=True)
        g_idx = jnp.min(jnp.where(is_g & (gl == gmax), lane, big), axis=-1, keepdims=True)
        e_lane = lane - ROUTER_E_OFF
        sel = (e_lane >= 0) & (e_lane < N_EXPERTS) & ((e_lane // E_PER_GROUP) == g_idx)
        l1 = jnp.where(sel, lg, NEG_INF)
        m1 = jnp.max(l1, axis=-1, keepdims=True)
        i1 = jnp.min(jnp.where(sel & (l1 == m1), lane, big), axis=-1, keepdims=True)
        sel2 = sel & (lane != i1)
        l2 = jnp.where(sel2, lg, NEG_INF)
        m2 = jnp.max(l2, axis=-1, keepdims=True)
        i2 = jnp.min(jnp.where(sel2 & (l2 == m2), lane, big), axis=-1, keepdims=True)
        e2 = jnp.exp(m2 - m1)
        w1 = p_top / (1.0 + e2)
        w2 = p_top * e2 / (1.0 + e2)
        gate = jnp.where(lane == i1, w1, 0.0) + jnp.where(lane == i2, w2, 0.0)
        gate = gate + jnp.where(lane == 0, g_idx.astype(F32), 0.0)
        gate_ref[bs, rs] = gate.reshape(pnb, pr, ROUTER_LANES)

    outs = [project(s) for s in range(nparts)]
    lgs = [logits(s, outs[s]) for s in range(nparts)]
    for s in range(nparts):
        route(s, lgs[s])


def _outproj(ya, yb, yc, x, mod, g_ffn, wo, wr, br, *, nb, r, nparts):
    B, S, D = x.shape
    row = lambda c: pl.BlockSpec((nb, r, c), lambda b, i: (b, i, 0))
    modc = lambda j: pl.BlockSpec((nb, 1, D), lambda b, i: (b, 0, j))
    full = lambda a: pl.BlockSpec(a.shape, lambda b, i: (0,) * a.ndim)
    return pl.pallas_call(
        functools.partial(_outproj_kernel, nb=nb, r=r, nparts=nparts),
        grid=(B // nb, S // r),
        in_specs=[row(A_WIDTH), row(B_WIDTH), row(C_D_INNER), row(D), modc(2), modc(3), modc(4),
                  full(g_ffn), full(wo), full(wr), full(br)],
        out_specs=[row(D), row(D), row(ROUTER_LANES)],
        out_shape=[jax.ShapeDtypeStruct((B, S, D), F32), jax.ShapeDtypeStruct((B, S, D), BF16),
                   jax.ShapeDtypeStruct((B, S, ROUTER_LANES), F32)],
        compiler_params=_cparams("arbitrary", "arbitrary"),
        name="outproj",
    )(ya, yb, yc, x, mod, mod, mod, g_ffn, wo, wr, br)


MOE_BLK = 160


def _moe_sorted_kernel(h_ref, gate_ref, x_ref, g2_ref, wgu_ref, wd_ref, tri_ref, *rest, nb, r, final):
    if final:
        shf_ref, scf_ref, gfin_ref, o_ref, xc_sc, yc_sc, p_sc, meta = rest
    else:
        o_ref, xc_sc, yc_sc, p_sc, meta = rest
    g = pl.program_id(2)
    R = nb * r
    D = x_ref.shape[-1]
    RC = xc_sc.shape[0]

    @pl.when(g == 0)
    def _():
        gate = gate_ref[...].reshape(R, ROUTER_LANES)
        lane = lax.broadcasted_iota(jnp.int32, (R, ROUTER_LANES), 1)
        onehot = jnp.where((lane < E_GROUPS) & (gate[:, 0:1] == lane.astype(F32)), 1.0, 0.0)
        pos = jnp.dot(tri_ref[...], onehot.astype(BF16), preferred_element_type=F32)
        rank = jnp.sum(onehot * pos, axis=1, keepdims=True)
        counts = jnp.sum(onehot, axis=0, keepdims=True)
        lane1 = lax.broadcasted_iota(jnp.int32, (1, ROUTER_LANES), 1)
        base = jnp.int32(0)
        baserow = jnp.zeros((1, ROUTER_LANES), F32)
        for gg in range(E_GROUPS):
            cnt = jnp.sum(jnp.where(lane1 == gg, counts, 0.0)).astype(jnp.int32)
            nblk = (cnt + (MOE_BLK - 1)) // MOE_BLK
            meta[gg] = base
            meta[E_GROUPS + gg] = nblk
            baserow = jnp.where(lane1 == gg, base.astype(F32), baserow)
            base = base + nblk * MOE_BLK
        cpos = jnp.sum(onehot * baserow, axis=1, keepdims=True) + rank
        colc = lax.broadcasted_iota(jnp.int32, (1, RC), 1).astype(F32)
        pm = jnp.where(cpos == colc, 1.0, 0.0).astype(BF16)
        p_sc[...] = pm
        g_hi = gate.astype(BF16)
        g_lo = (gate - g_hi.astype(F32)).astype(BF16)
        xext = jnp.concatenate([h_ref[...].reshape(R, D), g_hi, g_lo], axis=1)
        xc_sc[...] = lax.dot_general(pm, xext, (((0,), (0,)), ((), ())),
                                     preferred_element_type=F32).astype(BF16)
        yc_sc[...] = jnp.zeros(yc_sc.shape, BF16)

    base_g = meta[g]
    nblk_g = meta[E_GROUPS + g]
    lane_b = lax.broadcasted_iota(jnp.int32, (MOE_BLK, ROUTER_LANES), 1)

    def block(bi, carry):
        rows = pl.ds(pl.multiple_of(base_g + bi * MOE_BLK, MOE_BLK), MOE_BLK)
        xg = xc_sc[rows, 0:D]
        gg = (xc_sc[rows, D:D + ROUTER_LANES].astype(F32)
              + xc_sc[rows, D + ROUTER_LANES:D + 2 * ROUTER_LANES].astype(F32))
        acts = []
        for k in range(E_PER_GROUP):
            gu = jnp.dot(xg, wgu_ref[0, k], preferred_element_type=F32)
            gcol = jnp.sum(jnp.where(lane_b == ROUTER_E_OFF + g * E_PER_GROUP + k, gg, 0.0),
                           axis=-1, keepdims=True)
            acts.append((_silu(gu[:, 0:E_HIDDEN]) * gu[:, E_HIDDEN:] * gcol).astype(BF16))
        wd = wd_ref[0].reshape(E_PER_GROUP * E_HIDDEN, D)
        yc_sc[rows, :] = jnp.dot(jnp.concatenate(acts, axis=1), wd, preferred_element_type=F32).astype(BF16)
        return carry

    lax.fori_loop(0, nblk_g, block, 0)

    @pl.when(g == E_GROUPS - 1)
    def _():
        moe = jnp.dot(p_sc[...], yc_sc[...], preferred_element_type=F32)
        xo = x_ref[...].reshape(R, D) + _rows(g2_ref, nb, r) * moe
        if final:
            xo = xo * lax.rsqrt(jnp.mean(xo * xo, axis=-1, keepdims=True) + EPS) * gfin_ref[...]
            xo = xo * (1.0 + _rows(scf_ref, nb, r)) + _rows(shf_ref, nb, r)
        o_ref[...] = xo.reshape(nb, r, D)


def _moe_sorted(h2, gate, x1, mod, wgu, wd, layer, final, *, nb, r):
    B, S, D = x1.shape
    R = nb * r
    rc = R + E_GROUPS * MOE_BLK
    pos = np.arange(R)
    tri = jnp.asarray(pos[None, :] < pos[:, None], dtype=BF16)
    row = lambda c: pl.BlockSpec((nb, r, c), lambda b, i, g: (b, i, 0))
    modc = lambda j: pl.BlockSpec((nb, 1, D), lambda b, i, g: (b, 0, j))
    in_specs = [row(D), row(ROUTER_LANES), row(D), modc(5),
                pl.BlockSpec((1, E_PER_GROUP, D, 2 * E_HIDDEN), lambda b, i, g: (layer, g, 0, 0)),
                pl.BlockSpec((1, E_PER_GROUP, E_HIDDEN, D), lambda b, i, g: (layer, g, 0, 0)),
                pl.BlockSpec(tri.shape, lambda b, i, g: (0, 0))]
    args = [h2, gate, x1, mod, wgu, wd, tri]
    if final is not None:
        modf, gfin = final
        in_specs += [modc(0), modc(1), pl.BlockSpec(gfin.shape, lambda b, i, g: (0, 0))]
        args += [modf, modf, gfin]
    return pl.pallas_call(
        functools.partial(_moe_sorted_kernel, nb=nb, r=r, final=final is not None),
        grid=(B // nb, S // r, E_GROUPS),
        in_specs=in_specs,
        out_specs=row(D),
        out_shape=jax.ShapeDtypeStruct((B, S, D), F32),
        scratch_shapes=[pltpu.VMEM((rc, D + 2 * ROUTER_LANES), BF16), pltpu.VMEM((rc, D), BF16),
                        pltpu.VMEM((R, rc), BF16), pltpu.SMEM((2 * E_GROUPS,), jnp.int32)],
        compiler_params=_cparams("arbitrary", "arbitrary", "arbitrary"),
        name="moe_sorted",
    )(*args)


def _rope_tables(pos):
    half = B_ROPE // 2
    inv = 1.0 / (ROPE_THETA ** (jnp.arange(half, dtype=F32) / half))
    ang = pos.astype(F32)[:, None] * inv[None, :]
    cos, sin = jnp.cos(ang), jnp.sin(ang)
    return (jnp.tile(jnp.concatenate([cos, cos], axis=1), (1, B_HEADS)),
            jnp.tile(jnp.concatenate([-sin, sin], axis=1), (1, B_HEADS)))


def _swap_halves(w):
    half = w.shape[-1] // 2
    return jnp.concatenate([w[..., half:], w[..., :half]], axis=-1)


def _layer_weights(l, w_in, b_g_q, b_w_uq, b_g_kv, b_w_uk, b_w_uv, w_out, moe_w_rg, moe_b_rg, moe_w_re, moe_b_re):
    D = w_in.shape[1]
    w = w_in[l]
    o_cq = 3 * A_WIDTH
    o_ckv = o_cq + B_Q_LORA
    o_kpe = o_ckv + B_KV_LORA
    o_z = o_kpe + B_ROPE
    o_xbc = o_z + C_D_INNER
    o_dt = o_xbc + C_CONV_DIM
    wkpe = w[:, o_kpe:o_z]
    zc = lambda n: jnp.zeros((D, n), F32)
    w1 = jnp.concatenate([w[:, 0:o_cq], w[:, o_cq:o_ckv], zc(256 - B_Q_LORA), w[:, o_ckv:o_kpe],
                          wkpe, _swap_halves(wkpe), w[:, o_dt:o_dt + C_HEADS],
                          zc(LANES - 2 * B_ROPE - C_HEADS), w[:, o_z:o_xbc], w[:, o_xbc:o_dt]],
                         axis=1).astype(BF16)
    assert w1.shape[1] == _W1_N
    gq = jnp.pad(b_g_q[l], (0, 256 - B_Q_LORA))[None, :]
    uq = b_w_uq[l]
    pe = uq[:, :, B_NOPE:]
    wuq = jnp.concatenate([uq[:, :, :B_NOPE].reshape(B_Q_LORA, -1), pe.reshape(B_Q_LORA, -1),
                           _swap_halves(pe).reshape(B_Q_LORA, -1)], axis=1)
    wuq = jnp.pad(wuq, ((0, 256 - B_Q_LORA), (0, 0))).astype(BF16)
    eye = jnp.eye(B_HEADS, dtype=F32)
    bd = (jnp.transpose(b_w_uk[l], (1, 2, 0))[:, :, None, :] * eye[:, None, :, None]).reshape(
        B_HEADS * B_NOPE, B_HEADS * B_KV_LORA).astype(BF16)
    wuv = (jnp.transpose(b_w_uv[l], (1, 0, 2))[:, :, None, :] * eye[:, None, :, None]).reshape(
        B_HEADS * B_KV_LORA, B_WIDTH).astype(BF16)
    wr = jnp.zeros((D, ROUTER_LANES), F32)
    wr = wr.at[:, 0:E_GROUPS].set(moe_w_rg[l]).at[:, ROUTER_E_OFF:ROUTER_E_OFF + N_EXPERTS].set(moe_w_re[l])
    br = jnp.zeros((1, ROUTER_LANES), F32)
    br = br.at[0, 0:E_GROUPS].set(moe_b_rg[l]).at[0, ROUTER_E_OFF:ROUTER_E_OFF + N_EXPERTS].set(moe_b_re[l])
    return dict(w1=w1, gq=gq, wuq=wuq, bd=bd, gkv=b_g_kv[l][None, :], wuv=wuv, wo=w_out[l].astype(BF16), wr=wr, br=br)


def _band_bias(rel_bias, nq, nk, back):
    n = nk + nq - 1
    idx = np.clip(back + nq - 1 - np.arange(n + 1), -A_REL_CLIP, A_REL_CLIP) + A_REL_CLIP
    line = rel_bias[:, idx]
    skew = jnp.tile(line, (1, nq))[:, 0:nq * n].reshape(A_HEADS, nq, n)
    return skew[:, :, nq - 1:nq - 1 + nk].reshape(A_HEADS * nq, nk)


def _layer(x, mod, pos, lw, l, cache, prm, final, *, nb, r):
    B, S, D = x.shape
    cos_t, sin_t = _rope_tables(pos)
    (qkv, knew, vnew, qmla, kmla, ckv, kpe, z, xbc, dtr) = _inproj(
        x, mod, prm['g_mix'], lw['w1'], lw['gq'], lw['wuq'], lw['bd'], lw['gkv'], cos_t, sin_t, nb=nb, r=r,
        nparts=1)
    if cache is None:
        ya = _band_prompt(qkv, _band_bias(prm['a_rel_bias'], CHUNK, A_BAND, A_WINDOW))
        yb = _mla_prompt(qmla, kmla, lw['wuv'], bq=min(512, S), bk=1024)
        prefix = jnp.zeros((B, C_CONV_W - 1, C_CONV_DIM), F32)
        h0t = jnp.zeros((B, C_HEADS, C_D_STATE, C_HEAD_DIM), F32)
        lt, lc, nbat = min(S, 512), SSD_CHUNK, B
    else:
        nr = cache['a_kt'].shape[3]
        ya = _band_sample(qkv, cache['a_kt'], cache['a_vt'], l, _band_bias(prm['a_rel_bias'], S, nr + S, nr))
        yb = _mla_sample(qmla, kmla, cache['b_ckv'], cache['b_kpet'], l, lw['wuv'])
        prefix = cache['c_conv'][l]
        h0t = jnp.swapaxes(cache['c_ssm'][l], -1, -2)
        lt = lc = min(SSD_CHUNK, S)
        nbat = math.gcd(B, 4)
    yc, hft = _ssd(xbc, z, dtr, prefix, h0t, prm['c_conv_w'], prm['c_conv_b'], prm['c_dt_bias'], prm['c_a_log'],
                   prm['c_d_exp'], prm['c_g_norm'], nbat=nbat, lt=lt, lc=lc)
    conv_state = jnp.concatenate([prefix, xbc], axis=1)[:, -(C_CONV_W - 1):]
    x1, h2, gate = _outproj(ya, yb, yc, x, mod, prm['g_ffn'], lw['wo'], lw['wr'], lw['br'], nb=nb, r=r,
                            nparts=2)
    xo = _moe_sorted(h2, gate, x1, mod, prm['moe_w_gu'], prm['moe_w_down'], l, final, nb=nb, r=r)
    states = (knew.reshape(B, -1, A_HEADS, A_HEAD_DIM), vnew.reshape(B, -1, A_HEADS, A_HEAD_DIM),
              ckv, kpe, conv_state, jnp.swapaxes(hft, -1, -2))
    return xo, states


def kernel(x_prompt, x_sample, c_prompt, c_sample, cache_a_k, cache_a_v, cache_b_ckv, cache_b_kpe,
           state_c_conv, state_c_ssm, w_ada, b_ada, g_mix, w_in, a_rel_bias, b_g_q, b_w_uq, b_g_kv,
           b_w_uk, b_w_uv, c_conv_w, c_conv_b, c_dt_bias, c_a_log, c_d, c_g_norm, w_out, g_ffn,
           moe_w_rg, moe_b_rg, moe_w_re, moe_b_re, moe_w_gu, moe_w_down, g_final, w_ada_f, b_ada_f):
    depth = w_in.shape[0]
    D = x_prompt.shape[-1]
    Bp, Sp, _ = x_prompt.shape
    Bs, Ss, _ = x_sample.shape
    past = cache_b_ckv.shape[2]
    pos_p = jnp.arange(Sp)
    pos_s = past + jnp.arange(Ss)

    nc = Bp + Bs
    ncp = -(-nc // 8) * 8
    c_all = jnp.pad(jnp.concatenate([c_prompt, c_sample], axis=0), ((0, ncp - nc), (0, 0)))
    mod = _ada(c_all, w_ada, b_ada[:, None, :])
    modf = _ada(c_all, w_ada_f[None], b_ada_f[None, None, :])[0]

    nr = cache_a_k.shape[2]
    to_t = lambda c: jnp.transpose(c, (0, 1, 3, 4, 2)).reshape(depth, Bs, A_WIDTH, nr)
    cache = {'a_kt': to_t(cache_a_k), 'a_vt': to_t(cache_a_v), 'b_ckv': cache_b_ckv,
             'b_kpet': jnp.swapaxes(cache_b_kpe, 2, 3), 'c_conv': state_c_conv, 'c_ssm': state_c_ssm}

    rp = min(Sp, A_WINDOW)
    moe_w_gu = moe_w_gu.astype(BF16)
    moe_w_down = moe_w_down.astype(BF16)
    xp, xs = x_prompt, x_sample
    st_p, st_s = [], []
    for l in range(depth):
        lw = _layer_weights(l, w_in, b_g_q, b_w_uq, b_g_kv, b_w_uk, b_w_uv, w_out,
                            moe_w_rg, moe_b_rg, moe_w_re, moe_b_re)
        prm = {'g_mix': g_mix[l][None, :], 'a_rel_bias': a_rel_bias[l], 'c_conv_w': c_conv_w[l],
               'c_conv_b': c_conv_b[l][None, :], 'c_dt_bias': jnp.pad(c_dt_bias[l], (0, LANES - C_HEADS))[None, :],
               'c_a_log': jnp.pad(c_a_log[l], (0, LANES - C_HEADS))[None, :],
               'c_d_exp': jnp.repeat(c_d[l], C_HEAD_DIM)[None, :], 'c_g_norm': c_g_norm[l][None, :],
               'g_ffn': g_ffn[l][None, :], 'moe_w_gu': moe_w_gu, 'moe_w_down': moe_w_down}
        fin_p = (modf[0:Bp][:, None, :], g_final[None, :]) if l == depth - 1 else None
        fin_s = (modf[Bp:nc][:, None, :], g_final[None, :]) if l == depth - 1 else None
        xp, sp = _layer(xp, mod[l, 0:Bp][:, None, :], pos_p, lw, l, None, prm, fin_p, nb=1, r=rp)
        xs, ss = _layer(xs, mod[l, Bp:nc][:, None, :], pos_s, lw, l, cache, prm, fin_s, nb=Bs, r=Ss)
        st_p.append(sp)
        st_s.append(ss)
    stack = lambda st, k: jnp.stack([t[k] for t in st])
    return ((xp, xs) + tuple(stack(st_p, k) for k in range(6))
            + tuple(stack(st_s, k) for k in range(6)))
```

```python
import functools
import math

import numpy as np
import jax
import jax.numpy as jnp
from jax import lax
from jax.experimental import pallas as pl
from jax.experimental.pallas import tpu as pltpu

F32 = jnp.float32
BF16 = jnp.bfloat16

D_MODEL = 1024
CHUNK = 64
EPS = 1e-6
NEG_INF = -1e30

A_HEADS = 4
A_HEAD_DIM = 64
A_WIDTH = A_HEADS * A_HEAD_DIM
A_BACK = 8
A_BAND = (A_BACK + 1) * CHUNK
A_WINDOW = A_BACK * CHUNK
A_REL_CLIP = 128
A_SCALE = A_HEAD_DIM ** -0.5

B_HEADS = 4
B_Q_LORA = 192
B_KV_LORA = 128
B_NOPE = 64
B_ROPE = 32
B_V = 64
B_WIDTH = B_HEADS * B_V
B_SCALE = (B_NOPE + B_ROPE) ** -0.5
B_QK = B_KV_LORA + B_ROPE
ROPE_THETA = 10000.0

C_D_INNER = 512
C_HEAD_DIM = 64
C_HEADS = C_D_INNER // C_HEAD_DIM
C_GROUPS = 2
C_D_STATE = 64
C_CONV_W = 4
C_CONV_DIM = C_D_INNER + 2 * C_GROUPS * C_D_STATE
C_GN = C_GROUPS * C_D_STATE
C_HPG = C_HEADS // C_GROUPS
SSD_CHUNK = 64

E_GROUPS = 4
E_PER_GROUP = 4
N_EXPERTS = E_GROUPS * E_PER_GROUP
E_HIDDEN = 256
ROUTER_LANES = 128
ROUTER_E_OFF = 16

LANES = 128
VMEM_LIMIT = 56 * 1024 * 1024

LOG2E = math.log2(math.e)
B_QSCALE = B_SCALE * LOG2E
A_QSCALE = A_SCALE * LOG2E

_W1_QKV = 0
_W1_CQ = 3 * A_WIDTH
_W1_CKV = _W1_CQ + 256
_W1_KPE = _W1_CKV + B_KV_LORA
_W1_Z = _W1_KPE + LANES
_W1_XBC = _W1_Z + C_D_INNER
_W1_N = _W1_XBC + C_CONV_DIM


def _cparams(*sem):
    return pltpu.CompilerParams(dimension_semantics=sem, vmem_limit_bytes=VMEM_LIMIT)


def _silu(v):
    return v * jax.nn.sigmoid(v)


def _split3_lanes(v, lane):
    x3 = v + pltpu.roll(v, C_HEADS, axis=1) + pltpu.roll(v, 2 * C_HEADS, axis=1)
    hi = x3.astype(BF16).astype(F32)
    r1 = x3 - hi
    mid = r1.astype(BF16).astype(F32)
    lo = r1 - mid
    return jnp.where(lane < C_HEADS, hi, jnp.where(lane < 2 * C_HEADS, mid, lo)).astype(BF16)


def _ada_kernel(c_ref, w_ref, b_ref, o_ref):
    s = _silu(c_ref[...]).astype(BF16)
    o_ref[0] = jnp.dot(s, w_ref[0].astype(BF16), preferred_element_type=F32) + b_ref[0]


def _ada(c_all, w, b):
    nl, d, n = w.shape
    bp = c_all.shape[0]
    tn = 1024
    return pl.pallas_call(
        _ada_kernel,
        grid=(nl, n // tn),
        in_specs=[pl.BlockSpec((bp, d), lambda l, j: (0, 0)),
                  pl.BlockSpec((1, d, tn), lambda l, j: (l, 0, j)),
                  pl.BlockSpec((1, 1, tn), lambda l, j: (l, 0, j))],
        out_specs=pl.BlockSpec((1, bp, tn), lambda l, j: (l, 0, j)),
        out_shape=jax.ShapeDtypeStruct((nl, bp, n), F32),
        compiler_params=_cparams("arbitrary", "arbitrary"),
        name="ada",
    )(c_all, w, b)


def _rows(ref_or_val, nb, r):
    v = ref_or_val[...]
    return jnp.broadcast_to(v, (nb, r, v.shape[-1])).reshape(nb * r, v.shape[-1])


def _inproj_kernel(x_ref, sh_ref, sc_ref, g_ref, w1_ref, gq_ref, wuq_ref, bd_ref, gkv_ref, cos_ref, sin_ref,
                   qkv_ref, knew_ref, vnew_ref, qmla_ref, kmla_ref, ckv_ref, kpe_ref, z_ref, xbc_ref, dtr_ref,
                   *, nb, r, keep_tile, nparts):
    D = x_ref.shape[-1]
    pnb, pr = (nb // nparts, r) if nb > 1 else (1, r // nparts)
    PR = pnb * pr

    def sel(s):
        return (slice(s * pnb, (s + 1) * pnb), slice(None)) if nb > 1 else (slice(None), slice(s * pr, (s + 1) * pr))

    def per_batch(ref, s):
        v = ref[sel(s)[0]] if nb > 1 else ref[...]
        return jnp.broadcast_to(v, (pnb, pr, v.shape[-1])).reshape(PR, v.shape[-1])

    def per_pos(ref, s):
        v = ref[...] if nb > 1 else ref[sel(s)[1]]
        return jnp.broadcast_to(v, (pnb, pr, v.shape[-1])).reshape(PR, v.shape[-1])

    def project(s):
        bs, rs = sel(s)
        x = x_ref[bs, rs].reshape(PR, D)
        h = x * lax.rsqrt(jnp.mean(x * x, axis=-1, keepdims=True) + EPS) * g_ref[...]
        h = h * (1.0 + per_batch(sc_ref, s)) + per_batch(sh_ref, s)
        return jnp.dot(h.astype(BF16), w1_ref[...], preferred_element_type=F32)

    def finish(s, u):
        bs, rs = sel(s)

        def out3(v):
            return v.reshape(pnb, pr, v.shape[-1])

        qkv_ref[bs, rs, 0:A_WIDTH] = out3((u[:, 0:A_WIDTH] * A_QSCALE).astype(BF16))
        qkv_ref[bs, rs, A_WIDTH:3 * A_WIDTH] = out3(u[:, A_WIDTH:3 * A_WIDTH].astype(BF16))

        @pl.when(pl.program_id(1) == keep_tile)
        def _():
            knew_ref[bs, rs] = out3(u[:, A_WIDTH:2 * A_WIDTH])
            vnew_ref[bs, rs] = out3(u[:, 2 * A_WIDTH:3 * A_WIDTH])

        blk = u[:, _W1_CQ:_W1_CQ + 256]
        lane = lax.broadcasted_iota(jnp.int32, (1, 256), 1)
        cq = jnp.where(lane < B_Q_LORA, blk, 0.0)
        cqn = cq * lax.rsqrt(jnp.sum(cq * cq, axis=-1, keepdims=True) * (1.0 / B_Q_LORA) + EPS) * gq_ref[...]
        qb = jnp.dot(cqn.astype(BF16), wuq_ref[...], preferred_element_type=F32)
        nope_w = B_HEADS * B_NOPE
        rope_w = B_HEADS * B_ROPE
        qlat = jnp.dot(qb[:, 0:nope_w].astype(BF16), bd_ref[...], preferred_element_type=F32) * B_QSCALE
        cosr = per_pos(cos_ref, s)
        sinr = per_pos(sin_ref, s)
        qpe = (qb[:, nope_w:nope_w + rope_w] * cosr
               + qb[:, nope_w + rope_w:nope_w + 2 * rope_w] * sinr) * B_QSCALE
        for hd in range(B_HEADS):
            qmla_ref[bs, hd, rs, 0:B_KV_LORA] = out3(qlat[:, hd * B_KV_LORA:(hd + 1) * B_KV_LORA].astype(BF16))
            qmla_ref[bs, hd, rs, B_KV_LORA:B_QK] = out3(qpe[:, hd * B_ROPE:(hd + 1) * B_ROPE].astype(BF16))

        cr = u[:, _W1_CKV:_W1_CKV + B_KV_LORA]
        ckv = cr * lax.rsqrt(jnp.mean(cr * cr, axis=-1, keepdims=True) + EPS) * gkv_ref[...]
        ckv_ref[bs, rs] = out3(ckv)
        kmla_ref[bs, rs, 0:B_KV_LORA] = out3(ckv.astype(BF16))
        kb = u[:, _W1_KPE:_W1_KPE + LANES]
        rot = kb * cosr + pltpu.roll(kb, LANES - B_ROPE, axis=1) * sinr
        kpe_ref[bs, rs] = out3(rot[:, 0:B_ROPE])
        kmla_ref[bs, rs, B_KV_LORA:B_QK] = out3(rot[:, 0:B_ROPE].astype(BF16))
        lane1 = lax.broadcasted_iota(jnp.int32, (1, LANES), 1)
        dtr_ref[bs, rs] = out3(jnp.where(lane1 < C_HEADS, pltpu.roll(kb, LANES - 2 * B_ROPE, axis=1), 0.0))

        z_ref[bs, rs] = out3(u[:, _W1_Z:_W1_Z + C_D_INNER].astype(BF16))
        xbc_ref[bs, rs] = out3(u[:, _W1_XBC:_W1_XBC + C_CONV_DIM])

    us = [project(s) for s in range(nparts)]
    for s in range(nparts):
        finish(s, us[s])


def _inproj(x, mod, g_mix, w1, gq, wuq, bd, gkv, cos_t, sin_t, *, nb, r, nparts):
    B, S, D = x.shape
    nbt, nst = B // nb, S // r
    keep = min(S, A_WINDOW)
    assert keep == r and B % nb == 0 and S % r == 0
    grid = (nbt, nst)
    row = lambda c: pl.BlockSpec((nb, r, c), lambda b, i: (b, i, 0))
    full = lambda a: pl.BlockSpec(a.shape, lambda b, i: (0,) * a.ndim)
    in_specs = [row(D),
                pl.BlockSpec((nb, 1, D), lambda b, i: (b, 0, 0)),
                pl.BlockSpec((nb, 1, D), lambda b, i: (b, 0, 1)),
                full(g_mix), full(w1), full(gq), full(wuq), full(bd), full(gkv),
                pl.BlockSpec((r, LANES), lambda b, i: (i, 0)),
                pl.BlockSpec((r, LANES), lambda b, i: (i, 0))]
    out_shape = [jax.ShapeDtypeStruct((B, S, 3 * A_WIDTH), BF16),
                 jax.ShapeDtypeStruct((B, keep, A_WIDTH), F32),
                 jax.ShapeDtypeStruct((B, keep, A_WIDTH), F32),
                 jax.ShapeDtypeStruct((B, B_HEADS, S, B_QK), BF16),
                 jax.ShapeDtypeStruct((B, S, B_QK), BF16),
                 jax.ShapeDtypeStruct((B, S, B_KV_LORA), F32),
                 jax.ShapeDtypeStruct((B, S, B_ROPE), F32),
                 jax.ShapeDtypeStruct((B, S, C_D_INNER), BF16),
                 jax.ShapeDtypeStruct((B, S, C_CONV_DIM), F32),
                 jax.ShapeDtypeStruct((B, S, LANES), F32)]
    out_specs = [row(3 * A_WIDTH),
                 pl.BlockSpec((nb, keep, A_WIDTH), lambda b, i: (b, 0, 0)),
                 pl.BlockSpec((nb, keep, A_WIDTH), lambda b, i: (b, 0, 0)),
                 pl.BlockSpec((nb, B_HEADS, r, B_QK), lambda b, i: (b, 0, i, 0)),
                 row(B_QK), row(B_KV_LORA), row(B_ROPE), row(C_D_INNER), row(C_CONV_DIM), row(LANES)]
    return pl.pallas_call(
        functools.partial(_inproj_kernel, nb=nb, r=r, keep_tile=nst - 1, nparts=nparts),
        grid=grid, in_specs=in_specs, out_specs=out_specs, out_shape=out_shape,
        compiler_params=_cparams("arbitrary", "arbitrary"),
        name="inproj",
    )(x, mod, mod, g_mix, w1, gq, wuq, bd, gkv, cos_t, sin_t)


def _band_scores(q, k, hmask):
    qs = jnp.where(hmask, jnp.concatenate([q] * A_HEADS, axis=0), jnp.zeros((), BF16))
    return lax.dot_general(qs, k, (((1,), (1,)), ((), ())), preferred_element_type=F32)


def _band_finish(s, v, bias, valid, hmask):
    nq = s.shape[0] // A_HEADS
    s = s + bias
    if valid is not None:
        s = jnp.where(valid, s, NEG_INF)
    m = jnp.max(s, axis=-1, keepdims=True)
    e = jnp.exp2(s - m)
    den = jnp.sum(e, axis=-1, keepdims=True)
    o = jnp.dot(e.astype(BF16), v, preferred_element_type=F32)
    o = jnp.where(hmask, o / den, 0.0)
    y = o[0:nq]
    for hd in range(1, A_HEADS):
        y = y + o[hd * nq:(hd + 1) * nq]
    return y


def _head_mask(nq):
    row = lax.broadcasted_iota(jnp.int32, (A_HEADS * nq, A_WIDTH), 0)
    col = lax.broadcasted_iota(jnp.int32, (A_HEADS * nq, A_WIDTH), 1)
    return (row // nq) == (col // A_HEAD_DIM)


def _band_prompt_kernel(cur_ref, prev_ref, bias_ref, o_ref, kwin, vwin, *, tq):
    i = pl.program_id(1)
    nch = tq // CHUNK
    kwin[0:A_WINDOW] = prev_ref[0, tq - A_WINDOW:tq, A_WIDTH:2 * A_WIDTH]
    kwin[A_WINDOW:A_WINDOW + tq] = cur_ref[0, :, A_WIDTH:2 * A_WIDTH]
    vwin[0:A_WINDOW] = prev_ref[0, tq - A_WINDOW:tq, 2 * A_WIDTH:3 * A_WIDTH]
    vwin[A_WINDOW:A_WINDOW + tq] = cur_ref[0, :, 2 * A_WIDTH:3 * A_WIDTH]
    hmask = _head_mask(CHUNK)
    ucol = lax.broadcasted_iota(jnp.int32, (1, A_BAND), 1)

    def scores(c):
        return _band_scores(cur_ref[0, c * CHUNK:(c + 1) * CHUNK, 0:A_WIDTH],
                            kwin[c * CHUNK:c * CHUNK + A_BAND], hmask)

    def run(first_tile):
        bias2 = bias_ref[...] * LOG2E
        s_next = scores(0)
        for c in range(nch):
            s = s_next
            if c + 1 < nch:
                s_next = scores(c + 1)
            valid = (ucol // CHUNK + (c - A_BACK)) >= 0 if first_tile else None
            y = _band_finish(s, vwin[c * CHUNK:c * CHUNK + A_BAND], bias2, valid, hmask)
            o_ref[0, c * CHUNK:(c + 1) * CHUNK, :] = y.astype(BF16)

    @pl.when(i == 0)
    def _():
        run(True)

    @pl.when(i > 0)
    def _():
        run(False)


def _band_prompt(qkv, bias_stack):
    B, S, _ = qkv.shape
    tq = A_WINDOW
    assert S % tq == 0 and tq // CHUNK >= A_BACK
    return pl.pallas_call(
        functools.partial(_band_prompt_kernel, tq=tq),
        grid=(B, S // tq),
        in_specs=[pl.BlockSpec((1, tq, 3 * A_WIDTH), lambda b, i: (b, i, 0)),
                  pl.BlockSpec((1, tq, 3 * A_WIDTH), lambda b, i: (b, jnp.maximum(i - 1, 0), 0)),
                  pl.BlockSpec(bias_stack.shape, lambda b, i: (0, 0))],
        out_specs=pl.BlockSpec((1, tq, A_WIDTH), lambda b, i: (b, i, 0)),
        out_shape=jax.ShapeDtypeStruct((B, S, A_WIDTH), BF16),
        scratch_shapes=[pltpu.VMEM((A_WINDOW + tq, A_WIDTH), BF16), pltpu.VMEM((A_WINDOW + tq, A_WIDTH), BF16)],
        compiler_params=_cparams("arbitrary", "arbitrary"),
        name="band_prompt",
    )(qkv, qkv, bias_stack)


def _band_sample_kernel(qkv_ref, kt_ref, vt_ref, biasc_ref, biasn_ref, o_ref, *, nbb, ds):
    hmask = _head_mask(ds)
    nt_dims = (((1,), (1,)), ((), ()))
    bias_c = biasc_ref[...] * LOG2E
    bias_n = biasn_ref[...] * LOG2E

    def scores(b):
        qs = jnp.where(hmask, jnp.concatenate([qkv_ref[b, :, 0:A_WIDTH]] * A_HEADS, axis=0), jnp.zeros((), BF16))
        sc = jnp.dot(qs, kt_ref[0, b].astype(BF16), preferred_element_type=F32)
        sn = lax.dot_general(qs, qkv_ref[b, :, A_WIDTH:2 * A_WIDTH], nt_dims, preferred_element_type=F32)
        return sc, sn

    raw = [scores(b) for b in range(nbb)]
    for b in range(nbb):
        vn = qkv_ref[b, :, 2 * A_WIDTH:3 * A_WIDTH]
        sc = raw[b][0] + bias_c
        sn = raw[b][1] + bias_n
        m = jnp.maximum(jnp.max(sc, axis=-1, keepdims=True), jnp.max(sn, axis=-1, keepdims=True))
        ec = jnp.exp2(sc - m)
        en = jnp.exp2(sn - m)
        den = jnp.sum(ec, axis=-1, keepdims=True) + jnp.sum(en, axis=-1, keepdims=True)
        o = lax.dot_general(ec.astype(BF16), vt_ref[0, b].astype(BF16), nt_dims, preferred_element_type=F32)
        o = o + jnp.dot(en.astype(BF16), vn, preferred_element_type=F32)
        o = jnp.where(hmask, o / den, 0.0)
        y = o[0:ds]
        for hd in range(1, A_HEADS):
            y = y + o[hd * ds:(hd + 1) * ds]
        o_ref[b] = y.astype(BF16)


def _band_sample(qkv, cache_kt, cache_vt, layer, bias_stack):
    B, ds, _ = qkv.shape
    nr = cache_kt.shape[3]
    nbb = 4
    assert B % nbb == 0
    bias_c, bias_n = bias_stack[:, 0:nr], bias_stack[:, nr:]
    return pl.pallas_call(
        functools.partial(_band_sample_kernel, nbb=nbb, ds=ds),
        grid=(B // nbb,),
        in_specs=[pl.BlockSpec((nbb, ds, 3 * A_WIDTH), lambda b: (b, 0, 0)),
                  pl.BlockSpec((1, nbb, A_WIDTH, nr), lambda b: (layer, b, 0, 0)),
                  pl.BlockSpec((1, nbb, A_WIDTH, nr), lambda b: (layer, b, 0, 0)),
                  pl.BlockSpec(bias_c.shape, lambda b: (0, 0)),
                  pl.BlockSpec(bias_n.shape, lambda b: (0, 0))],
        out_specs=pl.BlockSpec((nbb, ds, A_WIDTH), lambda b: (b, 0, 0)),
        out_shape=jax.ShapeDtypeStruct((B, ds, A_WIDTH), BF16),
        compiler_params=_cparams("arbitrary"),
        name="band_sample",
    )(qkv, cache_kt, cache_vt, bias_c, bias_n)


def _mla_finish(acc, den, wuv_ref, bq):
    o = acc / den
    o_all = jnp.concatenate([o[hd * bq:(hd + 1) * bq] for hd in range(B_HEADS)], axis=1)
    return jnp.dot(o_all.astype(BF16), wuv_ref[...], preferred_element_type=F32)


def _mla_prompt_kernel(qi_ref, kj_ref, last_ref, q_ref, k_ref, wuv_ref, o_ref, m_sc, acc_sc, *, bq, bk, sk):
    p = pl.program_id(1)
    i = qi_ref[p]
    j = kj_ref[p]
    M = B_HEADS * bq

    @pl.when(j == 0)
    def _():
        m_sc[...] = jnp.full(m_sc.shape, NEG_INF, F32)
        acc_sc[...] = jnp.zeros(acc_sc.shape, F32)

    nt_dims = (((1,), (1,)), ((), ()))
    nsub = bk // sk

    def scores(q, t):
        return lax.dot_general(q, k_ref[0, t * sk:(t + 1) * sk, :], nt_dims, preferred_element_type=F32)

    def absorb(s, t, m_run, acc):
        m_new = jnp.maximum(m_run, jnp.max(s, axis=1, keepdims=True))
        alpha = jnp.exp2(m_run - m_new)
        pr = jnp.exp2(s - jnp.tile(m_new, (1, sk // LANES)))
        v_ext = jnp.concatenate([k_ref[0, t * sk:(t + 1) * sk, 0:B_KV_LORA], jnp.ones((sk, LANES), BF16)], axis=1)
        acc = acc * jnp.concatenate([alpha, alpha], axis=1) + jnp.dot(pr.astype(BF16), v_ext,
                                                                      preferred_element_type=F32)
        return m_new, acc

    @pl.when(last_ref[p] == 0)
    def _():
        q = q_ref[0].reshape(M, B_QK)
        m_run, acc = m_sc[...], acc_sc[...]
        s_next = scores(q, 0)
        for t in range(nsub):
            s = s_next
            if t + 1 < nsub:
                s_next = scores(q, t + 1)
            m_run, acc = absorb(s, t, m_run, acc)
        m_sc[...] = m_run
        acc_sc[...] = acc

    @pl.when(last_ref[p] == 1)
    def _():
        for t in range(nsub):
            @pl.when(j * bk + t * sk <= i * bq + (bq - 1))
            def _():
                s = scores(q_ref[0].reshape(M, B_QK), t)
                row = lax.broadcasted_iota(jnp.int32, (M, sk), 0)
                col = lax.broadcasted_iota(jnp.int32, (M, sk), 1)
                s = jnp.where((col + (j * bk + t * sk)) // CHUNK <= ((row % bq) + i * bq) // CHUNK, s, NEG_INF)
                m_sc[...], acc_sc[...] = absorb(s, t, m_sc[...], acc_sc[...])
        o_ref[0] = _mla_finish(acc_sc[:, 0:B_KV_LORA], acc_sc[:, B_KV_LORA:], wuv_ref, bq).astype(BF16)


def _mla_prompt(qmla, kmla, wuv_bd, *, bq, bk):
    B, S = kmla.shape[0:2]
    bk = min(bk, S)
    assert S % bq == 0 and S % bk == 0 and bq % CHUNK == 0 and bk % bq == 0
    nq = S // bq
    n_kv = [(i * bq + bq - 1) // bk + 1 for i in range(nq)]
    qi = np.concatenate([np.full(n, i, np.int32) for i, n in enumerate(n_kv)])
    kj = np.concatenate([np.arange(n, dtype=np.int32) for n in n_kv])
    last = np.concatenate([(np.arange(n) == n - 1).astype(np.int32) for n in n_kv])
    grid_spec = pltpu.PrefetchScalarGridSpec(
        num_scalar_prefetch=3,
        grid=(B, len(qi)),
        in_specs=[pl.BlockSpec((1, B_HEADS, bq, B_QK), lambda b, p, qi, kj, last: (b, 0, qi[p], 0)),
                  pl.BlockSpec((1, bk, B_QK), lambda b, p, qi, kj, last: (b, kj[p], 0)),
                  pl.BlockSpec(wuv_bd.shape, lambda b, p, qi, kj, last: (0, 0))],
        out_specs=pl.BlockSpec((1, bq, B_WIDTH), lambda b, p, qi, kj, last: (b, qi[p], 0)),
        scratch_shapes=[pltpu.VMEM((B_HEADS * bq, LANES), F32),
                        pltpu.VMEM((B_HEADS * bq, B_KV_LORA + LANES), F32)])
    return pl.pallas_call(
        functools.partial(_mla_prompt_kernel, bq=bq, bk=bk, sk=min(bk, 512)),
        grid_spec=grid_spec,
        out_shape=jax.ShapeDtypeStruct((B, S, B_WIDTH), BF16),
        compiler_params=_cparams("arbitrary", "arbitrary"),
        name="mla_prompt",
    )(jnp.asarray(qi), jnp.asarray(kj), jnp.asarray(last), qmla, kmla, wuv_bd)


def _mla_sample_kernel(q_ref, kn_ref, ckv_ref, kpet_ref, wuv_ref, o_ref, *, ds, nbb):
    M = B_HEADS * ds
    nt_dims = (((1,), (1,)), ((), ()))

    def scores(b):
        q = q_ref[b].reshape(M, B_QK)
        kn = kn_ref[b]
        ckv = ckv_ref[0, b].astype(BF16)
        sc = (lax.dot_general(q[:, 0:B_KV_LORA], ckv, nt_dims, preferred_element_type=F32)
              + jnp.dot(q[:, B_KV_LORA:B_QK], kpet_ref[0, b].astype(BF16), preferred_element_type=F32))
        sn = lax.dot_general(q, kn, nt_dims, preferred_element_type=F32)
        return sc, sn, ckv, kn

    def finish(b, sc, sn, ckv, kn):
        m = jnp.maximum(jnp.max(sc, axis=1, keepdims=True), jnp.max(sn, axis=1, keepdims=True))
        pc = jnp.exp2(sc - m)
        pn = jnp.exp2(sn - m)
        den = jnp.sum(pc, axis=1, keepdims=True) + jnp.sum(pn, axis=1, keepdims=True)
        acc = (jnp.dot(pc.astype(BF16), ckv, preferred_element_type=F32)
               + jnp.dot(pn.astype(BF16), kn[:, 0:B_KV_LORA], preferred_element_type=F32))
        o_ref[b] = _mla_finish(acc, den, wuv_ref, ds).astype(BF16)

    vals = [scores(b) for b in range(nbb)]
    for b in range(nbb):
        finish(b, *vals[b])


def _mla_sample(qmla, kmla, cache_ckv, cache_kpet, layer, wuv_bd):
    B, ds = kmla.shape[0:2]
    past = cache_ckv.shape[2]
    nbb = math.gcd(B, 4)
    return pl.pallas_call(
        functools.partial(_mla_sample_kernel, ds=ds, nbb=nbb),
        grid=(B // nbb,),
        in_specs=[pl.BlockSpec((nbb, B_HEADS, ds, B_QK), lambda b: (b, 0, 0, 0)),
                  pl.BlockSpec((nbb, ds, B_QK), lambda b: (b, 0, 0)),
                  pl.BlockSpec((1, nbb, past, B_KV_LORA), lambda b: (layer, b, 0, 0)),
                  pl.BlockSpec((1, nbb, B_ROPE, past), lambda b: (layer, b, 0, 0)),
                  pl.BlockSpec(wuv_bd.shape, lambda b: (0, 0))],
        out_specs=pl.BlockSpec((nbb, ds, B_WIDTH), lambda b: (b, 0, 0)),
        out_shape=jax.ShapeDtypeStruct((B, ds, B_WIDTH), BF16),
        compiler_params=_cparams("arbitrary"),
        name="mla_sample",
    )(qmla, kmla, cache_ckv, cache_kpet, wuv_bd)


def _ssd_kernel(xbc_ref, z_ref, dtr_ref, pre_ref, h0_ref, cw_ref, cb_ref, dtb_ref, alog_ref, dsk_ref, gn_ref,
                es_ref, ep_ref, tri_ref, y_ref, hout_ref,
                cbuf, xs_sc, b_sc, c_sc, ces_sc, cep_sc, dep_sc, st_sc, *, nbat, lt, lc, nt):
    t = pl.program_id(1)
    nch = lt // lc
    HS = C_HEADS * lc
    gw = C_HPG * C_HEAD_DIM
    same_exp = lc == C_HEAD_DIM

    @pl.when(t == 0)
    def _():
        st_sc[...] = jnp.zeros(st_sc.shape, F32)
        for b in range(nbat):
            cbuf[b, 8 - (C_CONV_W - 1):8] = pre_ref[b]
            for g in range(C_GROUPS):
                blk = jnp.concatenate([h0_ref[b, g * C_HPG + hl] for hl in range(C_HPG)], axis=1)
                st_sc[b, g * C_D_STATE:(g + 1) * C_D_STATE, g * gw:(g + 1) * gw] = blk

    hl = lax.broadcasted_iota(jnp.int32, (lt, LANES), 1)
    for b in range(nbat):
        cbuf[b, 8:8 + lt] = xbc_ref[b]
        acc = jnp.broadcast_to(cb_ref[...], (lt, C_CONV_DIM))
        for kk in range(C_CONV_W):
            off = 8 - (C_CONV_W - 1) + kk
            acc = acc + cbuf[b, off:off + lt] * cw_ref[kk:kk + 1, :]
        tail = cbuf[b, 8 + lt - (C_CONV_W - 1):8 + lt]
        cbuf[b, 8 - (C_CONV_W - 1):8] = tail
        xc = _silu(acc)
        xs_sc[b] = xc[:, 0:C_D_INNER]
        b_sc[b] = xc[:, C_D_INNER:C_D_INNER + C_GN]
        c_sc[b] = xc[:, C_D_INNER + C_GN:C_D_INNER + 2 * C_GN]

        dt = jnp.where(hl < C_HEADS, jax.nn.softplus(dtr_ref[b] + dtb_ref[...]), 0.0)
        da = dt * (-jnp.exp(alog_ref[...]))
        c3 = jnp.dot(tri_ref[...], _split3_lanes(da, hl), preferred_element_type=F32)
        cum = jnp.where(hl < C_HEADS, c3 + pltpu.roll(c3, LANES - C_HEADS, axis=1)
                        + pltpu.roll(c3, LANES - 2 * C_HEADS, axis=1), 0.0)
        cum3 = _split3_lanes(cum, hl)
        cep = jnp.dot(cum3, ep_ref[...], preferred_element_type=F32)
        cep_sc[b] = cep
        if not same_exp:
            ces_sc[b] = jnp.dot(cum3, es_ref[...], preferred_element_type=F32)
        dep_sc[b] = jnp.dot(_split3_lanes(dt, hl), ep_ref[...], preferred_element_type=F32)

    r_s = lax.broadcasted_iota(jnp.int32, (lc, HS), 0)
    l_s = lax.broadcasted_iota(jnp.int32, (lc, HS), 1)
    eye_t = (l_s % lc) == r_s
    causal_t = (l_s % lc) <= r_s
    r_w = lax.broadcasted_iota(jnp.int32, (HS, C_GN), 0)
    l_w = lax.broadcasted_iota(jnp.int32, (HS, C_GN), 1)
    gmask = (r_w // (C_HPG * lc)) == (l_w // C_D_STATE)
    r_b =lax.broadcasted_iota(jnp.int32, (C_HPG * lc, gw), 0)
    l_b = lax.broadcasted_iota(jnp.int32, (C_HPG * lc, gw), 1)
    bmask = (r_b // lc) == (l_b // C_HEAD_DIM)
    r_g = lax.broadcasted_iota(jnp.int32, (C_GN, C_D_INNER), 0)
    l_g = lax.broadcasted_iota(jnp.int32, (C_GN, C_D_INNER), 1)
    smask = (r_g // C_D_STATE) == (l_g // gw)

    def stage_cb(b, rows):
        cm = c_sc[b, rows, :].astype(BF16)
        bm = b_sc[b, rows, :]
        w_nt = jnp.where(gmask, jnp.concatenate([bm] * C_HEADS, axis=0), 0.0).astype(BF16)
        cbcat = lax.dot_general(cm, w_nt, (((1,), (1,)), ((), ())), preferred_element_type=F32)
        st = st_sc[b]
        yoff = jnp.dot(cm, st.astype(BF16), preferred_element_type=F32)
        return dict(bm=bm, cbcat=cbcat, st=st, yoff=yoff)

    def stage_diag(b, rows, v):
        ce_p = cep_sc[b, rows, :]
        ce_s = ce_p if same_exp else ces_sc[b, rows, :]
        xs = xs_sc[b, rows, :]
        rflat = jnp.sum(jnp.where(eye_t, ce_s, 0.0), axis=0, keepdims=True)
        lcat = jnp.exp(jnp.where(causal_t, ce_s - rflat, NEG_INF))
        mcat = (v['cbcat'] * lcat).astype(BF16)
        xdt = xs * dep_sc[b, rows, :]
        xdt_b = xdt.astype(BF16)
        ydiag = []
        for g in range(C_GROUPS):
            bdx = jnp.where(bmask, jnp.concatenate([xdt_b[:, g * gw:(g + 1) * gw]] * C_HPG, axis=0),
                            jnp.zeros((), BF16))
            ydiag.append(jnp.dot(mcat[:, g * C_HPG * lc:(g + 1) * C_HPG * lc], bdx, preferred_element_type=F32))
        v.update(ce_p=ce_p, xs=xs, xdt=xdt, ydiag=jnp.concatenate(ydiag, axis=1))

    def stage_state(b, rows, v):
        ce_p = v['ce_p']
        last = ce_p[lc - 1:lc, :]
        xw = (v['xdt'] * jnp.exp(last - ce_p)).astype(BF16)
        upd = lax.dot_general(v['bm'].astype(BF16), xw, (((0,), (0,)), ((), ())), preferred_element_type=F32)
        st_sc[b] = jnp.exp(last) * v['st'] + jnp.where(smask, upd, 0.0)

    def stage_out(b, rows, v):
        y = v['ydiag'] + v['yoff'] * jnp.exp(v['ce_p']) + dsk_ref[...] * v['xs']
        y = y * _silu(z_ref[b, rows, :].astype(F32))
        y = y * lax.rsqrt(jnp.mean(y * y, axis=-1, keepdims=True) + EPS) * gn_ref[...]
        y_ref[b, rows, :] = y.astype(BF16)

    def chunk(c, carry):
        rows = pl.ds(pl.multiple_of(c * lc, lc), lc)
        vals = [stage_cb(b, rows) for b in range(nbat)]
        for stage in (stage_diag, stage_state, stage_out):
            for b in range(nbat):
                stage(b, rows, vals[b])
        return carry

    lax.fori_loop(0, nch, chunk, 0)

    @pl.when(t == nt - 1)
    def _():
        for b in range(nbat):
            for hd in range(C_HEADS):
                g = hd // C_HPG
                hout_ref[b, hd] = st_sc[b, g * C_D_STATE:(g + 1) * C_D_STATE,
                                        hd * C_HEAD_DIM:(hd + 1) * C_HEAD_DIM]


def _ssd(xbc, z, dtr, prefix, h0t, cw, cb, dtb, alog, dsk, gn, *, nbat, lt, lc):
    B, S, _ = xbc.shape
    nt = S // lt
    assert S % lt == 0 and lt % lc == 0 and B % nbat == 0
    hs = C_HEADS * lc

    def expand(width, per_head):
        e = np.zeros((LANES, width), np.float32)
        for part in range(3):
            e[part * C_HEADS:(part + 1) * C_HEADS] = (np.arange(width)[None, :] // per_head
                                                      == np.arange(C_HEADS)[:, None])
        return jnp.asarray(e, dtype=BF16)

    es, ep = expand(hs, lc), expand(C_D_INNER, C_HEAD_DIM)
    pos = np.arange(lt)
    tri = jnp.asarray((pos[None, :] <= pos[:, None]) & (pos[None, :] // lc == pos[:, None] // lc), dtype=BF16)
    row = lambda c: pl.BlockSpec((nbat, lt, c), lambda b, t: (b, t, 0))
    per_b = lambda a: pl.BlockSpec((nbat,) + a.shape[1:], lambda b, t: (b,) + (0,) * (a.ndim - 1))
    full = lambda a: pl.BlockSpec(a.shape, lambda b, t: (0,) * a.ndim)
    sc = lambda *shape: pltpu.VMEM((nbat,) + shape, F32)
    return pl.pallas_call(
        functools.partial(_ssd_kernel, nbat=nbat, lt=lt, lc=lc, nt=nt),
        grid=(B // nbat, nt),
        in_specs=[row(C_CONV_DIM), row(C_D_INNER), row(LANES), per_b(prefix), per_b(h0t),
                  full(cw), full(cb), full(dtb), full(alog), full(dsk), full(gn), full(es), full(ep), full(tri)],
        out_specs=[row(C_D_INNER), per_b(h0t)],
        out_shape=[jax.ShapeDtypeStruct((B, S, C_D_INNER), BF16),
                   jax.ShapeDtypeStruct(h0t.shape, F32)],
        scratch_shapes=[sc(lt + 8, C_CONV_DIM), sc(lt, C_D_INNER), sc(lt, C_GN), sc(lt, C_GN),
                        sc(lt, hs) if lc != C_HEAD_DIM else sc(8, LANES),
                        sc(lt, C_D_INNER), sc(lt, C_D_INNER), sc(C_GN, C_D_INNER)],
        compiler_params=_cparams("arbitrary", "arbitrary"),
        name="ssd",
    )(xbc, z, dtr, prefix, h0t, cw, cb, dtb, alog, dsk, gn, es, ep, tri)


def _outproj_kernel(ya_ref, yb_ref, yc_ref, x_ref, g1_ref, sh_ref, sc_ref, gf_ref, wo_ref, wr_ref, br_ref,
                    x1_ref, h2_ref, gate_ref, *, nb, r, nparts):
    D = x_ref.shape[-1]
    pnb, pr = (nb // nparts, r) if nb > 1 else (1, r // nparts)
    PR = pnb * pr

    def sel_part(s):
        return (slice(s * pnb, (s + 1) * pnb), slice(None)) if nb > 1 else (slice(None), slice(s * pr, (s + 1) * pr))

    def in2(ref, s):
        bs, rs = sel_part(s)
        return ref[bs, rs].reshape(PR, ref.shape[-1])

    def per_batch(ref, s):
        v = ref[sel_part(s)[0]] if nb > 1 else ref[...]
        return jnp.broadcast_to(v, (pnb, pr, v.shape[-1])).reshape(PR, v.shape[-1])

    def project(s):
        o = jnp.dot(in2(ya_ref, s), wo_ref[0:A_WIDTH, :], preferred_element_type=F32)
        o = o + jnp.dot(in2(yb_ref, s), wo_ref[A_WIDTH:A_WIDTH + B_WIDTH, :], preferred_element_type=F32)
        return o + jnp.dot(in2(yc_ref, s), wo_ref[A_WIDTH + B_WIDTH:, :], preferred_element_type=F32)

    wr = wr_ref[...]
    w_hi = wr.astype(BF16)
    w_lo = (wr - w_hi.astype(F32)).astype(BF16)
    w_both = jnp.concatenate([w_hi, w_lo], axis=1)

    def logits(s, o):
        bs, rs = sel_part(s)
        x1 = in2(x_ref, s) + per_batch(g1_ref, s) * o
        x1_ref[bs, rs] = x1.reshape(pnb, pr, D)
        h2 = x1 * lax.rsqrt(jnp.mean(x1 * x1, axis=-1, keepdims=True) + EPS) * gf_ref[...]
        h2 = h2 * (1.0 + per_batch(sc_ref, s)) + per_batch(sh_ref, s)
        h_hi = h2.astype(BF16)
        h2_ref[bs, rs] = h_hi.reshape(pnb, pr, D)
        h_lo = (h2 - h_hi.astype(F32)).astype(BF16)
        hh = jnp.dot(h_hi, w_both, preferred_element_type=F32)
        return (hh[:, 0:ROUTER_LANES] + hh[:, ROUTER_LANES:]
                + jnp.dot(h_lo, w_hi, preferred_element_type=F32) + br_ref[...])

    def route(s, lg):
        bs, rs = sel_part(s)
        lane = lax.broadcasted_iota(jnp.int32, (PR, ROUTER_LANES), 1)
        big = jnp.int32(ROUTER_LANES)
        is_g = lane < E_GROUPS
        gl = jnp.where(is_g, lg, NEG_INF)
        gmax = jnp.max(gl, axis=-1, keepdims=True)
        p_top = 1.0 / jnp.sum(jnp.where(is_g, jnp.exp(gl - gmax), 0.0), axis=-1, keepdims=True)
        g_idx = jnp.min(jnp.where(is_g & (gl == gmax), lane, big), axis=-1, keepdims=True)
        e_lane = lane - ROUTER_E_OFF
        sel = (e_lane >= 0) & (e_lane < N_EXPERTS) & ((e_lane // E_PER_GROUP) == g_idx)
        l1 = jnp.where(sel, lg, NEG_INF)
        m1 = jnp.max(l1, axis=-1, keepdims=True)
        i1 = jnp.min(jnp.where(sel & (l1 == m1), lane, big), axis=-1, keepdims=True)
        sel2 = sel & (lane != i1)
        l2 = jnp.where(sel2, lg, NEG_INF)
        m2 = jnp.max(l2, axis=-1, keepdims=True)
        i2 = jnp.min(jnp.where(sel2 & (l2 == m2), lane, big), axis=-1, keepdims=True)
        e2 = jnp.exp(m2 - m1)
        w1 = p_top / (1.0 + e2)
        w2 = p_top * e2 / (1.0 + e2)
        gate = jnp.where(lane == i1, w1, 0.0) + jnp.where(lane == i2, w2, 0.0)
        gate = gate + jnp.where(lane == 0, g_idx.astype(F32), 0.0)
        gate_ref[bs, rs] = gate.reshape(pnb, pr, ROUTER_LANES)

    outs = [project(s) for s in range(nparts)]
    lgs = [logits(s, outs[s]) for s in range(nparts)]
    for s in range(nparts):
        route(s, lgs[s])


def _outproj(ya, yb, yc, x, mod, g_ffn, wo, wr, br, *, nb, r, nparts):
    B, S, D = x.shape
    row = lambda c: pl.BlockSpec((nb, r, c), lambda b, i: (b, i, 0))
    modc = lambda j: pl.BlockSpec((nb, 1, D), lambda b, i: (b, 0, j))
    full = lambda a: pl.BlockSpec(a.shape, lambda b, i: (0,) * a.ndim)
    return pl.pallas_call(
        functools.partial(_outproj_kernel, nb=nb, r=r, nparts=nparts),
        grid=(B // nb, S // r),
        in_specs=[row(A_WIDTH), row(B_WIDTH), row(C_D_INNER), row(D), modc(2), modc(3), modc(4),
                  full(g_ffn), full(wo), full(wr), full(br)],
        out_specs=[row(D), row(D), row(ROUTER_LANES)],
        out_shape=[jax.ShapeDtypeStruct((B, S, D), F32), jax.ShapeDtypeStruct((B, S, D), BF16),
                   jax.ShapeDtypeStruct((B, S, ROUTER_LANES), F32)],
        compiler_params=_cparams("arbitrary", "arbitrary"),
        name="outproj",
    )(ya, yb, yc, x, mod, mod, mod, g_ffn, wo, wr, br)


MOE_BLK = 160


def _moe_sorted_kernel(h_ref, gate_ref, x_ref, g2_ref, wgu_ref, wd_ref, tri_ref, *rest, nb, r, final):
    if final:
        shf_ref, scf_ref, gfin_ref, o_ref, xc_sc, yc_sc, p_sc, meta = rest
    else:
        o_ref, xc_sc, yc_sc, p_sc, meta = rest
    g = pl.program_id(2)
    R = nb * r
    D = x_ref.shape[-1]
    RC = xc_sc.shape[0]

    @pl.when(g == 0)
    def _():
        gate = gate_ref[...].reshape(R, ROUTER_LANES)
        lane = lax.broadcasted_iota(jnp.int32, (R, ROUTER_LANES), 1)
        onehot = jnp.where((lane < E_GROUPS) & (gate[:, 0:1] == lane.astype(F32)), 1.0, 0.0)
        pos = jnp.dot(tri_ref[...], onehot.astype(BF16), preferred_element_type=F32)
        rank = jnp.sum(onehot * pos, axis=1, keepdims=True)
        counts = jnp.sum(onehot, axis=0, keepdims=True)
        lane1 = lax.broadcasted_iota(jnp.int32, (1, ROUTER_LANES), 1)
        base = jnp.int32(0)
        baserow = jnp.zeros((1, ROUTER_LANES), F32)
        for gg in range(E_GROUPS):
            cnt = jnp.sum(jnp.where(lane1 == gg, counts, 0.0)).astype(jnp.int32)
            nblk = (cnt + (MOE_BLK - 1)) // MOE_BLK
            meta[gg] = base
            meta[E_GROUPS + gg] = nblk
            baserow = jnp.where(lane1 == gg, base.astype(F32), baserow)
            base = base + nblk * MOE_BLK
        cpos = jnp.sum(onehot * baserow, axis=1, keepdims=True) + rank
        colc = lax.broadcasted_iota(jnp.int32, (1, RC), 1).astype(F32)
        pm = jnp.where(cpos == colc, 1.0, 0.0).astype(BF16)
        p_sc[...] = pm
        g_hi = gate.astype(BF16)
        g_lo = (gate - g_hi.astype(F32)).astype(BF16)
        xext = jnp.concatenate([h_ref[...].reshape(R, D), g_hi, g_lo], axis=1)
        xc_sc[...] = lax.dot_general(pm, xext, (((0,), (0,)), ((), ())),
                                     preferred_element_type=F32).astype(BF16)
        yc_sc[...] = jnp.zeros(yc_sc.shape, BF16)

    base_g = meta[g]
    nblk_g = meta[E_GROUPS + g]
    lane_b = lax.broadcasted_iota(jnp.int32, (MOE_BLK, ROUTER_LANES), 1)

    def block(bi, carry):
        rows = pl.ds(pl.multiple_of(base_g + bi * MOE_BLK, MOE_BLK), MOE_BLK)
        xg = xc_sc[rows, 0:D]
        gg = (xc_sc[rows, D:D + ROUTER_LANES].astype(F32)
              + xc_sc[rows, D + ROUTER_LANES:D + 2 * ROUTER_LANES].astype(F32))
        acts = []
        for k in range(E_PER_GROUP):
            gu = jnp.dot(xg, wgu_ref[0, k], preferred_element_type=F32)
            gcol = jnp.sum(jnp.where(lane_b == ROUTER_E_OFF + g * E_PER_GROUP + k, gg, 0.0),
                           axis=-1, keepdims=True)
            acts.append((_silu(gu[:, 0:E_HIDDEN]) * gu[:, E_HIDDEN:] * gcol).astype(BF16))
        wd = wd_ref[0].reshape(E_PER_GROUP * E_HIDDEN, D)
        yc_sc[rows, :] = jnp.dot(jnp.concatenate(acts, axis=1), wd, preferred_element_type=F32).astype(BF16)
        return carry

    lax.fori_loop(0, nblk_g, block, 0)

    @pl.when(g == E_GROUPS - 1)
    def _():
        moe = jnp.dot(p_sc[...], yc_sc[...], preferred_element_type=F32)
        xo = x_ref[...].reshape(R, D) + _rows(g2_ref, nb, r) * moe
        if final:
            xo = xo * lax.rsqrt(jnp.mean(xo * xo, axis=-1, keepdims=True) + EPS) * gfin_ref[...]
            xo = xo * (1.0 + _rows(scf_ref, nb, r)) + _rows(shf_ref, nb, r)
        o_ref[...] = xo.reshape(nb, r, D)


def _moe_sorted(h2, gate, x1, mod, wgu, wd, layer, final, *, nb, r):
    B, S, D = x1.shape
    R = nb * r
    rc = R + E_GROUPS * MOE_BLK
    pos = np.arange(R)
    tri = jnp.asarray(pos[None, :] < pos[:, None], dtype=BF16)
    row = lambda c: pl.BlockSpec((nb, r, c), lambda b, i, g: (b, i, 0))
    modc = lambda j: pl.BlockSpec((nb, 1, D), lambda b, i, g: (b, 0, j))
    in_specs = [row(D), row(ROUTER_LANES), row(D), modc(5),
                pl.BlockSpec((1, E_PER_GROUP, D, 2 * E_HIDDEN), lambda b, i, g: (layer, g, 0, 0)),
                pl.BlockSpec((1, E_PER_GROUP, E_HIDDEN, D), lambda b, i, g: (layer, g, 0, 0)),
                pl.BlockSpec(tri.shape, lambda b, i, g: (0, 0))]
    args = [h2, gate, x1, mod, wgu, wd, tri]
    if final is not None:
        modf, gfin = final
        in_specs += [modc(0), modc(1), pl.BlockSpec(gfin.shape, lambda b, i, g: (0, 0))]
        args += [modf, modf, gfin]
    return pl.pallas_call(
        functools.partial(_moe_sorted_kernel, nb=nb, r=r, final=final is not None),
        grid=(B // nb, S // r, E_GROUPS),
        in_specs=in_specs,
        out_specs=row(D),
        out_shape=jax.ShapeDtypeStruct((B, S, D), F32),
        scratch_shapes=[pltpu.VMEM((rc, D + 2 * ROUTER_LANES), BF16), pltpu.VMEM((rc, D), BF16),
                        pltpu.VMEM((R, rc), BF16), pltpu.SMEM((2 * E_GROUPS,), jnp.int32)],
        compiler_params=_cparams("arbitrary", "arbitrary", "arbitrary"),
        name="moe_sorted",
    )(*args)


def _rope_tables(pos):
    half = B_ROPE // 2
    inv = 1.0 / (ROPE_THETA ** (jnp.arange(half, dtype=F32) / half))
    ang = pos.astype(F32)[:, None] * inv[None, :]
    cos, sin = jnp.cos(ang), jnp.sin(ang)
    return (jnp.tile(jnp.concatenate([cos, cos], axis=1), (1, B_HEADS)),
            jnp.tile(jnp.concatenate([-sin, sin], axis=1), (1, B_HEADS)))


def _swap_halves(w):
    half = w.shape[-1] // 2
    return jnp.concatenate([w[..., half:], w[..., :half]], axis=-1)


def _layer_weights(l, w_in, b_g_q, b_w_uq, b_g_kv, b_w_uk, b_w_uv, w_out, moe_w_rg, moe_b_rg, moe_w_re, moe_b_re):
    D = w_in.shape[1]
    w = w_in[l]
    o_cq = 3 * A_WIDTH
    o_ckv = o_cq + B_Q_LORA
    o_kpe = o_ckv + B_KV_LORA
    o_z = o_kpe + B_ROPE
    o_xbc = o_z + C_D_INNER
    o_dt = o_xbc + C_CONV_DIM
    wkpe = w[:, o_kpe:o_z]
    zc = lambda n: jnp.zeros((D, n), F32)
    w1 = jnp.concatenate([w[:, 0:o_cq], w[:, o_cq:o_ckv], zc(256 - B_Q_LORA), w[:, o_ckv:o_kpe],
                          wkpe, _swap_halves(wkpe), w[:, o_dt:o_dt + C_HEADS],
                          zc(LANES - 2 * B_ROPE - C_HEADS), w[:, o_z:o_xbc], w[:, o_xbc:o_dt]],
                         axis=1).astype(BF16)
    assert w1.shape[1] == _W1_N
    gq = jnp.pad(b_g_q[l], (0, 256 - B_Q_LORA))[None, :]
    uq = b_w_uq[l]
    pe = uq[:, :, B_NOPE:]
    wuq = jnp.concatenate([uq[:, :, :B_NOPE].reshape(B_Q_LORA, -1), pe.reshape(B_Q_LORA, -1),
                           _swap_halves(pe).reshape(B_Q_LORA, -1)], axis=1)
    wuq = jnp.pad(wuq, ((0, 256 - B_Q_LORA), (0, 0))).astype(BF16)
    eye = jnp.eye(B_HEADS, dtype=F32)
    bd = (jnp.transpose(b_w_uk[l], (1, 2, 0))[:, :, None, :] * eye[:, None, :, None]).reshape(
        B_HEADS * B_NOPE, B_HEADS * B_KV_LORA).astype(BF16)
    wuv = (jnp.transpose(b_w_uv[l], (1, 0, 2))[:, :, None, :] * eye[:, None, :, None]).reshape(
        B_HEADS * B_KV_LORA, B_WIDTH).astype(BF16)
    wr = jnp.zeros((D, ROUTER_LANES), F32)
    wr = wr.at[:, 0:E_GROUPS].set(moe_w_rg[l]).at[:, ROUTER_E_OFF:ROUTER_E_OFF + N_EXPERTS].set(moe_w_re[l])
    br = jnp.zeros((1, ROUTER_LANES), F32)
    br = br.at[0, 0:E_GROUPS].set(moe_b_rg[l]).at[0, ROUTER_E_OFF:ROUTER_E_OFF + N_EXPERTS].set(moe_b_re[l])
    return dict(w1=w1, gq=gq, wuq=wuq, bd=bd, gkv=b_g_kv[l][None, :], wuv=wuv, wo=w_out[l].astype(BF16), wr=wr, br=br)


def _band_bias(rel_bias, nq, nk, back):
    n = nk + nq - 1
    idx = np.clip(back + nq - 1 - np.arange(n + 1), -A_REL_CLIP, A_REL_CLIP) + A_REL_CLIP
    line = rel_bias[:, idx]
    skew = jnp.tile(line, (1, nq))[:, 0:nq * n].reshape(A_HEADS, nq, n)
    return skew[:, :, nq - 1:nq - 1 + nk].reshape(A_HEADS * nq, nk)


def _layer(x, mod, pos, lw, l, cache, prm, final, *, nb, r):
    B, S, D = x.shape
    cos_t, sin_t = _rope_tables(pos)
    (qkv, knew, vnew, qmla, kmla, ckv, kpe, z, xbc, dtr) = _inproj(
        x, mod, prm['g_mix'], lw['w1'], lw['gq'], lw['wuq'], lw['bd'], lw['gkv'], cos_t, sin_t, nb=nb, r=r,
        nparts=1)
    if cache is None:
        ya = _band_prompt(qkv, _band_bias(prm['a_rel_bias'], CHUNK, A_BAND, A_WINDOW))
        yb = _mla_prompt(qmla, kmla, lw['wuv'], bq=min(512, S), bk=1024)
        prefix = jnp.zeros((B, C_CONV_W - 1, C_CONV_DIM), F32)
        h0t = jnp.zeros((B, C_HEADS, C_D_STATE, C_HEAD_DIM), F32)
        lt, lc, nbat = min(S, 512), SSD_CHUNK, B
    else:
        nr = cache['a_kt'].shape[3]
        ya = _band_sample(qkv, cache['a_kt'], cache['a_vt'], l, _band_bias(prm['a_rel_bias'], S, nr + S, nr))
        yb = _mla_sample(qmla, kmla, cache['b_ckv'], cache['b_kpet'], l, lw['wuv'])
        prefix = cache['c_conv'][l]
        h0t = jnp.swapaxes(cache['c_ssm'][l], -1, -2)
        lt = lc = min(SSD_CHUNK, S)
        nbat = math.gcd(B, 4)
    yc, hft = _ssd(xbc, z, dtr, prefix, h0t, prm['c_conv_w'], prm['c_conv_b'], prm['c_dt_bias'], prm['c_a_log'],
                   prm['c_d_exp'], prm['c_g_norm'], nbat=nbat, lt=lt, lc=lc)
    conv_state = jnp.concatenate([prefix, xbc], axis=1)[:, -(C_CONV_W - 1):]
    x1, h2, gate = _outproj(ya, yb, yc, x, mod, prm['g_ffn'], lw['wo'], lw['wr'], lw['br'], nb=nb, r=r,
                            nparts=2)
    xo = _moe_sorted(h2, gate, x1, mod, prm['moe_w_gu'], prm['moe_w_down'], l, final, nb=nb, r=r)
    states = (knew.reshape(B, -1, A_HEADS, A_HEAD_DIM), vnew.reshape(B, -1, A_HEADS, A_HEAD_DIM),
              ckv, kpe, conv_state, jnp.swapaxes(hft, -1, -2))
    return xo, states


def kernel(x_prompt, x_sample, c_prompt, c_sample, cache_a_k, cache_a_v, cache_b_ckv, cache_b_kpe,
           state_c_conv, state_c_ssm, w_ada, b_ada, g_mix, w_in, a_rel_bias, b_g_q, b_w_uq, b_g_kv,
           b_w_uk, b_w_uv, c_conv_w, c_conv_b, c_dt_bias, c_a_log, c_d, c_g_norm, w_out, g_ffn,
           moe_w_rg, moe_b_rg, moe_w_re, moe_b_re, moe_w_gu, moe_w_down, g_final, w_ada_f, b_ada_f):
    depth = w_in.shape[0]
    D = x_prompt.shape[-1]
    Bp, Sp, _ = x_prompt.shape
    Bs, Ss, _ = x_sample.shape
    past = cache_b_ckv.shape[2]
    pos_p = jnp.arange(Sp)
    pos_s = past + jnp.arange(Ss)

    nc = Bp + Bs
    ncp = -(-nc // 8) * 8
    c_all = jnp.pad(jnp.concatenate([c_prompt, c_sample], axis=0), ((0, ncp - nc), (0, 0)))
    mod = _ada(c_all, w_ada, b_ada[:, None, :])
    modf = _ada(c_all, w_ada_f[None], b_ada_f[None, None, :])[0]

    nr = cache_a_k.shape[2]
    to_t = lambda c: jnp.transpose(c, (0, 1, 3, 4, 2)).reshape(depth, Bs, A_WIDTH, nr)
    cache = {'a_kt': to_t(cache_a_k), 'a_vt': to_t(cache_a_v), 'b_ckv': cache_b_ckv,
             'b_kpet': jnp.swapaxes(cache_b_kpe, 2, 3), 'c_conv': state_c_conv, 'c_ssm': state_c_ssm}

    rp = min(Sp, A_WINDOW)
    moe_w_gu = moe_w_gu.astype(BF16)
    moe_w_down = moe_w_down.astype(BF16)
    xp, xs = x_prompt, x_sample
    st_p, st_s = [], []
    for l in range(depth):
        lw = _layer_weights(l, w_in, b_g_q, b_w_uq, b_g_kv, b_w_uk, b_w_uv, w_out,
                            moe_w_rg, moe_b_rg, moe_w_re, moe_b_re)
        prm = {'g_mix': g_mix[l][None, :], 'a_rel_bias': a_rel_bias[l], 'c_conv_w': c_conv_w[l],
               'c_conv_b': c_conv_b[l][None, :], 'c_dt_bias': jnp.pad(c_dt_bias[l], (0, LANES - C_HEADS))[None, :],
               'c_a_log': jnp.pad(c_a_log[l], (0, LANES - C_HEADS))[None, :],
               'c_d_exp': jnp.repeat(c_d[l], C_HEAD_DIM)[None, :], 'c_g_norm': c_g_norm[l][None, :],
               'g_ffn': g_ffn[l][None, :], 'moe_w_gu': moe_w_gu, 'moe_w_down': moe_w_down}
        fin_p = (modf[0:Bp][:, None, :], g_final[None, :]) if l == depth - 1 else None
        fin_s = (modf[Bp:nc][:, None, :], g_final[None, :]) if l == depth - 1 else None
        xp, sp = _layer(xp, mod[l, 0:Bp][:, None, :], pos_p, lw, l, None, prm, fin_p, nb=1, r=rp)
        xs, ss = _layer(xs, mod[l, Bp:nc][:, None, :], pos_s, lw, l, cache, prm, fin_s, nb=Bs, r=Ss)
        st_p.append(sp)
        st_s.append(ss)
    stack = lambda st, k: jnp.stack([t[k] for t in st])
    return ((xp, xs) + tuple(stack(st_p, k) for k in range(6))
            + tuple(stack(st_s, k) for k in range(6)))
```
